```python
import jax, jax.numpy as jnp
from jax import lax
import numpy as np

D_MODEL = 1024
BATCH = 32
SEQ = 256
DEPTH = 2
DEC_BATCH = 4
DEC_SEQ = 4096
PAST_LEN = 512

GRID_W = 64
RET_HEADS = 4
RET_DK = 128
RET_DV = 256
RET_CHUNK = 128
HG_HEADS = 8
HG_DK = 128
HG_DV = D_MODEL // HG_HEADS
HG_CHUNK = 32
N_EXPERTS = 16
EXPERT_FF = 2048
CAPACITY_FACTOR = 2
ROPE_BASE = 10000.0
NORM_EPS = 1e-6
RET_QK_W = RET_HEADS * RET_DK
RET_V_W = RET_HEADS * RET_DV
HG_K_W = HG_HEADS * HG_DK
HG_V_W = HG_HEADS * HG_DV
IN_SPLIT_SIZES = (RET_QK_W, RET_QK_W, RET_V_W, RET_V_W, HG_K_W, HG_K_W, HG_K_W, HG_V_W, HG_V_W)
IN_COLS = RET_QK_W * 2 + RET_V_W * 2 + HG_K_W * 3 + HG_V_W * 2

kernel_name = 'bidir_retention_hgrn2_ecmoe_diffusion_step'


def rms_norm(x, g):
    xf = x.astype(jnp.float32)
    y = xf * lax.rsqrt(jnp.mean(xf * xf, axis=-1, keepdims=True) + NORM_EPS)
    return (y * g.astype(jnp.float32)).astype(x.dtype)


def rope_2d_tables(n_tokens):
    rows = n_tokens // GRID_W
    r, c = jnp.meshgrid(jnp.arange(rows), jnp.arange(GRID_W), indexing='ij')
    pos = jnp.stack([r.reshape(-1), c.reshape(-1)], axis=-1).astype(jnp.float32)
    nf = RET_DK // 4
    inv_freq = ROPE_BASE ** (-jnp.arange(nf, dtype=jnp.float32) / nf)
    ang = pos[:, :, None] * inv_freq
    return jnp.cos(ang), jnp.sin(ang)


def apply_rope_2d(x, cos, sin):
    shp = x.shape
    xr = x.reshape(shp[:-1] + (2, 2, RET_DK // 4))
    x1, x2 = xr[..., 0, :], xr[..., 1, :]
    out = jnp.stack([x1 * cos - x2 * sin, x2 * cos + x1 * sin], axis=-2)
    return out.reshape(shp)


def _to_chunks(a, size):
    b, h, t, d = a.shape
    return jnp.moveaxis(a.reshape(b, h, t // size, size, d), 2, 0)


def _from_chunks(a):
    n, b, h, size, d = a.shape
    return jnp.moveaxis(a, 0, 2).reshape(b, h, n * size, d)


def retention_scan(q, k, v, log_gamma, s0):
    L = RET_CHUNK
    pos = jnp.arange(L, dtype=jnp.float32)
    diff = pos[:, None] - pos[None, :]
    decay_intra = jnp.where(diff >= 0, jnp.exp(log_gamma[:, None, None] * jnp.maximum(diff, 0.0)), 0.0)
    q_decay = jnp.exp(log_gamma[:, None] * (pos + 1.0))[..., None]
    k_decay = jnp.exp(log_gamma[:, None] * (L - 1.0 - pos))[..., None]
    chunk_decay = jnp.exp(log_gamma * L)[:, None, None]

    def step(s, qkv):
        qc, kc, vc = qkv
        scores = jnp.einsum('bhik,bhjk->bhij', qc, kc) * decay_intra
        o = jnp.einsum('bhij,bhjv->bhiv', scores, vc) + jnp.einsum('bhik,bhkv->bhiv', qc * q_decay, s)
        s = s * chunk_decay + jnp.einsum('bhjk,bhjv->bhkv', kc * k_decay, vc)
        return s, o

    s_fin, o = lax.scan(step, s0.astype(jnp.float32), (_to_chunks(q, L), _to_chunks(k, L), _to_chunks(v, L)))
    return _from_chunks(o), s_fin


def gla_scan(q, k, v, log_f, s0):
    L = HG_CHUNK
    causal = jnp.tril(jnp.ones((L, L), dtype=bool))[None, None, :, :, None]

    def step(s, inp):
        qc, kc, vc, gc = inp
        b = jnp.cumsum(gc, axis=2)
        rel = jnp.exp(jnp.where(causal, b[:, :, :, None, :] - b[:, :, None, :, :], -jnp.inf))
        scores = jnp.einsum('bhik,bhjk,bhijk->bhij', qc, kc, rel)
        b_last = b[:, :, -1:, :]
        o = jnp.einsum('bhij,bhjv->bhiv', scores, vc) + jnp.einsum('bhik,bhkv->bhiv', qc * jnp.exp(b), s)
        s = jnp.exp(b_last)[:, :, 0, :, None] * s + jnp.einsum('bhjk,bhjv->bhkv', kc * jnp.exp(b_last - b), vc)
        return s, o

    s_fin, o = lax.scan(step, s0.astype(jnp.float32),
                        (_to_chunks(q, L), _to_chunks(k, L), _to_chunks(v, L), _to_chunks(log_f, L)))
    return _from_chunks(o), s_fin


def token_mixer(h, rope, s_ret0, s_hg0, w_in, ret_log_gamma, hg_lb, w_ret_o, w_hg_o, w_merge, w_out):
    B, T, _ = h.shape
    dt = h.dtype
    proj = jnp.einsum('btd,dc->btc', h, w_in).astype(jnp.float32)
    split_points = np.cumsum(IN_SPLIT_SIZES)[:-1].tolist()
    r_q, r_k, r_v, r_g, g_q, g_f_fwd, g_f_bwd, g_i, g_o = jnp.split(proj, split_points, axis=-1)

    def heads(a, n):
        return a.reshape(B, T, n, -1).transpose(0, 2, 1, 3)

    def merge_heads(a):
        return a.transpose(0, 2, 1, 3).reshape(B, T, -1)

    def flip(a):
        return a[:, :, ::-1]

    q = heads(r_q, RET_HEADS)
    k = heads(r_k, RET_HEADS) * (RET_DK ** -0.5)
    v = heads(r_v, RET_HEADS)
    if rope is not None:
        q = apply_rope_2d(q, *rope)
        k = apply_rope_2d(k, *rope)
    o_f, sr_f = retention_scan(q, k, v, ret_log_gamma[0], s_ret0[:, 0])
    o_b, sr_b = retention_scan(flip(q), flip(k), flip(v), ret_log_gamma[1], s_ret0[:, 1])
    o = o_f + flip(o_b)
    mu = jnp.mean(o, axis=-1, keepdims=True)
    var = jnp.mean(jnp.square(o - mu), axis=-1, keepdims=True)
    o = (o - mu) * lax.rsqrt(var + NORM_EPS)
    ret_out = jnp.einsum('btc,cd->btd', (merge_heads(o) * jax.nn.silu(r_g)).astype(dt), w_ret_o).astype(jnp.float32)

    qg = jax.nn.silu(heads(g_q, HG_HEADS)) * (HG_DK ** -0.5)
    vg = heads(g_i, HG_HEADS)

    def forget(z, lb):
        lbh = lb.reshape(HG_HEADS, 1, HG_DK)
        log_f = jnp.logaddexp(jnp.log(lbh), jnp.log1p(-lbh) + jax.nn.log_sigmoid(heads(z, HG_HEADS)))
        return -jnp.expm1(log_f), log_f

    k_f, lf_f = forget(g_f_fwd, hg_lb[0])
    k_b, lf_b = forget(g_f_bwd, hg_lb[1])
    og_f, sh_f = gla_scan(qg, k_f, vg, lf_f, s_hg0[:, 0])
    og_b, sh_b = gla_scan(flip(qg), flip(k_b), flip(vg), flip(lf_b), s_hg0[:, 1])
    og = og_f + flip(og_b)
    og = og * lax.rsqrt(jnp.mean(og * og, axis=-1, keepdims=True) + NORM_EPS)
    hg_out = jnp.einsum('btc,cd->btd', (merge_heads(og) * jax.nn.silu(g_o)).astype(dt), w_hg_o).astype(jnp.float32)

    gates = jax.nn.sigmoid(jnp.einsum('btd,dc->btc', h, w_merge).astype(jnp.float32))
    gate_r, gate_h = jnp.split(gates, 2, axis=-1)
    merged = gate_r * ret_out + gate_h * hg_out
    out = jnp.einsum('btd,de->bte', merged.astype(dt), w_out)
    return out, jnp.stack([sr_f, sr_b], axis=1), jnp.stack([sh_f, sh_b], axis=1)


def expert_choice_ffn(h, w_router, w_gate, w_up, w_down):
    n_tok = h.shape[0]
    cap = CAPACITY_FACTOR * n_tok // N_EXPERTS
    aff = jax.nn.softmax(jnp.einsum('nd,de->ne', h, w_router).astype(jnp.float32), axis=-1)
    w_sel, idx = lax.top_k(aff.T, cap)
    xe = h[idx]
    hid = jax.nn.silu(jnp.einsum('ecd,edf->ecf', xe, w_gate)) * jnp.einsum('ecd,edf->ecf', xe, w_up)
    ye = jnp.einsum('ecf,efd->ecd', hid, w_down).astype(jnp.float32)
    contrib = (ye * w_sel[..., None]).reshape(-1, h.shape[-1])
    return jnp.zeros((n_tok, h.shape[-1]), jnp.float32).at[idx.reshape(-1)].add(contrib).astype(h.dtype)


def trunk_layer(x, mod, rope, s_ret0, s_hg0, norm_mix_g, norm_ffn_g, w_in, ret_log_gamma, hg_lb,
                w_ret_o, w_hg_o, w_merge, w_out, w_router, w_exp_gate, w_exp_up, w_exp_down):
    B, T, D = x.shape
    shift1, scale1, gate1, shift2, scale2, gate2 = jnp.split(mod.astype(jnp.float32), 6, axis=-1)
    h = (rms_norm(x, norm_mix_g).astype(jnp.float32) * (1.0 + scale1) + shift1).astype(x.dtype)
    mix, s_ret, s_hg = token_mixer(h, rope, s_ret0, s_hg0, w_in, ret_log_gamma, hg_lb, w_ret_o, w_hg_o, w_merge, w_out)
    x = (x.astype(jnp.float32) + gate1 * mix.astype(jnp.float32)).astype(x.dtype)
    h = (rms_norm(x, norm_ffn_g).astype(jnp.float32) * (1.0 + scale2) + shift2).astype(x.dtype)
    ffn = expert_choice_ffn(h.reshape(B * T, D), w_router, w_exp_gate, w_exp_up, w_exp_down).reshape(B, T, D)
    x = (x.astype(jnp.float32) + gate2 * ffn.astype(jnp.float32)).astype(x.dtype)
    return x, s_ret, s_hg


def setup_inputs(seed: int = 0) -> dict:
    key = jax.random.key(seed)
    ks = jax.random.split(key, 24)
    D = D_MODEL
    f32 = jnp.float32

    def nrm(k, shape, scale):
        return jax.random.normal(k, shape, f32) * scale

    g0 = 1.0 - 2.0 ** (-5.0 - np.arange(RET_HEADS, dtype=np.float32))
    gamma_logit0 = jnp.asarray(np.log(g0) - np.log1p(-g0), f32)
    return {
        'x_prompt': nrm(ks[0], (BATCH, SEQ, D), 1.0),
        'x_sample': nrm(ks[1], (DEC_BATCH, DEC_SEQ, D), 1.0),
        'state_ret': nrm(ks[2], (DEC_BATCH, DEPTH, 2, RET_HEADS, RET_DK, RET_DV), 0.5),
        'state_hgrn': nrm(ks[3], (DEC_BATCH, DEPTH, 2, HG_HEADS, HG_DK, HG_DV), 0.5),
        'c': nrm(ks[4], (DEC_BATCH, D), 1.0),
        'c_ctx': nrm(ks[5], (D,), 1.0),
        'ada_w': nrm(ks[6], (DEPTH, D, 6 * D), 0.5 * D ** -0.5),
        'ada_b': nrm(ks[7], (DEPTH, 6 * D), 0.01),
        'norm_mix_g': 1.0 + nrm(ks[8], (DEPTH, D), 0.02),
        'norm_ffn_g': 1.0 + nrm(ks[9], (DEPTH, D), 0.02),
        'w_in': nrm(ks[10], (DEPTH, D, IN_COLS), D ** -0.5),
        'ret_gamma_logit': gamma_logit0 + nrm(ks[11], (DEPTH, 2, RET_HEADS), 0.1),
        'hg_lb_logit': nrm(ks[12], (DEPTH, 2, HG_K_W), 0.5),
        'w_ret_o': nrm(ks[13], (DEPTH, RET_V_W, D), RET_V_W ** -0.5),
        'w_hg_o': nrm(ks[14], (DEPTH, HG_V_W, D), HG_V_W ** -0.5),
        'w_merge': nrm(ks[15], (DEPTH, D, 2 * D), D ** -0.5),
        'w_out': nrm(ks[16], (DEPTH, D, D), D ** -0.5),
        'w_router': nrm(ks[17], (DEPTH, D, N_EXPERTS), D ** -0.5),
        'w_exp_gate': nrm(ks[18], (DEPTH, N_EXPERTS, D, EXPERT_FF), D ** -0.5),
        'w_exp_up': nrm(ks[19], (DEPTH, N_EXPERTS, D, EXPERT_FF), D ** -0.5),
        'w_exp_down': nrm(ks[20], (DEPTH, N_EXPERTS, EXPERT_FF, D), EXPERT_FF ** -0.5),
        'final_g': 1.0 + nrm(ks[21], (D,), 0.02),
    }


def reference(x_prompt, x_sample, state_ret, state_hgrn, c, c_ctx, ada_w, ada_b, norm_mix_g, norm_ffn_g,
              w_in, ret_gamma_logit, hg_lb_logit, w_ret_o, w_hg_o, w_merge, w_out, w_router,
              w_exp_gate, w_exp_up, w_exp_down, final_g):
    f32 = jnp.float32
    n_ctx_req = x_prompt.shape[0]
    n_lat_tok = x_sample.shape[1]
    rope = rope_2d_tables(n_lat_tok)
    ret_log_gamma = jax.nn.log_sigmoid(ret_gamma_logit.astype(f32))
    p_lb = jax.nn.softmax(hg_lb_logit.astype(f32), axis=0)
    hg_lb = jnp.clip(jnp.cumsum(p_lb, axis=0) - p_lb[0:1], 0.0, 1.0 - 1e-6)
    zeros_ret = jnp.zeros((n_ctx_req, 2, RET_HEADS, RET_DK, RET_DV), f32)
    zeros_hg = jnp.zeros((n_ctx_req, 2, HG_HEADS, HG_DK, HG_DV), f32)

    xp = x_prompt
    xs = x_sample
    new_ret, new_hg = [], []
    for l in range(DEPTH):
        layer_w = (norm_mix_g[l], norm_ffn_g[l], w_in[l], ret_log_gamma[l], hg_lb[l], w_ret_o[l], w_hg_o[l],
                   w_merge[l], w_out[l], w_router[l], w_exp_gate[l], w_exp_up[l], w_exp_down[l])
        mod_ctx = (jnp.einsum('d,de->e', jax.nn.silu(c_ctx), ada_w[l]) + ada_b[l])[None, None, :]
        mod_lat = (jnp.einsum('bd,de->be', jax.nn.silu(c), ada_w[l]) + ada_b[l])[:, None, :]
        xp, s_ret, s_hg = trunk_layer(xp, mod_ctx, None, zeros_ret, zeros_hg, *layer_w)
        new_ret.append(s_ret)
        new_hg.append(s_hg)
        xs, _, _ = trunk_layer(xs, mod_lat, rope, state_ret[:, l], state_hgrn[:, l], *layer_w)

    y_prompt = rms_norm(xp, final_g)
    y_sample = rms_norm(xs, final_g)
    new_state_ret = jnp.stack(new_ret, axis=1)
    new_state_hgrn = jnp.stack(new_hg, axis=1)
    return (y_prompt, y_sample, new_state_ret, new_state_hgrn)
```

```python
import functools

import jax
import jax.numpy as jnp
from jax import lax
from jax.experimental import pallas as pl
from jax.experimental.pallas import tpu as pltpu

F32 = jnp.float32
BF16 = jnp.bfloat16
I32 = jnp.int32
HIGHEST = lax.Precision.HIGHEST

NORM_EPS = 1e-6
ROPE_BASE = 10000.0
GRID_W = 64
CAPACITY_FACTOR = 2
RET_CHUNK = 128
HG_CHUNK = 64
HG_SUB = 16
HG_EXP_CLAMP = 80.0
MOD_ROWS = 8
LANE = 128
BF16_SUBLANE = 16
VMEM_LIMIT = 56 * 1024 * 1024


def _cparams(*sem):
    return pltpu.CompilerParams(dimension_semantics=sem, vmem_limit_bytes=VMEM_LIMIT)


def _dot(a, b, **kw):
    return jnp.dot(a, b, preferred_element_type=F32, **kw)


def _dot_nt(a, b, **kw):
    return lax.dot_general(a, b, (((1,), (1,)), ((), ())), preferred_element_type=F32, **kw)


def _dot_tn(a, b, **kw):
    return lax.dot_general(a, b, (((0,), (0,)), ((), ())), preferred_element_type=F32, **kw)


def _silu(x):
    return x * jax.nn.sigmoid(x)


def _mod_kernel(c_ref, w_ref, b_ref, o_ref):
    s = _silu(c_ref[...])
    o_ref[...] = _dot(s, w_ref[...], precision=HIGHEST) + b_ref[...]


def _mod_call(cvec, ada_w, ada_b):
    depth, d, six_d = ada_w.shape
    tn = 6 * LANE * 2
    assert six_d % tn == 0
    return pl.pallas_call(
        _mod_kernel,
        grid=(depth, six_d // tn),
        in_specs=[
            pl.BlockSpec((MOD_ROWS, d), lambda l, j: (0, 0)),
            pl.BlockSpec((None, d, tn), lambda l, j: (l, 0, j)),
            pl.BlockSpec((None, 1, tn), lambda l, j: (l, 0, j)),
        ],
        out_specs=pl.BlockSpec((None, MOD_ROWS, tn), lambda l, j: (l, 0, j)),
        out_shape=jax.ShapeDtypeStruct((depth, MOD_ROWS, six_d), F32),
        compiler_params=_cparams("arbitrary", "arbitrary"),
        name="adaln_mod",
    )(cvec, ada_w, ada_b.reshape(depth, 1, six_d))


def _rms(x):
    return x * lax.rsqrt(jnp.mean(x * x, axis=-1, keepdims=True) + NORM_EPS)


def _norm_kernel(*refs, d, residual, final):
    it = iter(refs)
    x_ref = next(it)
    f_ref, gate_ref = (next(it), next(it)) if residual else (None, None)
    mod_ref = None if final else next(it)
    g_ref = next(it)
    xo_ref = next(it) if residual else None
    h_ref = next(it)
    x = x_ref[...]
    if residual:
        x = x + gate_ref[...] * f_ref[...]
        xo_ref[...] = x
    y = _rms(x) * g_ref[...]
    if not final:
        y = y * (1.0 + mod_ref[:, d:2 * d]) + mod_ref[:, 0:d]
    h_ref[...] = y.astype(h_ref.dtype)


def _norm_call(x, ffn, mod_gate, mod_next, g, mod_row, tm):
    n, d = x.shape
    residual = ffn is not None
    final = mod_next is None
    row = pl.BlockSpec((tm, d), lambda i: (i, 0))
    in_specs, args = [row], [x]
    if residual:
        in_specs += [row, pl.BlockSpec((None, 1, d), lambda i: (mod_row(i), 0, 5))]
        args += [ffn, mod_gate]
    if not final:
        in_specs.append(pl.BlockSpec((None, 1, 2 * d), lambda i: (mod_row(i), 0, 0)))
        args.append(mod_next)
    in_specs.append(pl.BlockSpec((1, d), lambda i: (0, 0)))
    args.append(g.reshape(1, d))
    out_specs, out_shape = [row], [jax.ShapeDtypeStruct((n, d), F32 if final else BF16)]
    if residual:
        out_specs = [row, row]
        out_shape = [jax.ShapeDtypeStruct((n, d), F32)] + out_shape
    outs = pl.pallas_call(
        functools.partial(_norm_kernel, d=d, residual=residual, final=final),
        grid=(n // tm,),
        in_specs=in_specs,
        out_specs=out_specs,
        out_shape=out_shape,
        compiler_params=_cparams("arbitrary"),
        name="residual_norm",
    )(*args)
    return outs if residual else (x, outs[0])


def _proj_kernel(h_ref, w_ref, o_ref, wb_ref, *, act):
    @pl.when(pl.program_id(1) == 0)
    def _():
        wb_ref[...] = w_ref[...].astype(BF16)

    acc = _dot(h_ref[...], wb_ref[...])
    if act == "sigmoid":
        acc = jax.nn.sigmoid(acc)
    o_ref[...] = acc


def _proj_call(h, w, layer, tm, tn, act=None):
    n, d = h.shape
    cols = w.shape[-1]
    return pl.pallas_call(
        functools.partial(_proj_kernel, act=act),
        grid=(cols // tn, n // tm),
        in_specs=[
            pl.BlockSpec((tm, d), lambda j, i: (i, 0)),
            pl.BlockSpec((None, d, tn), lambda j, i: (layer, 0, j)),
        ],
        out_specs=pl.BlockSpec((tm, tn), lambda j, i: (i, j)),
        out_shape=jax.ShapeDtypeStruct((n, cols), F32),
        scratch_shapes=[pltpu.VMEM((d, tn), BF16)],
        compiler_params=_cparams("arbitrary", "arbitrary"),
        name="in_proj",
    )(h, w)


def _swap_half_pairs(x):
    lane = lax.broadcasted_iota(I32, x.shape, 1)
    return jnp.where((lane & 32) == 0, pltpu.roll(x, LANE - 32, 1), pltpu.roll(x, 32, 1))


def _ret_kernel(*refs, heads, dk, dv, chunk, n_chunks, has_init, has_rope):
    it = iter(refs)
    lg_ref = next(it)
    qkv = [[next(it) for _ in range(3)] for _ in range(2)]
    s0_ref = next(it) if has_init else None
    rope = [[next(it) for _ in range(2)] for _ in range(2)] if has_rope else None
    o_refs = [next(it), next(it)]
    sfin_ref = next(it)
    s_ref = next(it)

    c = pl.program_id(1)

    @pl.when(c == 0)
    def _():
        if has_init:
            s_ref[...] = s0_ref[...]
        else:
            s_ref[...] = jnp.zeros_like(s_ref)

    L = chunk
    ii = lax.broadcasted_iota(I32, (L, L), 0)
    jj = lax.broadcasted_iota(I32, (L, L), 1)
    pcol = lax.broadcasted_iota(I32, (L, 1), 0).astype(F32)
    for d in range(2):
        q_ref, k_ref, v_ref = qkv[d]
        if d == 0:
            dif = (ii - jj).astype(F32)
            q_pow, k_pow = pcol + 1.0, (L - 1.0) - pcol
        else:
            dif = (jj - ii).astype(F32)
            q_pow, k_pow = L - pcol, pcol
        for hh in range(heads):
            lg = lg_ref[d, hh]
            q = q_ref[:, hh * dk:(hh + 1) * dk]
            k = k_ref[:, hh * dk:(hh + 1) * dk] * (dk ** -0.5)
            v = v_ref[:, hh * dv:(hh + 1) * dv].astype(BF16)
            if has_rope:
                cos, sin = rope[d][0][...], rope[d][1][...]
                q = q * cos + _swap_half_pairs(q) * sin
                k = k * cos + _swap_half_pairs(k) * sin
            decay = jnp.where(dif >= 0.0, jnp.exp(lg * jnp.maximum(dif, 0.0)), 0.0)
            scores = _dot_nt(q.astype(BF16), k.astype(BF16)) * decay
            s_old = s_ref[d, hh]
            o = _dot(scores.astype(BF16), v) + _dot((q * jnp.exp(lg * q_pow)).astype(BF16), s_old.astype(BF16))
            kv = _dot((k * jnp.exp(lg * k_pow)).T.astype(BF16), v)
            s_ref[d, hh] = s_old * jnp.exp(jnp.full((1, 1), lg * L, F32)) + kv
            o_refs[d][:, hh * dv:(hh + 1) * dv] = o

    @pl.when(c == n_chunks - 1)
    def _():
        sfin_ref[...] = s_ref[...]


def _ret_call(proj, log_gamma, s0, rope, *, tok0, batch, seq, heads, dk, dv):
    L = RET_CHUNK
    n = seq // L
    qw, vw = heads * dk, heads * dv
    assert seq % L == 0 and tok0 % L == 0 and vw % qw == 0
    r0 = tok0 // L

    def fwd(b, c):
        return r0 + b * n + c

    def bwd(b, c):
        return r0 + b * n + (n - 1 - c)

    in_specs = [pl.BlockSpec(memory_space=pltpu.SMEM)]
    args = [log_gamma]
    for rmap in (fwd, bwd):
        in_specs += [
            pl.BlockSpec((L, qw), lambda b, c, rmap=rmap: (rmap(b, c), 0)),
            pl.BlockSpec((L, qw), lambda b, c, rmap=rmap: (rmap(b, c), 1)),
            pl.BlockSpec((L, vw), lambda b, c, rmap=rmap: (rmap(b, c), (2 * qw) // vw)),
        ]
        args += [proj, proj, proj]
    state_spec = pl.BlockSpec((None, 2, heads, dk, dv), lambda b, c: (b, 0, 0, 0, 0))
    if s0 is not None:
        in_specs.append(state_spec)
        args.append(s0)
    if rope is not None:
        for cmap in (lambda b, c: (c, 0), lambda b, c: (n - 1 - c, 0)):
            in_specs += [pl.BlockSpec((L, dk), cmap), pl.BlockSpec((L, dk), cmap)]
            args += [rope[0], rope[1]]
    n_tok = batch * seq
    return pl.pallas_call(
        functools.partial(_ret_kernel, heads=heads, dk=dk, dv=dv, chunk=L, n_chunks=n,
                          has_init=s0 is not None, has_rope=rope is not None),
        grid=(batch, n),
        in_specs=in_specs,
        out_specs=[
            pl.BlockSpec((L, vw), lambda b, c: (b * n + c, 0)),
            pl.BlockSpec((L, vw), lambda b, c: (b * n + (n - 1 - c), 0)),
            state_spec,
        ],
        out_shape=[
            jax.ShapeDtypeStruct((n_tok, vw), F32),
            jax.ShapeDtypeStruct((n_tok, vw), F32),
            jax.ShapeDtypeStruct((batch, 2, heads, dk, dv), F32),
        ],
        scratch_shapes=[pltpu.VMEM((2, heads, dk, dv), F32)],
        compiler_params=_cparams("arbitrary", "arbitrary"),
        name="retention_scan",
    )(*args)


def _hg_chunk(q, z, v, lb, st, cum_mat, tri_mask, rev, dk):
    C = q.shape[0]
    qg = _silu(q) * (dk ** -0.5)
    log_sig = jnp.minimum(z, 0.0) - jnp.log1p(jnp.exp(-jnp.abs(z)))
    a = jnp.log(lb)
    cc = jnp.log1p(-lb) + log_sig
    log_f = jnp.maximum(a, cc) + jnp.log1p(jnp.exp(-jnp.abs(a - cc)))
    k = (1.0 - lb) * jax.nn.sigmoid(-z)
    sums = _dot(cum_mat, log_f, precision=HIGHEST)
    b, anchor = sums[:C], sums[C:]
    qt = qg * jnp.exp(b - anchor)
    rows = lax.broadcasted_iota(I32, (C, 1), 0)
    parts = []
    for s in range(C // HG_SUB):
        a_s = anchor[s * HG_SUB:s * HG_SUB + 1]
        seen = rows >= s * HG_SUB if rev else rows < (s + 1) * HG_SUB
        ks = jnp.where(seen, k * jnp.exp(jnp.minimum(a_s - b, HG_EXP_CLAMP)), 0.0)
        parts.append(_dot_nt(qt[s * HG_SUB:(s + 1) * HG_SUB].astype(BF16), ks.astype(BF16)))
    att = jnp.where(tri_mask, jnp.concatenate(parts, axis=0), 0.0)
    o = _dot(att.astype(BF16), v.astype(BF16)) + _dot_nt((qt * jnp.exp(anchor)).astype(BF16), st.astype(BF16))
    b_tot = b[0:1] if rev else b[C - 1:C]
    k_st = k * jnp.exp(b_tot - b)
    st = st * jnp.exp(b_tot) + _dot_tn(v.astype(BF16), k_st.astype(BF16))
    return o, st


def _hg_kernel(*refs, heads, dk, dv, tt, n_blocks, has_init):
    it = iter(refs)
    qzv = [[next(it) for _ in range(3)] for _ in range(2)]
    lb_ref = next(it)
    s0_ref = next(it) if has_init else None
    o_refs = [next(it), next(it)]
    sfin_ref = next(it)
    st_ref = next(it)

    t = pl.program_id(1)

    @pl.when(t == 0)
    def _():
        for d in range(2):
            for hh in range(heads):
                st_ref[d, hh] = s0_ref[d, hh].T if has_init else jnp.zeros((dv, dk), F32)

    C = HG_CHUNK
    ii = lax.broadcasted_iota(I32, (C, C), 0)
    jj = lax.broadcasted_iota(I32, (C, C), 1)
    i2 = lax.broadcasted_iota(I32, (2 * C, C), 0)
    j2 = lax.broadcasted_iota(I32, (2 * C, C), 1)
    for d in range(2):
        rev = d == 1
        q_ref, z_ref, v_ref = qzv[d]
        o_ref = o_refs[d]
        sub_lo = ((i2 - C) // HG_SUB) * HG_SUB
        top = i2 < C
        if not rev:
            tri = jj <= ii
            cum = (top & (j2 <= i2)) | (~top & (j2 < sub_lo))
        else:
            tri = jj >= ii
            cum = (top & (j2 >= i2)) | (~top & (j2 >= sub_lo + HG_SUB))
        cum_mat = jnp.where(cum, 1.0, 0.0).astype(F32)

        def head_body(hh, carry, q_ref=q_ref, z_ref=z_ref, v_ref=v_ref, o_ref=o_ref, d=d, rev=rev,
                      tri=tri, cum_mat=cum_mat):
            ck = pl.ds(pl.multiple_of(hh * dk, LANE), dk)
            cv = pl.ds(pl.multiple_of(hh * dv, LANE), dv)
            lb = lb_ref[d, :, ck]
            st = st_ref[d, hh]
            n_ch = tt // C
            for ci in range(n_ch):
                r = pl.ds((n_ch - 1 - ci if rev else ci) * C, C)
                o, st = _hg_chunk(q_ref[r, ck], z_ref[r, ck], v_ref[r, cv], lb, st, cum_mat, tri, rev, dk)
                o_ref[r, cv] = o
            st_ref[d, hh] = st
            return carry

        lax.fori_loop(0, heads, head_body, 0)

    @pl.when(t == n_blocks - 1)
    def _():
        for d in range(2):
            for hh in range(heads):
                sfin_ref[d, hh] = st_ref[d, hh].T


def _hg_call(proj, lb, s0, *, tok0, batch, seq, heads, dk, dv, col_q, col_f, col_i):
    tt = min(seq, 256)
    n = seq // tt
    kw, vw = heads * dk, heads * dv
    assert seq % tt == 0 and tok0 % tt == 0 and tt % HG_CHUNK == 0
    assert col_q % kw == 0 and col_f % kw == 0 and col_i % vw == 0
    r0 = tok0 // tt

    def fwd(b, t):
        return r0 + b * n + t

    def bwd(b, t):
        return r0 + b * n + (n - 1 - t)

    in_specs, args = [], []
    for d, rmap in enumerate((fwd, bwd)):
        in_specs += [
            pl.BlockSpec((tt, kw), lambda b, t, rmap=rmap: (rmap(b, t), col_q // kw)),
            pl.BlockSpec((tt, kw), lambda b, t, rmap=rmap, d=d: (rmap(b, t), col_f // kw + d)),
            pl.BlockSpec((tt, vw), lambda b, t, rmap=rmap: (rmap(b, t), col_i // vw)),
        ]
        args += [proj, proj, proj]
    in_specs.append(pl.BlockSpec((2, 1, kw), lambda b, t: (0, 0, 0)))
    args.append(lb.reshape(2, 1, kw))
    state_spec = pl.BlockSpec((None, 2, heads, dk, dv), lambda b, t: (b, 0, 0, 0, 0))
    if s0 is not None:
        in_specs.append(state_spec)
        args.append(s0)
    n_tok = batch * seq
    return pl.pallas_call(
        functools.partial(_hg_kernel, heads=heads, dk=dk, dv=dv, tt=tt, n_blocks=n, has_init=s0 is not None),
        grid=(batch, n),
        in_specs=in_specs,
        out_specs=[
            pl.BlockSpec((tt, vw), lambda b, t: (b * n + t, 0)),
            pl.BlockSpec((tt, vw), lambda b, t: (b * n + (n - 1 - t), 0)),
            state_spec,
        ],
        out_shape=[
            jax.ShapeDtypeStruct((n_tok, vw), F32),
            jax.ShapeDtypeStruct((n_tok, vw), F32),
            jax.ShapeDtypeStruct((batch, 2, heads, dk, dv), F32),
        ],
        scratch_shapes=[pltpu.VMEM((2, heads, dv, dk), F32)],
        compiler_params=_cparams("arbitrary", "arbitrary"),
        name="hgrn2_scan",
    )(*args)


def _mix_out_kernel(orf_ref, orb_ref, rg_ref, ogf_ref, ogb_ref, go_ref, gates_ref, x_ref, mod_ref, g2_ref,
                    wr_ref, wh_ref, wo_ref, wrt_ref, x1_ref, h2_ref, lgt_ref, *, d, ret_heads, hg_heads):
    o = orf_ref[...] + orb_ref[...]
    dvr = o.shape[1] // ret_heads
    parts = []
    for hh in range(ret_heads):
        oh = o[:, hh * dvr:(hh + 1) * dvr]
        mu = jnp.mean(oh, axis=-1, keepdims=True)
        ctr = oh - mu
        parts.append(ctr * lax.rsqrt(jnp.mean(ctr * ctr, axis=-1, keepdims=True) + NORM_EPS))
    ret_in = (jnp.concatenate(parts, axis=1) * _silu(rg_ref[...])).astype(BF16)
    ret_out = _dot(ret_in, wr_ref[...])

    og = ogf_ref[...] + ogb_ref[...]
    dvh = og.shape[1] // hg_heads
    parts = [_rms(og[:, hh * dvh:(hh + 1) * dvh]) for hh in range(hg_heads)]
    hg_in = (jnp.concatenate(parts, axis=1) * _silu(go_ref[...])).astype(BF16)
    hg_out = _dot(hg_in, wh_ref[...])

    merged = gates_ref[:, :d] * ret_out + gates_ref[:, d:] * hg_out
    mix = _dot(merged.astype(BF16), wo_ref[...])
    x1 = x_ref[...] + mod_ref[:, 2 * d:3 * d] * mix
    x1_ref[...] = x1
    h2 = _rms(x1) * g2_ref[...]
    h2 = h2 * (1.0 + mod_ref[:, 4 * d:5 * d]) + mod_ref[:, 3 * d:4 * d]
    h2_ref[...] = h2.astype(BF16)
    lgt_ref[...] = _dot_nt(wrt_ref[...], h2, precision=HIGHEST)


def _mix_out_call(o_ret_f, o_ret_b, og_f, og_b, proj, gates, x, mod_l, mod_row, g2, w_ret_o, w_hg_o, w_out,
                  w_router_t, tm, *, ret_heads, hg_heads, col_rg, col_go):
    n, d = x.shape
    vr, vh = o_ret_f.shape[1], og_f.shape[1]
    ne = w_router_t.shape[0]
    assert col_rg % vr == 0 and col_go % vh == 0

    def row(w):
        return pl.BlockSpec((tm, w), lambda i: (i, 0))

    def full(a):
        return pl.BlockSpec(a.shape, lambda i: (0,) * a.ndim)

    return pl.pallas_call(
        functools.partial(_mix_out_kernel, d=d, ret_heads=ret_heads, hg_heads=hg_heads),
        grid=(n // tm,),
        in_specs=[
            row(vr), row(vr), pl.BlockSpec((tm, vr), lambda i: (i, col_rg // vr)),
            row(vh), row(vh), pl.BlockSpec((tm, vh), lambda i: (i, col_go // vh)),
            row(2 * d), row(d),
            pl.BlockSpec((None, 1, 6 * d), lambda i: (mod_row(i), 0, 0)),
            pl.BlockSpec((1, d), lambda i: (0, 0)),
            full(w_ret_o), full(w_hg_o), full(w_out), full(w_router_t),
        ],
        out_specs=[row(d), row(d), pl.BlockSpec((ne, tm), lambda i: (0, i))],
        out_shape=[
            jax.ShapeDtypeStruct((n, d), F32),
            jax.ShapeDtypeStruct((n, d), BF16),
            jax.ShapeDtypeStruct((ne, n), F32),
        ],
        compiler_params=_cparams("arbitrary"),
        name="mixer_out",
    )(o_ret_f, o_ret_b, proj, og_f, og_b, proj, gates, x, mod_l, g2.reshape(1, d), w_ret_o, w_hg_o, w_out,
      w_router_t)


def _route_kernel(lg_ref, aff_ref, sel_ref, posx_ref, flag_ref, *, cap, n):
    lg = lg_ref[...]
    ne = lg.shape[0]
    ex = jnp.exp(lg - jnp.max(lg, axis=0, keepdims=True))
    aff = ex / jnp.sum(ex, axis=0, keepdims=True)
    aff_ref[...] = aff
    bits = pltpu.bitcast(aff, I32)

    def search(i, lo):
        cand = lo | lax.shift_left(jnp.int32(1), 30 - i)
        cnt = jnp.sum(jnp.where(bits >= cand, 1.0, 0.0), axis=1, keepdims=True)
        return jnp.where(cnt >= cap, cand, lo)

    thr = lax.fori_loop(0, 31, search, jnp.zeros((ne, 1), I32))
    gt = bits > thr
    eq = bits == thr
    need = cap - jnp.sum(jnp.where(gt, 1.0, 0.0), axis=1, keepdims=True)
    upper = jnp.where(lax.broadcasted_iota(I32, (LANE, LANE), 0) < lax.broadcasted_iota(I32, (LANE, LANE), 1),
                      1.0, 0.0).astype(BF16)

    def exclusive_count(store):
        def body(ci, carry):
            sl = pl.ds(pl.multiple_of(ci * LANE, LANE), LANE)
            x = flag_ref[:, sl]
            store(sl, carry + _dot(x.astype(BF16), upper))
            return carry + jnp.sum(x, axis=1, keepdims=True)

        lax.fori_loop(0, n // LANE, body, jnp.zeros((ne, 1), F32))

    flag_ref[...] = jnp.where(eq, 1.0, 0.0)

    def store_tie_rank(sl, rank):
        posx_ref[:, sl] = rank.astype(I32)

    exclusive_count(store_tie_rank)
    sel = gt | (eq & (posx_ref[...].astype(F32) < need))
    sel_ref[...] = jnp.where(sel, 1, 0).astype(I32)
    flag_ref[...] = jnp.where(sel, 1.0, 0.0)

    def store_pos(sl, cnt):
        posx_ref[:, sl] = cnt.astype(I32)

    exclusive_count(store_pos)


def _route_call(logits_t, cap):
    ne, n = logits_t.shape
    spec = pl.BlockSpec((ne, n), lambda: (0, 0))
    return pl.pallas_call(
        functools.partial(_route_kernel, cap=cap, n=n),
        in_specs=[spec],
        out_specs=[spec, spec, spec],
        out_shape=[
            jax.ShapeDtypeStruct((ne, n), F32),
            jax.ShapeDtypeStruct((ne, n), I32),
            jax.ShapeDtypeStruct((ne, n), I32),
        ],
        scratch_shapes=[pltpu.VMEM((ne, n), F32)],
        compiler_params=pltpu.CompilerParams(vmem_limit_bytes=VMEM_LIMIT),
        name="expert_choice_route",
    )(logits_t)


def _moe_kernel(base_ref, rows_ref, lst_ref, h_ref, sel_ref, posx_ref, aff_ref, wg_ref, wu_ref, wd_ref,
                out_ref, xe_ref, yacc_ref, *, n_sub, sub, win, n_ff):
    tb, e, f = pl.program_id(0), pl.program_id(1), pl.program_id(2)
    ne = pl.num_programs(1)
    slot = tb * ne + e
    base = base_ref[slot]
    n_rows = rows_ref[slot]

    @pl.when((tb == 0) & (e == 0) & (f == 0))
    def _():
        yacc_ref[...] = jnp.zeros_like(yacc_ref)

    @pl.when((e == 0) & (f == 0))
    def _():
        out_ref[...] = jnp.zeros_like(out_ref)

    def windows(j, fn):
        lo = lst_ref[slot * (n_sub + 1) + j]
        hi = lst_ref[slot * (n_sub + 1) + j + 1]
        a0 = (lo // BF16_SUBLANE) * BF16_SUBLANE
        tok = pl.ds(j * sub, sub)
        picked = sel_ref[pl.ds(e, 1), tok] == 1
        local = posx_ref[pl.ds(e, 1), tok] - base
        slot_iota = lax.broadcasted_iota(I32, (win, sub), 0)

        def body(w, carry):
            a = pl.multiple_of(a0 + w * win, BF16_SUBLANE)
            fn(a, picked & ((local - a) == slot_iota), tok)
            return carry

        lax.fori_loop(0, (hi - a0 + win - 1) // win, body, 0)

    @pl.when(f == 0)
    def _():
        xe_ref[...] = jnp.zeros_like(xe_ref)

        def put(a, hit, tok):
            onehot = jnp.where(hit, 1.0, 0.0).astype(BF16)
            rows = pl.ds(a, win)
            xe_ref[rows, :] += _dot(onehot, h_ref[tok, :]).astype(BF16)

        for j in range(n_sub):
            windows(j, put)

    def ffn_tile(t, carry):
        rows = pl.ds(pl.multiple_of(t * win, win), win)
        x = xe_ref[rows, :]
        hid = (_silu(_dot(x, wg_ref[...])) * _dot(x, wu_ref[...])).astype(BF16)
        y = _dot(hid, wd_ref[...])

        @pl.when(f == 0)
        def _():
            yacc_ref[rows, :] = y

        @pl.when(f != 0)
        def _():
            yacc_ref[rows, :] += y

        return carry

    lax.fori_loop(0, (n_rows + win - 1) // win, ffn_tile, 0)

    @pl.when(f == n_ff - 1)
    def _():
        def take(a, hit, tok):
            weights = jnp.where(hit, aff_ref[pl.ds(e, 1), tok], 0.0)
            y = yacc_ref[pl.ds(a, win), :].astype(BF16)
            out_ref[tok, :] += _dot(weights.T.astype(BF16), y)

        for j in range(n_sub):
            windows(j, take)


def _moe_call(h2, sel, posx, aff, base, rows, lst, wg, wu, wd, layer, *, tb, sub, fc):
    n, d = h2.shape
    ne, ff = wg.shape[1], wg.shape[3]
    n_sub = tb // sub
    n_ff = ff // fc
    win = LANE
    r_max = tb + win
    grid_spec = pltpu.PrefetchScalarGridSpec(
        num_scalar_prefetch=3,
        grid=(n // tb, ne, n_ff),
        in_specs=[
            pl.BlockSpec((tb, d), lambda t, e, f, *_: (t, 0)),
            pl.BlockSpec((ne, tb), lambda t, e, f, *_: (0, t)),
            pl.BlockSpec((ne, tb), lambda t, e, f, *_: (0, t)),
            pl.BlockSpec((ne, tb), lambda t, e, f, *_: (0, t)),
            pl.BlockSpec((None, None, d, fc), lambda t, e, f, *_: (layer, e, 0, f)),
            pl.BlockSpec((None, None, d, fc), lambda t, e, f, *_: (layer, e, 0, f)),
            pl.BlockSpec((None, None, fc, d), lambda t, e, f, *_: (layer, e, f, 0)),
        ],
        out_specs=pl.BlockSpec((tb, d), lambda t, e, f, *_: (t, 0)),
        scratch_shapes=[pltpu.VMEM((r_max, d), BF16), pltpu.VMEM((r_max, d), F32)],
    )
    return pl.pallas_call(
        functools.partial(_moe_kernel, n_sub=n_sub, sub=sub, win=win, n_ff=n_ff),
        grid_spec=grid_spec,
        out_shape=jax.ShapeDtypeStruct((n, d), F32),
        compiler_params=_cparams("arbitrary", "arbitrary", "arbitrary"),
        name="expert_ffn",
    )(base, rows, lst, h2, sel, posx, aff, wg, wu, wd)


def _rope_tables(n_tokens, dk):
    rows = n_tokens // GRID_W
    r, c = jnp.meshgrid(jnp.arange(rows), jnp.arange(GRID_W), indexing="ij")
    pos = jnp.stack([r.reshape(-1), c.reshape(-1)], axis=-1).astype(F32)
    nf = dk // 4
    inv_freq = ROPE_BASE ** (-jnp.arange(nf, dtype=F32) / nf)
    ang = pos[:, :, None] * inv_freq
    cos, sin = jnp.cos(ang), jnp.sin(ang)
    cos_t = jnp.concatenate([cos[:, 0], cos[:, 0], cos[:, 1], cos[:, 1]], axis=-1)
    sin_t = jnp.concatenate([-sin[:, 0], sin[:, 0], -sin[:, 1], sin[:, 1]], axis=-1)
    return cos_t, sin_t


def kernel(x_prompt, x_sample, state_ret, state_hgrn, c, c_ctx, ada_w, ada_b, norm_mix_g, norm_ffn_g, w_in,
           ret_gamma_logit, hg_lb_logit, w_ret_o, w_hg_o, w_merge, w_out, w_router, w_exp_gate, w_exp_up,
           w_exp_down, final_g):
    b_ctx, t_ctx, d = x_prompt.shape
    b_lat, t_lat, _ = x_sample.shape
    depth = w_in.shape[0]
    ret_heads, ret_dk, ret_dv = state_ret.shape[3:]
    hg_heads, hg_dk, hg_dv = state_hgrn.shape[3:]
    ne = w_router.shape[-1]
    n_ctx, n_lat = b_ctx * t_ctx, b_lat * t_lat
    n_tok = n_ctx + n_lat
    qw, vw = ret_heads * ret_dk, ret_heads * ret_dv
    kw, hw = hg_heads * hg_dk, hg_heads * hg_dv
    col_rg = 2 * qw + vw
    col_gq = col_rg + vw
    col_gf = col_gq + kw
    col_gi = col_gf + 2 * kw
    col_go = col_gi + hw
    assert col_go + hw == w_in.shape[-1] and b_lat < MOD_ROWS

    tm = 256
    assert t_ctx % tm == 0 and t_lat % tm == 0
    tm_proj = 512 if (n_ctx % 512 == 0 and n_lat % 512 == 0) else tm
    tb = 2048 if (n_ctx % 2048 == 0 and n_lat % 2048 == 0) else 512
    sub = 256
    assert n_ctx % tb == 0 and n_lat % tb == 0

    def mod_row(i):
        return jnp.where(i < n_ctx // tm, 0, 1 + (i - n_ctx // tm) // (t_lat // tm))

    x = jnp.concatenate([x_prompt.reshape(n_ctx, d), x_sample.reshape(n_lat, d)], axis=0)
    cvec = jnp.zeros((MOD_ROWS, d), F32).at[0].set(c_ctx).at[1:1 + b_lat].set(c)
    mod = _mod_call(cvec, ada_w, ada_b).reshape(depth, MOD_ROWS, 1, 6 * d)

    log_gamma = jax.nn.log_sigmoid(ret_gamma_logit.astype(F32))
    p_lb = jax.nn.softmax(hg_lb_logit.astype(F32), axis=0)
    hg_lb = jnp.clip(jnp.cumsum(p_lb, axis=0) - p_lb[0:1], 0.0, 1.0 - 1e-6)
    rope = _rope_tables(t_lat, ret_dk)

    w_ret_o_b, w_hg_o_b, w_out_b = w_ret_o.astype(BF16), w_hg_o.astype(BF16), w_out.astype(BF16)
    wg_b, wu_b, wd_b = w_exp_gate.astype(BF16), w_exp_up.astype(BF16), w_exp_down.astype(BF16)
    w_router_t = jnp.swapaxes(w_router, 1, 2)
    fc = min(1024, w_exp_gate.shape[-1])
    tn = min(1024, w_merge.shape[-1])

    _, h = _norm_call(x, None, None, mod[0], norm_mix_g[0], mod_row, tm)
    new_ret, new_hg = [], []
    for l in range(depth):
        proj = _proj_call(h, w_in, l, tm_proj, tn)
        gates = _proj_call(h, w_merge, l, tm_proj, tn, act="sigmoid")

        ret_kw = dict(heads=ret_heads, dk=ret_dk, dv=ret_dv)
        orf_c, orb_c, s_ret = _ret_call(proj, log_gamma[l], None, None, tok0=0, batch=b_ctx, seq=t_ctx, **ret_kw)
        orf_l, orb_l, _ = _ret_call(proj, log_gamma[l], state_ret[:, l], rope, tok0=n_ctx, batch=b_lat, seq=t_lat,
                                    **ret_kw)
        hg_kw = dict(heads=hg_heads, dk=hg_dk, dv=hg_dv, col_q=col_gq, col_f=col_gf, col_i=col_gi)
        ogf_c, ogb_c, s_hg = _hg_call(proj, hg_lb[l], None, tok0=0, batch=b_ctx, seq=t_ctx, **hg_kw)
        ogf_l, ogb_l, _ = _hg_call(proj, hg_lb[l], state_hgrn[:, l], tok0=n_ctx, batch=b_lat, seq=t_lat, **hg_kw)
        new_ret.append(s_ret)
        new_hg.append(s_hg)

        cat = lambda a, b: jnp.concatenate([a, b], axis=0)
        x, h2, logits_t = _mix_out_call(
            cat(orf_c, orf_l), cat(orb_c, orb_l), cat(ogf_c, ogf_l), cat(ogb_c, ogb_l), proj, gates, x, mod[l],
            mod_row, norm_ffn_g[l], w_ret_o_b[l], w_hg_o_b[l], w_out_b[l], w_router_t[l], tm,
            ret_heads=ret_heads, hg_heads=hg_heads, col_rg=col_rg, col_go=col_go)

        routed = []
        for lo, n_set in ((0, n_ctx), (n_ctx, n_lat)):
            cap = CAPACITY_FACTOR * n_set // ne
            aff, sel, posx = _route_call(logits_t[:, lo:lo + n_set], cap)
            edges = jnp.concatenate([posx[:, ::sub], jnp.full((ne, 1), cap, I32)], axis=1)
            routed.append((aff, sel, posx, edges))
        aff, sel, posx = (jnp.concatenate([r[i] for r in routed], axis=1) for i in range(3))
        n_sub = tb // sub
        base, rows, lst = [], [], []
        for _, _, _, edges in routed:
            n_blk = (edges.shape[1] - 1) // n_sub
            starts = edges[:, 0:n_blk * n_sub:n_sub]
            base.append(starts.T)
            rows.append((edges[:, n_sub::n_sub] - starts).T)
            idx = jnp.arange(n_blk)[:, None] * n_sub + jnp.arange(n_sub + 1)[None, :]
            lst.append(jnp.transpose(edges[:, idx] - starts[:, :, None], (1, 0, 2)))
        base = jnp.concatenate(base, axis=0).reshape(-1)
        rows = jnp.concatenate(rows, axis=0).reshape(-1)
        lst = jnp.concatenate(lst, axis=0).reshape(-1)
        ffn = _moe_call(h2, sel, posx, aff, base, rows, lst, wg_b, wu_b, wd_b, l, tb=tb, sub=sub, fc=fc)

        if l + 1 < depth:
            x, h = _norm_call(x, ffn, mod[l], mod[l + 1], norm_mix_g[l + 1], mod_row, tm)
        else:
            _, y = _norm_call(x, ffn, mod[l], None, final_g, mod_row, tm)

    y_prompt = y[:n_ctx].reshape(b_ctx, t_ctx, d)
    y_sample = y[n_ctx:].reshape(b_lat, t_lat, d)
    return y_prompt, y_sample, jnp.stack(new_ret, axis=1), jnp.stack(new_hg, axis=1)
```

```python
import functools

import jax
import jax.numpy as jnp
from jax import lax
from jax.experimental import pallas as pl
from jax.experimental.pallas import tpu as pltpu

F32 = jnp.float32
BF16 = jnp.bfloat16
I32 = jnp.int32
HIGHEST = lax.Precision.HIGHEST

NORM_EPS = 1e-6
LOG2_E = 1.4426950408889634
ROPE_BASE = 10000.0
GRID_W = 64
CAPACITY_FACTOR = 2
RET_CHUNK = 128
HG_LEVELS = 6
HG_CHUNK = 1 << HG_LEVELS
MOD_ROWS = 8
LANE = 128
BF16_SUBLANE = 16
VMEM_LIMIT = 62 * 1024 * 1024


def _cparams(*sem):
    return pltpu.CompilerParams(dimension_semantics=sem, vmem_limit_bytes=VMEM_LIMIT)


def _dot(a, b, **kw):
    return jnp.dot(a, b, preferred_element_type=F32, **kw)


def _dot_nt(a, b, **kw):
    return lax.dot_general(a, b, (((1,), (1,)), ((), ())), preferred_element_type=F32, **kw)


def _dot_tn(a, b, **kw):
    return lax.dot_general(a, b, (((0,), (0,)), ((), ())), preferred_element_type=F32, **kw)


def _silu(x):
    return x * jax.nn.sigmoid(x)


def _mod_kernel(c_ref, w_ref, b_ref, o_ref):
    s = _silu(c_ref[...])
    o_ref[...] = _dot(s, w_ref[...], precision=HIGHEST) + b_ref[...]


def _mod_call(cvec, ada_w, ada_b):
    depth, d, six_d = ada_w.shape
    tn = 6 * LANE * 2
    assert six_d % tn == 0
    return pl.pallas_call(
        _mod_kernel,
        grid=(depth, six_d // tn),
        in_specs=[
            pl.BlockSpec((MOD_ROWS, d), lambda l, j: (0, 0)),
            pl.BlockSpec((None, d, tn), lambda l, j: (l, 0, j)),
            pl.BlockSpec((None, 1, tn), lambda l, j: (l, 0, j)),
        ],
        out_specs=pl.BlockSpec((None, MOD_ROWS, tn), lambda l, j: (l, 0, j)),
        out_shape=jax.ShapeDtypeStruct((depth, MOD_ROWS, six_d), F32),
        compiler_params=_cparams("arbitrary", "arbitrary"),
        name="adaln_mod",
    )(cvec, ada_w, ada_b.reshape(depth, 1, six_d))


def _rms(x):
    return x * lax.rsqrt(jnp.mean(x * x, axis=-1, keepdims=True) + NORM_EPS)


def _norm_kernel(*refs, d, residual, final):
    it = iter(refs)
    x_ref = next(it)
    f_ref, gate_ref = (next(it), next(it)) if residual else (None, None)
    mod_ref = None if final else next(it)
    g_ref = next(it)
    xo_ref = next(it) if residual else None
    h_ref = next(it)
    x = x_ref[...]
    if residual:
        x = x + gate_ref[...] * f_ref[...]
        xo_ref[...] = x
    y = _rms(x) * g_ref[...]
    if not final:
        y = y * (1.0 + mod_ref[:, d:2 * d]) + mod_ref[:, 0:d]
    h_ref[...] = y.astype(h_ref.dtype)


def _norm_call(x, ffn, mod_gate, mod_next, g, mod_row, tm):
    n, d = x.shape
    residual = ffn is not None
    final = mod_next is None
    row = pl.BlockSpec((tm, d), lambda i: (i, 0))
    in_specs, args = [row], [x]
    if residual:
        in_specs += [row, pl.BlockSpec((None, 1, d), lambda i: (mod_row(i), 0, 5))]
        args += [ffn, mod_gate]
    if not final:
        in_specs.append(pl.BlockSpec((None, 1, 2 * d), lambda i: (mod_row(i), 0, 0)))
        args.append(mod_next)
    in_specs.append(pl.BlockSpec((1, d), lambda i: (0, 0)))
    args.append(g.reshape(1, d))
    out_specs, out_shape = [row], [jax.ShapeDtypeStruct((n, d), F32 if final else BF16)]
    if residual:
        out_specs = [row, row]
        out_shape = [jax.ShapeDtypeStruct((n, d), F32)] + out_shape
    outs = pl.pallas_call(
        functools.partial(_norm_kernel, d=d, residual=residual, final=final),
        grid=(n // tm,),
        in_specs=in_specs,
        out_specs=out_specs,
        out_shape=out_shape,
        compiler_params=_cparams("arbitrary"),
        name="residual_norm",
    )(*args)
    return outs if residual else (x, outs[0])


def _proj_kernel(h_ref, w_ref, o_ref, wb_ref, *, act):
    @pl.when(pl.program_id(1) == 0)
    def _():
        wb_ref[...] = w_ref[...].astype(BF16)

    acc = _dot(h_ref[...], wb_ref[...])
    if act == "sigmoid":
        acc = jax.nn.sigmoid(acc)
    o_ref[...] = acc


def _proj_call(h, w, layer, tm, tn, act=None):
    n, d = h.shape
    cols = w.shape[-1]
    return pl.pallas_call(
        functools.partial(_proj_kernel, act=act),
        grid=(cols // tn, n // tm),
        in_specs=[
            pl.BlockSpec((tm, d), lambda j, i: (i, 0)),
            pl.BlockSpec((None, d, tn), lambda j, i: (layer, 0, j)),
        ],
        out_specs=pl.BlockSpec((tm, tn), lambda j, i: (i, j)),
        out_shape=jax.ShapeDtypeStruct((n, cols), F32),
        scratch_shapes=[pltpu.VMEM((d, tn), BF16)],
        compiler_params=_cparams("arbitrary", "arbitrary"),
        name="in_proj",
    )(h, w)


def _swap_half_pairs(x):
    lane = lax.broadcasted_iota(I32, x.shape, 1)
    return jnp.where((lane & 32) == 0, pltpu.roll(x, LANE - 32, 1), pltpu.roll(x, 32, 1))


def _ret_kernel(*refs, heads, dk, dv, chunk, n_chunks, has_init, has_rope):
    it = iter(refs)
    lg_ref = next(it)
    qkv = [[next(it) for _ in range(3)] for _ in range(2)]
    s0_ref = next(it) if has_init else None
    rope = [[next(it) for _ in range(2)] for _ in range(2)] if has_rope else None
    o_refs = [next(it), next(it)]
    sfin_ref = next(it)
    s_ref = next(it)

    c = pl.program_id(1)

    @pl.when(c == 0)
    def _():
        if has_init:
            s_ref[...] = s0_ref[...]
        else:
            s_ref[...] = jnp.zeros_like(s_ref)

    L = chunk
    ii = lax.broadcasted_iota(I32, (L, L), 0)
    jj = lax.broadcasted_iota(I32, (L, L), 1)
    pcol = lax.broadcasted_iota(I32, (L, 1), 0).astype(F32)
    for d in range(2):
        q_ref, k_ref, v_ref = qkv[d]
        if d == 0:
            dif = (ii - jj).astype(F32)
            q_pow, k_pow = pcol + 1.0, (L - 1.0) - pcol
        else:
            dif = (jj - ii).astype(F32)
            q_pow, k_pow = L - pcol, pcol
        for hh in range(heads):
            lg = lg_ref[d, hh]
            q = q_ref[:, hh * dk:(hh + 1) * dk]
            k = k_ref[:, hh * dk:(hh + 1) * dk] * (dk ** -0.5)
            v = v_ref[:, hh * dv:(hh + 1) * dv].astype(BF16)
            if has_rope:
                cos, sin = rope[d][0][...], rope[d][1][...]
                q = q * cos + _swap_half_pairs(q) * sin
                k = k * cos + _swap_half_pairs(k) * sin
            decay = jnp.where(dif >= 0.0, jnp.exp(lg * jnp.maximum(dif, 0.0)), 0.0)
            scores = _dot_nt(q.astype(BF16), k.astype(BF16)) * decay
            s_old = s_ref[d, hh]
            o = _dot(scores.astype(BF16), v) + _dot((q * jnp.exp(lg * q_pow)).astype(BF16), s_old.astype(BF16))
            kv = _dot((k * jnp.exp(lg * k_pow)).T.astype(BF16), v)
            s_ref[d, hh] = s_old * jnp.exp(jnp.full((1, 1), lg * L, F32)) + kv
            o_refs[d][:, hh * dv:(hh + 1) * dv] = o

    @pl.when(c == n_chunks - 1)
    def _():
        sfin_ref[...] = s_ref[...]


def _ret_call(proj, log_gamma, s0, rope, *, tok0, batch, seq, heads, dk, dv):
    L = RET_CHUNK
    n = seq // L
    qw, vw = heads * dk, heads * dv
    assert seq % L == 0 and tok0 % L == 0 and vw % qw == 0
    r0 = tok0 // L

    def fwd(b, c):
        return r0 + b * n + c

    def bwd(b, c):
        return r0 + b * n + (n - 1 - c)

    in_specs = [pl.BlockSpec(memory_space=pltpu.SMEM)]
    args = [log_gamma]
    for rmap in (fwd, bwd):
        in_specs += [
            pl.BlockSpec((L, qw), lambda b, c, rmap=rmap: (rmap(b, c), 0)),
            pl.BlockSpec((L, qw), lambda b, c, rmap=rmap: (rmap(b, c), 1)),
            pl.BlockSpec((L, vw), lambda b, c, rmap=rmap: (rmap(b, c), (2 * qw) // vw)),
        ]
        args += [proj, proj, proj]
    state_spec = pl.BlockSpec((None, 2, heads, dk, dv), lambda b, c: (b, 0, 0, 0, 0))
    if s0 is not None:
        in_specs.append(state_spec)
        args.append(s0)
    if rope is not None:
        for cmap in (lambda b, c: (c, 0), lambda b, c: (n - 1 - c, 0)):
            in_specs += [pl.BlockSpec((L, dk), cmap), pl.BlockSpec((L, dk), cmap)]
            args += [rope[0], rope[1]]
    n_tok = batch * seq
    return pl.pallas_call(
        functools.partial(_ret_kernel, heads=heads, dk=dk, dv=dv, chunk=L, n_chunks=n,
                          has_init=s0 is not None, has_rope=rope is not None),
        grid=(batch, n),
        in_specs=in_specs,
        out_specs=[
            pl.BlockSpec((L, vw), lambda b, c: (b * n + c, 0)),
            pl.BlockSpec((L, vw), lambda b, c: (b * n + (n - 1 - c), 0)),
            state_spec,
        ],
        out_shape=[
            jax.ShapeDtypeStruct((n_tok, vw), F32),
            jax.ShapeDtypeStruct((n_tok, vw), F32),
            jax.ShapeDtypeStruct((batch, 2, heads, dk, dv), F32),
        ],
        scratch_shapes=[pltpu.VMEM((2, heads, dk, dv), F32)],
        compiler_params=_cparams("arbitrary", "arbitrary"),
        name="retention_scan",
    )(*args)


def _hg_scan_index(shape, axis, rev):
    i = lax.broadcasted_iota(I32, shape, axis) & (HG_CHUNK - 1)
    return (HG_CHUNK - 1 - i) if rev else i


def _hg_window_matrix(rev):
    C = HG_CHUNK
    ti = _hg_scan_index((C, 2 * C), 0, rev)
    tj = _hg_scan_index((C, 2 * C), 1, rev)
    blocks = [tj <= ti]
    for l in range(2, HG_LEVELS + 1):
        anchor = ((ti >> l) << l) + (1 << (l - 1)) - 1
        upper = ((ti >> (l - 1)) & 1) == 1
        blocks.append((upper & (tj > anchor) & (tj <= ti)) | (~upper & (tj > ti) & (tj <= anchor)))
    blocks.append(tj > ti)
    return jnp.concatenate([jnp.where(m, 1.0, 0.0) for m in blocks], axis=0).astype(BF16)


def _hg_kernel(*refs, heads, dk, dv, tt, n_blocks, has_init):
    it = iter(refs)
    qzv = [[next(it) for _ in range(3)] for _ in range(2)]
    lb_ref = next(it)
    s0_ref = next(it) if has_init else None
    o_refs = [next(it), next(it)]
    sfin_ref = next(it)
    st_ref = next(it)
    win_ref = next(it)

    t = pl.program_id(1)

    @pl.when((pl.program_id(0) == 0) & (t == 0))
    def _():
        for d in range(2):
            win_ref[d] = _hg_window_matrix(d == 1)

    @pl.when(t == 0)
    def _():
        for d in range(2):
            for hh in range(heads):
                st_ref[d, hh] = s0_ref[d, hh].T if has_init else jnp.zeros((dv, dk), F32)

    C = HG_CHUNK
    n_ch = tt // C
    chunks = [slice(c * C, (c + 1) * C) for c in range(n_ch)]
    levels, uppers = [], []
    for d in range(2):
        ti = _hg_scan_index((C, C), 0, d == 1)
        tj = _hg_scan_index((C, C), 1, d == 1)
        level = jnp.where(tj < ti, 1, 0)
        for l in range(1, HG_LEVELS):
            level = level + jnp.where((tj < ti) & ((ti >> l) != (tj >> l)), 1, 0)
        t_col = _hg_scan_index((C, 1), 0, d == 1)
        levels.append(level)
        uppers.append([None] + [((t_col >> (l - 1)) & 1) == 1 for l in range(1, HG_LEVELS + 1)])

    def head_body(hh, carry):
        ck = pl.ds(pl.multiple_of(hh * dk, LANE), dk)
        cv = pl.ds(pl.multiple_of(hh * dv, LANE), dv)
        jobs = [(d, r) for d in range(2) for r in chunks]
        qg, k, f, hi, lo, v, vb, diag = [], [], [], [], [], [], [], []
        for d in range(2):
            q_ref, z_ref, v_ref = qzv[d]
            lb = lb_ref[d, :, ck]
            z = z_ref[:, ck]
            e = jnp.exp(-jnp.abs(z))
            den = 1.0 + e
            num = jnp.where(z >= 0.0, 1.0 + lb * e, e + lb)
            log_num = jnp.where((z < 0.0) & (lb <= 0.0), z, jnp.log(num))
            log_f = log_num - jnp.log(den)
            k.append((1.0 - lb) * (jnp.where(z > 0.0, e, 1.0) / den))
            f.append(1.0 - k[d])
            log2_f = log_f * LOG2_E
            hi.append(log2_f.astype(BF16))
            lo.append((log2_f - hi[d].astype(F32)).astype(BF16))
            qg.append(_silu(q_ref[:, ck]) * (dk ** -0.5))
            v.append(v_ref[:, cv])
            vb.append(v[d].astype(BF16))
            diag.append(jnp.sum(qg[d] * k[d], axis=-1, keepdims=True))
        wins = [win_ref[0], win_ref[1]]
        sums = [_dot(wins[d], jnp.concatenate([hi[d][r], lo[d][r]], axis=0)) for d, r in jobs]
        pairs = []
        for (d, r), s in zip(jobs, sums):
            ps = [jnp.where(uppers[d][1], qg[d][r] * f[d][r], k[d][r]).astype(BF16)]
            for l in range(2, HG_LEVELS + 1):
                scale = jnp.exp2(s[(l - 1) * C:l * C])
                ps.append((jnp.where(uppers[d][l], qg[d][r], k[d][r]) * scale).astype(BF16))
            pairs.append(ps)
        grams = [[_dot_nt(p, p) for p in ps] for ps in pairs]
        atts = []
        for (d, r), gs in zip(jobs, grams):
            att = jnp.where(levels[d] == 1, gs[0], 0.0)
            for l in range(2, HG_LEVELS + 1):
                att = jnp.where(levels[d] == l, gs[l - 1], att)
            atts.append(att.astype(BF16))
        k_out = [(k[d][r] * jnp.exp2(s[HG_LEVELS * C:])).astype(BF16) for (d, r), s in zip(jobs, sums)]
        q_in = [(qg[d][r] * jnp.exp2(s[:C])).astype(BF16) for (d, r), s in zip(jobs, sums)]
        decays = [jnp.exp2(s[0:1] if d == 1 else s[C - 1:C]) for (d, r), s in zip(jobs, sums)]
        outs = [_dot(att, vb[d][r]) + diag[d][r] * v[d][r] for att, (d, r) in zip(atts, jobs)]
        kvs = [_dot_tn(vb[d][r], ko) for (d, r), ko in zip(jobs, k_out)]
        for d in range(2):
            st = st_ref[d, hh]
            for c in (range(n_ch - 1, -1, -1) if d == 1 else range(n_ch)):
                j = d * n_ch + c
                o_refs[d][chunks[c], cv] = outs[j] + _dot_nt(q_in[j], st.astype(BF16))
                st = st * decays[j] + kvs[j]
            st_ref[d, hh] = st
        return carry

    lax.fori_loop(0, heads, head_body, 0)

    @pl.when(t == n_blocks - 1)
    def _():
        for d in range(2):
            for hh in range(heads):
                sfin_ref[d, hh] = st_ref[d, hh].T


def _hg_call(proj, lb, s0, *, tok0, batch, seq, heads, dk, dv, col_q, col_f, col_i):
    tt = min(seq, 256)
    n = seq // tt
    kw, vw = heads * dk, heads * dv
    assert seq % tt == 0 and tok0 % tt == 0 and tt % HG_CHUNK == 0
    assert col_q % kw == 0 and col_f % kw == 0 and col_i % vw == 0
    r0 = tok0 // tt

    def fwd(b, t):
        return r0 + b * n + t

    def bwd(b, t):
        return r0 + b * n + (n - 1 - t)

    in_specs, args = [], []
    for d, rmap in enumerate((fwd, bwd)):
        in_specs += [
            pl.BlockSpec((tt, kw), lambda b, t, rmap=rmap: (rmap(b, t), col_q // kw)),
            pl.BlockSpec((tt, kw), lambda b, t, rmap=rmap, d=d: (rmap(b, t), col_f // kw + d)),
            pl.BlockSpec((tt, vw), lambda b, t, rmap=rmap: (rmap(b, t), col_i // vw)),
        ]
        args += [proj, proj, proj]
    in_specs.append(pl.BlockSpec((2, 1, kw), lambda b, t: (0, 0, 0)))
    args.append(lb.reshape(2, 1, kw))
    state_spec = pl.BlockSpec((None, 2, heads, dk, dv), lambda b, t: (b, 0, 0, 0, 0))
    if s0 is not None:
        in_specs.append(state_spec)
        args.append(s0)
    n_tok = batch * seq
    return pl.pallas_call(
        functools.partial(_hg_kernel, heads=heads, dk=dk, dv=dv, tt=tt, n_blocks=n, has_init=s0 is not None),
        grid=(batch, n),
        in_specs=in_specs,
        out_specs=[
            pl.BlockSpec((tt, vw), lambda b, t: (b * n + t, 0)),
            pl.BlockSpec((tt, vw), lambda b, t: (b * n + (n - 1 - t), 0)),
            state_spec,
        ],
        out_shape=[
            jax.ShapeDtypeStruct((n_tok, vw), F32),
            jax.ShapeDtypeStruct((n_tok, vw), F32),
            jax.ShapeDtypeStruct((batch, 2, heads, dk, dv), F32),
        ],
        scratch_shapes=[pltpu.VMEM((2, heads, dv, dk), F32),
                        pltpu.VMEM((2, (HG_LEVELS + 1) * HG_CHUNK, 2 * HG_CHUNK), BF16)],
        compiler_params=_cparams("arbitrary", "arbitrary"),
        name="hgrn2_scan",
    )(*args)


def _mix_out_kernel(orf_ref, orb_ref, rg_ref, ogf_ref, ogb_ref, go_ref, gates_ref, x_ref, mod_ref, g2_ref,
                    wr_ref, wh_ref, wo_ref, wrt_ref, x1_ref, h2_ref, lgt_ref, *, d, ret_heads, hg_heads):
    o = orf_ref[...] + orb_ref[...]
    dvr = o.shape[1] // ret_heads
    parts = []
    for hh in range(ret_heads):
        oh = o[:, hh * dvr:(hh + 1) * dvr]
        mu = jnp.mean(oh, axis=-1, keepdims=True)
        ctr = oh - mu
        parts.append(ctr * lax.rsqrt(jnp.mean(ctr * ctr, axis=-1, keepdims=True) + NORM_EPS))
    ret_in = (jnp.concatenate(parts, axis=1) * _silu(rg_ref[...])).astype(BF16)
    ret_out = _dot(ret_in, wr_ref[...])

    og = ogf_ref[...] + ogb_ref[...]
    dvh = og.shape[1] // hg_heads
    parts = [_rms(og[:, hh * dvh:(hh + 1) * dvh]) for hh in range(hg_heads)]
    hg_in = (jnp.concatenate(parts, axis=1) * _silu(go_ref[...])).astype(BF16)
    hg_out = _dot(hg_in, wh_ref[...])

    merged = gates_ref[:, :d] * ret_out + gates_ref[:, d:] * hg_out
    mix = _dot(merged.astype(BF16), wo_ref[...])
    x1 = x_ref[...] + mod_ref[:, 2 * d:3 * d] * mix
    x1_ref[...] = x1
    h2 = _rms(x1) * g2_ref[...]
    h2 = h2 * (1.0 + mod_ref[:, 4 * d:5 * d]) + mod_ref[:, 3 * d:4 * d]
    h2_ref[...] = h2.astype(BF16)
    lgt_ref[...] = _dot_nt(wrt_ref[...], h2, precision=HIGHEST)


def _mix_out_call(o_ret_f, o_ret_b, og_f, og_b, proj, gates, x, mod_l, mod_row, g2, w_ret_o, w_hg_o, w_out,
                  w_router_t, tm, *, ret_heads, hg_heads, col_rg, col_go):
    n, d = x.shape
    vr, vh = o_ret_f.shape[1], og_f.shape[1]
    ne = w_router_t.shape[0]
    assert col_rg % vr == 0 and col_go % vh == 0

    def row(w):
        return pl.BlockSpec((tm, w), lambda i: (i, 0))

    def full(a):
        return pl.BlockSpec(a.shape, lambda i: (0,) * a.ndim)

    return pl.pallas_call(
        functools.partial(_mix_out_kernel, d=d, ret_heads=ret_heads, hg_heads=hg_heads),
        grid=(n // tm,),
        in_specs=[
            row(vr), row(vr), pl.BlockSpec((tm, vr), lambda i: (i, col_rg // vr)),
            row(vh), row(vh), pl.BlockSpec((tm, vh), lambda i: (i, col_go // vh)),
            row(2 * d), row(d),
            pl.BlockSpec((None, 1, 6 * d), lambda i: (mod_row(i), 0, 0)),
            pl.BlockSpec((1, d), lambda i: (0, 0)),
            full(w_ret_o), full(w_hg_o), full(w_out), full(w_router_t),
        ],
        out_specs=[row(d), row(d), pl.BlockSpec((ne, tm), lambda i: (0, i))],
        out_shape=[
            jax.ShapeDtypeStruct((n, d), F32),
            jax.ShapeDtypeStruct((n, d), BF16),
            jax.ShapeDtypeStruct((ne, n), F32),
        ],
        compiler_params=_cparams("arbitrary"),
        name="mixer_out",
    )(o_ret_f, o_ret_b, proj, og_f, og_b, proj, gates, x, mod_l, g2.reshape(1, d), w_ret_o, w_hg_o, w_out,
      w_router_t)


def _route_kernel(lg_ref, aff_ref, sel_ref, posx_ref, flag_ref, *, cap, n):
    lg = lg_ref[...]
    ne = lg.shape[0]
    ex = jnp.exp(lg - jnp.max(lg, axis=0, keepdims=True))
    aff = ex / jnp.sum(ex, axis=0, keepdims=True)
    aff_ref[...] = aff
    bits = pltpu.bitcast(aff, I32)

    def search(i, lo):
        cand = lo | lax.shift_left(jnp.int32(1), 30 - i)
        cnt = jnp.sum(jnp.where(bits >= cand, 1.0, 0.0), axis=1, keepdims=True)
        return jnp.where(cnt >= cap, cand, lo)

    thr = lax.fori_loop(0, 31, search, jnp.zeros((ne, 1), I32))
    gt = bits > thr
    eq = bits == thr
    need = cap - jnp.sum(jnp.where(gt, 1.0, 0.0), axis=1, keepdims=True)
    upper = jnp.where(lax.broadcasted_iota(I32, (LANE, LANE), 0) < lax.broadcasted_iota(I32, (LANE, LANE), 1),
                      1.0, 0.0).astype(BF16)

    def exclusive_count(store):
        def body(ci, carry):
            sl = pl.ds(pl.multiple_of(ci * LANE, LANE), LANE)
            x = flag_ref[:, sl]
            store(sl, carry + _dot(x.astype(BF16), upper))
            return carry + jnp.sum(x, axis=1, keepdims=True)

        lax.fori_loop(0, n // LANE, body, jnp.zeros((ne, 1), F32))

    flag_ref[...] = jnp.where(eq, 1.0, 0.0)

    def store_tie_rank(sl, rank):
        posx_ref[:, sl] = rank.astype(I32)

    exclusive_count(store_tie_rank)
    sel = gt | (eq & (posx_ref[...].astype(F32) < need))
    sel_ref[...] = jnp.where(sel, 1, 0).astype(I32)
    flag_ref[...] = jnp.where(sel, 1.0, 0.0)

    def store_pos(sl, cnt):
        posx_ref[:, sl] = cnt.astype(I32)

    exclusive_count(store_pos)


def _route_call(logits_t, cap):
    ne, n = logits_t.shape
    spec = pl.BlockSpec((ne, n), lambda: (0, 0))
    return pl.pallas_call(
        functools.partial(_route_kernel, cap=cap, n=n),
        in_specs=[spec],
        out_specs=[spec, spec, spec],
        out_shape=[
            jax.ShapeDtypeStruct((ne, n), F32),
            jax.ShapeDtypeStruct((ne, n), I32),
            jax.ShapeDtypeStruct((ne, n), I32),
        ],
        scratch_shapes=[pltpu.VMEM((ne, n), F32)],
        compiler_params=pltpu.CompilerParams(vmem_limit_bytes=VMEM_LIMIT),
        name="expert_choice_route",
    )(logits_t)


def _moe_kernel(base_ref, rows_ref, h_ref, sel_ref, posx_ref, aff_ref, wg_ref, wu_ref, wd_ref,
                out_ref, xe_ref, yacc_ref, *, rt, n_ff):
    tb, e, f = pl.program_id(0), pl.program_id(1), pl.program_id(2)
    slot = tb * pl.num_programs(1) + e
    base = base_ref[slot]
    n_rows = rows_ref[slot]

    @pl.when((e == 0) & (f == 0))
    def _():
        out_ref[...] = jnp.zeros_like(out_ref)

    n_tok = h_ref.shape[0]
    picked = sel_ref[pl.ds(e, 1), :] == 1
    local = posx_ref[pl.ds(e, 1), :] - base
    row_iota = lax.broadcasted_iota(I32, (rt, n_tok), 0)

    def row_tile(t, carry):
        rows = pl.ds(pl.multiple_of(t * rt, rt), rt)
        hit = picked & ((local - t * rt) == row_iota)

        @pl.when(f == 0)
        def _():
            xe_ref[rows, :] = _dot(jnp.where(hit, 1.0, 0.0).astype(BF16), h_ref[...]).astype(BF16)

        x = xe_ref[rows, :]
        hid = (_silu(_dot(x, wg_ref[...])) * _dot(x, wu_ref[...])).astype(BF16)
        y = _dot(hid, wd_ref[...])

        @pl.when(f == 0)
        def _():
            yacc_ref[rows, :] = y

        @pl.when(f != 0)
        def _():
            yacc_ref[rows, :] += y

        @pl.when(f == n_ff - 1)
        def _():
            weights = jnp.where(hit, aff_ref[pl.ds(e, 1), :], 0.0).astype(BF16)
            out_ref[...] += _dot_tn(weights, yacc_ref[rows, :].astype(BF16))

        return carry

    lax.fori_loop(0, (n_rows + rt - 1) // rt, row_tile, 0)


def _moe_call(h2, sel, posx, aff, base, rows, wg, wu, wd, layer, *, tb, fc):
    n, d = h2.shape
    ne, ff = wg.shape[1], wg.shape[3]
    n_ff = ff // fc
    rt = min(tb, -(-(5 * tb * CAPACITY_FACTOR) // (4 * ne * BF16_SUBLANE)) * BF16_SUBLANE)
    r_max = -(-tb // rt) * rt
    grid_spec = pltpu.PrefetchScalarGridSpec(
        num_scalar_prefetch=2,
        grid=(n // tb, ne, n_ff),
        in_specs=[
            pl.BlockSpec((tb, d), lambda t, e, f, *_: (t, 0)),
            pl.BlockSpec((ne, tb), lambda t, e, f, *_: (0, t)),
            pl.BlockSpec((ne, tb), lambda t, e, f, *_: (0, t)),
            pl.BlockSpec((ne, tb), lambda t, e, f, *_: (0, t)),
            pl.BlockSpec((None, None, d, fc), lambda t, e, f, *_: (layer, e, 0, f)),
            pl.BlockSpec((None, None, d, fc), lambda t, e, f, *_: (layer, e, 0, f)),
            pl.BlockSpec((None, None, fc, d), lambda t, e, f, *_: (layer, e, f, 0)),
        ],
        out_specs=pl.BlockSpec((tb, d), lambda t, e, f, *_: (t, 0)),
        scratch_shapes=[pltpu.VMEM((r_max, d), BF16), pltpu.VMEM((r_max, d), F32)],
    )
    return pl.pallas_call(
        functools.partial(_moe_kernel, rt=rt, n_ff=n_ff),
        grid_spec=grid_spec,
        out_shape=jax.ShapeDtypeStruct((n, d), F32),
        compiler_params=_cparams("arbitrary", "arbitrary", "arbitrary"),
        name="expert_ffn",
    )(base, rows, h2, sel, posx, aff, wg, wu, wd)


def _rope_tables(n_tokens, dk):
    rows = n_tokens // GRID_W
    r, c = jnp.meshgrid(jnp.arange(rows), jnp.arange(GRID_W), indexing="ij")
    pos = jnp.stack([r.reshape(-1), c.reshape(-1)], axis=-1).astype(F32)
    nf = dk // 4
    inv_freq = ROPE_BASE ** (-jnp.arange(nf, dtype=F32) / nf)
    ang = pos[:, :, None] * inv_freq
    cos, sin = jnp.cos(ang), jnp.sin(ang)
    cos_t = jnp.concatenate([cos[:, 0], cos[:, 0], cos[:, 1], cos[:, 1]], axis=-1)
    sin_t = jnp.concatenate([-sin[:, 0], sin[:, 0], -sin[:, 1], sin[:, 1]], axis=-1)
    return cos_t, sin_t


def kernel(x_prompt, x_sample, state_ret, state_hgrn, c, c_ctx, ada_w, ada_b, norm_mix_g, norm_ffn_g, w_in,
           ret_gamma_logit, hg_lb_logit, w_ret_o, w_hg_o, w_merge, w_out, w_router, w_exp_gate, w_exp_up,
           w_exp_down, final_g):
    b_ctx, t_ctx, d = x_prompt.shape
    b_lat, t_lat, _ = x_sample.shape
    depth = w_in.shape[0]
    ret_heads, ret_dk, ret_dv = state_ret.shape[3:]
    hg_heads, hg_dk, hg_dv = state_hgrn.shape[3:]
    ne = w_router.shape[-1]
    n_ctx, n_lat = b_ctx * t_ctx, b_lat * t_lat
    n_tok = n_ctx + n_lat
    qw, vw = ret_heads * ret_dk, ret_heads * ret_dv
    kw, hw = hg_heads * hg_dk, hg_heads * hg_dv
    col_rg = 2 * qw + vw
    col_gq = col_rg + vw
    col_gf = col_gq + kw
    col_gi = col_gf + 2 * kw
    col_go = col_gi + hw
    assert col_go + hw == w_in.shape[-1] and b_lat < MOD_ROWS

    tm = 256
    assert t_ctx % tm == 0 and t_lat % tm == 0
    tm_proj = 512 if (n_ctx % 512 == 0 and n_lat % 512 == 0) else tm
    tb = 2048 if (n_ctx % 2048 == 0 and n_lat % 2048 == 0) else 512
    assert n_ctx % tb == 0 and n_lat % tb == 0

    def mod_row(i):
        return jnp.where(i < n_ctx // tm, 0, 1 + (i - n_ctx // tm) // (t_lat // tm))

    x = jnp.concatenate([x_prompt.reshape(n_ctx, d), x_sample.reshape(n_lat, d)], axis=0)
    cvec = jnp.zeros((MOD_ROWS, d), F32).at[0].set(c_ctx).at[1:1 + b_lat].set(c)
    mod = _mod_call(cvec, ada_w, ada_b).reshape(depth, MOD_ROWS, 1, 6 * d)

    log_gamma = jax.nn.log_sigmoid(ret_gamma_logit.astype(F32))
    p_lb = jax.nn.softmax(hg_lb_logit.astype(F32), axis=0)
    hg_lb = jnp.clip(jnp.cumsum(p_lb, axis=0) - p_lb[0:1], 0.0, 1.0 - 1e-6)
    rope = _rope_tables(t_lat, ret_dk)

    w_ret_o_b, w_hg_o_b, w_out_b = w_ret_o.astype(BF16), w_hg_o.astype(BF16), w_out.astype(BF16)
    wg_b, wu_b, wd_b = w_exp_gate.astype(BF16), w_exp_up.astype(BF16), w_exp_down.astype(BF16)
    w_router_t = jnp.swapaxes(w_router, 1, 2)
    fc = min(1024, w_exp_gate.shape[-1])
    tn = min(1024, w_merge.shape[-1])

    _, h = _norm_call(x, None, None, mod[0], norm_mix_g[0], mod_row, tm)
    new_ret, new_hg = [], []
    for l in range(depth):
        proj = _proj_call(h, w_in, l, tm_proj, tn)
        gates = _proj_call(h, w_merge, l, tm_proj, tn, act="sigmoid")

        ret_kw = dict(heads=ret_heads, dk=ret_dk, dv=ret_dv)
        orf_c, orb_c, s_ret = _ret_call(proj, log_gamma[l], None, None, tok0=0, batch=b_ctx, seq=t_ctx, **ret_kw)
        orf_l, orb_l, _ = _ret_call(proj, log_gamma[l], state_ret[:, l], rope, tok0=n_ctx, batch=b_lat, seq=t_lat,
                                    **ret_kw)
        hg_kw = dict(heads=hg_heads, dk=hg_dk, dv=hg_dv, col_q=col_gq, col_f=col_gf, col_i=col_gi)
        ogf_c, ogb_c, s_hg = _hg_call(proj, hg_lb[l], None, tok0=0, batch=b_ctx, seq=t_ctx, **hg_kw)
        ogf_l, ogb_l, _ = _hg_call(proj, hg_lb[l], state_hgrn[:, l], tok0=n_ctx, batch=b_lat, seq=t_lat, **hg_kw)
        new_ret.append(s_ret)
        new_hg.append(s_hg)

        cat = lambda a, b: jnp.concatenate([a, b], axis=0)
        x, h2, logits_t = _mix_out_call(
            cat(orf_c, orf_l), cat(orb_c, orb_l), cat(ogf_c, ogf_l), cat(ogb_c, ogb_l), proj, gates, x, mod[l],
            mod_row, norm_ffn_g[l], w_ret_o_b[l], w_hg_o_b[l], w_out_b[l], w_router_t[l], tm,
            ret_heads=ret_heads, hg_heads=hg_heads, col_rg=col_rg, col_go=col_go)

        routed, base, rows = [], [], []
        for lo, n_set in ((0, n_ctx), (n_ctx, n_lat)):
            cap = CAPACITY_FACTOR * n_set // ne
            routed.append(_route_call(logits_t[:, lo:lo + n_set], cap))
            edges = jnp.concatenate([routed[-1][2][:, ::tb], jnp.full((ne, 1), cap, I32)], axis=1)
            base.append(edges[:, :-1].T)
            rows.append((edges[:, 1:] - edges[:, :-1]).T)
        aff, sel, posx = (jnp.concatenate([r[i] for r in routed], axis=1) for i in range(3))
        base = jnp.concatenate(base, axis=0).reshape(-1)
        rows = jnp.concatenate(rows, axis=0).reshape(-1)
        ffn = _moe_call(h2, sel, posx, aff, base, rows, wg_b, wu_b, wd_b, l, tb=tb, fc=fc)

        if l + 1 < depth:
            x, h = _norm_call(x, ffn, mod[l], mod[l + 1], norm_mix_g[l + 1], mod_row, tm)
        else:
            _, y = _norm_call(x, ffn, mod[l], None, final_g, mod_row, tm)

    y_prompt = y[:n_ctx].reshape(b_ctx, t_ctx, d)
    y_sample = y[n_ctx:].reshape(b_lat, t_lat, d)
    return y_prompt, y_sample, jnp.stack(new_ret, axis=1), jnp.stack(new_hg, axis=1)
```

```python
import functools

import jax
import jax.numpy as jnp
from jax import lax
from jax.experimental import pallas as pl
from jax.experimental.pallas import tpu as pltpu

F32 = jnp.float32
BF16 = jnp.bfloat16
I32 = jnp.int32
HIGHEST = lax.Precision.HIGHEST

NORM_EPS = 1e-6
LOG2_E = 1.4426950408889634
ROPE_BASE = 10000.0
GRID_W = 64
CAPACITY_FACTOR = 2
RET_CHUNK = 128
HG_LEVELS = 6
HG_CHUNK = 1 << HG_LEVELS
MOD_ROWS = 8
LANE = 128
BF16_SUBLANE = 16
MOE_ROUND = 64
VMEM_LIMIT = 62 * 1024 * 1024


def _cparams(*sem):
    return pltpu.CompilerParams(dimension_semantics=sem, vmem_limit_bytes=VMEM_LIMIT)


def _dot(a, b, **kw):
    return jnp.dot(a, b, preferred_element_type=F32, **kw)


def _dot_nt(a, b, **kw):
    return lax.dot_general(a, b, (((1,), (1,)), ((), ())), preferred_element_type=F32, **kw)


def _dot_tn(a, b, **kw):
    return lax.dot_general(a, b, (((0,), (0,)), ((), ())), preferred_element_type=F32, **kw)


def _silu(x):
    return x * jax.nn.sigmoid(x)


def _mod_kernel(c_ref, w_ref, b_ref, o_ref):
    s = _silu(c_ref[...])
    o_ref[...] = _dot(s, w_ref[...], precision=HIGHEST) + b_ref[...]


def _mod_call(cvec, ada_w, ada_b):
    depth, d, six_d = ada_w.shape
    tn = 6 * LANE * 2
    assert six_d % tn == 0
    return pl.pallas_call(
        _mod_kernel,
        grid=(depth, six_d // tn),
        in_specs=[
            pl.BlockSpec((MOD_ROWS, d), lambda l, j: (0, 0)),
            pl.BlockSpec((None, d, tn), lambda l, j: (l, 0, j)),
            pl.BlockSpec((None, 1, tn), lambda l, j: (l, 0, j)),
        ],
        out_specs=pl.BlockSpec((None, MOD_ROWS, tn), lambda l, j: (l, 0, j)),
        out_shape=jax.ShapeDtypeStruct((depth, MOD_ROWS, six_d), F32),
        compiler_params=_cparams("arbitrary", "arbitrary"),
        name="adaln_mod",
    )(cvec, ada_w, ada_b.reshape(depth, 1, six_d))


def _rms(x):
    return x * lax.rsqrt(jnp.mean(x * x, axis=-1, keepdims=True) + NORM_EPS)


def _norm_kernel(*refs, d, residual, final):
    it = iter(refs)
    x_ref = next(it)
    f_ref, gate_ref = (next(it), next(it)) if residual else (None, None)
    mod_ref = None if final else next(it)
    g_ref = next(it)
    xo_ref = next(it) if residual else None
    h_ref = next(it)
    x = x_ref[...]
    if residual:
        x = x + gate_ref[...] * f_ref[...]
        xo_ref[...] = x
    y = _rms(x) * g_ref[...]
    if not final:
        y = y * (1.0 + mod_ref[:, d:2 * d]) + mod_ref[:, 0:d]
    h_ref[...] = y.astype(h_ref.dtype)


def _norm_call(x, ffn, mod_gate, mod_next, g, mod_row, tm):
    n, d = x.shape
    residual = ffn is not None
    final = mod_next is None
    row = pl.BlockSpec((tm, d), lambda i: (i, 0))
    in_specs, args = [row], [x]
    if residual:
        in_specs += [row, pl.BlockSpec((None, 1, d), lambda i: (mod_row(i), 0, 5))]
        args += [ffn, mod_gate]
    if not final:
        in_specs.append(pl.BlockSpec((None, 1, 2 * d), lambda i: (mod_row(i), 0, 0)))
        args.append(mod_next)
    in_specs.append(pl.BlockSpec((1, d), lambda i: (0, 0)))
    args.append(g.reshape(1, d))
    out_specs, out_shape = [row], [jax.ShapeDtypeStruct((n, d), F32 if final else BF16)]
    if residual:
        out_specs = [row, row]
        out_shape = [jax.ShapeDtypeStruct((n, d), F32)] + out_shape
    outs = pl.pallas_call(
        functools.partial(_norm_kernel, d=d, residual=residual, final=final),
        grid=(n // tm,),
        in_specs=in_specs,
        out_specs=out_specs,
        out_shape=out_shape,
        compiler_params=_cparams("arbitrary"),
        name="residual_norm",
    )(*args)
    return outs if residual else (x, outs[0])


def _proj_kernel(h_ref, w_ref, o_ref, wb_ref, *, act):
    @pl.when(pl.program_id(1) == 0)
    def _():
        wb_ref[...] = w_ref[...].astype(BF16)

    acc = _dot(h_ref[...], wb_ref[...])
    if act == "sigmoid":
        acc = jax.nn.sigmoid(acc)
    o_ref[...] = acc


def _proj_call(h, w, layer, tm, tn, act=None):
    n, d = h.shape
    cols = w.shape[-1]
    return pl.pallas_call(
        functools.partial(_proj_kernel, act=act),
        grid=(cols // tn, n // tm),
        in_specs=[
            pl.BlockSpec((tm, d), lambda j, i: (i, 0)),
            pl.BlockSpec((None, d, tn), lambda j, i: (layer, 0, j)),
        ],
        out_specs=pl.BlockSpec((tm, tn), lambda j, i: (i, j)),
        out_shape=jax.ShapeDtypeStruct((n, cols), F32),
        scratch_shapes=[pltpu.VMEM((d, tn), BF16)],
        compiler_params=_cparams("arbitrary", "arbitrary"),
        name="in_proj",
    )(h, w)


def _swap_half_pairs(x):
    lane = lax.broadcasted_iota(I32, x.shape, 1)
    return jnp.where((lane & 32) == 0, pltpu.roll(x, LANE - 32, 1), pltpu.roll(x, 32, 1))


def _ret_kernel(*refs, heads, dk, dv, chunk, n_chunks, has_init, has_rope):
    it = iter(refs)
    lg_ref = next(it)
    qkv = [[next(it) for _ in range(3)] for _ in range(2)]
    s0_ref = next(it) if has_init else None
    rope = [[next(it) for _ in range(2)] for _ in range(2)] if has_rope else None
    o_refs = [next(it), next(it)]
    sfin_ref = next(it)
    s_ref = next(it)

    c = pl.program_id(1)

    @pl.when(c == 0)
    def _():
        if has_init:
            s_ref[...] = s0_ref[...]
        else:
            s_ref[...] = jnp.zeros_like(s_ref)

    L = chunk
    ii = lax.broadcasted_iota(I32, (L, L), 0)
    jj = lax.broadcasted_iota(I32, (L, L), 1)
    pcol = lax.broadcasted_iota(I32, (L, 1), 0).astype(F32)
    for d in range(2):
        q_ref, k_ref, v_ref = qkv[d]
        if d == 0:
            dif = (ii - jj).astype(F32)
            q_pow, k_pow = pcol + 1.0, (L - 1.0) - pcol
        else:
            dif = (jj - ii).astype(F32)
            q_pow, k_pow = L - pcol, pcol
        for hh in range(heads):
            lg = lg_ref[d, hh]
            q = q_ref[:, hh * dk:(hh + 1) * dk]
            k = k_ref[:, hh * dk:(hh + 1) * dk] * (dk ** -0.5)
            v = v_ref[:, hh * dv:(hh + 1) * dv].astype(BF16)
            if has_rope:
                cos, sin = rope[d][0][...], rope[d][1][...]
                q = q * cos + _swap_half_pairs(q) * sin
                k = k * cos + _swap_half_pairs(k) * sin
            decay = jnp.where(dif >= 0.0, jnp.exp(lg * jnp.maximum(dif, 0.0)), 0.0)
            scores = _dot_nt(q.astype(BF16), k.astype(BF16)) * decay
            s_old = s_ref[d, hh]
            o = _dot(scores.astype(BF16), v) + _dot((q * jnp.exp(lg * q_pow)).astype(BF16), s_old.astype(BF16))
            kv = _dot((k * jnp.exp(lg * k_pow)).T.astype(BF16), v)
            s_ref[d, hh] = s_old * jnp.exp(jnp.full((1, 1), lg * L, F32)) + kv
            o_refs[d][:, hh * dv:(hh + 1) * dv] = o

    @pl.when(c == n_chunks - 1)
    def _():
        sfin_ref[...] = s_ref[...]


def _ret_call(proj, log_gamma, s0, rope, *, tok0, batch, seq, heads, dk, dv):
    L = RET_CHUNK
    n = seq // L
    qw, vw = heads * dk, heads * dv
    assert seq % L == 0 and tok0 % L == 0 and vw % qw == 0
    r0 = tok0 // L

    def fwd(b, c):
        return r0 + b * n + c

    def bwd(b, c):
        return r0 + b * n + (n - 1 - c)

    in_specs = [pl.BlockSpec(memory_space=pltpu.SMEM)]
    args = [log_gamma]
    for rmap in (fwd, bwd):
        in_specs += [
            pl.BlockSpec((L, qw), lambda b, c, rmap=rmap: (rmap(b, c), 0)),
            pl.BlockSpec((L, qw), lambda b, c, rmap=rmap: (rmap(b, c), 1)),
            pl.BlockSpec((L, vw), lambda b, c, rmap=rmap: (rmap(b, c), (2 * qw) // vw)),
        ]
        args += [proj, proj, proj]
    state_spec = pl.BlockSpec((None, 2, heads, dk, dv), lambda b, c: (b, 0, 0, 0, 0))
    if s0 is not None:
        in_specs.append(state_spec)
        args.append(s0)
    if rope is not None:
        for cmap in (lambda b, c: (c, 0), lambda b, c: (n - 1 - c, 0)):
            in_specs += [pl.BlockSpec((L, dk), cmap), pl.BlockSpec((L, dk), cmap)]
            args += [rope[0], rope[1]]
    n_tok = batch * seq
    return pl.pallas_call(
        functools.partial(_ret_kernel, heads=heads, dk=dk, dv=dv, chunk=L, n_chunks=n,
                          has_init=s0 is not None, has_rope=rope is not None),
        grid=(batch, n),
        in_specs=in_specs,
        out_specs=[
            pl.BlockSpec((L, vw), lambda b, c: (b * n + c, 0)),
            pl.BlockSpec((L, vw), lambda b, c: (b * n + (n - 1 - c), 0)),
            state_spec,
        ],
        out_shape=[
            jax.ShapeDtypeStruct((n_tok, vw), F32),
            jax.ShapeDtypeStruct((n_tok, vw), F32),
            jax.ShapeDtypeStruct((batch, 2, heads, dk, dv), F32),
        ],
        scratch_shapes=[pltpu.VMEM((2, heads, dk, dv), F32)],
        compiler_params=_cparams("arbitrary", "arbitrary"),
        name="retention_scan",
    )(*args)


def _hg_scan_index(shape, axis, rev):
    i = lax.broadcasted_iota(I32, shape, axis) & (HG_CHUNK - 1)
    return (HG_CHUNK - 1 - i) if rev else i


def _hg_window_matrix(rev):
    C = HG_CHUNK
    ti = _hg_scan_index((C, 2 * C), 0, rev)
    tj = _hg_scan_index((C, 2 * C), 1, rev)
    blocks = [tj <= ti]
    for l in range(2, HG_LEVELS + 1):
        anchor = ((ti >> l) << l) + (1 << (l - 1)) - 1
        upper = ((ti >> (l - 1)) & 1) == 1
        blocks.append((upper & (tj > anchor) & (tj <= ti)) | (~upper & (tj > ti) & (tj <= anchor)))
    blocks.append(tj > ti)
    return jnp.concatenate([jnp.where(m, 1.0, 0.0) for m in blocks], axis=0).astype(BF16)


def _hg_kernel(*refs, heads, dk, dv, tt, n_blocks, has_init):
    it = iter(refs)
    qzv = [[next(it) for _ in range(3)] for _ in range(2)]
    lb_ref = next(it)
    s0_ref = next(it) if has_init else None
    o_refs = [next(it), next(it)]
    sfin_ref = next(it)
    st_ref = next(it)
    win_ref = next(it)

    t = pl.program_id(1)

    @pl.when((pl.program_id(0) == 0) & (t == 0))
    def _():
        for d in range(2):
            win_ref[d] = _hg_window_matrix(d == 1)

    @pl.when(t == 0)
    def _():
        for d in range(2):
            for hh in range(heads):
                st_ref[d, hh] = s0_ref[d, hh].T if has_init else jnp.zeros((dv, dk), F32)

    C = HG_CHUNK
    n_ch = tt // C
    chunks = [slice(c * C, (c + 1) * C) for c in range(n_ch)]
    levels, uppers = [], []
    for d in range(2):
        ti = _hg_scan_index((C, C), 0, d == 1)
        tj = _hg_scan_index((C, C), 1, d == 1)
        level = jnp.where(tj < ti, 1, 0)
        for l in range(1, HG_LEVELS):
            level = level + jnp.where((tj < ti) & ((ti >> l) != (tj >> l)), 1, 0)
        t_col = _hg_scan_index((C, 1), 0, d == 1)
        levels.append(level)
        uppers.append([None] + [((t_col >> (l - 1)) & 1) == 1 for l in range(1, HG_LEVELS + 1)])

    def head_body(hh, carry):
        ck = pl.ds(pl.multiple_of(hh * dk, LANE), dk)
        cv = pl.ds(pl.multiple_of(hh * dv, LANE), dv)
        jobs = [(d, r) for d in range(2) for r in chunks]
        qg, k, f, hi, lo, v, vb, diag = [], [], [], [], [], [], [], []
        for d in range(2):
            q_ref, z_ref, v_ref = qzv[d]
            lb = lb_ref[d, :, ck]
            z = z_ref[:, ck]
            e = jnp.exp(-jnp.abs(z))
            den = 1.0 + e
            num = jnp.where(z >= 0.0, 1.0 + lb * e, e + lb)
            log_num = jnp.where((z < 0.0) & (lb <= 0.0), z, jnp.log(num))
            log_f = log_num - jnp.log(den)
            k.append((1.0 - lb) * (jnp.where(z > 0.0, e, 1.0) / den))
            f.append(1.0 - k[d])
            log2_f = log_f * LOG2_E
            hi.append(log2_f.astype(BF16))
            lo.append((log2_f - hi[d].astype(F32)).astype(BF16))
            qg.append(_silu(q_ref[:, ck]) * (dk ** -0.5))
            v.append(v_ref[:, cv])
            vb.append(v[d].astype(BF16))
            diag.append(jnp.sum(qg[d] * k[d], axis=-1, keepdims=True))
        wins = [win_ref[0], win_ref[1]]
        sums = [_dot(wins[d], jnp.concatenate([hi[d][r], lo[d][r]], axis=0)) for d, r in jobs]
        pairs = []
        for (d, r), s in zip(jobs, sums):
            ps = [jnp.where(uppers[d][1], qg[d][r] * f[d][r], k[d][r]).astype(BF16)]
            for l in range(2, HG_LEVELS + 1):
                scale = jnp.exp2(s[(l - 1) * C:l * C])
                ps.append((jnp.where(uppers[d][l], qg[d][r], k[d][r]) * scale).astype(BF16))
            pairs.append(ps)
        grams = [[_dot_nt(p, p) for p in ps] for ps in pairs]
        atts = []
        for (d, r), gs in zip(jobs, grams):
            att = jnp.where(levels[d] == 1, gs[0], 0.0)
            for l in range(2, HG_LEVELS + 1):
                att = jnp.where(levels[d] == l, gs[l - 1], att)
            atts.append(att.astype(BF16))
        k_out = [(k[d][r] * jnp.exp2(s[HG_LEVELS * C:])).astype(BF16) for (d, r), s in zip(jobs, sums)]
        q_in = [(qg[d][r] * jnp.exp2(s[:C])).astype(BF16) for (d, r), s in zip(jobs, sums)]
        decays = [jnp.exp2(s[0:1] if d == 1 else s[C - 1:C]) for (d, r), s in zip(jobs, sums)]
        outs = [_dot(att, vb[d][r]) + diag[d][r] * v[d][r] for att, (d, r) in zip(atts, jobs)]
        kvs = [_dot_tn(vb[d][r], ko) for (d, r), ko in zip(jobs, k_out)]
        for d in range(2):
            st = st_ref[d, hh]
            for c in (range(n_ch - 1, -1, -1) if d == 1 else range(n_ch)):
                j = d * n_ch + c
                o_refs[d][chunks[c], cv] = outs[j] + _dot_nt(q_in[j], st.astype(BF16))
                st = st * decays[j] + kvs[j]
            st_ref[d, hh] = st
        return carry

    lax.fori_loop(0, heads, head_body, 0)

    @pl.when(t == n_blocks - 1)
    def _():
        for d in range(2):
            for hh in range(heads):
                sfin_ref[d, hh] = st_ref[d, hh].T


def _hg_call(proj, lb, s0, *, tok0, batch, seq, heads, dk, dv, col_q, col_f, col_i):
    tt = min(seq, 256)
    n = seq // tt
    kw, vw = heads * dk, heads * dv
    assert seq % tt == 0 and tok0 % tt == 0 and tt % HG_CHUNK == 0
    assert col_q % kw == 0 and col_f % kw == 0 and col_i % vw == 0
    r0 = tok0 // tt

    def fwd(b, t):
        return r0 + b * n + t

    def bwd(b, t):
        return r0 + b * n + (n - 1 - t)

    in_specs, args = [], []
    for d, rmap in enumerate((fwd, bwd)):
        in_specs += [
            pl.BlockSpec((tt, kw), lambda b, t, rmap=rmap: (rmap(b, t), col_q // kw)),
            pl.BlockSpec((tt, kw), lambda b, t, rmap=rmap, d=d: (rmap(b, t), col_f // kw + d)),
            pl.BlockSpec((tt, vw), lambda b, t, rmap=rmap: (rmap(b, t), col_i // vw)),
        ]
        args += [proj, proj, proj]
    in_specs.append(pl.BlockSpec((2, 1, kw), lambda b, t: (0, 0, 0)))
    args.append(lb.reshape(2, 1, kw))
    state_spec = pl.BlockSpec((None, 2, heads, dk, dv), lambda b, t: (b, 0, 0, 0, 0))
    if s0 is not None:
        in_specs.append(state_spec)
        args.append(s0)
    n_tok = batch * seq
    return pl.pallas_call(
        functools.partial(_hg_kernel, heads=heads, dk=dk, dv=dv, tt=tt, n_blocks=n, has_init=s0 is not None),
        grid=(batch, n),
        in_specs=in_specs,
        out_specs=[
            pl.BlockSpec((tt, vw), lambda b, t: (b * n + t, 0)),
            pl.BlockSpec((tt, vw), lambda b, t: (b * n + (n - 1 - t), 0)),
            state_spec,
        ],
        out_shape=[
            jax.ShapeDtypeStruct((n_tok, vw), F32),
            jax.ShapeDtypeStruct((n_tok, vw), F32),
            jax.ShapeDtypeStruct((batch, 2, heads, dk, dv), F32),
        ],
        scratch_shapes=[pltpu.VMEM((2, heads, dv, dk), F32),
                        pltpu.VMEM((2, (HG_LEVELS + 1) * HG_CHUNK, 2 * HG_CHUNK), BF16)],
        compiler_params=_cparams("arbitrary", "arbitrary"),
        name="hgrn2_scan",
    )(*args)


def _mix_out_kernel(orf_ref, orb_ref, rg_ref, ogf_ref, ogb_ref, go_ref, gates_ref, x_ref, mod_ref, g2_ref,
                    wr_ref, wh_ref, wo_ref, wrt_ref, x1_ref, h2_ref, lgt_ref, *, d, ret_heads, hg_heads):
    o = orf_ref[...] + orb_ref[...]
    dvr = o.shape[1] // ret_heads
    parts = []
    for hh in range(ret_heads):
        oh = o[:, hh * dvr:(hh + 1) * dvr]
        mu = jnp.mean(oh, axis=-1, keepdims=True)
        ctr = oh - mu
        parts.append(ctr * lax.rsqrt(jnp.mean(ctr * ctr, axis=-1, keepdims=True) + NORM_EPS))
    ret_in = (jnp.concatenate(parts, axis=1) * _silu(rg_ref[...])).astype(BF16)
    ret_out = _dot(ret_in, wr_ref[...])

    og = ogf_ref[...] + ogb_ref[...]
    dvh = og.shape[1] // hg_heads
    parts = [_rms(og[:, hh * dvh:(hh + 1) * dvh]) for hh in range(hg_heads)]
    hg_in = (jnp.concatenate(parts, axis=1) * _silu(go_ref[...])).astype(BF16)
    hg_out = _dot(hg_in, wh_ref[...])

    merged = gates_ref[:, :d] * ret_out + gates_ref[:, d:] * hg_out
    mix = _dot(merged.astype(BF16), wo_ref[...])
    x1 = x_ref[...] + mod_ref[:, 2 * d:3 * d] * mix
    x1_ref[...] = x1
    h2 = _rms(x1) * g2_ref[...]
    h2 = h2 * (1.0 + mod_ref[:, 4 * d:5 * d]) + mod_ref[:, 3 * d:4 * d]
    h2_ref[...] = h2.astype(BF16)
    lgt_ref[...] = _dot_nt(wrt_ref[...], h2, precision=HIGHEST)


def _mix_out_call(o_ret_f, o_ret_b, og_f, og_b, proj, gates, x, mod_l, mod_row, g2, w_ret_o, w_hg_o, w_out,
                  w_router_t, tm, *, ret_heads, hg_heads, col_rg, col_go):
    n, d = x.shape
    vr, vh = o_ret_f.shape[1], og_f.shape[1]
    ne = w_router_t.shape[0]
    assert col_rg % vr == 0 and col_go % vh == 0

    def row(w):
        return pl.BlockSpec((tm, w), lambda i: (i, 0))

    def full(a):
        return pl.BlockSpec(a.shape, lambda i: (0,) * a.ndim)

    return pl.pallas_call(
        functools.partial(_mix_out_kernel, d=d, ret_heads=ret_heads, hg_heads=hg_heads),
        grid=(n // tm,),
        in_specs=[
            row(vr), row(vr), pl.BlockSpec((tm, vr), lambda i: (i, col_rg // vr)),
            row(vh), row(vh), pl.BlockSpec((tm, vh), lambda i: (i, col_go // vh)),
            row(2 * d), row(d),
            pl.BlockSpec((None, 1, 6 * d), lambda i: (mod_row(i), 0, 0)),
            pl.BlockSpec((1, d), lambda i: (0, 0)),
            full(w_ret_o), full(w_hg_o), full(w_out), full(w_router_t),
        ],
        out_specs=[row(d), row(d), pl.BlockSpec((ne, tm), lambda i: (0, i))],
        out_shape=[
            jax.ShapeDtypeStruct((n, d), F32),
            jax.ShapeDtypeStruct((n, d), BF16),
            jax.ShapeDtypeStruct((ne, n), F32),
        ],
        compiler_params=_cparams("arbitrary"),
        name="mixer_out",
    )(o_ret_f, o_ret_b, proj, og_f, og_b, proj, gates, x, mod_l, g2.reshape(1, d), w_ret_o, w_hg_o, w_out,
      w_router_t)


def _route_kernel(lg_ref, aff_ref, sel_ref, posx_ref, flag_ref, *, cap, n):
    lg = lg_ref[...]
    ne = lg.shape[0]
    ex = jnp.exp(lg - jnp.max(lg, axis=0, keepdims=True))
    aff = ex / jnp.sum(ex, axis=0, keepdims=True)
    aff_ref[...] = aff
    bits = pltpu.bitcast(aff, I32)

    def search(i, lo):
        cand = lo | lax.shift_left(jnp.int32(1), 30 - i)
        cnt = jnp.sum(jnp.where(bits >= cand, 1.0, 0.0), axis=1, keepdims=True)
        return jnp.where(cnt >= cap, cand, lo)

    thr = lax.fori_loop(0, 31, search, jnp.zeros((ne, 1), I32))
    gt = bits > thr
    eq = bits == thr
    need = cap - jnp.sum(jnp.where(gt, 1.0, 0.0), axis=1, keepdims=True)
    upper = jnp.where(lax.broadcasted_iota(I32, (LANE, LANE), 0) < lax.broadcasted_iota(I32, (LANE, LANE), 1),
                      1.0, 0.0).astype(BF16)

    def exclusive_count(store):
        def body(ci, carry):
            sl = pl.ds(pl.multiple_of(ci * LANE, LANE), LANE)
            x = flag_ref[:, sl]
            store(sl, carry + _dot(x.astype(BF16), upper))
            return carry + jnp.sum(x, axis=1, keepdims=True)

        lax.fori_loop(0, n // LANE, body, jnp.zeros((ne, 1), F32))

    flag_ref[...] = jnp.where(eq, 1.0, 0.0)

    def store_tie_rank(sl, rank):
        posx_ref[:, sl] = rank.astype(I32)

    exclusive_count(store_tie_rank)
    sel = gt | (eq & (posx_ref[...].astype(F32) < need))
    sel_ref[...] = jnp.where(sel, 1, 0).astype(I32)
    flag_ref[...] = jnp.where(sel, 1.0, 0.0)

    def store_pos(sl, cnt):
        posx_ref[:, sl] = cnt.astype(I32)

    exclusive_count(store_pos)


def _route_call(logits_t, cap):
    ne, n = logits_t.shape
    spec = pl.BlockSpec((ne, n), lambda: (0, 0))
    return pl.pallas_call(
        functools.partial(_route_kernel, cap=cap, n=n),
        in_specs=[spec],
        out_specs=[spec, spec, spec],
        out_shape=[
            jax.ShapeDtypeStruct((ne, n), F32),
            jax.ShapeDtypeStruct((ne, n), I32),
            jax.ShapeDtypeStruct((ne, n), I32),
        ],
        scratch_shapes=[pltpu.VMEM((ne, n), F32)],
        compiler_params=pltpu.CompilerParams(vmem_limit_bytes=VMEM_LIMIT),
        name="expert_choice_route",
    )(logits_t)


def _moe_round_masks(b, r, base_ref, sel_ref, posx_ref, ne, value_rows=None):
    n_tok = sel_ref.shape[1]
    row_iota = lax.broadcasted_iota(I32, (MOE_ROUND, n_tok), 0)
    pieces = []
    for e in range(ne):
        local = posx_ref[e:e + 1, :] - (base_ref[b * ne + e] + r * MOE_ROUND)
        hit = (sel_ref[e:e + 1, :] == 1) & (local == row_iota)
        value = 1.0 if value_rows is None else value_rows[e:e + 1, :]
        pieces.append(jnp.where(hit, value, 0.0).astype(BF16))
    return jnp.concatenate(pieces, axis=0)


def _dispatch_kernel(base_ref, start_ref, pc_ref, nr_ref, h_ref, sel_ref, posx_ref, xe_zero_ref, xe_ref,
                     stage_ref, sem, *, ne):
    del xe_zero_ref
    b = pl.program_id(0)
    n_rounds = nr_ref[b]

    def pieces(r, slot, fn):
        for e in range(ne):
            rem = pc_ref[b * ne + e] - r * MOE_ROUND
            dst = start_ref[b * ne + e] + r * MOE_ROUND
            for size, cond, off in (
                    (MOE_ROUND, rem >= MOE_ROUND, 0),
                    (32, (rem > 0) & (rem < MOE_ROUND) & ((rem & 32) != 0), 0),
                    (16, (rem > 0) & (rem < MOE_ROUND) & ((rem & 16) != 0), rem & 32)):
                @pl.when(cond)
                def _(size=size, off=off, e=e, dst=dst):
                    src_rows = pl.ds(pl.multiple_of(e * MOE_ROUND + off, BF16_SUBLANE), size)
                    dst_rows = pl.ds(pl.multiple_of(dst + off, BF16_SUBLANE), size)
                    fn(pltpu.make_async_copy(stage_ref.at[slot, src_rows], xe_ref.at[e, dst_rows], sem.at[slot]))

    def round_body(r, carry):
        slot = r % 2
        onehot = _moe_round_masks(b, r, base_ref, sel_ref, posx_ref, ne)
        stage_ref[slot] = _dot(onehot, h_ref[...]).astype(BF16)

        @pl.when(r >= 1)
        def _():
            pieces(r - 1, 1 - slot, lambda cp: cp.wait())

        pieces(r, slot, lambda cp: cp.start())
        return carry

    lax.fori_loop(0, n_rounds, round_body, 0)

    @pl.when(n_rounds >= 1)
    def _():
        pieces(n_rounds - 1, (n_rounds - 1) % 2, lambda cp: cp.wait())


def _dispatch_call(h2, sel, posx, tables, list_rows, *, sb):
    n, d = h2.shape
    ne = sel.shape[0]
    xe_zero = jnp.zeros((ne, list_rows, d), BF16)
    grid_spec = pltpu.PrefetchScalarGridSpec(
        num_scalar_prefetch=4,
        grid=(n // sb,),
        in_specs=[
            pl.BlockSpec((sb, d), lambda b, *_: (b, 0)),
            pl.BlockSpec((ne, sb), lambda b, *_: (0, b)),
            pl.BlockSpec((ne, sb), lambda b, *_: (0, b)),
            pl.BlockSpec(memory_space=pl.ANY),
        ],
        out_specs=pl.BlockSpec(memory_space=pl.ANY),
        scratch_shapes=[pltpu.VMEM((2, ne * MOE_ROUND, d), BF16), pltpu.SemaphoreType.DMA((2,))],
    )
    return pl.pallas_call(
        functools.partial(_dispatch_kernel, ne=ne),
        grid_spec=grid_spec,
        out_shape=jax.ShapeDtypeStruct((ne, list_rows, d), BF16),
        input_output_aliases={7: 0},
        compiler_params=_cparams("arbitrary"),
        name="expert_dispatch",
    )(*tables, h2, sel, posx, xe_zero)


def _expert_ffn_kernel(tot_ref, x_ref, wg_ref, wu_ref, wd_ref, y_ref, wgb_ref, wub_ref, wdb_ref, acc_ref,
                       *, rt, n_ff):
    e, f, t = pl.program_id(0), pl.program_id(1), pl.program_id(2)

    @pl.when(t == 0)
    def _():
        wgb_ref[...] = wg_ref[...].astype(BF16)
        wub_ref[...] = wu_ref[...].astype(BF16)
        wdb_ref[...] = wd_ref[...].astype(BF16)

    rows = pl.ds(pl.multiple_of(t * rt, rt), rt)
    live = t * rt < tot_ref[e]

    @pl.when(live)
    def _():
        x = x_ref[...]
        hid = (_silu(_dot(x, wgb_ref[...])) * _dot(x, wub_ref[...])).astype(BF16)
        y = _dot(hid, wdb_ref[...])

        @pl.when(f == 0)
        def _():
            acc_ref[rows, :] = y

        @pl.when(f != 0)
        def _():
            acc_ref[rows, :] += y

    @pl.when((f == n_ff - 1) & live)
    def _():
        y_ref[...] = acc_ref[rows, :].astype(BF16)

    @pl.when((f == n_ff - 1) & jnp.logical_not(live))
    def _():
        y_ref[...] = jnp.zeros_like(y_ref)


def _expert_ffn_call(xe, totals, wg, wu, wd, layer, *, rt, fc):
    ne, list_rows, d = xe.shape
    ff = wg.shape[-1]
    n_ff = ff // fc
    grid_spec = pltpu.PrefetchScalarGridSpec(
        num_scalar_prefetch=1,
        grid=(ne, n_ff, list_rows // rt),
        in_specs=[
            pl.BlockSpec((None, rt, d), lambda e, f, t, *_: (e, t, 0)),
            pl.BlockSpec((None, None, d, fc), lambda e, f, t, *_: (layer, e, 0, f)),
            pl.BlockSpec((None, None, d, fc), lambda e, f, t, *_: (layer, e, 0, f)),
            pl.BlockSpec((None, None, fc, d), lambda e, f, t, *_: (layer, e, f, 0)),
        ],
        out_specs=pl.BlockSpec((None, rt, d), lambda e, f, t, *_: (e, jnp.where(f == n_ff - 1, t, 0), 0)),
        scratch_shapes=[pltpu.VMEM((d, fc), BF16), pltpu.VMEM((d, fc), BF16), pltpu.VMEM((fc, d), BF16),
                        pltpu.VMEM((list_rows, d), F32)],
    )
    return pl.pallas_call(
        functools.partial(_expert_ffn_kernel, rt=rt, n_ff=n_ff),
        grid_spec=grid_spec,
        out_shape=jax.ShapeDtypeStruct((ne, list_rows, d), BF16),
        compiler_params=_cparams("arbitrary", "arbitrary", "arbitrary"),
        name="expert_ffn",
    )(totals, xe, wg, wu, wd)


def _combine_kernel(base_ref, start_ref, pc_ref, nr_ref, sel_ref, posx_ref, aff_ref, y_ref, out_ref, ybuf_ref, sem,
                    *, ne):
    b = pl.program_id(0)
    n_rounds = nr_ref[b]

    @pl.when(b == 0)
    def _():
        ybuf_ref[...] = jnp.zeros_like(ybuf_ref)

    def windows(r, slot, fn):
        for e in range(ne):
            @pl.when(pc_ref[b * ne + e] - r * MOE_ROUND > 0)
            def _(e=e):
                src = pl.ds(pl.multiple_of(start_ref[b * ne + e] + r * MOE_ROUND, BF16_SUBLANE), MOE_ROUND)
                fn(pltpu.make_async_copy(y_ref.at[e, src], ybuf_ref.at[slot, pl.ds(e * MOE_ROUND, MOE_ROUND)],
                                         sem.at[slot]))

    out_ref[...] = jnp.zeros_like(out_ref)

    @pl.when(n_rounds >= 1)
    def _():
        windows(0, 0, lambda cp: cp.start())

    def round_body(r, carry):
        slot = r % 2

        @pl.when(r + 1 < n_rounds)
        def _():
            windows(r + 1, 1 - slot, lambda cp: cp.start())

        windows(r, slot, lambda cp: cp.wait())
        weights = _moe_round_masks(b, r, base_ref, sel_ref, posx_ref, ne, value_rows=aff_ref)
        out_ref[...] += _dot_tn(weights, ybuf_ref[slot])
        return carry

    lax.fori_loop(0, n_rounds, round_body, 0)


def _combine_call(ye, sel, posx, aff, tables, *, sb):
    ne, _, d = ye.shape
    n = sel.shape[1]
    grid_spec = pltpu.PrefetchScalarGridSpec(
        num_scalar_prefetch=4,
        grid=(n // sb,),
        in_specs=[
            pl.BlockSpec((ne, sb), lambda b, *_: (0, b)),
            pl.BlockSpec((ne, sb), lambda b, *_: (0, b)),
            pl.BlockSpec((ne, sb), lambda b, *_: (0, b)),
            pl.BlockSpec(memory_space=pl.ANY),
        ],
        out_specs=pl.BlockSpec((sb, d), lambda b, *_: (b, 0)),
        scratch_shapes=[pltpu.VMEM((2, ne * MOE_ROUND, d), BF16), pltpu.SemaphoreType.DMA((2,))],
    )
    return pl.pallas_call(
        functools.partial(_combine_kernel, ne=ne),
        grid_spec=grid_spec,
        out_shape=jax.ShapeDtypeStruct((n, d), F32),
        compiler_params=_cparams("arbitrary"),
        name="expert_combine",
    )(*tables, sel, posx, aff, ye)


def _moe_tables(posx_sets, caps, sb):
    base, cnt = [], []
    for posx, cap in zip(posx_sets, caps):
        ne = posx.shape[0]
        edges = jnp.concatenate([posx[:, ::sb], jnp.full((ne, 1), cap, I32)], axis=1)
        base.append(edges[:, :-1].T)
        cnt.append((edges[:, 1:] - edges[:, :-1]).T)
    base = jnp.concatenate(base, axis=0)
    cnt = jnp.concatenate(cnt, axis=0)
    pc = (cnt + (BF16_SUBLANE - 1)) // BF16_SUBLANE * BF16_SUBLANE
    ends = jnp.cumsum(pc, axis=0)
    start = ends - pc
    n_rounds = jnp.max((pc + (MOE_ROUND - 1)) // MOE_ROUND, axis=1)
    tables = tuple(a.reshape(-1).astype(I32) for a in (base, start, pc, n_rounds))
    return tables, ends[-1].astype(I32)


def _rope_tables(n_tokens, dk):
    rows = n_tokens // GRID_W
    r, c = jnp.meshgrid(jnp.arange(rows), jnp.arange(GRID_W), indexing="ij")
    pos = jnp.stack([r.reshape(-1), c.reshape(-1)], axis=-1).astype(F32)
    nf = dk // 4
    inv_freq = ROPE_BASE ** (-jnp.arange(nf, dtype=F32) / nf)
    ang = pos[:, :, None] * inv_freq
    cos, sin = jnp.cos(ang), jnp.sin(ang)
    cos_t = jnp.concatenate([cos[:, 0], cos[:, 0], cos[:, 1], cos[:, 1]], axis=-1)
    sin_t = jnp.concatenate([-sin[:, 0], sin[:, 0], -sin[:, 1], sin[:, 1]], axis=-1)
    return cos_t, sin_t


def kernel(x_prompt, x_sample, state_ret, state_hgrn, c, c_ctx, ada_w, ada_b, norm_mix_g, norm_ffn_g, w_in,
           ret_gamma_logit, hg_lb_logit, w_ret_o, w_hg_o, w_merge, w_out, w_router, w_exp_gate, w_exp_up,
           w_exp_down, final_g):
    b_ctx, t_ctx, d = x_prompt.shape
    b_lat, t_lat, _ = x_sample.shape
    depth = w_in.shape[0]
    ret_heads, ret_dk, ret_dv = state_ret.shape[3:]
    hg_heads, hg_dk, hg_dv = state_hgrn.shape[3:]
    ne = w_router.shape[-1]
    n_ctx, n_lat = b_ctx * t_ctx, b_lat * t_lat
    n_tok = n_ctx + n_lat
    qw, vw = ret_heads * ret_dk, ret_heads * ret_dv
    kw, hw = hg_heads * hg_dk, hg_heads * hg_dv
    col_rg = 2 * qw + vw
    col_gq = col_rg + vw
    col_gf = col_gq + kw
    col_gi = col_gf + 2 * kw
    col_go = col_gi + hw
    assert col_go + hw == w_in.shape[-1] and b_lat < MOD_ROWS

    tm = 256
    assert t_ctx % tm == 0 and t_lat % tm == 0
    tm_proj = 512 if (n_ctx % 512 == 0 and n_lat % 512 == 0) else tm
    sb = 1024 if (n_ctx % 4096 == 0 and n_lat % 4096 == 0) else 256
    assert n_ctx % sb == 0 and n_lat % sb == 0
    sets = ((0, n_ctx), (n_ctx, n_lat))
    caps = [CAPACITY_FACTOR * n_set // ne for _, n_set in sets]
    rt = 512
    list_rows = -(-(sum(caps) + BF16_SUBLANE * (n_tok // sb)) // rt) * rt

    def mod_row(i):
        return jnp.where(i < n_ctx // tm, 0, 1 + (i - n_ctx // tm) // (t_lat // tm))

    x = jnp.concatenate([x_prompt.reshape(n_ctx, d), x_sample.reshape(n_lat, d)], axis=0)
    cvec = jnp.zeros((MOD_ROWS, d), F32).at[0].set(c_ctx).at[1:1 + b_lat].set(c)
    mod = _mod_call(cvec, ada_w, ada_b).reshape(depth, MOD_ROWS, 1, 6 * d)

    log_gamma = jax.nn.log_sigmoid(ret_gamma_logit.astype(F32))
    p_lb = jax.nn.softmax(hg_lb_logit.astype(F32), axis=0)
    hg_lb = jnp.clip(jnp.cumsum(p_lb, axis=0) - p_lb[0:1], 0.0, 1.0 - 1e-6)
    rope = _rope_tables(t_lat, ret_dk)

    w_ret_o_b, w_hg_o_b, w_out_b = w_ret_o.astype(BF16), w_hg_o.astype(BF16), w_out.astype(BF16)
    w_router_t = jnp.swapaxes(w_router, 1, 2)
    fc = min(1024, w_exp_gate.shape[-1])
    tn = min(1024, w_merge.shape[-1])

    _, h = _norm_call(x, None, None, mod[0], norm_mix_g[0], mod_row, tm)
    new_ret, new_hg = [], []
    for l in range(depth):
        proj = _proj_call(h, w_in, l, tm_proj, tn)
        gates = _proj_call(h, w_merge, l, tm_proj, tn, act="sigmoid")

        ret_kw = dict(heads=ret_heads, dk=ret_dk, dv=ret_dv)
        orf_c, orb_c, s_ret = _ret_call(proj, log_gamma[l], None, None, tok0=0, batch=b_ctx, seq=t_ctx, **ret_kw)
        orf_l, orb_l, _ = _ret_call(proj, log_gamma[l], state_ret[:, l], rope, tok0=n_ctx, batch=b_lat, seq=t_lat,
                                    **ret_kw)
        hg_kw = dict(heads=hg_heads, dk=hg_dk, dv=hg_dv, col_q=col_gq, col_f=col_gf, col_i=col_gi)
        ogf_c, ogb_c, s_hg = _hg_call(proj, hg_lb[l], None, tok0=0, batch=b_ctx, seq=t_ctx, **hg_kw)
        ogf_l, ogb_l, _ = _hg_call(proj, hg_lb[l], state_hgrn[:, l], tok0=n_ctx, batch=b_lat, seq=t_lat, **hg_kw)
        new_ret.append(s_ret)
        new_hg.append(s_hg)

        cat = lambda a, b: jnp.concatenate([a, b], axis=0)
        x, h2, logits_t = _mix_out_call(
            cat(orf_c, orf_l), cat(orb_c, orb_l), cat(ogf_c, ogf_l), cat(ogb_c, ogb_l), proj, gates, x, mod[l],
            mod_row, norm_ffn_g[l], w_ret_o_b[l], w_hg_o_b[l], w_out_b[l], w_router_t[l], tm,
            ret_heads=ret_heads, hg_heads=hg_heads, col_rg=col_rg, col_go=col_go)

        routed = [_route_call(logits_t[:, lo:lo + n_set], cap) for (lo, n_set), cap in zip(sets, caps)]
        aff, sel, posx = (jnp.concatenate([r[i] for r in routed], axis=1) for i in range(3))
        tables, totals = _moe_tables([r[2] for r in routed], caps, sb)
        xe = _dispatch_call(h2, sel, posx, tables, list_rows, sb=sb)
        ye = _expert_ffn_call(xe, totals, w_exp_gate, w_exp_up, w_exp_down, l, rt=rt, fc=fc)
        ffn = _combine_call(ye, sel, posx, aff, tables, sb=sb)

        if l + 1 < depth:
            x, h = _norm_call(x, ffn, mod[l], mod[l + 1], norm_mix_g[l + 1], mod_row, tm)
        else:
            _, y = _norm_call(x, ffn, mod[l], None, final_g, mod_row, tm)

    y_prompt = y[:n_ctx].reshape(b_ctx, t_ctx, d)
    y_sample = y[n_ctx:].reshape(b_lat, t_lat, d)
    return y_prompt, y_sample, jnp.stack(new_ret, axis=1), jnp.stack(new_hg, axis=1)
```

```python
import functools

import jax
import jax.numpy as jnp
from jax import lax
from jax.experimental import pallas as pl
from jax.experimental.pallas import tpu as pltpu

F32 = jnp.float32
BF16 = jnp.bfloat16
I32 = jnp.int32
HIGHEST = lax.Precision.HIGHEST

NORM_EPS = 1e-6
LOG2_E = 1.4426950408889634
ROPE_BASE = 10000.0
GRID_W = 64
CAPACITY_FACTOR = 2
RET_CHUNK = 128
HG_LEVELS = 6
HG_CHUNK = 1 << HG_LEVELS
MOD_ROWS = 8
LANE = 128
BF16_SUBLANE = 16
MOE_ROUND = 64
VMEM_LIMIT = 62 * 1024 * 1024


def _cparams(*sem):
    return pltpu.CompilerParams(dimension_semantics=sem, vmem_limit_bytes=VMEM_LIMIT)


def _dot(a, b, **kw):
    return jnp.dot(a, b, preferred_element_type=F32, **kw)


def _dot_nt(a, b, **kw):
    return lax.dot_general(a, b, (((1,), (1,)), ((), ())), preferred_element_type=F32, **kw)


def _dot_tn(a, b, **kw):
    return lax.dot_general(a, b, (((0,), (0,)), ((), ())), preferred_element_type=F32, **kw)


def _silu(x):
    return x * jax.nn.sigmoid(x)


def _mod_kernel(c_ref, w_ref, b_ref, o_ref):
    s = _silu(c_ref[...])
    o_ref[...] = _dot(s, w_ref[...], precision=HIGHEST) + b_ref[...]


def _mod_call(cvec, ada_w, ada_b):
    depth, d, six_d = ada_w.shape
    tn = 6 * LANE * 2
    assert six_d % tn == 0
    return pl.pallas_call(
        _mod_kernel,
        grid=(depth, six_d // tn),
        in_specs=[
            pl.BlockSpec((MOD_ROWS, d), lambda l, j: (0, 0)),
            pl.BlockSpec((None, d, tn), lambda l, j: (l, 0, j)),
            pl.BlockSpec((None, 1, tn), lambda l, j: (l, 0, j)),
        ],
        out_specs=pl.BlockSpec((None, MOD_ROWS, tn), lambda l, j: (l, 0, j)),
        out_shape=jax.ShapeDtypeStruct((depth, MOD_ROWS, six_d), F32),
        compiler_params=_cparams("arbitrary", "arbitrary"),
        name="adaln_mod",
    )(cvec, ada_w, ada_b.reshape(depth, 1, six_d))


def _rms(x):
    return x * lax.rsqrt(jnp.mean(x * x, axis=-1, keepdims=True) + NORM_EPS)


def _norm_kernel(*refs, d, residual, final):
    it = iter(refs)
    x_ref = next(it)
    f_ref, gate_ref = (next(it), next(it)) if residual else (None, None)
    mod_ref = None if final else next(it)
    g_ref = next(it)
    xo_ref = next(it) if (residual and not final) else None
    h_ref = next(it)
    x = x_ref[...]
    if residual:
        x = x + gate_ref[...] * f_ref[...]
        if xo_ref is not None:
            xo_ref[...] = x
    y = _rms(x) * g_ref[...]
    if not final:
        y = y * (1.0 + mod_ref[:, d:2 * d]) + mod_ref[:, 0:d]
    h_ref[...] = y.astype(h_ref.dtype)


def _norm_call(x, ffn, mod_gate, mod_next, g, mod_row, tm, rows=None):
    n, d = x.shape
    residual = ffn is not None
    final = mod_next is None
    r0, n_out = (0, n) if rows is None else rows
    blk0 = r0 // tm
    row = pl.BlockSpec((tm, d), lambda i: (blk0 + i, 0))
    in_specs, args = [row], [x]
    if residual:
        in_specs += [row, pl.BlockSpec((None, 1, d), lambda i: (mod_row(blk0 + i), 0, 5))]
        args += [ffn, mod_gate]
    if not final:
        in_specs.append(pl.BlockSpec((None, 1, 2 * d), lambda i: (mod_row(blk0 + i), 0, 0)))
        args.append(mod_next)
    in_specs.append(pl.BlockSpec((1, d), lambda i: (0, 0)))
    args.append(g.reshape(1, d))
    out_row = pl.BlockSpec((tm, d), lambda i: (i, 0))
    out_specs, out_shape = [out_row], [jax.ShapeDtypeStruct((n_out, d), F32 if final else BF16)]
    if residual and not final:
        out_specs = [out_row, out_row]
        out_shape = [jax.ShapeDtypeStruct((n_out, d), F32)] + out_shape
    outs = pl.pallas_call(
        functools.partial(_norm_kernel, d=d, residual=residual, final=final),
        grid=(n_out // tm,),
        in_specs=in_specs,
        out_specs=out_specs,
        out_shape=out_shape,
        compiler_params=_cparams("arbitrary"),
        name="residual_norm",
    )(*args)
    return outs if len(outs) == 2 else (x, outs[0])


def _proj_kernel(h_ref, w_ref, o_ref, wb_ref):
    @pl.when(pl.program_id(1) == 0)
    def _():
        wb_ref[...] = w_ref[...].astype(BF16)

    o_ref[...] = _dot(h_ref[...], wb_ref[...])


def _proj_call(h, w, layer, tm, tn):
    n, d = h.shape
    cols = w.shape[-1]
    return pl.pallas_call(
        _proj_kernel,
        grid=(cols // tn, n // tm),
        in_specs=[
            pl.BlockSpec((tm, d), lambda j, i: (i, 0)),
            pl.BlockSpec((None, d, tn), lambda j, i: (layer, 0, j)),
        ],
        out_specs=pl.BlockSpec((tm, tn), lambda j, i: (i, j)),
        out_shape=jax.ShapeDtypeStruct((n, cols), F32),
        scratch_shapes=[pltpu.VMEM((d, tn), BF16)],
        compiler_params=_cparams("arbitrary", "arbitrary"),
        name="in_proj",
    )(h, w)


def _swap_half_pairs(x):
    lane = lax.broadcasted_iota(I32, x.shape, 1)
    return jnp.where((lane & 32) == 0, pltpu.roll(x, LANE - 32, 1), pltpu.roll(x, 32, 1))


def _ret_kernel(*refs, heads, dk, dv, chunk, n_chunks, has_init, has_rope):
    it = iter(refs)
    lg_ref = next(it)
    qkv = [[next(it) for _ in range(3)] for _ in range(2)]
    s0_ref = next(it) if has_init else None
    rope = [[next(it) for _ in range(2)] for _ in range(2)] if has_rope else None
    o_refs = [next(it), next(it)]
    sfin_ref = next(it)
    s_ref = next(it)

    c = pl.program_id(1)

    @pl.when(c == 0)
    def _():
        if has_init:
            s_ref[...] = s0_ref[...]
        else:
            s_ref[...] = jnp.zeros_like(s_ref)

    L = chunk
    ii = lax.broadcasted_iota(I32, (L, L), 0)
    jj = lax.broadcasted_iota(I32, (L, L), 1)
    pcol = lax.broadcasted_iota(I32, (L, 1), 0).astype(F32)
    jobs = [(d, hh) for d in range(2) for hh in range(heads)]
    qs, ks, vs, decays, lgs = [], [], [], [], []
    for d, hh in jobs:
        q_ref, k_ref, v_ref = qkv[d]
        q = q_ref[:, hh * dk:(hh + 1) * dk]
        k = k_ref[:, hh * dk:(hh + 1) * dk] * (dk ** -0.5)
        if has_rope:
            cos, sin = rope[d][0][...], rope[d][1][...]
            q = q * cos + _swap_half_pairs(q) * sin
            k = k * cos + _swap_half_pairs(k) * sin
        dif = ((ii - jj) if d == 0 else (jj - ii)).astype(F32)
        lg = lg_ref[d, hh]
        qs.append(q)
        ks.append(k)
        vs.append(v_ref[:, hh * dv:(hh + 1) * dv].astype(BF16))
        decays.append(jnp.where(dif >= 0.0, jnp.exp(lg * jnp.maximum(dif, 0.0)), 0.0))
        lgs.append(lg)
    scores = [(_dot_nt(q.astype(BF16), k.astype(BF16)) * dec).astype(BF16) for q, k, dec in zip(qs, ks, decays)]
    q_in, k_out = [], []
    for (d, hh), q, k, lg in zip(jobs, qs, ks, lgs):
        q_pow, k_pow = (pcol + 1.0, (L - 1.0) - pcol) if d == 0 else (L - pcol, pcol)
        q_in.append((q * jnp.exp(lg * q_pow)).astype(BF16))
        k_out.append((k * jnp.exp(lg * k_pow)).T.astype(BF16))
    s_old = [s_ref[d, hh] for d, hh in jobs]
    outs = [_dot(sc, v) + _dot(qi, s.astype(BF16)) for sc, v, qi, s in zip(scores, vs, q_in, s_old)]
    kvs = [_dot(ko, v) for ko, v in zip(k_out, vs)]
    for (d, hh), o, kv, s, lg in zip(jobs, outs, kvs, s_old, lgs):
        o_refs[d][:, hh * dv:(hh + 1) * dv] = o
        s_ref[d, hh] = s * jnp.exp(jnp.full((1, 1), lg * L, F32)) + kv

    @pl.when(c == n_chunks - 1)
    def _():
        sfin_ref[...] = s_ref[...]


def _ret_call(proj, log_gamma, s0, rope, *, tok0, batch, seq, heads, dk, dv):
    L = RET_CHUNK
    n = seq // L
    qw, vw = heads * dk, heads * dv
    assert seq % L == 0 and tok0 % L == 0 and vw % qw == 0
    r0 = tok0 // L

    def fwd(b, c):
        return r0 + b * n + c

    def bwd(b, c):
        return r0 + b * n + (n - 1 - c)

    in_specs = [pl.BlockSpec(memory_space=pltpu.SMEM)]
    args = [log_gamma]
    for rmap in (fwd, bwd):
        in_specs += [
            pl.BlockSpec((L, qw), lambda b, c, rmap=rmap: (rmap(b, c), 0)),
            pl.BlockSpec((L, qw), lambda b, c, rmap=rmap: (rmap(b, c), 1)),
            pl.BlockSpec((L, vw), lambda b, c, rmap=rmap: (rmap(b, c), (2 * qw) // vw)),
        ]
        args += [proj, proj, proj]
    state_spec = pl.BlockSpec((None, 2, heads, dk, dv), lambda b, c: (b, 0, 0, 0, 0))
    if s0 is not None:
        in_specs.append(state_spec)
        args.append(s0)
    if rope is not None:
        for cmap in (lambda b, c: (c, 0), lambda b, c: (n - 1 - c, 0)):
            in_specs += [pl.BlockSpec((L, dk), cmap), pl.BlockSpec((L, dk), cmap)]
            args += [rope[0], rope[1]]
    n_tok = batch * seq
    return pl.pallas_call(
        functools.partial(_ret_kernel, heads=heads, dk=dk, dv=dv, chunk=L, n_chunks=n,
                          has_init=s0 is not None, has_rope=rope is not None),
        grid=(batch, n),
        in_specs=in_specs,
        out_specs=[
            pl.BlockSpec((L, vw), lambda b, c: (b * n + c, 0)),
            pl.BlockSpec((L, vw), lambda b, c: (b * n + (n - 1 - c), 0)),
            state_spec,
        ],
        out_shape=[
            jax.ShapeDtypeStruct((n_tok, vw), F32),
            jax.ShapeDtypeStruct((n_tok, vw), F32),
            jax.ShapeDtypeStruct((batch, 2, heads, dk, dv), F32),
        ],
        scratch_shapes=[pltpu.VMEM((2, heads, dk, dv), F32)],
        compiler_params=_cparams("arbitrary", "arbitrary"),
        name="retention_scan",
    )(*args)


def _hg_scan_index(shape, axis, rev):
    i = lax.broadcasted_iota(I32, shape, axis) & (HG_CHUNK - 1)
    return (HG_CHUNK - 1 - i) if rev else i


def _hg_window_matrix(rev):
    C = HG_CHUNK
    ti = _hg_scan_index((C, 2 * C), 0, rev)
    tj = _hg_scan_index((C, 2 * C), 1, rev)
    blocks = [tj <= ti]
    for l in range(2, HG_LEVELS + 1):
        anchor = ((ti >> l) << l) + (1 << (l - 1)) - 1
        upper = ((ti >> (l - 1)) & 1) == 1
        blocks.append((upper & (tj > anchor) & (tj <= ti)) | (~upper & (tj > ti) & (tj <= anchor)))
    blocks.append(tj > ti)
    return jnp.concatenate([jnp.where(m, 1.0, 0.0) for m in blocks], axis=0).astype(BF16)


def _hg_kernel(*refs, heads, dk, dv, tt, n_blocks, has_init):
    it = iter(refs)
    qzv = [[next(it) for _ in range(3)] for _ in range(2)]
    lb_ref = next(it)
    s0_ref = next(it) if has_init else None
    o_refs = [next(it), next(it)]
    sfin_ref = next(it)
    st_ref = next(it)
    win_ref = next(it)

    t = pl.program_id(1)

    @pl.when((pl.program_id(0) == 0) & (t == 0))
    def _():
        for d in range(2):
            win_ref[d] = _hg_window_matrix(d == 1)

    @pl.when(t == 0)
    def _():
        for d in range(2):
            for hh in range(heads):
                st_ref[d, hh] = s0_ref[d, hh].T if has_init else jnp.zeros((dv, dk), F32)

    C = HG_CHUNK
    n_ch = tt // C
    chunks = [slice(c * C, (c + 1) * C) for c in range(n_ch)]
    levels, uppers = [], []
    for d in range(2):
        ti = _hg_scan_index((C, C), 0, d == 1)
        tj = _hg_scan_index((C, C), 1, d == 1)
        level = jnp.where(tj < ti, 1, 0)
        for l in range(1, HG_LEVELS):
            level = level + jnp.where((tj < ti) & ((ti >> l) != (tj >> l)), 1, 0)
        t_col = _hg_scan_index((C, 1), 0, d == 1)
        levels.append(level)
        uppers.append([None] + [((t_col >> (l - 1)) & 1) == 1 for l in range(1, HG_LEVELS + 1)])

    def head_body(hh, carry):
        ck = pl.ds(pl.multiple_of(hh * dk, LANE), dk)
        cv = pl.ds(pl.multiple_of(hh * dv, LANE), dv)
        jobs = [(d, r) for d in range(2) for r in chunks]
        qg, k, f, hi, lo, v, vb, diag = [], [], [], [], [], [], [], []
        for d in range(2):
            q_ref, z_ref, v_ref = qzv[d]
            lb = lb_ref[d, :, ck]
            z = z_ref[:, ck]
            e = jnp.exp(-jnp.abs(z))
            den = 1.0 + e
            num = jnp.where(z >= 0.0, 1.0 + lb * e, e + lb)
            log_num = jnp.where((z < 0.0) & (lb <= 0.0), z, jnp.log(num))
            log_f = log_num - jnp.log(den)
            k.append((1.0 - lb) * (jnp.where(z > 0.0, e, 1.0) / den))
            f.append(1.0 - k[d])
            log2_f = log_f * LOG2_E
            hi.append(log2_f.astype(BF16))
            lo.append((log2_f - hi[d].astype(F32)).astype(BF16))
            qg.append(_silu(q_ref[:, ck]) * (dk ** -0.5))
            v.append(v_ref[:, cv])
            vb.append(v[d].astype(BF16))
            diag.append(jnp.sum(qg[d] * k[d], axis=-1, keepdims=True))
        wins = [win_ref[0], win_ref[1]]
        sums = [_dot(wins[d], jnp.concatenate([hi[d][r], lo[d][r]], axis=0)) for d, r in jobs]
        pairs = []
        for (d, r), s in zip(jobs, sums):
            ps = [jnp.where(uppers[d][1], qg[d][r] * f[d][r], k[d][r]).astype(BF16)]
            for l in range(2, HG_LEVELS + 1):
                scale = jnp.exp2(s[(l - 1) * C:l * C])
                ps.append((jnp.where(uppers[d][l], qg[d][r], k[d][r]) * scale).astype(BF16))
            pairs.append(ps)
        grams = [[_dot_nt(p, p) for p in ps] for ps in pairs]
        atts = []
        for (d, r), gs in zip(jobs, grams):
            att = jnp.where(levels[d] == 1, gs[0], 0.0)
            for l in range(2, HG_LEVELS + 1):
                att = jnp.where(levels[d] == l, gs[l - 1], att)
            atts.append(att.astype(BF16))
        k_out = [(k[d][r] * jnp.exp2(s[HG_LEVELS * C:])).astype(BF16) for (d, r), s in zip(jobs, sums)]
        q_in = [(qg[d][r] * jnp.exp2(s[:C])).astype(BF16) for (d, r), s in zip(jobs, sums)]
        decays = [jnp.exp2(s[0:1] if d == 1 else s[C - 1:C]) for (d, r), s in zip(jobs, sums)]
        outs = [_dot(att, vb[d][r]) + diag[d][r] * v[d][r] for att, (d, r) in zip(atts, jobs)]
        kvs = [_dot_tn(vb[d][r], ko) for (d, r), ko in zip(jobs, k_out)]
        for d in range(2):
            st = st_ref[d, hh]
            for c in (range(n_ch - 1, -1, -1) if d == 1 else range(n_ch)):
                j = d * n_ch + c
                o_refs[d][chunks[c], cv] = outs[j] + _dot_nt(q_in[j], st.astype(BF16))
                st = st * decays[j] + kvs[j]
            st_ref[d, hh] = st
        return carry

    lax.fori_loop(0, heads, head_body, 0)

    @pl.when(t == n_blocks - 1)
    def _():
        for d in range(2):
            for hh in range(heads):
                sfin_ref[d, hh] = st_ref[d, hh].T


def _hg_call(proj, lb, s0, *, tok0, batch, seq, heads, dk, dv, col_q, col_f, col_i):
    tt = min(seq, 256)
    n = seq // tt
    kw, vw = heads * dk, heads * dv
    assert seq % tt == 0 and tok0 % tt == 0 and tt % HG_CHUNK == 0
    assert col_q % kw == 0 and col_f % kw == 0 and col_i % vw == 0
    r0 = tok0 // tt

    def fwd(b, t):
        return r0 + b * n + t

    def bwd(b, t):
        return r0 + b * n + (n - 1 - t)

    in_specs, args = [], []
    for d, rmap in enumerate((fwd, bwd)):
        in_specs += [
            pl.BlockSpec((tt, kw), lambda b, t, rmap=rmap: (rmap(b, t), col_q // kw)),
            pl.BlockSpec((tt, kw), lambda b, t, rmap=rmap, d=d: (rmap(b, t), col_f // kw + d)),
            pl.BlockSpec((tt, vw), lambda b, t, rmap=rmap: (rmap(b, t), col_i // vw)),
        ]
        args += [proj, proj, proj]
    in_specs.append(pl.BlockSpec((2, 1, kw), lambda b, t: (0, 0, 0)))
    args.append(lb.reshape(2, 1, kw))
    state_spec = pl.BlockSpec((None, 2, heads, dk, dv), lambda b, t: (b, 0, 0, 0, 0))
    if s0 is not None:
        in_specs.append(state_spec)
        args.append(s0)
    n_tok = batch * seq
    return pl.pallas_call(
        functools.partial(_hg_kernel, heads=heads, dk=dk, dv=dv, tt=tt, n_blocks=n, has_init=s0 is not None),
        grid=(batch, n),
        in_specs=in_specs,
        out_specs=[
            pl.BlockSpec((tt, vw), lambda b, t: (b * n + t, 0)),
            pl.BlockSpec((tt, vw), lambda b, t: (b * n + (n - 1 - t), 0)),
            state_spec,
        ],
        out_shape=[
            jax.ShapeDtypeStruct((n_tok, vw), F32),
            jax.ShapeDtypeStruct((n_tok, vw), F32),
            jax.ShapeDtypeStruct((batch, 2, heads, dk, dv), F32),
        ],
        scratch_shapes=[pltpu.VMEM((2, heads, dv, dk), F32),
                        pltpu.VMEM((2, (HG_LEVELS + 1) * HG_CHUNK, 2 * HG_CHUNK), BF16)],
        compiler_params=_cparams("arbitrary", "arbitrary"),
        name="hgrn2_scan",
    )(*args)


def _mix_out_kernel(*refs, d, ret_heads, hg_heads, ctx_blocks):
    scans = [refs[4 * p:4 * p + 4] for p in range(2)]
    (rg_ref, go_ref, h_ref, x_ref, mod_ref, g2_ref, wr_ref, wh_ref, wm_ref, wo_ref, wrt_ref,
     x1_ref, h2_ref, lgt_ref, o_ref, og_ref) = refs[8:]

    for p, in_set in enumerate((pl.program_id(0) < ctx_blocks, pl.program_id(0) >= ctx_blocks)):
        @pl.when(in_set)
        def _(p=p):
            o_ref[...] = scans[p][0][...] + scans[p][1][...]
            og_ref[...] = scans[p][2][...] + scans[p][3][...]

    o = o_ref[...]
    dvr = o.shape[1] // ret_heads
    parts = []
    for hh in range(ret_heads):
        oh = o[:, hh * dvr:(hh + 1) * dvr]
        mu = jnp.mean(oh, axis=-1, keepdims=True)
        ctr = oh - mu
        parts.append(ctr * lax.rsqrt(jnp.mean(ctr * ctr, axis=-1, keepdims=True) + NORM_EPS))
    ret_in = (jnp.concatenate(parts, axis=1) * _silu(rg_ref[...])).astype(BF16)
    ret_out = _dot(ret_in, wr_ref[...])

    og = og_ref[...]
    dvh = og.shape[1] // hg_heads
    parts = [_rms(og[:, hh * dvh:(hh + 1) * dvh]) for hh in range(hg_heads)]
    hg_in = (jnp.concatenate(parts, axis=1) * _silu(go_ref[...])).astype(BF16)
    hg_out = _dot(hg_in, wh_ref[...])

    gates = jax.nn.sigmoid(_dot(h_ref[...], wm_ref[...]))
    merged = gates[:, :d] * ret_out + gates[:, d:] * hg_out
    mix = _dot(merged.astype(BF16), wo_ref[...])
    x1 = x_ref[...] + mod_ref[:, 2 * d:3 * d] * mix
    x1_ref[...] = x1
    h2 = _rms(x1) * g2_ref[...]
    h2 = h2 * (1.0 + mod_ref[:, 4 * d:5 * d]) + mod_ref[:, 3 * d:4 * d]
    h2_ref[...] = h2.astype(BF16)
    lgt_ref[...] = _dot_nt(wrt_ref[...], h2, precision=HIGHEST)


def _mix_out_call(scans_ctx, scans_lat, proj, h, x, mod_l, mod_row, g2, w_ret_o, w_hg_o, w_merge, w_out,
                  w_router_t, tm, *, ret_heads, hg_heads, col_rg, col_go):
    n, d = x.shape
    vr, vh = scans_ctx[0].shape[1], scans_ctx[2].shape[1]
    ne = w_router_t.shape[0]
    ctx_blocks = scans_ctx[0].shape[0] // tm
    assert col_rg % vr == 0 and col_go % vh == 0

    def row(w):
        return pl.BlockSpec((tm, w), lambda i: (i, 0))

    def full(a):
        return pl.BlockSpec(a.shape, lambda i: (0,) * a.ndim)

    ctx_row = lambda a: pl.BlockSpec((tm, a.shape[1]), lambda i: (jnp.minimum(i, ctx_blocks - 1), 0))
    lat_row = lambda a: pl.BlockSpec((tm, a.shape[1]), lambda i: (jnp.maximum(i - ctx_blocks, 0), 0))
    return pl.pallas_call(
        functools.partial(_mix_out_kernel, d=d, ret_heads=ret_heads, hg_heads=hg_heads, ctx_blocks=ctx_blocks),
        grid=(n // tm,),
        in_specs=[ctx_row(a) for a in scans_ctx] + [lat_row(a) for a in scans_lat] + [
            pl.BlockSpec((tm, vr), lambda i: (i, col_rg // vr)),
            pl.BlockSpec((tm, vh), lambda i: (i, col_go // vh)),
            row(d), row(d),
            pl.BlockSpec((None, 1, 6 * d), lambda i: (mod_row(i), 0, 0)),
            pl.BlockSpec((1, d), lambda i: (0, 0)),
            full(w_ret_o), full(w_hg_o), full(w_merge), full(w_out), full(w_router_t),
        ],
        out_specs=[row(d), row(d), pl.BlockSpec((ne, tm), lambda i: (0, i))],
        out_shape=[
            jax.ShapeDtypeStruct((n, d), F32),
            jax.ShapeDtypeStruct((n, d), BF16),
            jax.ShapeDtypeStruct((ne, n), F32),
        ],
        scratch_shapes=[pltpu.VMEM((tm, vr), F32), pltpu.VMEM((tm, vh), F32)],
        compiler_params=_cparams("arbitrary"),
        name="mixer_out",
    )(*scans_ctx, *scans_lat, proj, proj, h, x, mod_l, g2.reshape(1, d), w_ret_o, w_hg_o, w_merge, w_out,
      w_router_t)


def _route_kernel(lg_ref, aff_ref, sel_ref, posx_ref, flag_ref, *, cap, n):
    lg = lg_ref[...]
    ne = lg.shape[0]
    ex = jnp.exp(lg - jnp.max(lg, axis=0, keepdims=True))
    aff = ex / jnp.sum(ex, axis=0, keepdims=True)
    aff_ref[...] = aff
    bits = pltpu.bitcast(aff, I32)

    def search(i, lo):
        cand = lo | lax.shift_left(jnp.int32(1), 30 - i)
        cnt = jnp.sum(jnp.where(bits >= cand, 1.0, 0.0), axis=1, keepdims=True)
        return jnp.where(cnt >= cap, cand, lo)

    thr = lax.fori_loop(0, 31, search, jnp.zeros((ne, 1), I32))
    gt = bits > thr
    eq = bits == thr
    need = cap - jnp.sum(jnp.where(gt, 1.0, 0.0), axis=1, keepdims=True)
    upper = jnp.where(lax.broadcasted_iota(I32, (LANE, LANE), 0) < lax.broadcasted_iota(I32, (LANE, LANE), 1),
                      1.0, 0.0).astype(BF16)

    def exclusive_count(store):
        def body(ci, carry):
            sl = pl.ds(pl.multiple_of(ci * LANE, LANE), LANE)
            x = flag_ref[:, sl]
            store(sl, carry + _dot(x.astype(BF16), upper))
            return carry + jnp.sum(x, axis=1, keepdims=True)

        lax.fori_loop(0, n // LANE, body, jnp.zeros((ne, 1), F32))

    flag_ref[...] = jnp.where(eq, 1.0, 0.0)

    def store_tie_rank(sl, rank):
        posx_ref[:, sl] = rank.astype(I32)

    exclusive_count(store_tie_rank)
    sel = gt | (eq & (posx_ref[...].astype(F32) < need))
    sel_ref[...] = jnp.where(sel, 1, 0).astype(I32)
    flag_ref[...] = jnp.where(sel, 1.0, 0.0)

    def store_pos(sl, cnt):
        posx_ref[:, sl] = cnt.astype(I32)

    exclusive_count(store_pos)


def _route_call(logits_t, cap):
    ne, n = logits_t.shape
    spec = pl.BlockSpec((ne, n), lambda: (0, 0))
    return pl.pallas_call(
        functools.partial(_route_kernel, cap=cap, n=n),
        in_specs=[spec],
        out_specs=[spec, spec, spec],
        out_shape=[
            jax.ShapeDtypeStruct((ne, n), F32),
            jax.ShapeDtypeStruct((ne, n), I32),
            jax.ShapeDtypeStruct((ne, n), I32),
        ],
        scratch_shapes=[pltpu.VMEM((ne, n), F32)],
        compiler_params=pltpu.CompilerParams(vmem_limit_bytes=VMEM_LIMIT),
        name="expert_choice_route",
    )(logits_t)


def _moe_round_masks(b, r, base_ref, sel_ref, posx_ref, ne, value_rows=None):
    n_tok = sel_ref.shape[1]
    row_iota = lax.broadcasted_iota(I32, (MOE_ROUND, n_tok), 0)
    pieces = []
    for e in range(ne):
        local = posx_ref[e:e + 1, :] - (base_ref[b * ne + e] + r * MOE_ROUND)
        hit = (sel_ref[e:e + 1, :] == 1) & (local == row_iota)
        value = 1.0 if value_rows is None else value_rows[e:e + 1, :]
        pieces.append(jnp.where(hit, value, 0.0).astype(BF16))
    return jnp.concatenate(pieces, axis=0)


def _dispatch_kernel(base_ref, start_ref, pc_ref, nr_ref, h_ref, sel_ref, posx_ref, xe_zero_ref, xe_ref,
                     stage_ref, sem, *, ne):
    del xe_zero_ref
    b = pl.program_id(0)
    n_rounds = nr_ref[b]

    def pieces(r, slot, fn):
        for e in range(ne):
            rem = pc_ref[b * ne + e] - r * MOE_ROUND
            dst = start_ref[b * ne + e] + r * MOE_ROUND
            for size, cond, off in (
                    (MOE_ROUND, rem >= MOE_ROUND, 0),
                    (32, (rem > 0) & (rem < MOE_ROUND) & ((rem & 32) != 0), 0),
                    (16, (rem > 0) & (rem < MOE_ROUND) & ((rem & 16) != 0), rem & 32)):
                @pl.when(cond)
                def _(size=size, off=off, e=e, dst=dst):
                    src_rows = pl.ds(pl.multiple_of(e * MOE_ROUND + off, BF16_SUBLANE), size)
                    dst_rows = pl.ds(pl.multiple_of(dst + off, BF16_SUBLANE), size)
                    fn(pltpu.make_async_copy(stage_ref.at[slot, src_rows], xe_ref.at[e, dst_rows], sem.at[slot]))

    def round_body(r, carry):
        slot = r % 2
        onehot = _moe_round_masks(b, r, base_ref, sel_ref, posx_ref, ne)
        stage_ref[slot] = _dot(onehot, h_ref[...]).astype(BF16)

        @pl.when(r >= 1)
        def _():
            pieces(r - 1, 1 - slot, lambda cp: cp.wait())

        pieces(r, slot, lambda cp: cp.start())
        return carry

    lax.fori_loop(0, n_rounds, round_body, 0)

    @pl.when(n_rounds >= 1)
    def _():
        pieces(n_rounds - 1, (n_rounds - 1) % 2, lambda cp: cp.wait())


def _dispatch_call(h2, sel, posx, tables, list_rows, *, sb):
    n, d = h2.shape
    ne = sel.shape[0]
    xe_zero = jnp.zeros((ne, list_rows, d), BF16)
    grid_spec = pltpu.PrefetchScalarGridSpec(
        num_scalar_prefetch=4,
        grid=(n // sb,),
        in_specs=[
            pl.BlockSpec((sb, d), lambda b, *_: (b, 0)),
            pl.BlockSpec((ne, sb), lambda b, *_: (0, b)),
            pl.BlockSpec((ne, sb), lambda b, *_: (0, b)),
            pl.BlockSpec(memory_space=pl.ANY),
        ],
        out_specs=pl.BlockSpec(memory_space=pl.ANY),
        scratch_shapes=[pltpu.VMEM((2, ne * MOE_ROUND, d), BF16), pltpu.SemaphoreType.DMA((2,))],
    )
    return pl.pallas_call(
        functools.partial(_dispatch_kernel, ne=ne),
        grid_spec=grid_spec,
        out_shape=jax.ShapeDtypeStruct((ne, list_rows, d), BF16),
        input_output_aliases={7: 0},
        compiler_params=_cparams("arbitrary"),
        name="expert_dispatch",
    )(*tables, h2, sel, posx, xe_zero)


def _expert_ffn_kernel(tot_ref, x_ref, wg_ref, wu_ref, wd_ref, y_ref, wgb_ref, wub_ref, wdb_ref, acc_ref,
                       *, rt, n_ff):
    e, f, t = pl.program_id(0), pl.program_id(1), pl.program_id(2)

    @pl.when(t == 0)
    def _():
        wgb_ref[...] = wg_ref[...].astype(BF16)
        wub_ref[...] = wu_ref[...].astype(BF16)
        wdb_ref[...] = wd_ref[...].astype(BF16)

    rows = pl.ds(pl.multiple_of(t * rt, rt), rt)
    live = t * rt < tot_ref[e]

    @pl.when(live)
    def _():
        x = x_ref[...]
        hid = (_silu(_dot(x, wgb_ref[...])) * _dot(x, wub_ref[...])).astype(BF16)
        y = _dot(hid, wdb_ref[...])

        @pl.when(f == 0)
        def _():
            acc_ref[rows, :] = y

        @pl.when(f != 0)
        def _():
            acc_ref[rows, :] += y

    @pl.when((f == n_ff - 1) & live)
    def _():
        y_ref[...] = acc_ref[rows, :].astype(BF16)

    @pl.when((f == n_ff - 1) & jnp.logical_not(live))
    def _():
        y_ref[...] = jnp.zeros_like(y_ref)


def _expert_ffn_call(xe, totals, wg, wu, wd, layer, *, rt, fc):
    ne, list_rows, d = xe.shape
    ff = wg.shape[-1]
    n_ff = ff // fc
    grid_spec = pltpu.PrefetchScalarGridSpec(
        num_scalar_prefetch=1,
        grid=(ne, n_ff, list_rows // rt),
        in_specs=[
            pl.BlockSpec((None, rt, d), lambda e, f, t, *_: (e, t, 0)),
            pl.BlockSpec((None, None, d, fc), lambda e, f, t, *_: (layer, e, 0, f)),
            pl.BlockSpec((None, None, d, fc), lambda e, f, t, *_: (layer, e, 0, f)),
            pl.BlockSpec((None, None, fc, d), lambda e, f, t, *_: (layer, e, f, 0)),
        ],
        out_specs=pl.BlockSpec((None, rt, d), lambda e, f, t, *_: (e, jnp.where(f == n_ff - 1, t, 0), 0)),
        scratch_shapes=[pltpu.VMEM((d, fc), BF16), pltpu.VMEM((d, fc), BF16), pltpu.VMEM((fc, d), BF16),
                        pltpu.VMEM((list_rows, d), F32)],
    )
    return pl.pallas_call(
        functools.partial(_expert_ffn_kernel, rt=rt, n_ff=n_ff),
        grid_spec=grid_spec,
        out_shape=jax.ShapeDtypeStruct((ne, list_rows, d), BF16),
        compiler_params=_cparams("arbitrary", "arbitrary", "arbitrary"),
        name="expert_ffn",
    )(totals, xe, wg, wu, wd)


def _combine_kernel(base_ref, start_ref, pc_ref, nr_ref, sel_ref, posx_ref, aff_ref, y_ref, out_ref, ybuf_ref, sem,
                    *, ne):
    b = pl.program_id(0)
    n_rounds = nr_ref[b]

    @pl.when(b == 0)
    def _():
        ybuf_ref[...] = jnp.zeros_like(ybuf_ref)

    def windows(r, slot, fn):
        for e in range(ne):
            @pl.when(pc_ref[b * ne + e] - r * MOE_ROUND > 0)
            def _(e=e):
                src = pl.ds(pl.multiple_of(start_ref[b * ne + e] + r * MOE_ROUND, BF16_SUBLANE), MOE_ROUND)
                fn(pltpu.make_async_copy(y_ref.at[e, src], ybuf_ref.at[slot, pl.ds(e * MOE_ROUND, MOE_ROUND)],
                                         sem.at[slot]))

    out_ref[...] = jnp.zeros_like(out_ref)

    @pl.when(n_rounds >= 1)
    def _():
        windows(0, 0, lambda cp: cp.start())

    def round_body(r, carry):
        slot = r % 2

        @pl.when(r + 1 < n_rounds)
        def _():
            windows(r + 1, 1 - slot, lambda cp: cp.start())

        windows(r, slot, lambda cp: cp.wait())
        weights = _moe_round_masks(b, r, base_ref, sel_ref, posx_ref, ne, value_rows=aff_ref)
        out_ref[...] += _dot_tn(weights, ybuf_ref[slot])
        return carry

    lax.fori_loop(0, n_rounds, round_body, 0)


def _combine_call(ye, sel, posx, aff, tables, *, sb):
    ne, _, d = ye.shape
    n = sel.shape[1]
    grid_spec = pltpu.PrefetchScalarGridSpec(
        num_scalar_prefetch=4,
        grid=(n // sb,),
        in_specs=[
            pl.BlockSpec((ne, sb), lambda b, *_: (0, b)),
            pl.BlockSpec((ne, sb), lambda b, *_: (0, b)),
            pl.BlockSpec((ne, sb), lambda b, *_: (0, b)),
            pl.BlockSpec(memory_space=pl.ANY),
        ],
        out_specs=pl.BlockSpec((sb, d), lambda b, *_: (b, 0)),
        scratch_shapes=[pltpu.VMEM((2, ne * MOE_ROUND, d), BF16), pltpu.SemaphoreType.DMA((2,))],
    )
    return pl.pallas_call(
        functools.partial(_combine_kernel, ne=ne),
        grid_spec=grid_spec,
        out_shape=jax.ShapeDtypeStruct((n, d), F32),
        compiler_params=_cparams("arbitrary"),
        name="expert_combine",
    )(*tables, sel, posx, aff, ye)


def _moe_tables(posx_sets, caps, sb):
    base, cnt = [], []
    for posx, cap in zip(posx_sets, caps):
        ne = posx.shape[0]
        edges = jnp.concatenate([posx[:, ::sb], jnp.full((ne, 1), cap, I32)], axis=1)
        base.append(edges[:, :-1].T)
        cnt.append((edges[:, 1:] - edges[:, :-1]).T)
    base = jnp.concatenate(base, axis=0)
    cnt = jnp.concatenate(cnt, axis=0)
    pc = (cnt + (BF16_SUBLANE - 1)) // BF16_SUBLANE * BF16_SUBLANE
    ends = jnp.cumsum(pc, axis=0)
    start = ends - pc
    n_rounds = jnp.max((pc + (MOE_ROUND - 1)) // MOE_ROUND, axis=1)
    tables = tuple(a.reshape(-1).astype(I32) for a in (base, start, pc, n_rounds))
    return tables, ends[-1].astype(I32)


def _rope_tables(n_tokens, dk):
    rows = n_tokens // GRID_W
    r, c = jnp.meshgrid(jnp.arange(rows), jnp.arange(GRID_W), indexing="ij")
    pos = jnp.stack([r.reshape(-1), c.reshape(-1)], axis=-1).astype(F32)
    nf = dk // 4
    inv_freq = ROPE_BASE ** (-jnp.arange(nf, dtype=F32) / nf)
    ang = pos[:, :, None] * inv_freq
    cos, sin = jnp.cos(ang), jnp.sin(ang)
    cos_t = jnp.concatenate([cos[:, 0], cos[:, 0], cos[:, 1], cos[:, 1]], axis=-1)
    sin_t = jnp.concatenate([-sin[:, 0], sin[:, 0], -sin[:, 1], sin[:, 1]], axis=-1)
    return cos_t, sin_t


def kernel(x_prompt, x_sample, state_ret, state_hgrn, c, c_ctx, ada_w, ada_b, norm_mix_g, norm_ffn_g, w_in,
           ret_gamma_logit, hg_lb_logit, w_ret_o, w_hg_o, w_merge, w_out, w_router, w_exp_gate, w_exp_up,
           w_exp_down, final_g):
    b_ctx, t_ctx, d = x_prompt.shape
    b_lat, t_lat, _ = x_sample.shape
    depth = w_in.shape[0]
    ret_heads, ret_dk, ret_dv = state_ret.shape[3:]
    hg_heads, hg_dk, hg_dv = state_hgrn.shape[3:]
    ne = w_router.shape[-1]
    n_ctx, n_lat = b_ctx * t_ctx, b_lat * t_lat
    n_tok = n_ctx + n_lat
    qw, vw = ret_heads * ret_dk, ret_heads * ret_dv
    kw, hw = hg_heads * hg_dk, hg_heads * hg_dv
    col_rg = 2 * qw + vw
    col_gq = col_rg + vw
    col_gf = col_gq + kw
    col_gi = col_gf + 2 * kw
    col_go = col_gi + hw
    assert col_go + hw == w_in.shape[-1] and b_lat < MOD_ROWS

    tm = 256
    assert t_ctx % tm == 0 and t_lat % tm == 0
    tm_proj = 1024 if n_tok % 1024 == 0 else tm
    sb = 1024 if (n_ctx % 4096 == 0 and n_lat % 4096 == 0) else 256
    assert n_ctx % sb == 0 and n_lat % sb == 0
    sets = ((0, n_ctx), (n_ctx, n_lat))
    caps = [CAPACITY_FACTOR * n_set // ne for _, n_set in sets]
    rt = 512
    list_rows = -(-(sum(caps) + BF16_SUBLANE * (n_tok // sb)) // rt) * rt

    def mod_row(i):
        return jnp.where(i < n_ctx // tm, 0, 1 + (i - n_ctx // tm) // (t_lat // tm))

    x = jnp.concatenate([x_prompt.reshape(n_ctx, d), x_sample.reshape(n_lat, d)], axis=0)
    cvec = jnp.zeros((MOD_ROWS, d), F32).at[0].set(c_ctx).at[1:1 + b_lat].set(c)
    mod = _mod_call(cvec, ada_w, ada_b).reshape(depth, MOD_ROWS, 1, 6 * d)

    log_gamma = jax.nn.log_sigmoid(ret_gamma_logit.astype(F32))
    p_lb = jax.nn.softmax(hg_lb_logit.astype(F32), axis=0)
    hg_lb = jnp.clip(jnp.cumsum(p_lb, axis=0) - p_lb[0:1], 0.0, 1.0 - 1e-6)
    rope = _rope_tables(t_lat, ret_dk)

    w_ret_o_b, w_hg_o_b, w_out_b = w_ret_o.astype(BF16), w_hg_o.astype(BF16), w_out.astype(BF16)
    w_merge_b = w_merge.astype(BF16)
    w_router_t = jnp.swapaxes(w_router, 1, 2)
    fc = min(1024, w_exp_gate.shape[-1])
    tn = min(2048, w_in.shape[-1])

    _, h = _norm_call(x, None, None, mod[0], norm_mix_g[0], mod_row, tm)
    new_ret, new_hg = [], []
    for l in range(depth):
        proj = _proj_call(h, w_in, l, tm_proj, tn)

        ret_kw = dict(heads=ret_heads, dk=ret_dk, dv=ret_dv)
        orf_c, orb_c, s_ret = _ret_call(proj, log_gamma[l], None, None, tok0=0, batch=b_ctx, seq=t_ctx, **ret_kw)
        orf_l, orb_l, _ = _ret_call(proj, log_gamma[l], state_ret[:, l], rope, tok0=n_ctx, batch=b_lat, seq=t_lat,
                                    **ret_kw)
        hg_kw = dict(heads=hg_heads, dk=hg_dk, dv=hg_dv, col_q=col_gq, col_f=col_gf, col_i=col_gi)
        ogf_c, ogb_c, s_hg = _hg_call(proj, hg_lb[l], None, tok0=0, batch=b_ctx, seq=t_ctx, **hg_kw)
        ogf_l, ogb_l, _ = _hg_call(proj, hg_lb[l], state_hgrn[:, l], tok0=n_ctx, batch=b_lat, seq=t_lat, **hg_kw)
        new_ret.append(s_ret)
        new_hg.append(s_hg)

        x, h2, logits_t = _mix_out_call(
            (orf_c, orb_c, ogf_c, ogb_c), (orf_l, orb_l, ogf_l, ogb_l), proj, h, x, mod[l],
            mod_row, norm_ffn_g[l], w_ret_o_b[l], w_hg_o_b[l], w_merge_b[l], w_out_b[l], w_router_t[l], tm,
            ret_heads=ret_heads, hg_heads=hg_heads, col_rg=col_rg, col_go=col_go)

        routed = [_route_call(logits_t[:, lo:lo + n_set], cap) for (lo, n_set), cap in zip(sets, caps)]
        aff, sel, posx = (jnp.concatenate([r[i] for r in routed], axis=1) for i in range(3))
        tables, totals = _moe_tables([r[2] for r in routed], caps, sb)
        xe = _dispatch_call(h2, sel, posx, tables, list_rows, sb=sb)
        ye = _expert_ffn_call(xe, totals, w_exp_gate, w_exp_up, w_exp_down, l, rt=rt, fc=fc)
        ffn = _combine_call(ye, sel, posx, aff, tables, sb=sb)

        if l + 1 < depth:
            x, h = _norm_call(x, ffn, mod[l], mod[l + 1], norm_mix_g[l + 1], mod_row, tm)
        else:
            _, y_prompt = _norm_call(x, ffn, mod[l], None, final_g, mod_row, tm, rows=(0, n_ctx))
            _, y_sample = _norm_call(x, ffn, mod[l], None, final_g, mod_row, tm, rows=(n_ctx, n_lat))

    y_prompt = y_prompt.reshape(b_ctx, t_ctx, d)
    y_sample = y_sample.reshape(b_lat, t_lat, d)
    return y_prompt, y_sample, jnp.stack(new_ret, axis=1), jnp.stack(new_hg, axis=1)
```

```python
import functools

import jax
import jax.numpy as jnp
from jax import lax
from jax.experimental import pallas as pl
from jax.experimental.pallas import tpu as pltpu

F32 = jnp.float32
BF16 = jnp.bfloat16
I32 = jnp.int32
HIGHEST = lax.Precision.HIGHEST

NORM_EPS = 1e-6
LOG2_E = 1.4426950408889634
ROPE_BASE = 10000.0
GRID_W = 64
CAPACITY_FACTOR = 2
RET_CHUNK = 128
RET_CHUNKS_PER_STEP = 2
HG_LEVELS = 6
HG_CHUNK = 1 << HG_LEVELS
MOD_ROWS = 8
LANE = 128
BF16_SUBLANE = 16
MOE_ROUND = 64
VMEM_LIMIT = 62 * 1024 * 1024


def _cparams(*sem):
    return pltpu.CompilerParams(dimension_semantics=sem, vmem_limit_bytes=VMEM_LIMIT)


def _dot(a, b, **kw):
    return jnp.dot(a, b, preferred_element_type=F32, **kw)


def _dot_nt(a, b, **kw):
    return lax.dot_general(a, b, (((1,), (1,)), ((), ())), preferred_element_type=F32, **kw)


def _dot_tn(a, b, **kw):
    return lax.dot_general(a, b, (((0,), (0,)), ((), ())), preferred_element_type=F32, **kw)


def _silu(x):
    return x * jax.nn.sigmoid(x)


def _mod_kernel(c_ref, w_ref, b_ref, o_ref):
    s = _silu(c_ref[...])
    o_ref[...] = _dot(s, w_ref[...], precision=HIGHEST) + b_ref[...]


def _mod_call(cvec, ada_w, ada_b):
    depth, d, six_d = ada_w.shape
    tn = 6 * LANE * 2
    assert six_d % tn == 0
    return pl.pallas_call(
        _mod_kernel,
        grid=(depth, six_d // tn),
        in_specs=[
            pl.BlockSpec((MOD_ROWS, d), lambda l, j: (0, 0)),
            pl.BlockSpec((None, d, tn), lambda l, j: (l, 0, j)),
            pl.BlockSpec((None, 1, tn), lambda l, j: (l, 0, j)),
        ],
        out_specs=pl.BlockSpec((None, MOD_ROWS, tn), lambda l, j: (l, 0, j)),
        out_shape=jax.ShapeDtypeStruct((depth, MOD_ROWS, six_d), F32),
        compiler_params=_cparams("arbitrary", "arbitrary"),
        name="adaln_mod",
    )(cvec, ada_w, ada_b.reshape(depth, 1, six_d))


def _rms(x):
    return x * lax.rsqrt(jnp.mean(x * x, axis=-1, keepdims=True) + NORM_EPS)


def _norm_kernel(*refs, d, residual, final):
    it = iter(refs)
    x_ref = next(it)
    f_ref, gate_ref = (next(it), next(it)) if residual else (None, None)
    mod_ref = None if final else next(it)
    g_ref = next(it)
    xo_ref = next(it) if (residual and not final) else None
    h_ref = next(it)
    x = x_ref[...]
    if residual:
        x = x + gate_ref[...] * f_ref[...]
        if xo_ref is not None:
            xo_ref[...] = x
    y = _rms(x) * g_ref[...]
    if not final:
        y = y * (1.0 + mod_ref[:, d:2 * d]) + mod_ref[:, 0:d]
    h_ref[...] = y.astype(h_ref.dtype)


def _norm_call(x, ffn, mod_gate, mod_next, g, mod_row, tm, rows=None):
    n, d = x.shape
    residual = ffn is not None
    final = mod_next is None
    r0, n_out = (0, n) if rows is None else rows
    blk0 = r0 // tm
    row = pl.BlockSpec((tm, d), lambda i: (blk0 + i, 0))
    in_specs, args = [row], [x]
    if residual:
        in_specs += [row, pl.BlockSpec((None, 1, d), lambda i: (mod_row(blk0 + i), 0, 5))]
        args += [ffn, mod_gate]
    if not final:
        in_specs.append(pl.BlockSpec((None, 1, 2 * d), lambda i: (mod_row(blk0 + i), 0, 0)))
        args.append(mod_next)
    in_specs.append(pl.BlockSpec((1, d), lambda i: (0, 0)))
    args.append(g.reshape(1, d))
    out_row = pl.BlockSpec((tm, d), lambda i: (i, 0))
    out_specs, out_shape = [out_row], [jax.ShapeDtypeStruct((n_out, d), F32 if final else BF16)]
    if residual and not final:
        out_specs = [out_row, out_row]
        out_shape = [jax.ShapeDtypeStruct((n_out, d), F32)] + out_shape
    outs = pl.pallas_call(
        functools.partial(_norm_kernel, d=d, residual=residual, final=final),
        grid=(n_out // tm,),
        in_specs=in_specs,
        out_specs=out_specs,
        out_shape=out_shape,
        compiler_params=_cparams("arbitrary"),
        name="residual_norm",
    )(*args)
    return outs if len(outs) == 2 else (x, outs[0])


def _proj_kernel(h_ref, w_ref, o_ref, wb_ref):
    @pl.when(pl.program_id(1) == 0)
    def _():
        wb_ref[...] = w_ref[...].astype(BF16)

    o_ref[...] = _dot(h_ref[...], wb_ref[...])


def _proj_call(h, w, layer, tm, tn):
    n, d = h.shape
    cols = w.shape[-1]
    return pl.pallas_call(
        _proj_kernel,
        grid=(cols // tn, n // tm),
        in_specs=[
            pl.BlockSpec((tm, d), lambda j, i: (i, 0)),
            pl.BlockSpec((None, d, tn), lambda j, i: (layer, 0, j)),
        ],
        out_specs=pl.BlockSpec((tm, tn), lambda j, i: (i, j)),
        out_shape=jax.ShapeDtypeStruct((n, cols), F32),
        scratch_shapes=[pltpu.VMEM((d, tn), BF16)],
        compiler_params=_cparams("arbitrary", "arbitrary"),
        name="in_proj",
    )(h, w)


def _swap_half_pairs(x):
    lane = lax.broadcasted_iota(I32, x.shape, 1)
    return jnp.where((lane & 32) == 0, pltpu.roll(x, LANE - 32, 1), pltpu.roll(x, 32, 1))


def _ret_kernel(*refs, heads, dk, dv, chunk, n_chunks, has_init, has_rope):
    it = iter(refs)
    lg_ref = next(it)
    qkv = [[next(it) for _ in range(3)] for _ in range(2)]
    s0_ref = next(it) if has_init else None
    rope = [[next(it) for _ in range(2)] for _ in range(2)] if has_rope else None
    o_refs = [next(it), next(it)]
    sfin_ref = next(it)
    s_ref = next(it)

    c = pl.program_id(1)

    @pl.when(c == 0)
    def _():
        if has_init:
            s_ref[...] = s0_ref[...]
        else:
            s_ref[...] = jnp.zeros_like(s_ref)

    L = chunk
    ii = lax.broadcasted_iota(I32, (L, L), 0)
    jj = lax.broadcasted_iota(I32, (L, L), 1)
    pcol = lax.broadcasted_iota(I32, (L, 1), 0).astype(F32)
    n_sub = qkv[0][0].shape[0] // L
    jobs = [(d, hh, r) for d in range(2) for hh in range(heads) for r in range(n_sub)]
    qs, ks, vs, scores, q_in, k_out = [], [], [], [], [], []
    for d, hh, r in jobs:
        q_ref, k_ref, v_ref = qkv[d]
        rows = slice(r * L, (r + 1) * L)
        q = q_ref[rows, hh * dk:(hh + 1) * dk]
        k = k_ref[rows, hh * dk:(hh + 1) * dk] * (dk ** -0.5)
        if has_rope:
            cos, sin = rope[d][0][rows, :], rope[d][1][rows, :]
            q = q * cos + _swap_half_pairs(q) * sin
            k = k * cos + _swap_half_pairs(k) * sin
        qs.append(q)
        ks.append(k)
        vs.append(v_ref[rows, hh * dv:(hh + 1) * dv].astype(BF16))
    for (d, hh, r), q, k in zip(jobs, qs, ks):
        dif = ((ii - jj) if d == 0 else (jj - ii)).astype(F32)
        lg = lg_ref[d, hh]
        decay = jnp.where(dif >= 0.0, jnp.exp(lg * jnp.maximum(dif, 0.0)), 0.0)
        scores.append((_dot_nt(q.astype(BF16), k.astype(BF16)) * decay).astype(BF16))
    for (d, hh, r), q, k in zip(jobs, qs, ks):
        lg = lg_ref[d, hh]
        q_pow, k_pow = (pcol + 1.0, (L - 1.0) - pcol) if d == 0 else (L - pcol, pcol)
        q_in.append((q * jnp.exp(lg * q_pow)).astype(BF16))
        k_out.append((k * jnp.exp(lg * k_pow)).T.astype(BF16))
    intra = [_dot(sc, v) for sc, v in zip(scores, vs)]
    kvs = [_dot(ko, v) for ko, v in zip(k_out, vs)]
    for d in range(2):
        for hh in range(heads):
            s = s_ref[d, hh]
            chunk_decay = jnp.exp(jnp.full((1, 1), lg_ref[d, hh] * L, F32))
            for r in (range(n_sub) if d == 0 else range(n_sub - 1, -1, -1)):
                j = (d * heads + hh) * n_sub + r
                o_refs[d][r * L:(r + 1) * L, hh * dv:(hh + 1) * dv] = intra[j] + _dot(q_in[j], s.astype(BF16))
                s = s * chunk_decay + kvs[j]
            s_ref[d, hh] = s

    @pl.when(c == n_chunks - 1)
    def _():
        sfin_ref[...] = s_ref[...]


def _ret_call(proj, log_gamma, s0, rope, *, tok0, batch, seq, heads, dk, dv):
    rb = RET_CHUNK * RET_CHUNKS_PER_STEP
    n = seq // rb
    qw, vw = heads * dk, heads * dv
    assert seq % rb == 0 and tok0 % rb == 0 and vw % qw == 0
    r0 = tok0 // rb

    def fwd(b, c):
        return r0 + b * n + c

    def bwd(b, c):
        return r0 + b * n + (n - 1 - c)

    in_specs = [pl.BlockSpec(memory_space=pltpu.SMEM)]
    args = [log_gamma]
    for rmap in (fwd, bwd):
        in_specs += [
            pl.BlockSpec((rb, qw), lambda b, c, rmap=rmap: (rmap(b, c), 0)),
            pl.BlockSpec((rb, qw), lambda b, c, rmap=rmap: (rmap(b, c), 1)),
            pl.BlockSpec((rb, vw), lambda b, c, rmap=rmap: (rmap(b, c), (2 * qw) // vw)),
        ]
        args += [proj, proj, proj]
    state_spec = pl.BlockSpec((None, 2, heads, dk, dv), lambda b, c: (b, 0, 0, 0, 0))
    if s0 is not None:
        in_specs.append(state_spec)
        args.append(s0)
    if rope is not None:
        for cmap in (lambda b, c: (c, 0), lambda b, c: (n - 1 - c, 0)):
            in_specs += [pl.BlockSpec((rb, dk), cmap), pl.BlockSpec((rb, dk), cmap)]
            args += [rope[0], rope[1]]
    n_tok = batch * seq
    return pl.pallas_call(
        functools.partial(_ret_kernel, heads=heads, dk=dk, dv=dv, chunk=RET_CHUNK, n_chunks=n,
                          has_init=s0 is not None, has_rope=rope is not None),
        grid=(batch, n),
        in_specs=in_specs,
        out_specs=[
            pl.BlockSpec((rb, vw), lambda b, c: (b * n + c, 0)),
            pl.BlockSpec((rb, vw), lambda b, c: (b * n + (n - 1 - c), 0)),
            state_spec,
        ],
        out_shape=[
            jax.ShapeDtypeStruct((n_tok, vw), F32),
            jax.ShapeDtypeStruct((n_tok, vw), F32),
            jax.ShapeDtypeStruct((batch, 2, heads, dk, dv), F32),
        ],
        scratch_shapes=[pltpu.VMEM((2, heads, dk, dv), F32)],
        compiler_params=_cparams("arbitrary", "arbitrary"),
        name="retention_scan",
    )(*args)


def _hg_scan_index(shape, axis, rev):
    i = lax.broadcasted_iota(I32, shape, axis) & (HG_CHUNK - 1)
    return (HG_CHUNK - 1 - i) if rev else i


def _hg_window_matrix(rev):
    C = HG_CHUNK
    ti = _hg_scan_index((C, 2 * C), 0, rev)
    tj = _hg_scan_index((C, 2 * C), 1, rev)
    blocks = [tj <= ti]
    for l in range(2, HG_LEVELS + 1):
        anchor = ((ti >> l) << l) + (1 << (l - 1)) - 1
        upper = ((ti >> (l - 1)) & 1) == 1
        blocks.append((upper & (tj > anchor) & (tj <= ti)) | (~upper & (tj > ti) & (tj <= anchor)))
    blocks.append(tj > ti)
    return jnp.concatenate([jnp.where(m, 1.0, 0.0) for m in blocks], axis=0).astype(BF16)


def _hg_kernel(*refs, heads, dk, dv, tt, n_blocks, has_init):
    it = iter(refs)
    qzv = [[next(it) for _ in range(3)] for _ in range(2)]
    lb_ref = next(it)
    s0_ref = next(it) if has_init else None
    o_refs = [next(it), next(it)]
    sfin_ref = next(it)
    st_ref = next(it)
    win_ref = next(it)

    t = pl.program_id(1)

    @pl.when((pl.program_id(0) == 0) & (t == 0))
    def _():
        for d in range(2):
            win_ref[d] = _hg_window_matrix(d == 1)

    @pl.when(t == 0)
    def _():
        for d in range(2):
            for hh in range(heads):
                st_ref[d, hh] = s0_ref[d, hh].T if has_init else jnp.zeros((dv, dk), F32)

    C = HG_CHUNK
    n_ch = tt // C
    chunks = [slice(c * C, (c + 1) * C) for c in range(n_ch)]
    levels, uppers = [], []
    for d in range(2):
        ti = _hg_scan_index((C, C), 0, d == 1)
        tj = _hg_scan_index((C, C), 1, d == 1)
        level = jnp.where(tj < ti, 1, 0)
        for l in range(1, HG_LEVELS):
            level = level + jnp.where((tj < ti) & ((ti >> l) != (tj >> l)), 1, 0)
        t_col = _hg_scan_index((C, 1), 0, d == 1)
        levels.append(level)
        uppers.append([None] + [((t_col >> (l - 1)) & 1) == 1 for l in range(1, HG_LEVELS + 1)])

    def head_body(hh, carry):
        ck = pl.ds(pl.multiple_of(hh * dk, LANE), dk)
        cv = pl.ds(pl.multiple_of(hh * dv, LANE), dv)
        jobs = [(d, r) for d in range(2) for r in chunks]
        qg, k, f, hi, lo, v, vb, diag = [], [], [], [], [], [], [], []
        for d in range(2):
            q_ref, z_ref, v_ref = qzv[d]
            lb = lb_ref[d, :, ck]
            z = z_ref[:, ck]
            e = jnp.exp(-jnp.abs(z))
            den = 1.0 + e
            num = jnp.where(z >= 0.0, 1.0 + lb * e, e + lb)
            log_num = jnp.where((z < 0.0) & (lb <= 0.0), z, jnp.log(num))
            log_f = log_num - jnp.log(den)
            k.append((1.0 - lb) * (jnp.where(z > 0.0, e, 1.0) / den))
            f.append(1.0 - k[d])
            log2_f = log_f * LOG2_E
            hi.append(log2_f.astype(BF16))
            lo.append((log2_f - hi[d].astype(F32)).astype(BF16))
            qg.append(_silu(q_ref[:, ck]) * (dk ** -0.5))
            v.append(v_ref[:, cv])
            vb.append(v[d].astype(BF16))
            diag.append(jnp.sum(qg[d] * k[d], axis=-1, keepdims=True))
        wins = [win_ref[0], win_ref[1]]
        sums = [_dot(wins[d], jnp.concatenate([hi[d][r], lo[d][r]], axis=0)) for d, r in jobs]
        pairs = []
        for (d, r), s in zip(jobs, sums):
            ps = [jnp.where(uppers[d][1], qg[d][r] * f[d][r], k[d][r]).astype(BF16)]
            for l in range(2, HG_LEVELS + 1):
                scale = jnp.exp2(s[(l - 1) * C:l * C])
                ps.append((jnp.where(uppers[d][l], qg[d][r], k[d][r]) * scale).astype(BF16))
            pairs.append(ps)
        grams = [[_dot_nt(p, p) for p in ps] for ps in pairs]
        atts = []
        for (d, r), gs in zip(jobs, grams):
            att = jnp.where(levels[d] == 1, gs[0], 0.0)
            for l in range(2, HG_LEVELS + 1):
                att = jnp.where(levels[d] == l, gs[l - 1], att)
            atts.append(att.astype(BF16))
        k_out = [(k[d][r] * jnp.exp2(s[HG_LEVELS * C:])).astype(BF16) for (d, r), s in zip(jobs, sums)]
        q_in = [(qg[d][r] * jnp.exp2(s[:C])).astype(BF16) for (d, r), s in zip(jobs, sums)]
        decays = [jnp.exp2(s[0:1] if d == 1 else s[C - 1:C]) for (d, r), s in zip(jobs, sums)]
        outs = [_dot(att, vb[d][r]) + diag[d][r] * v[d][r] for att, (d, r) in zip(atts, jobs)]
        kvs = [_dot_tn(vb[d][r], ko) for (d, r), ko in zip(jobs, k_out)]
        for d in range(2):
            st = st_ref[d, hh]
            for c in (range(n_ch - 1, -1, -1) if d == 1 else range(n_ch)):
                j = d * n_ch + c
                o_refs[d][chunks[c], cv] = outs[j] + _dot_nt(q_in[j], st.astype(BF16))
                st = st * decays[j] + kvs[j]
            st_ref[d, hh] = st
        return carry

    lax.fori_loop(0, heads, head_body, 0)

    @pl.when(t == n_blocks - 1)
    def _():
        for d in range(2):
            for hh in range(heads):
                sfin_ref[d, hh] = st_ref[d, hh].T


def _hg_call(proj, lb, s0, *, tok0, batch, seq, heads, dk, dv, col_q, col_f, col_i):
    tt = min(seq, 256)
    n = seq // tt
    kw, vw = heads * dk, heads * dv
    assert seq % tt == 0 and tok0 % tt == 0 and tt % HG_CHUNK == 0
    assert col_q % kw == 0 and col_f % kw == 0 and col_i % vw == 0
    r0 = tok0 // tt

    def fwd(b, t):
        return r0 + b * n + t

    def bwd(b, t):
        return r0 + b * n + (n - 1 - t)

    in_specs, args = [], []
    for d, rmap in enumerate((fwd, bwd)):
        in_specs += [
            pl.BlockSpec((tt, kw), lambda b, t, rmap=rmap: (rmap(b, t), col_q // kw)),
            pl.BlockSpec((tt, kw), lambda b, t, rmap=rmap, d=d: (rmap(b, t), col_f // kw + d)),
            pl.BlockSpec((tt, vw), lambda b, t, rmap=rmap: (rmap(b, t), col_i // vw)),
        ]
        args += [proj, proj, proj]
    in_specs.append(pl.BlockSpec((2, 1, kw), lambda b, t: (0, 0, 0)))
    args.append(lb.reshape(2, 1, kw))
    state_spec = pl.BlockSpec((None, 2, heads, dk, dv), lambda b, t: (b, 0, 0, 0, 0))
    if s0 is not None:
        in_specs.append(state_spec)
        args.append(s0)
    n_tok = batch * seq
    return pl.pallas_call(
        functools.partial(_hg_kernel, heads=heads, dk=dk, dv=dv, tt=tt, n_blocks=n, has_init=s0 is not None),
        grid=(batch, n),
        in_specs=in_specs,
        out_specs=[
            pl.BlockSpec((tt, vw), lambda b, t: (b * n + t, 0)),
            pl.BlockSpec((tt, vw), lambda b, t: (b * n + (n - 1 - t), 0)),
            state_spec,
        ],
        out_shape=[
            jax.ShapeDtypeStruct((n_tok, vw), F32),
            jax.ShapeDtypeStruct((n_tok, vw), F32),
            jax.ShapeDtypeStruct((batch, 2, heads, dk, dv), F32),
        ],
        scratch_shapes=[pltpu.VMEM((2, heads, dv, dk), F32),
                        pltpu.VMEM((2, (HG_LEVELS + 1) * HG_CHUNK, 2 * HG_CHUNK), BF16)],
        compiler_params=_cparams("arbitrary", "arbitrary"),
        name="hgrn2_scan",
    )(*args)


def _mix_out_kernel(*refs, d, ret_heads, hg_heads, ctx_blocks):
    scans = [refs[4 * p:4 * p + 4] for p in range(2)]
    (rg_ref, go_ref, h_ref, x_ref, mod_ref, g2_ref, wr_ref, wh_ref, wm_ref, wo_ref, wrt_ref,
     x1_ref, h2_ref, lgt_ref, o_ref, og_ref) = refs[8:]

    for p, in_set in enumerate((pl.program_id(0) < ctx_blocks, pl.program_id(0) >= ctx_blocks)):
        @pl.when(in_set)
        def _(p=p):
            o_ref[...] = scans[p][0][...] + scans[p][1][...]
            og_ref[...] = scans[p][2][...] + scans[p][3][...]

    o = o_ref[...]
    dvr = o.shape[1] // ret_heads
    parts = []
    for hh in range(ret_heads):
        oh = o[:, hh * dvr:(hh + 1) * dvr]
        mu = jnp.mean(oh, axis=-1, keepdims=True)
        ctr = oh - mu
        parts.append(ctr * lax.rsqrt(jnp.mean(ctr * ctr, axis=-1, keepdims=True) + NORM_EPS))
    ret_in = (jnp.concatenate(parts, axis=1) * _silu(rg_ref[...])).astype(BF16)
    ret_out = _dot(ret_in, wr_ref[...])

    og = og_ref[...]
    dvh = og.shape[1] // hg_heads
    parts = [_rms(og[:, hh * dvh:(hh + 1) * dvh]) for hh in range(hg_heads)]
    hg_in = (jnp.concatenate(parts, axis=1) * _silu(go_ref[...])).astype(BF16)
    hg_out = _dot(hg_in, wh_ref[...])

    gates = jax.nn.sigmoid(_dot(h_ref[...], wm_ref[...]))
    merged = gates[:, :d] * ret_out + gates[:, d:] * hg_out
    mix = _dot(merged.astype(BF16), wo_ref[...])
    x1 = x_ref[...] + mod_ref[:, 2 * d:3 * d] * mix
    x1_ref[...] = x1
    h2 = _rms(x1) * g2_ref[...]
    h2 = h2 * (1.0 + mod_ref[:, 4 * d:5 * d]) + mod_ref[:, 3 * d:4 * d]
    h2_ref[...] = h2.astype(BF16)
    h_hi = h2.astype(BF16)
    h_lo = (h2 - h_hi.astype(F32)).astype(BF16)
    w_hi, w_lo = wrt_ref[0], wrt_ref[1]
    lgt_ref[...] = _dot_nt(w_hi, h_hi) + (_dot_nt(w_hi, h_lo) + _dot_nt(w_lo, h_hi))


def _mix_out_call(scans_ctx, scans_lat, proj, h, x, mod_l, mod_row, g2, w_ret_o, w_hg_o, w_merge, w_out,
                  w_router_t, tm, *, ret_heads, hg_heads, col_rg, col_go):
    n, d = x.shape
    vr, vh = scans_ctx[0].shape[1], scans_ctx[2].shape[1]
    ne = w_router_t.shape[1]
    ctx_blocks = scans_ctx[0].shape[0] // tm
    assert col_rg % vr == 0 and col_go % vh == 0

    def row(w):
        return pl.BlockSpec((tm, w), lambda i: (i, 0))

    def full(a):
        return pl.BlockSpec(a.shape, lambda i: (0,) * a.ndim)

    ctx_row = lambda a: pl.BlockSpec((tm, a.shape[1]), lambda i: (jnp.minimum(i, ctx_blocks - 1), 0))
    lat_row = lambda a: pl.BlockSpec((tm, a.shape[1]), lambda i: (jnp.maximum(i - ctx_blocks, 0), 0))
    return pl.pallas_call(
        functools.partial(_mix_out_kernel, d=d, ret_heads=ret_heads, hg_heads=hg_heads, ctx_blocks=ctx_blocks),
        grid=(n // tm,),
        in_specs=[ctx_row(a) for a in scans_ctx] + [lat_row(a) for a in scans_lat] + [
            pl.BlockSpec((tm, vr), lambda i: (i, col_rg // vr)),
            pl.BlockSpec((tm, vh), lambda i: (i, col_go // vh)),
            row(d), row(d),
            pl.BlockSpec((None, 1, 6 * d), lambda i: (mod_row(i), 0, 0)),
            pl.BlockSpec((1, d), lambda i: (0, 0)),
            full(w_ret_o), full(w_hg_o), full(w_merge), full(w_out), full(w_router_t),
        ],
        out_specs=[row(d), row(d), pl.BlockSpec((ne, tm), lambda i: (0, i))],
        out_shape=[
            jax.ShapeDtypeStruct((n, d), F32),
            jax.ShapeDtypeStruct((n, d), BF16),
            jax.ShapeDtypeStruct((ne, n), F32),
        ],
        scratch_shapes=[pltpu.VMEM((tm, vr), F32), pltpu.VMEM((tm, vh), F32)],
        compiler_params=_cparams("arbitrary"),
        name="mixer_out",
    )(*scans_ctx, *scans_lat, proj, proj, h, x, mod_l, g2.reshape(1, d), w_ret_o, w_hg_o, w_merge, w_out,
      w_router_t)


def _route_kernel(lg_ref, aff_ref, sel_ref, posx_ref, flag_ref, *, cap, n):
    lg = lg_ref[...]
    ne = lg.shape[0]
    ex = jnp.exp(lg - jnp.max(lg, axis=0, keepdims=True))
    aff = ex / jnp.sum(ex, axis=0, keepdims=True)
    aff_ref[...] = aff
    bits = pltpu.bitcast(aff, I32)

    def search(i, lo):
        cand = lo | lax.shift_left(jnp.int32(1), 30 - i)
        cnt = jnp.sum(jnp.where(bits >= cand, 1.0, 0.0), axis=1, keepdims=True)
        return jnp.where(cnt >= cap, cand, lo)

    thr = lax.fori_loop(0, 31, search, jnp.zeros((ne, 1), I32))
    gt = bits > thr
    eq = bits == thr
    need = cap - jnp.sum(jnp.where(gt, 1.0, 0.0), axis=1, keepdims=True)
    upper = jnp.where(lax.broadcasted_iota(I32, (LANE, LANE), 0) < lax.broadcasted_iota(I32, (LANE, LANE), 1),
                      1.0, 0.0).astype(BF16)

    def exclusive_count(store):
        def body(ci, carry):
            sl = pl.ds(pl.multiple_of(ci * LANE, LANE), LANE)
            x = flag_ref[:, sl]
            store(sl, carry + _dot(x.astype(BF16), upper))
            return carry + jnp.sum(x, axis=1, keepdims=True)

        lax.fori_loop(0, n // LANE, body, jnp.zeros((ne, 1), F32))

    flag_ref[...] = jnp.where(eq, 1.0, 0.0)

    def store_tie_rank(sl, rank):
        posx_ref[:, sl] = rank.astype(I32)

    exclusive_count(store_tie_rank)
    sel = gt | (eq & (posx_ref[...].astype(F32) < need))
    sel_ref[...] = jnp.where(sel, 1, 0).astype(I32)
    flag_ref[...] = jnp.where(sel, 1.0, 0.0)

    def store_pos(sl, cnt):
        posx_ref[:, sl] = cnt.astype(I32)

    exclusive_count(store_pos)


def _route_call(logits_t, cap):
    ne, n = logits_t.shape
    spec = pl.BlockSpec((ne, n), lambda: (0, 0))
    return pl.pallas_call(
        functools.partial(_route_kernel, cap=cap, n=n),
        in_specs=[spec],
        out_specs=[spec, spec, spec],
        out_shape=[
            jax.ShapeDtypeStruct((ne, n), F32),
            jax.ShapeDtypeStruct((ne, n), I32),
            jax.ShapeDtypeStruct((ne, n), I32),
        ],
        scratch_shapes=[pltpu.VMEM((ne, n), F32)],
        compiler_params=pltpu.CompilerParams(vmem_limit_bytes=VMEM_LIMIT),
        name="expert_choice_route",
    )(logits_t)


def _moe_round_masks(b, r, base_ref, sel_ref, posx_ref, ne, value_rows=None):
    n_tok = sel_ref.shape[1]
    row_iota = lax.broadcasted_iota(I32, (MOE_ROUND, n_tok), 0)
    pieces = []
    for e in range(ne):
        local = posx_ref[e:e + 1, :] - (base_ref[b * ne + e] + r * MOE_ROUND)
        hit = (sel_ref[e:e + 1, :] == 1) & (local == row_iota)
        value = 1.0 if value_rows is None else value_rows[e:e + 1, :]
        pieces.append(jnp.where(hit, value, 0.0).astype(BF16))
    return jnp.concatenate(pieces, axis=0)


def _dispatch_kernel(base_ref, start_ref, pc_ref, nr_ref, h_ref, sel_ref, posx_ref, xe_zero_ref, xe_ref,
                     stage_ref, sem, *, ne):
    del xe_zero_ref
    b = pl.program_id(0)
    n_rounds = nr_ref[b]

    def pieces(r, slot, fn):
        for e in range(ne):
            rem = pc_ref[b * ne + e] - r * MOE_ROUND
            dst = start_ref[b * ne + e] + r * MOE_ROUND
            for size, cond, off in (
                    (MOE_ROUND, rem >= MOE_ROUND, 0),
                    (32, (rem > 0) & (rem < MOE_ROUND) & ((rem & 32) != 0), 0),
                    (16, (rem > 0) & (rem < MOE_ROUND) & ((rem & 16) != 0), rem & 32)):
                @pl.when(cond)
                def _(size=size, off=off, e=e, dst=dst):
                    src_rows = pl.ds(pl.multiple_of(e * MOE_ROUND + off, BF16_SUBLANE), size)
                    dst_rows = pl.ds(pl.multiple_of(dst + off, BF16_SUBLANE), size)
                    fn(pltpu.make_async_copy(stage_ref.at[slot, src_rows], xe_ref.at[e, dst_rows], sem.at[slot]))

    def round_body(r, carry):
        slot = r % 2
        onehot = _moe_round_masks(b, r, base_ref, sel_ref, posx_ref, ne)
        stage_ref[slot] = _dot(onehot, h_ref[...]).astype(BF16)

        @pl.when(r >= 1)
        def _():
            pieces(r - 1, 1 - slot, lambda cp: cp.wait())

        pieces(r, slot, lambda cp: cp.start())
        return carry

    lax.fori_loop(0, n_rounds, round_body, 0)

    @pl.when(n_rounds >= 1)
    def _():
        pieces(n_rounds - 1, (n_rounds - 1) % 2, lambda cp: cp.wait())


def _dispatch_call(h2, sel, posx, tables, list_rows, *, sb):
    n, d = h2.shape
    ne = sel.shape[0]
    xe_zero = jnp.zeros((ne, list_rows, d), BF16)
    grid_spec = pltpu.PrefetchScalarGridSpec(
        num_scalar_prefetch=4,
        grid=(n // sb,),
        in_specs=[
            pl.BlockSpec((sb, d), lambda b, *_: (b, 0)),
            pl.BlockSpec((ne, sb), lambda b, *_: (0, b)),
            pl.BlockSpec((ne, sb), lambda b, *_: (0, b)),
            pl.BlockSpec(memory_space=pl.ANY),
        ],
        out_specs=pl.BlockSpec(memory_space=pl.ANY),
        scratch_shapes=[pltpu.VMEM((2, ne * MOE_ROUND, d), BF16), pltpu.SemaphoreType.DMA((2,))],
    )
    return pl.pallas_call(
        functools.partial(_dispatch_kernel, ne=ne),
        grid_spec=grid_spec,
        out_shape=jax.ShapeDtypeStruct((ne, list_rows, d), BF16),
        input_output_aliases={7: 0},
        compiler_params=_cparams("arbitrary"),
        name="expert_dispatch",
    )(*tables, h2, sel, posx, xe_zero)


def _expert_ffn_kernel(tot_ref, x_ref, wg_ref, wu_ref, wd_ref, y_ref, wgb_ref, wub_ref, wdb_ref, acc_ref,
                       *, rt, n_ff):
    e, f, t = pl.program_id(0), pl.program_id(1), pl.program_id(2)

    @pl.when(t == 0)
    def _():
        wgb_ref[...] = wg_ref[...].astype(BF16)
        wub_ref[...] = wu_ref[...].astype(BF16)
        wdb_ref[...] = wd_ref[...].astype(BF16)

    rows = pl.ds(pl.multiple_of(t * rt, rt), rt)
    live = t * rt < tot_ref[e]

    @pl.when(live)
    def _():
        x = x_ref[...]
        hid = (_silu(_dot(x, wgb_ref[...])) * _dot(x, wub_ref[...])).astype(BF16)
        y = _dot(hid, wdb_ref[...])

        @pl.when(f == 0)
        def _():
            acc_ref[rows, :] = y

        @pl.when(f != 0)
        def _():
            acc_ref[rows, :] += y

    @pl.when((f == n_ff - 1) & live)
    def _():
        y_ref[...] = acc_ref[rows, :].astype(BF16)

    @pl.when((f == n_ff - 1) & jnp.logical_not(live))
    def _():
        y_ref[...] = jnp.zeros_like(y_ref)


def _expert_ffn_call(xe, totals, wg, wu, wd, layer, *, rt, fc):
    ne, list_rows, d = xe.shape
    ff = wg.shape[-1]
    n_ff = ff // fc
    grid_spec = pltpu.PrefetchScalarGridSpec(
        num_scalar_prefetch=1,
        grid=(ne, n_ff, list_rows // rt),
        in_specs=[
            pl.BlockSpec((None, rt, d), lambda e, f, t, *_: (e, t, 0)),
            pl.BlockSpec((None, None, d, fc), lambda e, f, t, *_: (layer, e, 0, f)),
            pl.BlockSpec((None, None, d, fc), lambda e, f, t, *_: (layer, e, 0, f)),
            pl.BlockSpec((None, None, fc, d), lambda e, f, t, *_: (layer, e, f, 0)),
        ],
        out_specs=pl.BlockSpec((None, rt, d), lambda e, f, t, *_: (e, jnp.where(f == n_ff - 1, t, 0), 0)),
        scratch_shapes=[pltpu.VMEM((d, fc), BF16), pltpu.VMEM((d, fc), BF16), pltpu.VMEM((fc, d), BF16),
                        pltpu.VMEM((list_rows, d), F32)],
    )
    return pl.pallas_call(
        functools.partial(_expert_ffn_kernel, rt=rt, n_ff=n_ff),
        grid_spec=grid_spec,
        out_shape=jax.ShapeDtypeStruct((ne, list_rows, d), BF16),
        compiler_params=_cparams("arbitrary", "arbitrary", "arbitrary"),
        name="expert_ffn",
    )(totals, xe, wg, wu, wd)


def _combine_kernel(base_ref, start_ref, pc_ref, nr_ref, sel_ref, posx_ref, aff_ref, y_ref, out_ref, ybuf_ref, sem,
                    *, ne):
    b = pl.program_id(0)
    n_rounds = nr_ref[b]

    @pl.when(b == 0)
    def _():
        ybuf_ref[...] = jnp.zeros_like(ybuf_ref)

    def windows(r, slot, fn):
        for e in range(ne):
            @pl.when(pc_ref[b * ne + e] - r * MOE_ROUND > 0)
            def _(e=e):
                src = pl.ds(pl.multiple_of(start_ref[b * ne + e] + r * MOE_ROUND, BF16_SUBLANE), MOE_ROUND)
                fn(pltpu.make_async_copy(y_ref.at[e, src], ybuf_ref.at[slot, pl.ds(e * MOE_ROUND, MOE_ROUND)],
                                         sem.at[slot]))

    out_ref[...] = jnp.zeros_like(out_ref)

    @pl.when(n_rounds >= 1)
    def _():
        windows(0, 0, lambda cp: cp.start())

    def round_body(r, carry):
        slot = r % 2

        @pl.when(r + 1 < n_rounds)
        def _():
            windows(r + 1, 1 - slot, lambda cp: cp.start())

        windows(r, slot, lambda cp: cp.wait())
        weights = _moe_round_masks(b, r, base_ref, sel_ref, posx_ref, ne, value_rows=aff_ref)
        out_ref[...] += _dot_tn(weights, ybuf_ref[slot])
        return carry

    lax.fori_loop(0, n_rounds, round_body, 0)


def _combine_call(ye, sel, posx, aff, tables, *, sb):
    ne, _, d = ye.shape
    n = sel.shape[1]
    grid_spec = pltpu.PrefetchScalarGridSpec(
        num_scalar_prefetch=4,
        grid=(n // sb,),
        in_specs=[
            pl.BlockSpec((ne, sb), lambda b, *_: (0, b)),
            pl.BlockSpec((ne, sb), lambda b, *_: (0, b)),
            pl.BlockSpec((ne, sb), lambda b, *_: (0, b)),
            pl.BlockSpec(memory_space=pl.ANY),
        ],
        out_specs=pl.BlockSpec((sb, d), lambda b, *_: (b, 0)),
        scratch_shapes=[pltpu.VMEM((2, ne * MOE_ROUND, d), BF16), pltpu.SemaphoreType.DMA((2,))],
    )
    return pl.pallas_call(
        functools.partial(_combine_kernel, ne=ne),
        grid_spec=grid_spec,
        out_shape=jax.ShapeDtypeStruct((n, d), F32),
        compiler_params=_cparams("arbitrary"),
        name="expert_combine",
    )(*tables, sel, posx, aff, ye)


def _moe_tables(posx_sets, caps, sb):
    base, cnt = [], []
    for posx, cap in zip(posx_sets, caps):
        ne = posx.shape[0]
        edges = jnp.concatenate([posx[:, ::sb], jnp.full((ne, 1), cap, I32)], axis=1)
        base.append(edges[:, :-1].T)
        cnt.append((edges[:, 1:] - edges[:, :-1]).T)
    base = jnp.concatenate(base, axis=0)
    cnt = jnp.concatenate(cnt, axis=0)
    pc = (cnt + (BF16_SUBLANE - 1)) // BF16_SUBLANE * BF16_SUBLANE
    ends = jnp.cumsum(pc, axis=0)
    start = ends - pc
    n_rounds = jnp.max((pc + (MOE_ROUND - 1)) // MOE_ROUND, axis=1)
    tables = tuple(a.reshape(-1).astype(I32) for a in (base, start, pc, n_rounds))
    return tables, ends[-1].astype(I32)


def _rope_tables(n_tokens, dk):
    rows = n_tokens // GRID_W
    r, c = jnp.meshgrid(jnp.arange(rows), jnp.arange(GRID_W), indexing="ij")
    pos = jnp.stack([r.reshape(-1), c.reshape(-1)], axis=-1).astype(F32)
    nf = dk // 4
    inv_freq = ROPE_BASE ** (-jnp.arange(nf, dtype=F32) / nf)
    ang = pos[:, :, None] * inv_freq
    cos, sin = jnp.cos(ang), jnp.sin(ang)
    cos_t = jnp.concatenate([cos[:, 0], cos[:, 0], cos[:, 1], cos[:, 1]], axis=-1)
    sin_t = jnp.concatenate([-sin[:, 0], sin[:, 0], -sin[:, 1], sin[:, 1]], axis=-1)
    return cos_t, sin_t


def kernel(x_prompt, x_sample, state_ret, state_hgrn, c, c_ctx, ada_w, ada_b, norm_mix_g, norm_ffn_g, w_in,
           ret_gamma_logit, hg_lb_logit, w_ret_o, w_hg_o, w_merge, w_out, w_router, w_exp_gate, w_exp_up,
           w_exp_down, final_g):
    b_ctx, t_ctx, d = x_prompt.shape
    b_lat, t_lat, _ = x_sample.shape
    depth = w_in.shape[0]
    ret_heads, ret_dk, ret_dv = state_ret.shape[3:]
    hg_heads, hg_dk, hg_dv = state_hgrn.shape[3:]
    ne = w_router.shape[-1]
    n_ctx, n_lat = b_ctx * t_ctx, b_lat * t_lat
    n_tok = n_ctx + n_lat
    qw, vw = ret_heads * ret_dk, ret_heads * ret_dv
    kw, hw = hg_heads * hg_dk, hg_heads * hg_dv
    col_rg = 2 * qw + vw
    col_gq = col_rg + vw
    col_gf = col_gq + kw
    col_gi = col_gf + 2 * kw
    col_go = col_gi + hw
    assert col_go + hw == w_in.shape[-1] and b_lat < MOD_ROWS

    tm = 256
    assert t_ctx % tm == 0 and t_lat % tm == 0
    tm_proj = 1024 if n_tok % 1024 == 0 else tm
    sb = 512 if (n_ctx % 4096 == 0 and n_lat % 4096 == 0) else 256
    assert n_ctx % sb == 0 and n_lat % sb == 0
    sets = ((0, n_ctx), (n_ctx, n_lat))
    caps = [CAPACITY_FACTOR * n_set // ne for _, n_set in sets]
    rt = 512
    list_rows = -(-(sum(caps) + BF16_SUBLANE * (n_tok // sb)) // rt) * rt

    def mod_row(i):
        return jnp.where(i < n_ctx // tm, 0, 1 + (i - n_ctx // tm) // (t_lat // tm))

    x = jnp.concatenate([x_prompt.reshape(n_ctx, d), x_sample.reshape(n_lat, d)], axis=0)
    cvec = jnp.zeros((MOD_ROWS, d), F32).at[0].set(c_ctx).at[1:1 + b_lat].set(c)
    mod = _mod_call(cvec, ada_w, ada_b).reshape(depth, MOD_ROWS, 1, 6 * d)

    log_gamma = jax.nn.log_sigmoid(ret_gamma_logit.astype(F32))
    p_lb = jax.nn.softmax(hg_lb_logit.astype(F32), axis=0)
    hg_lb = jnp.clip(jnp.cumsum(p_lb, axis=0) - p_lb[0:1], 0.0, 1.0 - 1e-6)
    rope = _rope_tables(t_lat, ret_dk)

    w_ret_o_b, w_hg_o_b, w_out_b = w_ret_o.astype(BF16), w_hg_o.astype(BF16), w_out.astype(BF16)
    w_merge_b = w_merge.astype(BF16)
    w_router_t = jnp.swapaxes(w_router, 1, 2)
    w_router_hi = w_router_t.astype(BF16)
    w_router_t = jnp.stack([w_router_hi, (w_router_t - w_router_hi.astype(F32)).astype(BF16)], axis=1)
    fc = min(1024, w_exp_gate.shape[-1])
    tn = min(2048, w_in.shape[-1])

    _, h = _norm_call(x, None, None, mod[0], norm_mix_g[0], mod_row, tm)
    new_ret, new_hg = [], []
    for l in range(depth):
        proj = _proj_call(h, w_in, l, tm_proj, tn)

        ret_kw = dict(heads=ret_heads, dk=ret_dk, dv=ret_dv)
        orf_c, orb_c, s_ret = _ret_call(proj, log_gamma[l], None, None, tok0=0, batch=b_ctx, seq=t_ctx, **ret_kw)
        orf_l, orb_l, _ = _ret_call(proj, log_gamma[l], state_ret[:, l], rope, tok0=n_ctx, batch=b_lat, seq=t_lat,
                                    **ret_kw)
        hg_kw = dict(heads=hg_heads, dk=hg_dk, dv=hg_dv, col_q=col_gq, col_f=col_gf, col_i=col_gi)
        ogf_c, ogb_c, s_hg = _hg_call(proj, hg_lb[l], None, tok0=0, batch=b_ctx, seq=t_ctx, **hg_kw)
        ogf_l, ogb_l, _ = _hg_call(proj, hg_lb[l], state_hgrn[:, l], tok0=n_ctx, batch=b_lat, seq=t_lat, **hg_kw)
        new_ret.append(s_ret)
        new_hg.append(s_hg)

        x, h2, logits_t = _mix_out_call(
            (orf_c, orb_c, ogf_c, ogb_c), (orf_l, orb_l, ogf_l, ogb_l), proj, h, x, mod[l],
            mod_row, norm_ffn_g[l], w_ret_o_b[l], w_hg_o_b[l], w_merge_b[l], w_out_b[l], w_router_t[l], tm,
            ret_heads=ret_heads, hg_heads=hg_heads, col_rg=col_rg, col_go=col_go)

        routed = [_route_call(logits_t[:, lo:lo + n_set], cap) for (lo, n_set), cap in zip(sets, caps)]
        aff, sel, posx = (jnp.concatenate([r[i] for r in routed], axis=1) for i in range(3))
        tables, totals = _moe_tables([r[2] for r in routed], caps, sb)
        xe = _dispatch_call(h2, sel, posx, tables, list_rows, sb=sb)
        ye = _expert_ffn_call(xe, totals, w_exp_gate, w_exp_up, w_exp_down, l, rt=rt, fc=fc)
        ffn = _combine_call(ye, sel, posx, aff, tables, sb=sb)

        if l + 1 < depth:
            x, h = _norm_call(x, ffn, mod[l], mod[l + 1], norm_mix_g[l + 1], mod_row, tm)
        else:
            _, y_prompt = _norm_call(x, ffn, mod[l], None, final_g, mod_row, tm, rows=(0, n_ctx))
            _, y_sample = _norm_call(x, ffn, mod[l], None, final_g, mod_row, tm, rows=(n_ctx, n_lat))

    y_prompt = y_prompt.reshape(b_ctx, t_ctx, d)
    y_sample = y_sample.reshape(b_lat, t_lat, d)
    return y_prompt, y_sample, jnp.stack(new_ret, axis=1), jnp.stack(new_hg, axis=1)
```

```python
import functools

import jax
import jax.numpy as jnp
from jax import lax
from jax.experimental import pallas as pl
from jax.experimental.pallas import tpu as pltpu

F32 = jnp.float32
BF16 = jnp.bfloat16
I32 = jnp.int32
HIGHEST = lax.Precision.HIGHEST

NORM_EPS = 1e-6
LOG2_E = 1.4426950408889634
ROPE_BASE = 10000.0
GRID_W = 64
CAPACITY_FACTOR = 2
RET_CHUNK = 128
RET_CHUNKS_PER_STEP = 2
HG_LEVELS = 6
HG_CHUNK = 1 << HG_LEVELS
MOD_ROWS = 8
LANE = 128
BF16_SUBLANE = 16
MOE_ROUND = 64
VMEM_LIMIT = 62 * 1024 * 1024


def _cparams(*sem):
    return pltpu.CompilerParams(dimension_semantics=sem, vmem_limit_bytes=VMEM_LIMIT)


def _dot(a, b, **kw):
    return jnp.dot(a, b, preferred_element_type=F32, **kw)


def _dot_nt(a, b, **kw):
    return lax.dot_general(a, b, (((1,), (1,)), ((), ())), preferred_element_type=F32, **kw)


def _dot_tn(a, b, **kw):
    return lax.dot_general(a, b, (((0,), (0,)), ((), ())), preferred_element_type=F32, **kw)


def _silu(x):
    return x * jax.nn.sigmoid(x)


def _mod_kernel(c_ref, w_ref, b_ref, o_ref):
    s = _silu(c_ref[...])
    o_ref[...] = _dot(s, w_ref[...], precision=HIGHEST) + b_ref[...]


def _mod_call(cvec, ada_w, ada_b):
    depth, d, six_d = ada_w.shape
    tn = 6 * LANE * 2
    assert six_d % tn == 0
    return pl.pallas_call(
        _mod_kernel,
        grid=(depth, six_d // tn),
        in_specs=[
            pl.BlockSpec((MOD_ROWS, d), lambda l, j: (0, 0)),
            pl.BlockSpec((None, d, tn), lambda l, j: (l, 0, j)),
            pl.BlockSpec((None, 1, tn), lambda l, j: (l, 0, j)),
        ],
        out_specs=pl.BlockSpec((None, MOD_ROWS, tn), lambda l, j: (l, 0, j)),
        out_shape=jax.ShapeDtypeStruct((depth, MOD_ROWS, six_d), F32),
        compiler_params=_cparams("arbitrary", "arbitrary"),
        name="adaln_mod",
    )(cvec, ada_w, ada_b.reshape(depth, 1, six_d))


def _rms(x):
    return x * lax.rsqrt(jnp.mean(x * x, axis=-1, keepdims=True) + NORM_EPS)


def _norm_kernel(*refs, d, residual, final):
    it = iter(refs)
    x_ref = next(it)
    f_ref, gate_ref = (next(it), next(it)) if residual else (None, None)
    mod_ref = None if final else next(it)
    g_ref = next(it)
    xo_ref = next(it) if (residual and not final) else None
    h_ref = next(it)
    x = x_ref[...]
    if residual:
        x = x + gate_ref[...] * f_ref[...]
        if xo_ref is not None:
            xo_ref[...] = x
    y = _rms(x) * g_ref[...]
    if not final:
        y = y * (1.0 + mod_ref[:, d:2 * d]) + mod_ref[:, 0:d]
    h_ref[...] = y.astype(h_ref.dtype)


def _norm_call(x, ffn, mod_gate, mod_next, g, mod_row, tm, rows=None):
    n, d = x.shape
    residual = ffn is not None
    final = mod_next is None
    r0, n_out = (0, n) if rows is None else rows
    blk0 = r0 // tm
    row = pl.BlockSpec((tm, d), lambda i: (blk0 + i, 0))
    in_specs, args = [row], [x]
    if residual:
        in_specs += [row, pl.BlockSpec((None, 1, d), lambda i: (mod_row(blk0 + i), 0, 5))]
        args += [ffn, mod_gate]
    if not final:
        in_specs.append(pl.BlockSpec((None, 1, 2 * d), lambda i: (mod_row(blk0 + i), 0, 0)))
        args.append(mod_next)
    in_specs.append(pl.BlockSpec((1, d), lambda i: (0, 0)))
    args.append(g.reshape(1, d))
    out_row = pl.BlockSpec((tm, d), lambda i: (i, 0))
    out_specs, out_shape = [out_row], [jax.ShapeDtypeStruct((n_out, d), F32 if final else BF16)]
    if residual and not final:
        out_specs = [out_row, out_row]
        out_shape = [jax.ShapeDtypeStruct((n_out, d), F32)] + out_shape
    outs = pl.pallas_call(
        functools.partial(_norm_kernel, d=d, residual=residual, final=final),
        grid=(n_out // tm,),
        in_specs=in_specs,
        out_specs=out_specs,
        out_shape=out_shape,
        compiler_params=_cparams("arbitrary"),
        name="residual_norm",
    )(*args)
    return outs if len(outs) == 2 else (x, outs[0])


def _proj_kernel(h_ref, w_ref, o_ref, wb_ref):
    @pl.when(pl.program_id(1) == 0)
    def _():
        wb_ref[...] = w_ref[...].astype(BF16)

    o_ref[...] = _dot(h_ref[...], wb_ref[...])


def _proj_call(h, w, layer, tm, tn):
    n, d = h.shape
    cols = w.shape[-1]
    return pl.pallas_call(
        _proj_kernel,
        grid=(cols // tn, n // tm),
        in_specs=[
            pl.BlockSpec((tm, d), lambda j, i: (i, 0)),
            pl.BlockSpec((None, d, tn), lambda j, i: (layer, 0, j)),
        ],
        out_specs=pl.BlockSpec((tm, tn), lambda j, i: (i, j)),
        out_shape=jax.ShapeDtypeStruct((n, cols), F32),
        scratch_shapes=[pltpu.VMEM((d, tn), BF16)],
        compiler_params=_cparams("arbitrary", "arbitrary"),
        name="in_proj",
    )(h, w)


def _swap_half_pairs(x):
    lane = lax.broadcasted_iota(I32, x.shape, 1)
    return jnp.where((lane & 32) == 0, pltpu.roll(x, LANE - 32, 1), pltpu.roll(x, 32, 1))


def _ret_kernel(*refs, heads, dk, dv, chunk, n_chunks, has_init, has_rope):
    it = iter(refs)
    lg_ref = next(it)
    qkv = [[next(it) for _ in range(3)] for _ in range(2)]
    s0_ref = next(it) if has_init else None
    rope = [[next(it) for _ in range(2)] for _ in range(2)] if has_rope else None
    o_refs = [next(it), next(it)]
    sfin_ref = next(it)
    s_ref = next(it)

    c = pl.program_id(1)

    @pl.when(c == 0)
    def _():
        if has_init:
            s_ref[...] = s0_ref[...]
        else:
            s_ref[...] = jnp.zeros_like(s_ref)

    L = chunk
    ii = lax.broadcasted_iota(I32, (L, L), 0)
    jj = lax.broadcasted_iota(I32, (L, L), 1)
    pcol = lax.broadcasted_iota(I32, (L, 1), 0).astype(F32)
    n_sub = qkv[0][0].shape[0] // L
    jobs = [(d, hh, r) for d in range(2) for hh in range(heads) for r in range(n_sub)]
    qs, ks, vs, scores, q_in, k_out = [], [], [], [], [], []
    for d, hh, r in jobs:
        q_ref, k_ref, v_ref = qkv[d]
        rows = slice(r * L, (r + 1) * L)
        q = q_ref[rows, hh * dk:(hh + 1) * dk]
        k = k_ref[rows, hh * dk:(hh + 1) * dk] * (dk ** -0.5)
        if has_rope:
            cos, sin = rope[d][0][rows, :], rope[d][1][rows, :]
            q = q * cos + _swap_half_pairs(q) * sin
            k = k * cos + _swap_half_pairs(k) * sin
        qs.append(q)
        ks.append(k)
        vs.append(v_ref[rows, hh * dv:(hh + 1) * dv].astype(BF16))
    for (d, hh, r), q, k in zip(jobs, qs, ks):
        dif = ((ii - jj) if d == 0 else (jj - ii)).astype(F32)
        lg = lg_ref[d, hh]
        decay = jnp.where(dif >= 0.0, jnp.exp(lg * jnp.maximum(dif, 0.0)), 0.0)
        scores.append((_dot_nt(q.astype(BF16), k.astype(BF16)) * decay).astype(BF16))
    for (d, hh, r), q, k in zip(jobs, qs, ks):
        lg = lg_ref[d, hh]
        q_pow, k_pow = (pcol + 1.0, (L - 1.0) - pcol) if d == 0 else (L - pcol, pcol)
        q_in.append((q * jnp.exp(lg * q_pow)).astype(BF16))
        k_out.append((k * jnp.exp(lg * k_pow)).T.astype(BF16))
    intra = [_dot(sc, v) for sc, v in zip(scores, vs)]
    kvs = [_dot(ko, v) for ko, v in zip(k_out, vs)]
    for d in range(2):
        for hh in range(heads):
            s = s_ref[d, hh]
            chunk_decay = jnp.exp(jnp.full((1, 1), lg_ref[d, hh] * L, F32))
            for r in (range(n_sub) if d == 0 else range(n_sub - 1, -1, -1)):
                j = (d * heads + hh) * n_sub + r
                o_refs[d][r * L:(r + 1) * L, hh * dv:(hh + 1) * dv] = intra[j] + _dot(q_in[j], s.astype(BF16))
                s = s * chunk_decay + kvs[j]
            s_ref[d, hh] = s

    @pl.when(c == n_chunks - 1)
    def _():
        sfin_ref[...] = s_ref[...]


def _ret_call(proj, log_gamma, s0, rope, *, tok0, batch, seq, heads, dk, dv):
    rb = RET_CHUNK * RET_CHUNKS_PER_STEP
    n = seq // rb
    qw, vw = heads * dk, heads * dv
    assert seq % rb == 0 and tok0 % rb == 0 and vw % qw == 0
    r0 = tok0 // rb

    def fwd(b, c):
        return r0 + b * n + c

    def bwd(b, c):
        return r0 + b * n + (n - 1 - c)

    in_specs = [pl.BlockSpec(memory_space=pltpu.SMEM)]
    args = [log_gamma]
    for rmap in (fwd, bwd):
        in_specs += [
            pl.BlockSpec((rb, qw), lambda b, c, rmap=rmap: (rmap(b, c), 0)),
            pl.BlockSpec((rb, qw), lambda b, c, rmap=rmap: (rmap(b, c), 1)),
            pl.BlockSpec((rb, vw), lambda b, c, rmap=rmap: (rmap(b, c), (2 * qw) // vw)),
        ]
        args += [proj, proj, proj]
    state_spec = pl.BlockSpec((None, 2, heads, dk, dv), lambda b, c: (b, 0, 0, 0, 0))
    if s0 is not None:
        in_specs.append(state_spec)
        args.append(s0)
    if rope is not None:
        for cmap in (lambda b, c: (c, 0), lambda b, c: (n - 1 - c, 0)):
            in_specs += [pl.BlockSpec((rb, dk), cmap), pl.BlockSpec((rb, dk), cmap)]
            args += [rope[0], rope[1]]
    n_tok = batch * seq
    return pl.pallas_call(
        functools.partial(_ret_kernel, heads=heads, dk=dk, dv=dv, chunk=RET_CHUNK, n_chunks=n,
                          has_init=s0 is not None, has_rope=rope is not None),
        grid=(batch, n),
        in_specs=in_specs,
        out_specs=[
            pl.BlockSpec((rb, vw), lambda b, c: (b * n + c, 0)),
            pl.BlockSpec((rb, vw), lambda b, c: (b * n + (n - 1 - c), 0)),
            state_spec,
        ],
        out_shape=[
            jax.ShapeDtypeStruct((n_tok, vw), F32),
            jax.ShapeDtypeStruct((n_tok, vw), F32),
            jax.ShapeDtypeStruct((batch, 2, heads, dk, dv), F32),
        ],
        scratch_shapes=[pltpu.VMEM((2, heads, dk, dv), F32)],
        compiler_params=_cparams("arbitrary", "arbitrary"),
        name="retention_scan",
    )(*args)


def _hg_scan_index(shape, axis, rev):
    i = lax.broadcasted_iota(I32, shape, axis) & (HG_CHUNK - 1)
    return (HG_CHUNK - 1 - i) if rev else i


def _hg_window_matrix(rev):
    C = HG_CHUNK
    ti = _hg_scan_index((C, 2 * C), 0, rev)
    tj = _hg_scan_index((C, 2 * C), 1, rev)
    blocks = [tj <= ti]
    for l in range(2, HG_LEVELS + 1):
        anchor = ((ti >> l) << l) + (1 << (l - 1)) - 1
        upper = ((ti >> (l - 1)) & 1) == 1
        blocks.append((upper & (tj > anchor) & (tj <= ti)) | (~upper & (tj > ti) & (tj <= anchor)))
    blocks.append(tj > ti)
    return jnp.concatenate([jnp.where(m, 1.0, 0.0) for m in blocks], axis=0).astype(BF16)


def _hg_kernel(*refs, heads, dk, dv, tt, n_blocks, has_init):
    it = iter(refs)
    qzv = [[next(it) for _ in range(3)] for _ in range(2)]
    lb_ref = next(it)
    s0_ref = next(it) if has_init else None
    o_refs = [next(it), next(it)]
    sfin_ref = next(it)
    st_ref = next(it)
    win_ref = next(it)

    t = pl.program_id(1)

    @pl.when((pl.program_id(0) == 0) & (t == 0))
    def _():
        for d in range(2):
            win_ref[d] = _hg_window_matrix(d == 1)

    @pl.when(t == 0)
    def _():
        for d in range(2):
            for hh in range(heads):
                st_ref[d, hh] = s0_ref[d, hh].T if has_init else jnp.zeros((dv, dk), F32)

    C = HG_CHUNK
    n_ch = tt // C
    chunks = [slice(c * C, (c + 1) * C) for c in range(n_ch)]
    levels, uppers = [], []
    for d in range(2):
        ti = _hg_scan_index((C, C), 0, d == 1)
        tj = _hg_scan_index((C, C), 1, d == 1)
        level = jnp.where(tj < ti, 1, 0)
        for l in range(1, HG_LEVELS):
            level = level + jnp.where((tj < ti) & ((ti >> l) != (tj >> l)), 1, 0)
        t_col = _hg_scan_index((C, 1), 0, d == 1)
        levels.append(level)
        uppers.append([None] + [((t_col >> (l - 1)) & 1) == 1 for l in range(1, HG_LEVELS + 1)])

    def head_body(hh, carry):
        ck = pl.ds(pl.multiple_of(hh * dk, LANE), dk)
        cv = pl.ds(pl.multiple_of(hh * dv, LANE), dv)
        jobs = [(d, r) for d in range(2) for r in chunks]
        qg, k, f, hi, lo, v, vb, diag = [], [], [], [], [], [], [], []
        for d in range(2):
            q_ref, z_ref, v_ref = qzv[d]
            lb = lb_ref[d, :, ck]
            z = z_ref[:, ck]
            e = jnp.exp(-jnp.abs(z))
            den = 1.0 + e
            num = jnp.where(z >= 0.0, 1.0 + lb * e, e + lb)
            log_num = jnp.where((z < 0.0) & (lb <= 0.0), z, jnp.log(num))
            log_f = log_num - jnp.log(den)
            k.append((1.0 - lb) * (jnp.where(z > 0.0, e, 1.0) / den))
            f.append(1.0 - k[d])
            log2_f = log_f * LOG2_E
            hi.append(log2_f.astype(BF16))
            lo.append((log2_f - hi[d].astype(F32)).astype(BF16))
            qg.append(_silu(q_ref[:, ck]) * (dk ** -0.5))
            v.append(v_ref[:, cv])
            vb.append(v[d].astype(BF16))
            diag.append(jnp.sum(qg[d] * k[d], axis=-1, keepdims=True))
        wins = [win_ref[0], win_ref[1]]
        sums = [_dot(wins[d], jnp.concatenate([hi[d][r], lo[d][r]], axis=0)) for d, r in jobs]
        pairs = []
        for (d, r), s in zip(jobs, sums):
            ps = [jnp.where(uppers[d][1], qg[d][r] * f[d][r], k[d][r]).astype(BF16)]
            for l in range(2, HG_LEVELS + 1):
                scale = jnp.exp2(s[(l - 1) * C:l * C])
                ps.append((jnp.where(uppers[d][l], qg[d][r], k[d][r]) * scale).astype(BF16))
            pairs.append(ps)
        grams = [[_dot_nt(p, p) for p in ps] for ps in pairs]
        atts = []
        for (d, r), gs in zip(jobs, grams):
            att = jnp.where(levels[d] == 1, gs[0], 0.0)
            for l in range(2, HG_LEVELS + 1):
                att = jnp.where(levels[d] == l, gs[l - 1], att)
            atts.append(att.astype(BF16))
        k_out = [(k[d][r] * jnp.exp2(s[HG_LEVELS * C:])).astype(BF16) for (d, r), s in zip(jobs, sums)]
        q_in = [(qg[d][r] * jnp.exp2(s[:C])).astype(BF16) for (d, r), s in zip(jobs, sums)]
        decays = [jnp.exp2(s[0:1] if d == 1 else s[C - 1:C]) for (d, r), s in zip(jobs, sums)]
        outs = [_dot(att, vb[d][r]) + diag[d][r] * v[d][r] for att, (d, r) in zip(atts, jobs)]
        kvs = [_dot_tn(vb[d][r], ko) for (d, r), ko in zip(jobs, k_out)]
        for d in range(2):
            st = st_ref[d, hh]
            for c in (range(n_ch - 1, -1, -1) if d == 1 else range(n_ch)):
                j = d * n_ch + c
                o_refs[d][chunks[c], cv] = outs[j] + _dot_nt(q_in[j], st.astype(BF16))
                st = st * decays[j] + kvs[j]
            st_ref[d, hh] = st
        return carry

    lax.fori_loop(0, heads, head_body, 0)

    @pl.when(t == n_blocks - 1)
    def _():
        for d in range(2):
            for hh in range(heads):
                sfin_ref[d, hh] = st_ref[d, hh].T


def _hg_call(proj, lb, s0, *, tok0, batch, seq, heads, dk, dv, col_q, col_f, col_i):
    tt = min(seq, 256)
    n = seq // tt
    kw, vw = heads * dk, heads * dv
    assert seq % tt == 0 and tok0 % tt == 0 and tt % HG_CHUNK == 0
    assert col_q % kw == 0 and col_f % kw == 0 and col_i % vw == 0
    r0 = tok0 // tt

    def fwd(b, t):
        return r0 + b * n + t

    def bwd(b, t):
        return r0 + b * n + (n - 1 - t)

    in_specs, args = [], []
    for d, rmap in enumerate((fwd, bwd)):
        in_specs += [
            pl.BlockSpec((tt, kw), lambda b, t, rmap=rmap: (rmap(b, t), col_q // kw)),
            pl.BlockSpec((tt, kw), lambda b, t, rmap=rmap, d=d: (rmap(b, t), col_f // kw + d)),
            pl.BlockSpec((tt, vw), lambda b, t, rmap=rmap: (rmap(b, t), col_i // vw)),
        ]
        args += [proj, proj, proj]
    in_specs.append(pl.BlockSpec((2, 1, kw), lambda b, t: (0, 0, 0)))
    args.append(lb.reshape(2, 1, kw))
    state_spec = pl.BlockSpec((None, 2, heads, dk, dv), lambda b, t: (b, 0, 0, 0, 0))
    if s0 is not None:
        in_specs.append(state_spec)
        args.append(s0)
    n_tok = batch * seq
    return pl.pallas_call(
        functools.partial(_hg_kernel, heads=heads, dk=dk, dv=dv, tt=tt, n_blocks=n, has_init=s0 is not None),
        grid=(batch, n),
        in_specs=in_specs,
        out_specs=[
            pl.BlockSpec((tt, vw), lambda b, t: (b * n + t, 0)),
            pl.BlockSpec((tt, vw), lambda b, t: (b * n + (n - 1 - t), 0)),
            state_spec,
        ],
        out_shape=[
            jax.ShapeDtypeStruct((n_tok, vw), F32),
            jax.ShapeDtypeStruct((n_tok, vw), F32),
            jax.ShapeDtypeStruct((batch, 2, heads, dk, dv), F32),
        ],
        scratch_shapes=[pltpu.VMEM((2, heads, dv, dk), F32),
                        pltpu.VMEM((2, (HG_LEVELS + 1) * HG_CHUNK, 2 * HG_CHUNK), BF16)],
        compiler_params=_cparams("arbitrary", "arbitrary"),
        name="hgrn2_scan",
    )(*args)


def _mix_out_kernel(*refs, d, ret_heads, hg_heads, ctx_blocks):
    scans = [refs[4 * p:4 * p + 4] for p in range(2)]
    (rg_ref, go_ref, h_ref, x_ref, mod_ref, g2_ref, wr_ref, wh_ref, wm_ref, wo_ref, wrt_ref,
     x1_ref, h2_ref, lgt_ref, o_ref, og_ref) = refs[8:]

    for p, in_set in enumerate((pl.program_id(0) < ctx_blocks, pl.program_id(0) >= ctx_blocks)):
        @pl.when(in_set)
        def _(p=p):
            o_ref[...] = scans[p][0][...] + scans[p][1][...]
            og_ref[...] = scans[p][2][...] + scans[p][3][...]

    dvr = o_ref.shape[1] // ret_heads
    dvh = og_ref.shape[1] // hg_heads
    tm = o_ref.shape[0]
    halves = [slice(0, tm // 2), slice(tm // 2, tm)]
    ret_in, hg_in = [], []
    for r in halves:
        parts = []
        for hh in range(ret_heads):
            oh = o_ref[r, hh * dvr:(hh + 1) * dvr]
            ctr = oh - jnp.mean(oh, axis=-1, keepdims=True)
            parts.append(ctr * lax.rsqrt(jnp.mean(ctr * ctr, axis=-1, keepdims=True) + NORM_EPS))
        ret_in.append((jnp.concatenate(parts, axis=1) * _silu(rg_ref[r, :])).astype(BF16))
        parts = [_rms(og_ref[r, hh * dvh:(hh + 1) * dvh]) for hh in range(hg_heads)]
        hg_in.append((jnp.concatenate(parts, axis=1) * _silu(go_ref[r, :])).astype(BF16))
    ret_out = [_dot(a, wr_ref[...]) for a in ret_in]
    hg_out = [_dot(a, wh_ref[...]) for a in hg_in]
    gates = [jax.nn.sigmoid(_dot(h_ref[r, :], wm_ref[...])) for r in halves]
    merged = [(g[:, :d] * a + g[:, d:] * b).astype(BF16) for g, a, b in zip(gates, ret_out, hg_out)]
    mix = [_dot(m, wo_ref[...]) for m in merged]
    w_hi, w_lo = wrt_ref[0], wrt_ref[1]
    for r, m in zip(halves, mix):
        x1 = x_ref[r, :] + mod_ref[:, 2 * d:3 * d] * m
        x1_ref[r, :] = x1
        h2 = _rms(x1) * g2_ref[...]
        h2 = h2 * (1.0 + mod_ref[:, 4 * d:5 * d]) + mod_ref[:, 3 * d:4 * d]
        h2_ref[r, :] = h2.astype(BF16)
        h_hi = h2.astype(BF16)
        h_lo = (h2 - h_hi.astype(F32)).astype(BF16)
        lgt_ref[:, r] = _dot_nt(w_hi, h_hi) + (_dot_nt(w_hi, h_lo) + _dot_nt(w_lo, h_hi))


def _mix_out_call(scans_ctx, scans_lat, proj, h, x, mod_l, mod_row, g2, w_ret_o, w_hg_o, w_merge, w_out,
                  w_router_t, tm, *, ret_heads, hg_heads, col_rg, col_go):
    n, d = x.shape
    vr, vh = scans_ctx[0].shape[1], scans_ctx[2].shape[1]
    ne = w_router_t.shape[1]
    ctx_blocks = scans_ctx[0].shape[0] // tm
    assert col_rg % vr == 0 and col_go % vh == 0

    def row(w):
        return pl.BlockSpec((tm, w), lambda i: (i, 0))

    def full(a):
        return pl.BlockSpec(a.shape, lambda i: (0,) * a.ndim)

    ctx_row = lambda a: pl.BlockSpec((tm, a.shape[1]), lambda i: (jnp.minimum(i, ctx_blocks - 1), 0))
    lat_row = lambda a: pl.BlockSpec((tm, a.shape[1]), lambda i: (jnp.maximum(i - ctx_blocks, 0), 0))
    return pl.pallas_call(
        functools.partial(_mix_out_kernel, d=d, ret_heads=ret_heads, hg_heads=hg_heads, ctx_blocks=ctx_blocks),
        grid=(n // tm,),
        in_specs=[ctx_row(a) for a in scans_ctx] + [lat_row(a) for a in scans_lat] + [
            pl.BlockSpec((tm, vr), lambda i: (i, col_rg // vr)),
            pl.BlockSpec((tm, vh), lambda i: (i, col_go // vh)),
            row(d), row(d),
            pl.BlockSpec((None, 1, 6 * d), lambda i: (mod_row(i), 0, 0)),
            pl.BlockSpec((1, d), lambda i: (0, 0)),
            full(w_ret_o), full(w_hg_o), full(w_merge), full(w_out), full(w_router_t),
        ],
        out_specs=[row(d), row(d), pl.BlockSpec((ne, tm), lambda i: (0, i))],
        out_shape=[
            jax.ShapeDtypeStruct((n, d), F32),
            jax.ShapeDtypeStruct((n, d), BF16),
            jax.ShapeDtypeStruct((ne, n), F32),
        ],
        scratch_shapes=[pltpu.VMEM((tm, vr), F32), pltpu.VMEM((tm, vh), F32)],
        compiler_params=_cparams("arbitrary"),
        name="mixer_out",
    )(*scans_ctx, *scans_lat, proj, proj, h, x, mod_l, g2.reshape(1, d), w_ret_o, w_hg_o, w_merge, w_out,
      w_router_t)


def _route_kernel(lg_ref, aff_ref, sel_ref, posx_ref, flag_ref, *, cap, n):
    lg = lg_ref[...]
    ne = lg.shape[0]
    ex = jnp.exp(lg - jnp.max(lg, axis=0, keepdims=True))
    aff = ex / jnp.sum(ex, axis=0, keepdims=True)
    aff_ref[...] = aff
    bits = pltpu.bitcast(aff, I32)

    def search(i, lo):
        cand = lo | lax.shift_left(jnp.int32(1), 30 - i)
        cnt = jnp.sum(jnp.where(bits >= cand, 1.0, 0.0), axis=1, keepdims=True)
        return jnp.where(cnt >= cap, cand, lo)

    thr = lax.fori_loop(0, 31, search, jnp.zeros((ne, 1), I32))
    gt = bits > thr
    eq = bits == thr
    need = cap - jnp.sum(jnp.where(gt, 1.0, 0.0), axis=1, keepdims=True)
    upper = jnp.where(lax.broadcasted_iota(I32, (LANE, LANE), 0) < lax.broadcasted_iota(I32, (LANE, LANE), 1),
                      1.0, 0.0).astype(BF16)

    def exclusive_count(store):
        def body(ci, carry):
            sl = pl.ds(pl.multiple_of(ci * LANE, LANE), LANE)
            x = flag_ref[:, sl]
            store(sl, carry + _dot(x.astype(BF16), upper))
            return carry + jnp.sum(x, axis=1, keepdims=True)

        lax.fori_loop(0, n // LANE, body, jnp.zeros((ne, 1), F32))

    flag_ref[...] = jnp.where(eq, 1.0, 0.0)

    def store_tie_rank(sl, rank):
        posx_ref[:, sl] = rank.astype(I32)

    exclusive_count(store_tie_rank)
    sel = gt | (eq & (posx_ref[...].astype(F32) < need))
    sel_ref[...] = jnp.where(sel, 1, 0).astype(I32)
    flag_ref[...] = jnp.where(sel, 1.0, 0.0)

    def store_pos(sl, cnt):
        posx_ref[:, sl] = cnt.astype(I32)

    exclusive_count(store_pos)


def _route_call(logits_t, cap):
    ne, n = logits_t.shape
    spec = pl.BlockSpec((ne, n), lambda: (0, 0))
    return pl.pallas_call(
        functools.partial(_route_kernel, cap=cap, n=n),
        in_specs=[spec],
        out_specs=[spec, spec, spec],
        out_shape=[
            jax.ShapeDtypeStruct((ne, n), F32),
            jax.ShapeDtypeStruct((ne, n), I32),
            jax.ShapeDtypeStruct((ne, n), I32),
        ],
        scratch_shapes=[pltpu.VMEM((ne, n), F32)],
        compiler_params=pltpu.CompilerParams(vmem_limit_bytes=VMEM_LIMIT),
        name="expert_choice_route",
    )(logits_t)


def _moe_round_masks(b, r, base_ref, sel_ref, posx_ref, ne, value_rows=None):
    n_tok = sel_ref.shape[1]
    row_iota = lax.broadcasted_iota(I32, (MOE_ROUND, n_tok), 0)
    pieces = []
    for e in range(ne):
        local = posx_ref[e:e + 1, :] - (base_ref[b * ne + e] + r * MOE_ROUND)
        hit = (sel_ref[e:e + 1, :] == 1) & (local == row_iota)
        value = 1.0 if value_rows is None else value_rows[e:e + 1, :]
        pieces.append(jnp.where(hit, value, 0.0).astype(BF16))
    return jnp.concatenate(pieces, axis=0)


def _dispatch_kernel(base_ref, start_ref, pc_ref, nr_ref, h_ref, sel_ref, posx_ref, xe_zero_ref, xe_ref,
                     stage_ref, sem, *, ne):
    del xe_zero_ref
    b = pl.program_id(0)
    n_rounds = nr_ref[b]

    def pieces(r, slot, fn):
        for e in range(ne):
            rem = pc_ref[b * ne + e] - r * MOE_ROUND
            dst = start_ref[b * ne + e] + r * MOE_ROUND
            for size, cond, off in (
                    (MOE_ROUND, rem >= MOE_ROUND, 0),
                    (32, (rem > 0) & (rem < MOE_ROUND) & ((rem & 32) != 0), 0),
                    (16, (rem > 0) & (rem < MOE_ROUND) & ((rem & 16) != 0), rem & 32)):
                @pl.when(cond)
                def _(size=size, off=off, e=e, dst=dst):
                    src_rows = pl.ds(pl.multiple_of(e * MOE_ROUND + off, BF16_SUBLANE), size)
                    dst_rows = pl.ds(pl.multiple_of(dst + off, BF16_SUBLANE), size)
                    fn(pltpu.make_async_copy(stage_ref.at[slot, src_rows], xe_ref.at[e, dst_rows], sem.at[slot]))

    def round_body(r, carry):
        slot = r % 2
        onehot = _moe_round_masks(b, r, base_ref, sel_ref, posx_ref, ne)
        stage_ref[slot] = _dot(onehot, h_ref[...]).astype(BF16)

        @pl.when(r >= 1)
        def _():
            pieces(r - 1, 1 - slot, lambda cp: cp.wait())

        pieces(r, slot, lambda cp: cp.start())
        return carry

    lax.fori_loop(0, n_rounds, round_body, 0)

    @pl.when(n_rounds >= 1)
    def _():
        pieces(n_rounds - 1, (n_rounds - 1) % 2, lambda cp: cp.wait())


def _dispatch_call(h2, sel, posx, tables, list_rows, *, sb):
    n, d = h2.shape
    ne = sel.shape[0]
    xe_zero = jnp.zeros((ne, list_rows, d), BF16)
    grid_spec = pltpu.PrefetchScalarGridSpec(
        num_scalar_prefetch=4,
        grid=(n // sb,),
        in_specs=[
            pl.BlockSpec((sb, d), lambda b, *_: (b, 0)),
            pl.BlockSpec((ne, sb), lambda b, *_: (0, b)),
            pl.BlockSpec((ne, sb), lambda b, *_: (0, b)),
            pl.BlockSpec(memory_space=pl.ANY),
        ],
        out_specs=pl.BlockSpec(memory_space=pl.ANY),
        scratch_shapes=[pltpu.VMEM((2, ne * MOE_ROUND, d), BF16), pltpu.SemaphoreType.DMA((2,))],
    )
    return pl.pallas_call(
        functools.partial(_dispatch_kernel, ne=ne),
        grid_spec=grid_spec,
        out_shape=jax.ShapeDtypeStruct((ne, list_rows, d), BF16),
        input_output_aliases={7: 0},
        compiler_params=_cparams("arbitrary"),
        name="expert_dispatch",
    )(*tables, h2, sel, posx, xe_zero)


def _expert_ffn_kernel(tot_ref, x_ref, wg_ref, wu_ref, wd_ref, y_ref, wgb_ref, wub_ref, wdb_ref, acc_ref,
                       *, rt, n_ff):
    e, f, t = pl.program_id(0), pl.program_id(1), pl.program_id(2)

    @pl.when(t == 0)
    def _():
        wgb_ref[...] = wg_ref[...].astype(BF16)
        wub_ref[...] = wu_ref[...].astype(BF16)
        wdb_ref[...] = wd_ref[...].astype(BF16)

    rows = pl.ds(pl.multiple_of(t * rt, rt), rt)
    live = t * rt < tot_ref[e]

    @pl.when(live)
    def _():
        x = x_ref[...]
        hid = (_silu(_dot(x, wgb_ref[...])) * _dot(x, wub_ref[...])).astype(BF16)
        y = _dot(hid, wdb_ref[...])

        @pl.when(f == 0)
        def _():
            acc_ref[rows, :] = y

        @pl.when(f != 0)
        def _():
            acc_ref[rows, :] += y

    @pl.when((f == n_ff - 1) & live)
    def _():
        y_ref[...] = acc_ref[rows, :].astype(BF16)

    @pl.when((f == n_ff - 1) & jnp.logical_not(live))
    def _():
        y_ref[...] = jnp.zeros_like(y_ref)


def _expert_ffn_call(xe, totals, wg, wu, wd, layer, *, rt, fc):
    ne, list_rows, d = xe.shape
    ff = wg.shape[-1]
    n_ff = ff // fc
    grid_spec = pltpu.PrefetchScalarGridSpec(
        num_scalar_prefetch=1,
        grid=(ne, n_ff, list_rows // rt),
        in_specs=[
            pl.BlockSpec((None, rt, d), lambda e, f, t, *_: (e, t, 0)),
            pl.BlockSpec((None, None, d, fc), lambda e, f, t, *_: (layer, e, 0, f)),
            pl.BlockSpec((None, None, d, fc), lambda e, f, t, *_: (layer, e, 0, f)),
            pl.BlockSpec((None, None, fc, d), lambda e, f, t, *_: (layer, e, f, 0)),
        ],
        out_specs=pl.BlockSpec((None, rt, d), lambda e, f, t, *_: (e, jnp.where(f == n_ff - 1, t, 0), 0)),
        scratch_shapes=[pltpu.VMEM((d, fc), BF16), pltpu.VMEM((d, fc), BF16), pltpu.VMEM((fc, d), BF16),
                        pltpu.VMEM((list_rows, d), F32)],
    )
    return pl.pallas_call(
        functools.partial(_expert_ffn_kernel, rt=rt, n_ff=n_ff),
        grid_spec=grid_spec,
        out_shape=jax.ShapeDtypeStruct((ne, list_rows, d), BF16),
        compiler_params=_cparams("arbitrary", "arbitrary", "arbitrary"),
        name="expert_ffn",
    )(totals, xe, wg, wu, wd)


def _combine_kernel(base_ref, start_ref, pc_ref, nr_ref, sel_ref, posx_ref, aff_ref, y_ref, out_ref, ybuf_ref, sem,
                    *, ne):
    b = pl.program_id(0)
    n_rounds = nr_ref[b]

    @pl.when(b == 0)
    def _():
        ybuf_ref[...] = jnp.zeros_like(ybuf_ref)

    def windows(r, slot, fn):
        for e in range(ne):
            @pl.when(pc_ref[b * ne + e] - r * MOE_ROUND > 0)
            def _(e=e):
                src = pl.ds(pl.multiple_of(start_ref[b * ne + e] + r * MOE_ROUND, BF16_SUBLANE), MOE_ROUND)
                fn(pltpu.make_async_copy(y_ref.at[e, src], ybuf_ref.at[slot, pl.ds(e * MOE_ROUND, MOE_ROUND)],
                                         sem.at[slot]))

    out_ref[...] = jnp.zeros_like(out_ref)

    @pl.when(n_rounds >= 1)
    def _():
        windows(0, 0, lambda cp: cp.start())

    def round_body(r, carry):
        slot = r % 2

        @pl.when(r + 1 < n_rounds)
        def _():
            windows(r + 1, 1 - slot, lambda cp: cp.start())

        windows(r, slot, lambda cp: cp.wait())
        weights = _moe_round_masks(b, r, base_ref, sel_ref, posx_ref, ne, value_rows=aff_ref)
        out_ref[...] += _dot_tn(weights, ybuf_ref[slot])
        return carry

    lax.fori_loop(0, n_rounds, round_body, 0)


def _combine_call(ye, sel, posx, aff, tables, *, sb):
    ne, _, d = ye.shape
    n = sel.shape[1]
    grid_spec = pltpu.PrefetchScalarGridSpec(
        num_scalar_prefetch=4,
        grid=(n // sb,),
        in_specs=[
            pl.BlockSpec((ne, sb), lambda b, *_: (0, b)),
            pl.BlockSpec((ne, sb), lambda b, *_: (0, b)),
            pl.BlockSpec((ne, sb), lambda b, *_: (0, b)),
            pl.BlockSpec(memory_space=pl.ANY),
        ],
        out_specs=pl.BlockSpec((sb, d), lambda b, *_: (b, 0)),
        scratch_shapes=[pltpu.VMEM((2, ne * MOE_ROUND, d), BF16), pltpu.SemaphoreType.DMA((2,))],
    )
    return pl.pallas_call(
        functools.partial(_combine_kernel, ne=ne),
        grid_spec=grid_spec,
        out_shape=jax.ShapeDtypeStruct((n, d), F32),
        compiler_params=_cparams("arbitrary"),
        name="expert_combine",
    )(*tables, sel, posx, aff, ye)


def _moe_tables(posx_sets, caps, sb):
    base, cnt = [], []
    for posx, cap in zip(posx_sets, caps):
        ne = posx.shape[0]
        edges = jnp.concatenate([posx[:, ::sb], jnp.full((ne, 1), cap, I32)], axis=1)
        base.append(edges[:, :-1].T)
        cnt.append((edges[:, 1:] - edges[:, :-1]).T)
    base = jnp.concatenate(base, axis=0)
    cnt = jnp.concatenate(cnt, axis=0)
    pc = (cnt + (BF16_SUBLANE - 1)) // BF16_SUBLANE * BF16_SUBLANE
    ends = jnp.cumsum(pc, axis=0)
    start = ends - pc
    n_rounds = jnp.max((pc + (MOE_ROUND - 1)) // MOE_ROUND, axis=1)
    tables = tuple(a.reshape(-1).astype(I32) for a in (base, start, pc, n_rounds))
    return tables, ends[-1].astype(I32)


def _rope_tables(n_tokens, dk):
    rows = n_tokens // GRID_W
    r, c = jnp.meshgrid(jnp.arange(rows), jnp.arange(GRID_W), indexing="ij")
    pos = jnp.stack([r.reshape(-1), c.reshape(-1)], axis=-1).astype(F32)
    nf = dk // 4
    inv_freq = ROPE_BASE ** (-jnp.arange(nf, dtype=F32) / nf)
    ang = pos[:, :, None] * inv_freq
    cos, sin = jnp.cos(ang), jnp.sin(ang)
    cos_t = jnp.concatenate([cos[:, 0], cos[:, 0], cos[:, 1], cos[:, 1]], axis=-1)
    sin_t = jnp.concatenate([-sin[:, 0], sin[:, 0], -sin[:, 1], sin[:, 1]], axis=-1)
    return cos_t, sin_t


def kernel(x_prompt, x_sample, state_ret, state_hgrn, c, c_ctx, ada_w, ada_b, norm_mix_g, norm_ffn_g, w_in,
           ret_gamma_logit, hg_lb_logit, w_ret_o, w_hg_o, w_merge, w_out, w_router, w_exp_gate, w_exp_up,
           w_exp_down, final_g):
    b_ctx, t_ctx, d = x_prompt.shape
    b_lat, t_lat, _ = x_sample.shape
    depth = w_in.shape[0]
    ret_heads, ret_dk, ret_dv = state_ret.shape[3:]
    hg_heads, hg_dk, hg_dv = state_hgrn.shape[3:]
    ne = w_router.shape[-1]
    n_ctx, n_lat = b_ctx * t_ctx, b_lat * t_lat
    n_tok = n_ctx + n_lat
    qw, vw = ret_heads * ret_dk, ret_heads * ret_dv
    kw, hw = hg_heads * hg_dk, hg_heads * hg_dv
    col_rg = 2 * qw + vw
    col_gq = col_rg + vw
    col_gf = col_gq + kw
    col_gi = col_gf + 2 * kw
    col_go = col_gi + hw
    assert col_go + hw == w_in.shape[-1] and b_lat < MOD_ROWS

    tm = 256
    assert t_ctx % tm == 0 and t_lat % tm == 0
    tm_proj = 1024 if n_tok % 1024 == 0 else tm
    sb = 1024 if (n_ctx % 4096 == 0 and n_lat % 4096 == 0) else 256
    assert n_ctx % sb == 0 and n_lat % sb == 0
    sets = ((0, n_ctx), (n_ctx, n_lat))
    caps = [CAPACITY_FACTOR * n_set // ne for _, n_set in sets]
    list_rows = sum(caps) + BF16_SUBLANE * (n_tok // sb)
    n_tiles = max(1, list_rows // 512)
    rt = -(-list_rows // (n_tiles * BF16_SUBLANE)) * BF16_SUBLANE
    list_rows = n_tiles * rt

    def mod_rows(tmx):
        return lambda i: jnp.where(i < n_ctx // tmx, 0, 1 + (i - n_ctx // tmx) // (t_lat // tmx))

    mod_row = mod_rows(tm)
    tm_norm = 512 if (n_ctx % 512 == 0 and t_lat % 512 == 0) else tm
    norm_row = mod_rows(tm_norm)

    x = jnp.concatenate([x_prompt.reshape(n_ctx, d), x_sample.reshape(n_lat, d)], axis=0)
    cvec = jnp.zeros((MOD_ROWS, d), F32).at[0].set(c_ctx).at[1:1 + b_lat].set(c)
    mod = _mod_call(cvec, ada_w, ada_b).reshape(depth, MOD_ROWS, 1, 6 * d)

    log_gamma = jax.nn.log_sigmoid(ret_gamma_logit.astype(F32))
    p_lb = jax.nn.softmax(hg_lb_logit.astype(F32), axis=0)
    hg_lb = jnp.clip(jnp.cumsum(p_lb, axis=0) - p_lb[0:1], 0.0, 1.0 - 1e-6)
    rope = _rope_tables(t_lat, ret_dk)

    w_ret_o_b, w_hg_o_b, w_out_b = w_ret_o.astype(BF16), w_hg_o.astype(BF16), w_out.astype(BF16)
    w_merge_b = w_merge.astype(BF16)
    w_router_t = jnp.swapaxes(w_router, 1, 2)
    w_router_hi = w_router_t.astype(BF16)
    w_router_t = jnp.stack([w_router_hi, (w_router_t - w_router_hi.astype(F32)).astype(BF16)], axis=1)
    fc = min(1024, w_exp_gate.shape[-1])
    tn = min(2048, w_in.shape[-1])

    _, h = _norm_call(x, None, None, mod[0], norm_mix_g[0], norm_row, tm_norm)
    new_ret, new_hg = [], []
    for l in range(depth):
        proj = _proj_call(h, w_in, l, tm_proj, tn)

        ret_kw = dict(heads=ret_heads, dk=ret_dk, dv=ret_dv)
        orf_c, orb_c, s_ret = _ret_call(proj, log_gamma[l], None, None, tok0=0, batch=b_ctx, seq=t_ctx, **ret_kw)
        orf_l, orb_l, _ = _ret_call(proj, log_gamma[l], state_ret[:, l], rope, tok0=n_ctx, batch=b_lat, seq=t_lat,
                                    **ret_kw)
        hg_kw = dict(heads=hg_heads, dk=hg_dk, dv=hg_dv, col_q=col_gq, col_f=col_gf, col_i=col_gi)
        ogf_c, ogb_c, s_hg = _hg_call(proj, hg_lb[l], None, tok0=0, batch=b_ctx, seq=t_ctx, **hg_kw)
        ogf_l, ogb_l, _ = _hg_call(proj, hg_lb[l], state_hgrn[:, l], tok0=n_ctx, batch=b_lat, seq=t_lat, **hg_kw)
        new_ret.append(s_ret)
        new_hg.append(s_hg)

        x, h2, logits_t = _mix_out_call(
            (orf_c, orb_c, ogf_c, ogb_c), (orf_l, orb_l, ogf_l, ogb_l), proj, h, x, mod[l],
            mod_row, norm_ffn_g[l], w_ret_o_b[l], w_hg_o_b[l], w_merge_b[l], w_out_b[l], w_router_t[l], tm,
            ret_heads=ret_heads, hg_heads=hg_heads, col_rg=col_rg, col_go=col_go)

        routed = [_route_call(logits_t[:, lo:lo + n_set], cap) for (lo, n_set), cap in zip(sets, caps)]
        aff, sel, posx = (jnp.concatenate([r[i] for r in routed], axis=1) for i in range(3))
        tables, totals = _moe_tables([r[2] for r in routed], caps, sb)
        xe = _dispatch_call(h2, sel, posx, tables, list_rows, sb=sb)
        ye = _expert_ffn_call(xe, totals, w_exp_gate, w_exp_up, w_exp_down, l, rt=rt, fc=fc)
        ffn = _combine_call(ye, sel, posx, aff, tables, sb=sb)

        if l + 1 < depth:
            x, h = _norm_call(x, ffn, mod[l], mod[l + 1], norm_mix_g[l + 1], norm_row, tm_norm)
        else:
            _, y_prompt = _norm_call(x, ffn, mod[l], None, final_g, norm_row, tm_norm, rows=(0, n_ctx))
            _, y_sample = _norm_call(x, ffn, mod[l], None, final_g, norm_row, tm_norm, rows=(n_ctx, n_lat))

    y_prompt = y_prompt.reshape(b_ctx, t_ctx, d)
    y_sample = y_sample.reshape(b_lat, t_lat, d)
    return y_prompt, y_sample, jnp.stack(new_ret, axis=1), jnp.stack(new_hg, axis=1)
```

```python
import functools

import jax
import jax.numpy as jnp
from jax import lax
from jax.experimental import pallas as pl
from jax.experimental.pallas import tpu as pltpu

F32 = jnp.float32
BF16 = jnp.bfloat16
I32 = jnp.int32
HIGHEST = lax.Precision.HIGHEST

NORM_EPS = 1e-6
LOG2_E = 1.4426950408889634
ROPE_BASE = 10000.0
GRID_W = 64
CAPACITY_FACTOR = 2
RET_CHUNK = 128
RET_CHUNKS_PER_STEP = 2
HG_LEVELS = 6
HG_CHUNK = 1 << HG_LEVELS
HG_LOG_SIGMOID_LINEAR = -60.0
MOD_ROWS = 8
LANE = 128
F32_SUBLANE = 8
BF16_SUBLANE = 16
MOE_ROUND = 64
VMEM_LIMIT = 62 * 1024 * 1024


def _cparams(*sem):
    return pltpu.CompilerParams(dimension_semantics=sem, vmem_limit_bytes=VMEM_LIMIT)


def _dot(a, b, **kw):
    return jnp.dot(a, b, preferred_element_type=F32, **kw)


def _dot_nt(a, b, **kw):
    return lax.dot_general(a, b, (((1,), (1,)), ((), ())), preferred_element_type=F32, **kw)


def _dot_tn(a, b, **kw):
    return lax.dot_general(a, b, (((0,), (0,)), ((), ())), preferred_element_type=F32, **kw)


def _silu(x):
    return x * jax.nn.sigmoid(x)


def _mod_kernel(c_ref, w_ref, b_ref, o_ref):
    s = _silu(c_ref[...])
    o_ref[...] = _dot(s, w_ref[...], precision=HIGHEST) + b_ref[...]


def _mod_call(cvec, ada_w, ada_b):
    depth, d, six_d = ada_w.shape
    tn = 6 * LANE * 2
    assert six_d % tn == 0
    return pl.pallas_call(
        _mod_kernel,
        grid=(depth, six_d // tn),
        in_specs=[
            pl.BlockSpec((MOD_ROWS, d), lambda l, j: (0, 0)),
            pl.BlockSpec((None, d, tn), lambda l, j: (l, 0, j)),
            pl.BlockSpec((None, 1, tn), lambda l, j: (l, 0, j)),
        ],
        out_specs=pl.BlockSpec((None, MOD_ROWS, tn), lambda l, j: (l, 0, j)),
        out_shape=jax.ShapeDtypeStruct((depth, MOD_ROWS, six_d), F32),
        compiler_params=_cparams("arbitrary", "arbitrary"),
        name="adaln_mod",
    )(cvec, ada_w, ada_b.reshape(depth, 1, six_d))


def _rms(x):
    return x * lax.rsqrt(jnp.mean(x * x, axis=-1, keepdims=True) + NORM_EPS)


def _norm_kernel(*refs, d, residual, final):
    it = iter(refs)
    x_ref = next(it)
    f_ref, gate_ref = (next(it), next(it)) if residual else (None, None)
    mod_ref = None if final else next(it)
    g_ref = next(it)
    xo_ref = next(it) if (residual and not final) else None
    h_ref = next(it)
    x = x_ref[...]
    if residual:
        x = x + gate_ref[...] * f_ref[...]
        if xo_ref is not None:
            xo_ref[...] = x
    y = _rms(x) * g_ref[...]
    if not final:
        y = y * (1.0 + mod_ref[:, d:2 * d]) + mod_ref[:, 0:d]
    h_ref[...] = y.astype(h_ref.dtype)


def _norm_call(x, ffn, mod_gate, mod_next, g, mod_row, tm, rows=None):
    n, d = x.shape
    residual = ffn is not None
    final = mod_next is None
    r0, n_out = (0, n) if rows is None else rows
    blk0 = r0 // tm
    row = pl.BlockSpec((tm, d), lambda i: (blk0 + i, 0))
    in_specs, args = [row], [x]
    if residual:
        in_specs += [row, pl.BlockSpec((None, 1, d), lambda i: (mod_row(blk0 + i), 0, 5))]
        args += [ffn, mod_gate]
    if not final:
        in_specs.append(pl.BlockSpec((None, 1, 2 * d), lambda i: (mod_row(blk0 + i), 0, 0)))
        args.append(mod_next)
    in_specs.append(pl.BlockSpec((1, d), lambda i: (0, 0)))
    args.append(g.reshape(1, d))
    out_row = pl.BlockSpec((tm, d), lambda i: (i, 0))
    out_specs, out_shape = [out_row], [jax.ShapeDtypeStruct((n_out, d), F32 if final else BF16)]
    if residual and not final:
        out_specs = [out_row, out_row]
        out_shape = [jax.ShapeDtypeStruct((n_out, d), F32)] + out_shape
    outs = pl.pallas_call(
        functools.partial(_norm_kernel, d=d, residual=residual, final=final),
        grid=(n_out // tm,),
        in_specs=in_specs,
        out_specs=out_specs,
        out_shape=out_shape,
        compiler_params=_cparams("arbitrary"),
        name="residual_norm",
    )(*args)
    return outs if len(outs) == 2 else (x, outs[0])


def _proj_kernel(h_ref, w_ref, o_ref, wb_ref):
    @pl.when(pl.program_id(1) == 0)
    def _():
        wb_ref[...] = w_ref[...].astype(BF16)

    o_ref[...] = _dot(h_ref[...], wb_ref[...])


def _proj_call(h, w, layer, tm, tn):
    n, d = h.shape
    cols = w.shape[-1]
    return pl.pallas_call(
        _proj_kernel,
        grid=(cols // tn, n // tm),
        in_specs=[
            pl.BlockSpec((tm, d), lambda j, i: (i, 0)),
            pl.BlockSpec((None, d, tn), lambda j, i: (layer, 0, j)),
        ],
        out_specs=pl.BlockSpec((tm, tn), lambda j, i: (i, j)),
        out_shape=jax.ShapeDtypeStruct((n, cols), F32),
        scratch_shapes=[pltpu.VMEM((d, tn), BF16)],
        compiler_params=_cparams("arbitrary", "arbitrary"),
        name="in_proj",
    )(h, w)


def _swap_half_pairs(x):
    lane = lax.broadcasted_iota(I32, x.shape, 1)
    return jnp.where((lane & 32) == 0, pltpu.roll(x, LANE - 32, 1), pltpu.roll(x, 32, 1))


def _ret_kernel(*refs, heads, dk, dv, chunk, n_chunks, has_init, has_rope):
    it = iter(refs)
    lg_ref = next(it)
    qkv = [[next(it) for _ in range(3)] for _ in range(2)]
    s0_ref = next(it) if has_init else None
    rope = [[next(it) for _ in range(2)] for _ in range(2)] if has_rope else None
    o_refs = [next(it), next(it)]
    sfin_ref = next(it)
    s_ref = next(it)

    c = pl.program_id(1)

    @pl.when(c == 0)
    def _():
        if has_init:
            s_ref[...] = s0_ref[...]
        else:
            s_ref[...] = jnp.zeros_like(s_ref)

    L = chunk
    ii = lax.broadcasted_iota(I32, (L, L), 0)
    jj = lax.broadcasted_iota(I32, (L, L), 1)
    pcol = lax.broadcasted_iota(I32, (L, 1), 0).astype(F32)
    n_sub = qkv[0][0].shape[0] // L
    jobs = [(d, hh, r) for d in range(2) for hh in range(heads) for r in range(n_sub)]
    qs, ks, vs, scores, q_in, k_out = [], [], [], [], [], []
    for d, hh, r in jobs:
        q_ref, k_ref, v_ref = qkv[d]
        rows = slice(r * L, (r + 1) * L)
        q = q_ref[rows, hh * dk:(hh + 1) * dk]
        k = k_ref[rows, hh * dk:(hh + 1) * dk] * (dk ** -0.5)
        if has_rope:
            cos, sin = rope[d][0][rows, :], rope[d][1][rows, :]
            q = q * cos + _swap_half_pairs(q) * sin
            k = k * cos + _swap_half_pairs(k) * sin
        qs.append(q)
        ks.append(k)
        vs.append(v_ref[rows, hh * dv:(hh + 1) * dv].astype(BF16))
    for (d, hh, r), q, k in zip(jobs, qs, ks):
        dif = ((ii - jj) if d == 0 else (jj - ii)).astype(F32)
        lg = lg_ref[d, hh]
        decay = jnp.where(dif >= 0.0, jnp.exp(lg * jnp.maximum(dif, 0.0)), 0.0)
        scores.append((_dot_nt(q.astype(BF16), k.astype(BF16)) * decay).astype(BF16))
    for (d, hh, r), q, k in zip(jobs, qs, ks):
        lg = lg_ref[d, hh]
        q_pow, k_pow = (pcol + 1.0, (L - 1.0) - pcol) if d == 0 else (L - pcol, pcol)
        q_in.append((q * jnp.exp(lg * q_pow)).astype(BF16))
        k_out.append((k * jnp.exp(lg * k_pow)).T.astype(BF16))
    intra = [_dot(sc, v) for sc, v in zip(scores, vs)]
    kvs = [_dot(ko, v) for ko, v in zip(k_out, vs)]
    for d in range(2):
        for hh in range(heads):
            s = s_ref[d, hh]
            chunk_decay = jnp.exp(jnp.full((1, 1), lg_ref[d, hh] * L, F32))
            for r in (range(n_sub) if d == 0 else range(n_sub - 1, -1, -1)):
                j = (d * heads + hh) * n_sub + r
                o_refs[d][r * L:(r + 1) * L, hh * dv:(hh + 1) * dv] = intra[j] + _dot(q_in[j], s.astype(BF16))
                s = s * chunk_decay + kvs[j]
            s_ref[d, hh] = s

    @pl.when(c == n_chunks - 1)
    def _():
        sfin_ref[...] = s_ref[...]


def _ret_call(proj, log_gamma, s0, rope, *, tok0, batch, seq, heads, dk, dv):
    rb = RET_CHUNK * RET_CHUNKS_PER_STEP
    n = seq // rb
    qw, vw = heads * dk, heads * dv
    assert seq % rb == 0 and tok0 % rb == 0 and vw % qw == 0
    r0 = tok0 // rb

    def fwd(b, c):
        return r0 + b * n + c

    def bwd(b, c):
        return r0 + b * n + (n - 1 - c)

    in_specs = [pl.BlockSpec(memory_space=pltpu.SMEM)]
    args = [log_gamma]
    for rmap in (fwd, bwd):
        in_specs += [
            pl.BlockSpec((rb, qw), lambda b, c, rmap=rmap: (rmap(b, c), 0)),
            pl.BlockSpec((rb, qw), lambda b, c, rmap=rmap: (rmap(b, c), 1)),
            pl.BlockSpec((rb, vw), lambda b, c, rmap=rmap: (rmap(b, c), (2 * qw) // vw)),
        ]
        args += [proj, proj, proj]
    state_spec = pl.BlockSpec((None, 2, heads, dk, dv), lambda b, c: (b, 0, 0, 0, 0))
    if s0 is not None:
        in_specs.append(state_spec)
        args.append(s0)
    if rope is not None:
        for cmap in (lambda b, c: (c, 0), lambda b, c: (n - 1 - c, 0)):
            in_specs += [pl.BlockSpec((rb, dk), cmap), pl.BlockSpec((rb, dk), cmap)]
            args += [rope[0], rope[1]]
    n_tok = batch * seq
    return pl.pallas_call(
        functools.partial(_ret_kernel, heads=heads, dk=dk, dv=dv, chunk=RET_CHUNK, n_chunks=n,
                          has_init=s0 is not None, has_rope=rope is not None),
        grid=(batch, n),
        in_specs=in_specs,
        out_specs=[
            pl.BlockSpec((rb, vw), lambda b, c: (b * n + c, 0)),
            pl.BlockSpec((rb, vw), lambda b, c: (b * n + (n - 1 - c), 0)),
            state_spec,
        ],
        out_shape=[
            jax.ShapeDtypeStruct((n_tok, vw), F32),
            jax.ShapeDtypeStruct((n_tok, vw), F32),
            jax.ShapeDtypeStruct((batch, 2, heads, dk, dv), F32),
        ],
        scratch_shapes=[pltpu.VMEM((2, heads, dk, dv), F32)],
        compiler_params=_cparams("arbitrary", "arbitrary"),
        name="retention_scan",
    )(*args)


def _hg_scan_index(shape, axis, rev):
    i = lax.broadcasted_iota(I32, shape, axis) & (HG_CHUNK - 1)
    return (HG_CHUNK - 1 - i) if rev else i


def _hg_window_matrix(rev):
    C = HG_CHUNK
    ti = _hg_scan_index((C, 2 * C), 0, rev)
    tj = _hg_scan_index((C, 2 * C), 1, rev)
    blocks = [tj <= ti]
    for l in range(2, HG_LEVELS + 1):
        anchor = ((ti >> l) << l) + (1 << (l - 1)) - 1
        upper = ((ti >> (l - 1)) & 1) == 1
        blocks.append((upper & (tj > anchor) & (tj <= ti)) | (~upper & (tj > ti) & (tj <= anchor)))
    blocks.append(tj > ti)
    return jnp.concatenate([jnp.where(m, 1.0, 0.0) for m in blocks], axis=0).astype(BF16)


def _hg_kernel(*refs, heads, dk, dv, tt, n_blocks, has_init):
    it = iter(refs)
    qzv = [[next(it) for _ in range(3)] for _ in range(2)]
    lb_ref = next(it)
    s0_ref = next(it) if has_init else None
    o_refs = [next(it), next(it)]
    sfin_ref = next(it)
    st_ref = next(it)
    win_ref = next(it)
    gate_a = [next(it) for _ in range(6)]
    gate_b = [next(it) for _ in range(6)]

    t = pl.program_id(1)

    @pl.when((pl.program_id(0) == 0) & (t == 0))
    def _():
        for d in range(2):
            win_ref[d] = _hg_window_matrix(d == 1)

    @pl.when(t == 0)
    def _():
        for d in range(2):
            for hh in range(heads):
                st_ref[d, hh] = s0_ref[d, hh].T if has_init else jnp.zeros((dv, dk), F32)

    C = HG_CHUNK
    n_ch = tt // C
    chunks = [slice(c * C, (c + 1) * C) for c in range(n_ch)]
    levels, uppers = [], []
    for d in range(2):
        ti = _hg_scan_index((C, C), 0, d == 1)
        tj = _hg_scan_index((C, C), 1, d == 1)
        level = jnp.where(tj < ti, 1, 0)
        for l in range(1, HG_LEVELS):
            level = level + jnp.where((tj < ti) & ((ti >> l) != (tj >> l)), 1, 0)
        t_col = _hg_scan_index((C, 1), 0, d == 1)
        levels.append(level)
        uppers.append([None] + [((t_col >> (l - 1)) & 1) == 1 for l in range(1, HG_LEVELS + 1)])

    def gates(hh, g_refs):
        qg_ref, k_ref, f_ref, diag_ref, hi_ref, lo_ref = g_refs
        ck = pl.ds(pl.multiple_of(hh * dk, LANE), dk)
        for d in range(2):
            q_ref, z_ref, _ = qzv[d]
            lb = lb_ref[d, :, ck]
            z = z_ref[:, ck]
            e = jnp.exp(-jnp.abs(z))
            inv = 1.0 / (1.0 + e)
            pos = z > 0.0
            f = lb + (1.0 - lb) * (jnp.where(pos, 1.0, e) * inv)
            k = (1.0 - lb) * (jnp.where(pos, e, 1.0) * inv)
            tiny = (lb <= 0.0) & (z < HG_LOG_SIGMOID_LINEAR)
            log2_f = jnp.where(tiny, z * LOG2_E, jnp.log2(f))
            hi = log2_f.astype(BF16)
            qg = _silu(q_ref[:, ck]) * (dk ** -0.5)
            qg_ref[d], k_ref[d], f_ref[d] = qg, k, f
            hi_ref[d] = hi
            lo_ref[d] = (log2_f - hi.astype(F32)).astype(BF16)
            diag_ref[d] = jnp.broadcast_to(jnp.sum(qg * k, axis=-1, keepdims=True), qg.shape)

    def dots(hh, g_refs):
        qg_ref, k_ref, f_ref, diag_ref, hi_ref, lo_ref = g_refs
        cv = pl.ds(pl.multiple_of(hh * dv, LANE), dv)
        jobs = [(d, r) for d in range(2) for r in chunks]
        qg, k, f, diag = ([ref[d] for d in range(2)] for ref in (qg_ref, k_ref, f_ref, diag_ref))
        v = [qzv[d][2][:, cv] for d in range(2)]
        vb = [a.astype(BF16) for a in v]
        wins = [win_ref[0], win_ref[1]]
        sums = [_dot(wins[d], jnp.concatenate([hi_ref[d, r, :], lo_ref[d, r, :]], axis=0)) for d, r in jobs]

        def query_or_key(d, r, l):
            half = 1 << (l - 1)
            if half % F32_SUBLANE:
                return jnp.where(uppers[d][l], qg[d][r], k[d][r])
            segs = []
            for s in range(C // half):
                src = qg[d] if (s % 2 == 1) != (d == 1) else k[d]
                segs.append(src[r.start + s * half:r.start + (s + 1) * half])
            return jnp.concatenate(segs, axis=0)

        pairs = []
        for (d, r), s in zip(jobs, sums):
            ps = [jnp.where(uppers[d][1], qg[d][r] * f[d][r], k[d][r]).astype(BF16)]
            for l in range(2, HG_LEVELS + 1):
                scale = jnp.exp2(s[(l - 1) * C:l * C])
                ps.append((query_or_key(d, r, l) * scale).astype(BF16))
            pairs.append(ps)
        grams = [[_dot_nt(p, p) for p in ps] for ps in pairs]
        atts = []
        for (d, r), gs in zip(jobs, grams):
            att = jnp.where(levels[d] == 1, gs[0], 0.0)
            for l in range(2, HG_LEVELS + 1):
                att = jnp.where(levels[d] == l, gs[l - 1], att)
            atts.append(att.astype(BF16))
        k_out = [(k[d][r] * jnp.exp2(s[HG_LEVELS * C:])).astype(BF16) for (d, r), s in zip(jobs, sums)]
        q_in = [(qg[d][r] * jnp.exp2(s[:C])).astype(BF16) for (d, r), s in zip(jobs, sums)]
        decays = [jnp.exp2(s[0:1] if d == 1 else s[C - 1:C]) for (d, r), s in zip(jobs, sums)]
        outs = [_dot(att, vb[d][r]) + diag[d][r] * v[d][r] for att, (d, r) in zip(atts, jobs)]
        kvs = [_dot_tn(vb[d][r], ko) for (d, r), ko in zip(jobs, k_out)]
        for d in range(2):
            st = st_ref[d, hh]
            for c in (range(n_ch - 1, -1, -1) if d == 1 else range(n_ch)):
                j = d * n_ch + c
                o_refs[d][chunks[c], cv] = outs[j] + _dot_nt(q_in[j], st.astype(BF16))
                st = st * decays[j] + kvs[j]
            st_ref[d, hh] = st

    gates(0, gate_a)

    def head_pair(i, carry):
        gates(2 * i + 1, gate_b)
        dots(2 * i, gate_a)
        dots(2 * i + 1, gate_b)
        gates(jnp.minimum(2 * i + 2, heads - 1), gate_a)
        return carry

    lax.fori_loop(0, heads // 2, head_pair, 0)

    @pl.when(t == n_blocks - 1)
    def _():
        for d in range(2):
            for hh in range(heads):
                sfin_ref[d, hh] = st_ref[d, hh].T


def _hg_call(proj, lb, s0, *, tok0, batch, seq, heads, dk, dv, col_q, col_f, col_i):
    tt = min(seq, 256)
    n = seq // tt
    kw, vw = heads * dk, heads * dv
    assert seq % tt == 0 and tok0 % tt == 0 and tt % HG_CHUNK == 0
    assert col_q % kw == 0 and col_f % kw == 0 and col_i % vw == 0
    r0 = tok0 // tt

    def fwd(b, t):
        return r0 + b * n + t

    def bwd(b, t):
        return r0 + b * n + (n - 1 - t)

    in_specs, args = [], []
    for d, rmap in enumerate((fwd, bwd)):
        in_specs += [
            pl.BlockSpec((tt, kw), lambda b, t, rmap=rmap: (rmap(b, t), col_q // kw)),
            pl.BlockSpec((tt, kw), lambda b, t, rmap=rmap, d=d: (rmap(b, t), col_f // kw + d)),
            pl.BlockSpec((tt, vw), lambda b, t, rmap=rmap: (rmap(b, t), col_i // vw)),
        ]
        args += [proj, proj, proj]
    in_specs.append(pl.BlockSpec((2, 1, kw), lambda b, t: (0, 0, 0)))
    args.append(lb.reshape(2, 1, kw))
    state_spec = pl.BlockSpec((None, 2, heads, dk, dv), lambda b, t: (b, 0, 0, 0, 0))
    if s0 is not None:
        in_specs.append(state_spec)
        args.append(s0)
    n_tok = batch * seq
    return pl.pallas_call(
        functools.partial(_hg_kernel, heads=heads, dk=dk, dv=dv, tt=tt, n_blocks=n, has_init=s0 is not None),
        grid=(batch, n),
        in_specs=in_specs,
        out_specs=[
            pl.BlockSpec((tt, vw), lambda b, t: (b * n + t, 0)),
            pl.BlockSpec((tt, vw), lambda b, t: (b * n + (n - 1 - t), 0)),
            state_spec,
        ],
        out_shape=[
            jax.ShapeDtypeStruct((n_tok, vw), F32),
            jax.ShapeDtypeStruct((n_tok, vw), F32),
            jax.ShapeDtypeStruct((batch, 2, heads, dk, dv), F32),
        ],
        scratch_shapes=[pltpu.VMEM((2, heads, dv, dk), F32),
                        pltpu.VMEM((2, (HG_LEVELS + 1) * HG_CHUNK, 2 * HG_CHUNK), BF16)]
        + 2 * ([pltpu.VMEM((2, tt, dk), F32)] * 4 + [pltpu.VMEM((2, tt, dk), BF16)] * 2),
        compiler_params=_cparams("arbitrary", "arbitrary"),
        name="hgrn2_scan",
    )(*args)


def _mix_out_kernel(*refs, d, ret_heads, hg_heads, ctx_blocks):
    scans = [refs[4 * p:4 * p + 4] for p in range(2)]
    (rg_ref, go_ref, h_ref, x_ref, mod_ref, g2_ref, wr_ref, wh_ref, wm_ref, wo_ref, wrt_ref,
     x1_ref, h2_ref, lgt_ref, o_ref, og_ref) = refs[8:]

    for p, in_set in enumerate((pl.program_id(0) < ctx_blocks, pl.program_id(0) >= ctx_blocks)):
        @pl.when(in_set)
        def _(p=p):
            o_ref[...] = scans[p][0][...] + scans[p][1][...]
            og_ref[...] = scans[p][2][...] + scans[p][3][...]

    dvr = o_ref.shape[1] // ret_heads
    dvh = og_ref.shape[1] // hg_heads
    tm = o_ref.shape[0]
    halves = [slice(0, tm // 2), slice(tm // 2, tm)]
    ret_in, hg_in = [], []
    for r in halves:
        parts = []
        for hh in range(ret_heads):
            oh = o_ref[r, hh * dvr:(hh + 1) * dvr]
            ctr = oh - jnp.mean(oh, axis=-1, keepdims=True)
            parts.append(ctr * lax.rsqrt(jnp.mean(ctr * ctr, axis=-1, keepdims=True) + NORM_EPS))
        ret_in.append((jnp.concatenate(parts, axis=1) * _silu(rg_ref[r, :])).astype(BF16))
        parts = [_rms(og_ref[r, hh * dvh:(hh + 1) * dvh]) for hh in range(hg_heads)]
        hg_in.append((jnp.concatenate(parts, axis=1) * _silu(go_ref[r, :])).astype(BF16))
    ret_out = [_dot(a, wr_ref[...]) for a in ret_in]
    hg_out = [_dot(a, wh_ref[...]) for a in hg_in]
    gates = [jax.nn.sigmoid(_dot(h_ref[r, :], wm_ref[...])) for r in halves]
    merged = [(g[:, :d] * a + g[:, d:] * b).astype(BF16) for g, a, b in zip(gates, ret_out, hg_out)]
    mix = [_dot(m, wo_ref[...]) for m in merged]
    w_hi, w_lo = wrt_ref[0], wrt_ref[1]
    for r, m in zip(halves, mix):
        x1 = x_ref[r, :] + mod_ref[:, 2 * d:3 * d] * m
        x1_ref[r, :] = x1
        h2 = _rms(x1) * g2_ref[...]
        h2 = h2 * (1.0 + mod_ref[:, 4 * d:5 * d]) + mod_ref[:, 3 * d:4 * d]
        h2_ref[r, :] = h2.astype(BF16)
        h_hi = h2.astype(BF16)
        h_lo = (h2 - h_hi.astype(F32)).astype(BF16)
        lgt_ref[:, r] = _dot_nt(w_hi, h_hi) + (_dot_nt(w_hi, h_lo) + _dot_nt(w_lo, h_hi))


def _mix_out_call(scans_ctx, scans_lat, proj, h, x, mod_l, mod_row, g2, w_ret_o, w_hg_o, w_merge, w_out,
                  w_router_t, tm, *, ret_heads, hg_heads, col_rg, col_go):
    n, d = x.shape
    vr, vh = scans_ctx[0].shape[1], scans_ctx[2].shape[1]
    ne = w_router_t.shape[1]
    ctx_blocks = scans_ctx[0].shape[0] // tm
    assert col_rg % vr == 0 and col_go % vh == 0

    def row(w):
        return pl.BlockSpec((tm, w), lambda i: (i, 0))

    def full(a):
        return pl.BlockSpec(a.shape, lambda i: (0,) * a.ndim)

    ctx_row = lambda a: pl.BlockSpec((tm, a.shape[1]), lambda i: (jnp.minimum(i, ctx_blocks - 1), 0))
    lat_row = lambda a: pl.BlockSpec((tm, a.shape[1]), lambda i: (jnp.maximum(i - ctx_blocks, 0), 0))
    return pl.pallas_call(
        functools.partial(_mix_out_kernel, d=d, ret_heads=ret_heads, hg_heads=hg_heads, ctx_blocks=ctx_blocks),
        grid=(n // tm,),
        in_specs=[ctx_row(a) for a in scans_ctx] + [lat_row(a) for a in scans_lat] + [
            pl.BlockSpec((tm, vr), lambda i: (i, col_rg // vr)),
            pl.BlockSpec((tm, vh), lambda i: (i, col_go // vh)),
            row(d), row(d),
            pl.BlockSpec((None, 1, 6 * d), lambda i: (mod_row(i), 0, 0)),
            pl.BlockSpec((1, d), lambda i: (0, 0)),
            full(w_ret_o), full(w_hg_o), full(w_merge), full(w_out), full(w_router_t),
        ],
        out_specs=[row(d), row(d), pl.BlockSpec((ne, tm), lambda i: (0, i))],
        out_shape=[
            jax.ShapeDtypeStruct((n, d), F32),
            jax.ShapeDtypeStruct((n, d), BF16),
            jax.ShapeDtypeStruct((ne, n), F32),
        ],
        scratch_shapes=[pltpu.VMEM((tm, vr), F32), pltpu.VMEM((tm, vh), F32)],
        compiler_params=_cparams("arbitrary"),
        name="mixer_out",
    )(*scans_ctx, *scans_lat, proj, proj, h, x, mod_l, g2.reshape(1, d), w_ret_o, w_hg_o, w_merge, w_out,
      w_router_t)


def _route_kernel(lg_ref, aff_ref, sel_ref, posx_ref, flag_ref, *, cap, n):
    lg = lg_ref[...]
    ne = lg.shape[0]
    ex = jnp.exp(lg - jnp.max(lg, axis=0, keepdims=True))
    aff = ex / jnp.sum(ex, axis=0, keepdims=True)
    aff_ref[...] = aff
    bits = pltpu.bitcast(aff, I32)

    def search(i, lo):
        cand = lo | lax.shift_left(jnp.int32(1), 30 - i)
        cnt = jnp.sum(jnp.where(bits >= cand, 1.0, 0.0), axis=1, keepdims=True)
        return jnp.where(cnt >= cap, cand, lo)

    thr = lax.fori_loop(0, 31, search, jnp.zeros((ne, 1), I32))
    gt = bits > thr
    eq = bits == thr
    need = cap - jnp.sum(jnp.where(gt, 1.0, 0.0), axis=1, keepdims=True)
    upper = jnp.where(lax.broadcasted_iota(I32, (LANE, LANE), 0) < lax.broadcasted_iota(I32, (LANE, LANE), 1),
                      1.0, 0.0).astype(BF16)

    def exclusive_count(store):
        def body(ci, carry):
            sl = pl.ds(pl.multiple_of(ci * LANE, LANE), LANE)
            x = flag_ref[:, sl]
            store(sl, carry + _dot(x.astype(BF16), upper))
            return carry + jnp.sum(x, axis=1, keepdims=True)

        lax.fori_loop(0, n // LANE, body, jnp.zeros((ne, 1), F32))

    flag_ref[...] = jnp.where(eq, 1.0, 0.0)

    def store_tie_rank(sl, rank):
        posx_ref[:, sl] = rank.astype(I32)

    exclusive_count(store_tie_rank)
    sel = gt | (eq & (posx_ref[...].astype(F32) < need))
    sel_ref[...] = jnp.where(sel, 1, 0).astype(I32)
    flag_ref[...] = jnp.where(sel, 1.0, 0.0)

    def store_pos(sl, cnt):
        posx_ref[:, sl] = cnt.astype(I32)

    exclusive_count(store_pos)


def _route_call(logits_t, cap):
    ne, n = logits_t.shape
    spec = pl.BlockSpec((ne, n), lambda: (0, 0))
    return pl.pallas_call(
        functools.partial(_route_kernel, cap=cap, n=n),
        in_specs=[spec],
        out_specs=[spec, spec, spec],
        out_shape=[
            jax.ShapeDtypeStruct((ne, n), F32),
            jax.ShapeDtypeStruct((ne, n), I32),
            jax.ShapeDtypeStruct((ne, n), I32),
        ],
        scratch_shapes=[pltpu.VMEM((ne, n), F32)],
        compiler_params=pltpu.CompilerParams(vmem_limit_bytes=VMEM_LIMIT),
        name="expert_choice_route",
    )(logits_t)


def _moe_round_masks(b, r, base_ref, sel_ref, posx_ref, ne, value_rows=None):
    n_tok = sel_ref.shape[1]
    row_iota = lax.broadcasted_iota(I32, (MOE_ROUND, n_tok), 0)
    pieces = []
    for e in range(ne):
        local = posx_ref[e:e + 1, :] - (base_ref[b * ne + e] + r * MOE_ROUND)
        hit = (sel_ref[e:e + 1, :] == 1) & (local == row_iota)
        value = 1.0 if value_rows is None else value_rows[e:e + 1, :]
        pieces.append(jnp.where(hit, value, 0.0).astype(BF16))
    return jnp.concatenate(pieces, axis=0)


def _dispatch_kernel(base_ref, start_ref, pc_ref, nr_ref, h_ref, sel_ref, posx_ref, xe_zero_ref, xe_ref,
                     stage_ref, sem, *, ne):
    del xe_zero_ref
    b = pl.program_id(0)
    n_rounds = nr_ref[b]

    def pieces(r, slot, fn):
        for e in range(ne):
            rem = pc_ref[b * ne + e] - r * MOE_ROUND
            dst = start_ref[b * ne + e] + r * MOE_ROUND
            for size, cond, off in (
                    (MOE_ROUND, rem >= MOE_ROUND, 0),
                    (32, (rem > 0) & (rem < MOE_ROUND) & ((rem & 32) != 0), 0),
                    (16, (rem > 0) & (rem < MOE_ROUND) & ((rem & 16) != 0), rem & 32)):
                @pl.when(cond)
                def _(size=size, off=off, e=e, dst=dst):
                    src_rows = pl.ds(pl.multiple_of(e * MOE_ROUND + off, BF16_SUBLANE), size)
                    dst_rows = pl.ds(pl.multiple_of(dst + off, BF16_SUBLANE), size)
                    fn(pltpu.make_async_copy(stage_ref.at[slot, src_rows], xe_ref.at[e, dst_rows], sem.at[slot]))

    def round_body(r, carry):
        slot = r % 2
        onehot = _moe_round_masks(b, r, base_ref, sel_ref, posx_ref, ne)
        stage_ref[slot] = _dot(onehot, h_ref[...]).astype(BF16)

        @pl.when(r >= 1)
        def _():
            pieces(r - 1, 1 - slot, lambda cp: cp.wait())

        pieces(r, slot, lambda cp: cp.start())
        return carry

    lax.fori_loop(0, n_rounds, round_body, 0)

    @pl.when(n_rounds >= 1)
    def _():
        pieces(n_rounds - 1, (n_rounds - 1) % 2, lambda cp: cp.wait())


def _dispatch_call(h2, sel, posx, tables, list_rows, *, sb):
    n, d = h2.shape
    ne = sel.shape[0]
    xe_zero = jnp.zeros((ne, list_rows, d), BF16)
    grid_spec = pltpu.PrefetchScalarGridSpec(
        num_scalar_prefetch=4,
        grid=(n // sb,),
        in_specs=[
            pl.BlockSpec((sb, d), lambda b, *_: (b, 0)),
            pl.BlockSpec((ne, sb), lambda b, *_: (0, b)),
            pl.BlockSpec((ne, sb), lambda b, *_: (0, b)),
            pl.BlockSpec(memory_space=pl.ANY),
        ],
        out_specs=pl.BlockSpec(memory_space=pl.ANY),
        scratch_shapes=[pltpu.VMEM((2, ne * MOE_ROUND, d), BF16), pltpu.SemaphoreType.DMA((2,))],
    )
    return pl.pallas_call(
        functools.partial(_dispatch_kernel, ne=ne),
        grid_spec=grid_spec,
        out_shape=jax.ShapeDtypeStruct((ne, list_rows, d), BF16),
        input_output_aliases={7: 0},
        compiler_params=_cparams("arbitrary"),
        name="expert_dispatch",
    )(*tables, h2, sel, posx, xe_zero)


def _expert_ffn_kernel(tot_ref, x_ref, wg_ref, wu_ref, wd_ref, y_ref, wgb_ref, wub_ref, wdb_ref, acc_ref,
                       *, rt, n_ff):
    e, f, t = pl.program_id(0), pl.program_id(1), pl.program_id(2)

    @pl.when(t == 0)
    def _():
        wgb_ref[...] = wg_ref[...].astype(BF16)
        wub_ref[...] = wu_ref[...].astype(BF16)
        wdb_ref[...] = wd_ref[...].astype(BF16)

    rows = pl.ds(pl.multiple_of(t * rt, rt), rt)
    live = t * rt < tot_ref[e]

    @pl.when(live)
    def _():
        x = x_ref[...]
        hid = (_silu(_dot(x, wgb_ref[...])) * _dot(x, wub_ref[...])).astype(BF16)
        y = _dot(hid, wdb_ref[...])

        @pl.when(f == 0)
        def _():
            acc_ref[rows, :] = y

        @pl.when(f != 0)
        def _():
            acc_ref[rows, :] += y

    @pl.when((f == n_ff - 1) & live)
    def _():
        y_ref[...] = acc_ref[rows, :].astype(BF16)

    @pl.when((f == n_ff - 1) & jnp.logical_not(live))
    def _():
        y_ref[...] = jnp.zeros_like(y_ref)


def _expert_ffn_call(xe, totals, wg, wu, wd, layer, *, rt, fc):
    ne, list_rows, d = xe.shape
    ff = wg.shape[-1]
    n_ff = ff // fc
    grid_spec = pltpu.PrefetchScalarGridSpec(
        num_scalar_prefetch=1,
        grid=(ne, n_ff, list_rows // rt),
        in_specs=[
            pl.BlockSpec((None, rt, d), lambda e, f, t, *_: (e, t, 0)),
            pl.BlockSpec((None, None, d, fc), lambda e, f, t, *_: (layer, e, 0, f)),
            pl.BlockSpec((None, None, d, fc), lambda e, f, t, *_: (layer, e, 0, f)),
            pl.BlockSpec((None, None, fc, d), lambda e, f, t, *_: (layer, e, f, 0)),
        ],
        out_specs=pl.BlockSpec((None, rt, d), lambda e, f, t, *_: (e, jnp.where(f == n_ff - 1, t, 0), 0)),
        scratch_shapes=[pltpu.VMEM((d, fc), BF16), pltpu.VMEM((d, fc), BF16), pltpu.VMEM((fc, d), BF16),
                        pltpu.VMEM((list_rows, d), F32)],
    )
    return pl.pallas_call(
        functools.partial(_expert_ffn_kernel, rt=rt, n_ff=n_ff),
        grid_spec=grid_spec,
        out_shape=jax.ShapeDtypeStruct((ne, list_rows, d), BF16),
        compiler_params=_cparams("arbitrary", "arbitrary", "arbitrary"),
        name="expert_ffn",
    )(totals, xe, wg, wu, wd)


def _combine_kernel(base_ref, start_ref, pc_ref, nr_ref, sel_ref, posx_ref, aff_ref, y_ref, out_ref, ybuf_ref, sem,
                    *, ne):
    b = pl.program_id(0)
    n_rounds = nr_ref[b]

    @pl.when(b == 0)
    def _():
        ybuf_ref[...] = jnp.zeros_like(ybuf_ref)

    def windows(r, slot, fn):
        for e in range(ne):
            @pl.when(pc_ref[b * ne + e] - r * MOE_ROUND > 0)
            def _(e=e):
                src = pl.ds(pl.multiple_of(start_ref[b * ne + e] + r * MOE_ROUND, BF16_SUBLANE), MOE_ROUND)
                fn(pltpu.make_async_copy(y_ref.at[e, src], ybuf_ref.at[slot, pl.ds(e * MOE_ROUND, MOE_ROUND)],
                                         sem.at[slot]))

    out_ref[...] = jnp.zeros_like(out_ref)

    @pl.when(n_rounds >= 1)
    def _():
        windows(0, 0, lambda cp: cp.start())

    def round_body(r, carry):
        slot = r % 2

        @pl.when(r + 1 < n_rounds)
        def _():
            windows(r + 1, 1 - slot, lambda cp: cp.start())

        windows(r, slot, lambda cp: cp.wait())
        weights = _moe_round_masks(b, r, base_ref, sel_ref, posx_ref, ne, value_rows=aff_ref)
        out_ref[...] += _dot_tn(weights, ybuf_ref[slot])
        return carry

    lax.fori_loop(0, n_rounds, round_body, 0)


def _combine_call(ye, sel, posx, aff, tables, *, sb):
    ne, _, d = ye.shape
    n = sel.shape[1]
    grid_spec = pltpu.PrefetchScalarGridSpec(
        num_scalar_prefetch=4,
        grid=(n // sb,),
        in_specs=[
            pl.BlockSpec((ne, sb), lambda b, *_: (0, b)),
            pl.BlockSpec((ne, sb), lambda b, *_: (0, b)),
            pl.BlockSpec((ne, sb), lambda b, *_: (0, b)),
            pl.BlockSpec(memory_space=pl.ANY),
        ],
        out_specs=pl.BlockSpec((sb, d), lambda b, *_: (b, 0)),
        scratch_shapes=[pltpu.VMEM((2, ne * MOE_ROUND, d), BF16), pltpu.SemaphoreType.DMA((2,))],
    )
    return pl.pallas_call(
        functools.partial(_combine_kernel, ne=ne),
        grid_spec=grid_spec,
        out_shape=jax.ShapeDtypeStruct((n, d), F32),
        compiler_params=_cparams("arbitrary"),
        name="expert_combine",
    )(*tables, sel, posx, aff, ye)


def _moe_tables(posx_sets, caps, sb):
    base, cnt = [], []
    for posx, cap in zip(posx_sets, caps):
        ne = posx.shape[0]
        edges = jnp.concatenate([posx[:, ::sb], jnp.full((ne, 1), cap, I32)], axis=1)
        base.append(edges[:, :-1].T)
        cnt.append((edges[:, 1:] - edges[:, :-1]).T)
    base = jnp.concatenate(base, axis=0)
    cnt = jnp.concatenate(cnt, axis=0)
    pc = (cnt + (BF16_SUBLANE - 1)) // BF16_SUBLANE * BF16_SUBLANE
    ends = jnp.cumsum(pc, axis=0)
    start = ends - pc
    n_rounds = jnp.max((pc + (MOE_ROUND - 1)) // MOE_ROUND, axis=1)
    tables = tuple(a.reshape(-1).astype(I32) for a in (base, start, pc, n_rounds))
    return tables, ends[-1].astype(I32)


def _rope_tables(n_tokens, dk):
    rows = n_tokens // GRID_W
    r, c = jnp.meshgrid(jnp.arange(rows), jnp.arange(GRID_W), indexing="ij")
    pos = jnp.stack([r.reshape(-1), c.reshape(-1)], axis=-1).astype(F32)
    nf = dk // 4
    inv_freq = ROPE_BASE ** (-jnp.arange(nf, dtype=F32) / nf)
    ang = pos[:, :, None] * inv_freq
    cos, sin = jnp.cos(ang), jnp.sin(ang)
    cos_t = jnp.concatenate([cos[:, 0], cos[:, 0], cos[:, 1], cos[:, 1]], axis=-1)
    sin_t = jnp.concatenate([-sin[:, 0], sin[:, 0], -sin[:, 1], sin[:, 1]], axis=-1)
    return cos_t, sin_t


def kernel(x_prompt, x_sample, state_ret, state_hgrn, c, c_ctx, ada_w, ada_b, norm_mix_g, norm_ffn_g, w_in,
           ret_gamma_logit, hg_lb_logit, w_ret_o, w_hg_o, w_merge, w_out, w_router, w_exp_gate, w_exp_up,
           w_exp_down, final_g):
    b_ctx, t_ctx, d = x_prompt.shape
    b_lat, t_lat, _ = x_sample.shape
    depth = w_in.shape[0]
    ret_heads, ret_dk, ret_dv = state_ret.shape[3:]
    hg_heads, hg_dk, hg_dv = state_hgrn.shape[3:]
    ne = w_router.shape[-1]
    n_ctx, n_lat = b_ctx * t_ctx, b_lat * t_lat
    n_tok = n_ctx + n_lat
    qw, vw = ret_heads * ret_dk, ret_heads * ret_dv
    kw, hw = hg_heads * hg_dk, hg_heads * hg_dv
    col_rg = 2 * qw + vw
    col_gq = col_rg + vw
    col_gf = col_gq + kw
    col_gi = col_gf + 2 * kw
    col_go = col_gi + hw
    assert col_go + hw == w_in.shape[-1] and b_lat < MOD_ROWS

    tm = 256
    assert t_ctx % tm == 0 and t_lat % tm == 0
    tm_proj = 1024 if n_tok % 1024 == 0 else tm
    sb = 1024 if (n_ctx % 4096 == 0 and n_lat % 4096 == 0) else 256
    assert n_ctx % sb == 0 and n_lat % sb == 0
    sets = ((0, n_ctx), (n_ctx, n_lat))
    caps = [CAPACITY_FACTOR * n_set // ne for _, n_set in sets]
    list_rows = sum(caps) + BF16_SUBLANE * (n_tok // sb)
    n_tiles = max(1, list_rows // 512)
    rt = -(-list_rows // (n_tiles * BF16_SUBLANE)) * BF16_SUBLANE
    list_rows = n_tiles * rt

    def mod_rows(tmx):
        return lambda i: jnp.where(i < n_ctx // tmx, 0, 1 + (i - n_ctx // tmx) // (t_lat // tmx))

    mod_row = mod_rows(tm)
    tm_norm = 512 if (n_ctx % 512 == 0 and t_lat % 512 == 0) else tm
    norm_row = mod_rows(tm_norm)

    x = jnp.concatenate([x_prompt.reshape(n_ctx, d), x_sample.reshape(n_lat, d)], axis=0)
    cvec = jnp.zeros((MOD_ROWS, d), F32).at[0].set(c_ctx).at[1:1 + b_lat].set(c)
    mod = _mod_call(cvec, ada_w, ada_b).reshape(depth, MOD_ROWS, 1, 6 * d)

    log_gamma = jax.nn.log_sigmoid(ret_gamma_logit.astype(F32))
    p_lb = jax.nn.softmax(hg_lb_logit.astype(F32), axis=0)
    hg_lb = jnp.clip(jnp.cumsum(p_lb, axis=0) - p_lb[0:1], 0.0, 1.0 - 1e-6)
    rope = _rope_tables(t_lat, ret_dk)

    w_ret_o_b, w_hg_o_b, w_out_b = w_ret_o.astype(BF16), w_hg_o.astype(BF16), w_out.astype(BF16)
    w_merge_b = w_merge.astype(BF16)
    w_router_t = jnp.swapaxes(w_router, 1, 2)
    w_router_hi = w_router_t.astype(BF16)
    w_router_t = jnp.stack([w_router_hi, (w_router_t - w_router_hi.astype(F32)).astype(BF16)], axis=1)
    fc = min(1024, w_exp_gate.shape[-1])
    tn = min(2048, w_in.shape[-1])

    _, h = _norm_call(x, None, None, mod[0], norm_mix_g[0], norm_row, tm_norm)
    new_ret, new_hg = [], []
    for l in range(depth):
        proj = _proj_call(h, w_in, l, tm_proj, tn)

        ret_kw = dict(heads=ret_heads, dk=ret_dk, dv=ret_dv)
        orf_c, orb_c, s_ret = _ret_call(proj, log_gamma[l], None, None, tok0=0, batch=b_ctx, seq=t_ctx, **ret_kw)
        orf_l, orb_l, _ = _ret_call(proj, log_gamma[l], state_ret[:, l], rope, tok0=n_ctx, batch=b_lat, seq=t_lat,
                                    **ret_kw)
        hg_kw = dict(heads=hg_heads, dk=hg_dk, dv=hg_dv, col_q=col_gq, col_f=col_gf, col_i=col_gi)
        ogf_c, ogb_c, s_hg = _hg_call(proj, hg_lb[l], None, tok0=0, batch=b_ctx, seq=t_ctx, **hg_kw)
        ogf_l, ogb_l, _ = _hg_call(proj, hg_lb[l], state_hgrn[:, l], tok0=n_ctx, batch=b_lat, seq=t_lat, **hg_kw)
        new_ret.append(s_ret)
        new_hg.append(s_hg)

        x, h2, logits_t = _mix_out_call(
            (orf_c, orb_c, ogf_c, ogb_c), (orf_l, orb_l, ogf_l, ogb_l), proj, h, x, mod[l],
            mod_row, norm_ffn_g[l], w_ret_o_b[l], w_hg_o_b[l], w_merge_b[l], w_out_b[l], w_router_t[l], tm,
            ret_heads=ret_heads, hg_heads=hg_heads, col_rg=col_rg, col_go=col_go)

        routed = [_route_call(logits_t[:, lo:lo + n_set], cap) for (lo, n_set), cap in zip(sets, caps)]
        aff, sel, posx = (jnp.concatenate([r[i] for r in routed], axis=1) for i in range(3))
        tables, totals = _moe_tables([r[2] for r in routed], caps, sb)
        xe = _dispatch_call(h2, sel, posx, tables, list_rows, sb=sb)
        ye = _expert_ffn_call(xe, totals, w_exp_gate, w_exp_up, w_exp_down, l, rt=rt, fc=fc)
        ffn = _combine_call(ye, sel, posx, aff, tables, sb=sb)

        if l + 1 < depth:
            x, h = _norm_call(x, ffn, mod[l], mod[l + 1], norm_mix_g[l + 1], norm_row, tm_norm)
        else:
            _, y_prompt = _norm_call(x, ffn, mod[l], None, final_g, norm_row, tm_norm, rows=(0, n_ctx))
            _, y_sample = _norm_call(x, ffn, mod[l], None, final_g, norm_row, tm_norm, rows=(n_ctx, n_lat))

    y_prompt = y_prompt.reshape(b_ctx, t_ctx, d)
    y_sample = y_sample.reshape(b_lat, t_lat, d)
    return y_prompt, y_sample, jnp.stack(new_ret, axis=1), jnp.stack(new_hg, axis=1)
```

```python
import functools

import jax
import jax.numpy as jnp
from jax import lax
from jax.experimental import pallas as pl
from jax.experimental.pallas import tpu as pltpu

F32 = jnp.float32
BF16 = jnp.bfloat16
I32 = jnp.int32
HIGHEST = lax.Precision.HIGHEST

NORM_EPS = 1e-6
LOG2_E = 1.4426950408889634
ROPE_BASE = 10000.0
GRID_W = 64
CAPACITY_FACTOR = 2
RET_CHUNK = 128
RET_CHUNKS_PER_STEP = 2
HG_LEVELS = 6
HG_CHUNK = 1 << HG_LEVELS
HG_LOG_SIGMOID_LINEAR = -60.0
MOD_ROWS = 8
LANE = 128
F32_SUBLANE = 8
BF16_SUBLANE = 16
MOE_ROUND = 64
VMEM_LIMIT = 62 * 1024 * 1024


def _cparams(*sem):
    return pltpu.CompilerParams(dimension_semantics=sem, vmem_limit_bytes=VMEM_LIMIT)


def _dot(a, b, **kw):
    return jnp.dot(a, b, preferred_element_type=F32, **kw)


def _dot_nt(a, b, **kw):
    return lax.dot_general(a, b, (((1,), (1,)), ((), ())), preferred_element_type=F32, **kw)


def _dot_tn(a, b, **kw):
    return lax.dot_general(a, b, (((0,), (0,)), ((), ())), preferred_element_type=F32, **kw)


def _silu(x):
    return x * jax.nn.sigmoid(x)


def _mod_kernel(c_ref, w_ref, b_ref, o_ref):
    s = _silu(c_ref[...])
    o_ref[...] = _dot(s, w_ref[...], precision=HIGHEST) + b_ref[...]


def _mod_call(cvec, ada_w, ada_b):
    depth, d, six_d = ada_w.shape
    tn = 6 * LANE * 2
    assert six_d % tn == 0
    return pl.pallas_call(
        _mod_kernel,
        grid=(depth, six_d // tn),
        in_specs=[
            pl.BlockSpec((MOD_ROWS, d), lambda l, j: (0, 0)),
            pl.BlockSpec((None, d, tn), lambda l, j: (l, 0, j)),
            pl.BlockSpec((None, 1, tn), lambda l, j: (l, 0, j)),
        ],
        out_specs=pl.BlockSpec((None, MOD_ROWS, tn), lambda l, j: (l, 0, j)),
        out_shape=jax.ShapeDtypeStruct((depth, MOD_ROWS, six_d), F32),
        compiler_params=_cparams("arbitrary", "arbitrary"),
        name="adaln_mod",
    )(cvec, ada_w, ada_b.reshape(depth, 1, six_d))


def _rms(x):
    return x * lax.rsqrt(jnp.mean(x * x, axis=-1, keepdims=True) + NORM_EPS)


def _norm_kernel(*refs, d, residual, final):
    it = iter(refs)
    x_ref = next(it)
    f_ref, gate_ref = (next(it), next(it)) if residual else (None, None)
    mod_ref = None if final else next(it)
    g_ref = next(it)
    xo_ref = next(it) if (residual and not final) else None
    h_ref = next(it)
    x = x_ref[...]
    if residual:
        x = x + gate_ref[...] * f_ref[...]
        if xo_ref is not None:
            xo_ref[...] = x
    y = _rms(x) * g_ref[...]
    if not final:
        y = y * (1.0 + mod_ref[:, d:2 * d]) + mod_ref[:, 0:d]
    h_ref[...] = y.astype(h_ref.dtype)


def _norm_call(x, ffn, mod_gate, mod_next, g, mod_row, tm, rows=None):
    n, d = x.shape
    residual = ffn is not None
    final = mod_next is None
    r0, n_out = (0, n) if rows is None else rows
    blk0 = r0 // tm
    row = pl.BlockSpec((tm, d), lambda i: (blk0 + i, 0))
    in_specs, args = [row], [x]
    if residual:
        in_specs += [row, pl.BlockSpec((None, 1, d), lambda i: (mod_row(blk0 + i), 0, 5))]
        args += [ffn, mod_gate]
    if not final:
        in_specs.append(pl.BlockSpec((None, 1, 2 * d), lambda i: (mod_row(blk0 + i), 0, 0)))
        args.append(mod_next)
    in_specs.append(pl.BlockSpec((1, d), lambda i: (0, 0)))
    args.append(g.reshape(1, d))
    out_row = pl.BlockSpec((tm, d), lambda i: (i, 0))
    out_specs, out_shape = [out_row], [jax.ShapeDtypeStruct((n_out, d), F32 if final else BF16)]
    if residual and not final:
        out_specs = [out_row, out_row]
        out_shape = [jax.ShapeDtypeStruct((n_out, d), F32)] + out_shape
    outs = pl.pallas_call(
        functools.partial(_norm_kernel, d=d, residual=residual, final=final),
        grid=(n_out // tm,),
        in_specs=in_specs,
        out_specs=out_specs,
        out_shape=out_shape,
        compiler_params=_cparams("arbitrary"),
        name="residual_norm",
    )(*args)
    return outs if len(outs) == 2 else (x, outs[0])


def _proj_kernel(h_ref, w_ref, o_ref, wb_ref):
    @pl.when(pl.program_id(1) == 0)
    def _():
        wb_ref[...] = w_ref[...].astype(BF16)

    o_ref[...] = _dot(h_ref[...], wb_ref[...])


def _proj_call(h, w, layer, tm, tn):
    n, d = h.shape
    cols = w.shape[-1]
    return pl.pallas_call(
        _proj_kernel,
        grid=(cols // tn, n // tm),
        in_specs=[
            pl.BlockSpec((tm, d), lambda j, i: (i, 0)),
            pl.BlockSpec((None, d, tn), lambda j, i: (layer, 0, j)),
        ],
        out_specs=pl.BlockSpec((tm, tn), lambda j, i: (i, j)),
        out_shape=jax.ShapeDtypeStruct((n, cols), F32),
        scratch_shapes=[pltpu.VMEM((d, tn), BF16)],
        compiler_params=_cparams("arbitrary", "arbitrary"),
        name="in_proj",
    )(h, w)


def _swap_half_pairs(x):
    lane = lax.broadcasted_iota(I32, x.shape, 1)
    return jnp.where((lane & 32) == 0, pltpu.roll(x, LANE - 32, 1), pltpu.roll(x, 32, 1))


def _ret_kernel(*refs, heads, dk, dv, chunk, n_chunks, has_init, has_rope):
    it = iter(refs)
    lg_ref = next(it)
    qkv = [[next(it) for _ in range(3)] for _ in range(2)]
    s0_ref = next(it) if has_init else None
    rope = [[next(it) for _ in range(2)] for _ in range(2)] if has_rope else None
    o_refs = [next(it), next(it)]
    sfin_ref = next(it)
    s_ref = next(it)

    c = pl.program_id(1)

    @pl.when(c == 0)
    def _():
        if has_init:
            s_ref[...] = s0_ref[...]
        else:
            s_ref[...] = jnp.zeros_like(s_ref)

    L = chunk
    ii = lax.broadcasted_iota(I32, (L, L), 0)
    jj = lax.broadcasted_iota(I32, (L, L), 1)
    pcol = lax.broadcasted_iota(I32, (L, 1), 0).astype(F32)
    n_sub = qkv[0][0].shape[0] // L
    jobs = [(d, hh, r) for d in range(2) for hh in range(heads) for r in range(n_sub)]
    qs, ks, vs, scores, q_in, k_out = [], [], [], [], [], []
    for d, hh, r in jobs:
        q_ref, k_ref, v_ref = qkv[d]
        rows = slice(r * L, (r + 1) * L)
        q = q_ref[rows, hh * dk:(hh + 1) * dk]
        k = k_ref[rows, hh * dk:(hh + 1) * dk] * (dk ** -0.5)
        if has_rope:
            cos, sin = rope[d][0][rows, :], rope[d][1][rows, :]
            q = q * cos + _swap_half_pairs(q) * sin
            k = k * cos + _swap_half_pairs(k) * sin
        qs.append(q)
        ks.append(k)
        vs.append(v_ref[rows, hh * dv:(hh + 1) * dv].astype(BF16))
    for (d, hh, r), q, k in zip(jobs, qs, ks):
        dif = ((ii - jj) if d == 0 else (jj - ii)).astype(F32)
        lg = lg_ref[d, hh]
        decay = jnp.where(dif >= 0.0, jnp.exp(lg * jnp.maximum(dif, 0.0)), 0.0)
        scores.append((_dot_nt(q.astype(BF16), k.astype(BF16)) * decay).astype(BF16))
    for (d, hh, r), q, k in zip(jobs, qs, ks):
        lg = lg_ref[d, hh]
        q_pow, k_pow = (pcol + 1.0, (L - 1.0) - pcol) if d == 0 else (L - pcol, pcol)
        q_in.append((q * jnp.exp(lg * q_pow)).astype(BF16))
        k_out.append((k * jnp.exp(lg * k_pow)).T.astype(BF16))
    intra = [_dot(sc, v) for sc, v in zip(scores, vs)]
    kvs = [_dot(ko, v) for ko, v in zip(k_out, vs)]
    for d in range(2):
        for hh in range(heads):
            s = s_ref[d, hh]
            chunk_decay = jnp.exp(jnp.full((1, 1), lg_ref[d, hh] * L, F32))
            for r in (range(n_sub) if d == 0 else range(n_sub - 1, -1, -1)):
                j = (d * heads + hh) * n_sub + r
                o_refs[d][r * L:(r + 1) * L, hh * dv:(hh + 1) * dv] = intra[j] + _dot(q_in[j], s.astype(BF16))
                s = s * chunk_decay + kvs[j]
            s_ref[d, hh] = s

    @pl.when(c == n_chunks - 1)
    def _():
        sfin_ref[...] = s_ref[...]


def _ret_call(proj, log_gamma, s0, rope, *, tok0, batch, seq, heads, dk, dv):
    rb = RET_CHUNK * RET_CHUNKS_PER_STEP
    n = seq // rb
    qw, vw = heads * dk, heads * dv
    assert seq % rb == 0 and tok0 % rb == 0 and vw % qw == 0
    r0 = tok0 // rb

    def fwd(b, c):
        return r0 + b * n + c

    def bwd(b, c):
        return r0 + b * n + (n - 1 - c)

    in_specs = [pl.BlockSpec(memory_space=pltpu.SMEM)]
    args = [log_gamma]
    for rmap in (fwd, bwd):
        in_specs += [
            pl.BlockSpec((rb, qw), lambda b, c, rmap=rmap: (rmap(b, c), 0)),
            pl.BlockSpec((rb, qw), lambda b, c, rmap=rmap: (rmap(b, c), 1)),
            pl.BlockSpec((rb, vw), lambda b, c, rmap=rmap: (rmap(b, c), (2 * qw) // vw)),
        ]
        args += [proj, proj, proj]
    state_spec = pl.BlockSpec((None, 2, heads, dk, dv), lambda b, c: (b, 0, 0, 0, 0))
    if s0 is not None:
        in_specs.append(state_spec)
        args.append(s0)
    if rope is not None:
        for cmap in (lambda b, c: (c, 0), lambda b, c: (n - 1 - c, 0)):
            in_specs += [pl.BlockSpec((rb, dk), cmap), pl.BlockSpec((rb, dk), cmap)]
            args += [rope[0], rope[1]]
    n_tok = batch * seq
    return pl.pallas_call(
        functools.partial(_ret_kernel, heads=heads, dk=dk, dv=dv, chunk=RET_CHUNK, n_chunks=n,
                          has_init=s0 is not None, has_rope=rope is not None),
        grid=(batch, n),
        in_specs=in_specs,
        out_specs=[
            pl.BlockSpec((rb, vw), lambda b, c: (b * n + c, 0)),
            pl.BlockSpec((rb, vw), lambda b, c: (b * n + (n - 1 - c), 0)),
            state_spec,
        ],
        out_shape=[
            jax.ShapeDtypeStruct((n_tok, vw), F32),
            jax.ShapeDtypeStruct((n_tok, vw), F32),
            jax.ShapeDtypeStruct((batch, 2, heads, dk, dv), F32),
        ],
        scratch_shapes=[pltpu.VMEM((2, heads, dk, dv), F32)],
        compiler_params=_cparams("arbitrary", "arbitrary"),
        name="retention_scan",
    )(*args)


def _hg_scan_index(shape, axis, rev):
    i = lax.broadcasted_iota(I32, shape, axis) & (HG_CHUNK - 1)
    return (HG_CHUNK - 1 - i) if rev else i


def _hg_window_matrix(rev):
    C = HG_CHUNK
    ti = _hg_scan_index((C, 2 * C), 0, rev)
    tj = _hg_scan_index((C, 2 * C), 1, rev)
    blocks = [tj <= ti]
    for l in range(2, HG_LEVELS + 1):
        anchor = ((ti >> l) << l) + (1 << (l - 1)) - 1
        upper = ((ti >> (l - 1)) & 1) == 1
        blocks.append((upper & (tj > anchor) & (tj <= ti)) | (~upper & (tj > ti) & (tj <= anchor)))
    blocks.append(tj > ti)
    return jnp.concatenate([jnp.where(m, 1.0, 0.0) for m in blocks], axis=0).astype(BF16)


def _hg_kernel(*refs, heads, dk, dv, tt, n_blocks, has_init):
    it = iter(refs)
    qzv = [[next(it) for _ in range(3)] for _ in range(2)]
    lb_ref = next(it)
    s0_ref = next(it) if has_init else None
    o_refs = [next(it), next(it)]
    sfin_ref = next(it)
    st_ref = next(it)
    win_ref = next(it)
    gate_a = [next(it) for _ in range(6)]
    gate_b = [next(it) for _ in range(6)]

    t = pl.program_id(1)

    @pl.when((pl.program_id(0) == 0) & (t == 0))
    def _():
        for d in range(2):
            win_ref[d] = _hg_window_matrix(d == 1)

    @pl.when(t == 0)
    def _():
        for d in range(2):
            for hh in range(heads):
                st_ref[d, hh] = s0_ref[d, hh].T if has_init else jnp.zeros((dv, dk), F32)

    C = HG_CHUNK
    n_ch = tt // C
    chunks = [slice(c * C, (c + 1) * C) for c in range(n_ch)]
    levels, uppers = [], []
    for d in range(2):
        ti = _hg_scan_index((C, C), 0, d == 1)
        tj = _hg_scan_index((C, C), 1, d == 1)
        level = jnp.where(tj < ti, 1, 0)
        for l in range(1, HG_LEVELS):
            level = level + jnp.where((tj < ti) & ((ti >> l) != (tj >> l)), 1, 0)
        t_col = _hg_scan_index((C, 1), 0, d == 1)
        levels.append(level)
        uppers.append([None] + [((t_col >> (l - 1)) & 1) == 1 for l in range(1, HG_LEVELS + 1)])

    def gates(hh, g_refs):
        qg_ref, k_ref, f_ref, diag_ref, hi_ref, lo_ref = g_refs
        ck = pl.ds(pl.multiple_of(hh * dk, LANE), dk)
        for d in range(2):
            q_ref, z_ref, _ = qzv[d]
            lb = lb_ref[d, :, ck]
            z = z_ref[:, ck]
            e = jnp.exp(-jnp.abs(z))
            inv = 1.0 / (1.0 + e)
            pos = z > 0.0
            f = lb + (1.0 - lb) * (jnp.where(pos, 1.0, e) * inv)
            k = (1.0 - lb) * (jnp.where(pos, e, 1.0) * inv)
            tiny = (lb <= 0.0) & (z < HG_LOG_SIGMOID_LINEAR)
            log2_f = jnp.where(tiny, z * LOG2_E, jnp.log2(f))
            hi = log2_f.astype(BF16)
            qg = _silu(q_ref[:, ck]) * (dk ** -0.5)
            qg_ref[d], k_ref[d], f_ref[d] = qg, k, f
            hi_ref[d] = hi
            lo_ref[d] = (log2_f - hi.astype(F32)).astype(BF16)
            diag_ref[d] = jnp.broadcast_to(jnp.sum(qg * k, axis=-1, keepdims=True), qg.shape)

    def dots(hh, g_refs):
        qg_ref, k_ref, f_ref, diag_ref, hi_ref, lo_ref = g_refs
        cv = pl.ds(pl.multiple_of(hh * dv, LANE), dv)
        jobs = [(d, r) for d in range(2) for r in chunks]
        qg, k, f, diag = ([ref[d] for d in range(2)] for ref in (qg_ref, k_ref, f_ref, diag_ref))
        v = [qzv[d][2][:, cv] for d in range(2)]
        vb = [a.astype(BF16) for a in v]
        wins = [win_ref[0], win_ref[1]]
        sums = [_dot(wins[d], jnp.concatenate([hi_ref[d, r, :], lo_ref[d, r, :]], axis=0)) for d, r in jobs]

        def query_or_key(d, r, l):
            half = 1 << (l - 1)
            if half % F32_SUBLANE:
                return jnp.where(uppers[d][l], qg[d][r], k[d][r])
            segs = []
            for s in range(C // half):
                src = qg[d] if (s % 2 == 1) != (d == 1) else k[d]
                segs.append(src[r.start + s * half:r.start + (s + 1) * half])
            return jnp.concatenate(segs, axis=0)

        pairs = []
        for (d, r), s in zip(jobs, sums):
            ps = [jnp.where(uppers[d][1], qg[d][r] * f[d][r], k[d][r]).astype(BF16)]
            for l in range(2, HG_LEVELS + 1):
                scale = jnp.exp2(s[(l - 1) * C:l * C])
                ps.append((query_or_key(d, r, l) * scale).astype(BF16))
            pairs.append(ps)
        grams = [[_dot_nt(p, p) for p in ps] for ps in pairs]
        atts = []
        for (d, r), gs in zip(jobs, grams):
            att = jnp.where(levels[d] == 1, gs[0], 0.0)
            for l in range(2, HG_LEVELS + 1):
                att = jnp.where(levels[d] == l, gs[l - 1], att)
            atts.append(att.astype(BF16))
        k_out = [(k[d][r] * jnp.exp2(s[HG_LEVELS * C:])).astype(BF16) for (d, r), s in zip(jobs, sums)]
        q_in = [(qg[d][r] * jnp.exp2(s[:C])).astype(BF16) for (d, r), s in zip(jobs, sums)]
        decays = [jnp.exp2(s[0:1] if d == 1 else s[C - 1:C]) for (d, r), s in zip(jobs, sums)]
        outs = [_dot(att, vb[d][r]) + diag[d][r] * v[d][r] for att, (d, r) in zip(atts, jobs)]
        kvs = [_dot_tn(vb[d][r], ko) for (d, r), ko in zip(jobs, k_out)]
        for d in range(2):
            st = st_ref[d, hh]
            for c in (range(n_ch - 1, -1, -1) if d == 1 else range(n_ch)):
                j = d * n_ch + c
                o_refs[d][chunks[c], cv] = outs[j] + _dot_nt(q_in[j], st.astype(BF16))
                st = st * decays[j] + kvs[j]
            st_ref[d, hh] = st

    gates(0, gate_a)

    def head_pair(i, carry):
        gates(2 * i + 1, gate_b)
        dots(2 * i, gate_a)
        dots(2 * i + 1, gate_b)
        gates(jnp.minimum(2 * i + 2, heads - 1), gate_a)
        return carry

    lax.fori_loop(0, heads // 2, head_pair, 0)

    @pl.when(t == n_blocks - 1)
    def _():
        for d in range(2):
            for hh in range(heads):
                sfin_ref[d, hh] = st_ref[d, hh].T


def _hg_call(proj, lb, s0, *, tok0, batch, seq, heads, dk, dv, col_q, col_f, col_i):
    tt = min(seq, 256)
    n = seq // tt
    kw, vw = heads * dk, heads * dv
    assert seq % tt == 0 and tok0 % tt == 0 and tt % HG_CHUNK == 0
    assert col_q % kw == 0 and col_f % kw == 0 and col_i % vw == 0
    r0 = tok0 // tt

    def fwd(b, t):
        return r0 + b * n + t

    def bwd(b, t):
        return r0 + b * n + (n - 1 - t)

    in_specs, args = [], []
    for d, rmap in enumerate((fwd, bwd)):
        in_specs += [
            pl.BlockSpec((tt, kw), lambda b, t, rmap=rmap: (rmap(b, t), col_q // kw)),
            pl.BlockSpec((tt, kw), lambda b, t, rmap=rmap, d=d: (rmap(b, t), col_f // kw + d)),
            pl.BlockSpec((tt, vw), lambda b, t, rmap=rmap: (rmap(b, t), col_i // vw)),
        ]
        args += [proj, proj, proj]
    in_specs.append(pl.BlockSpec((2, 1, kw), lambda b, t: (0, 0, 0)))
    args.append(lb.reshape(2, 1, kw))
    state_spec = pl.BlockSpec((None, 2, heads, dk, dv), lambda b, t: (b, 0, 0, 0, 0))
    if s0 is not None:
        in_specs.append(state_spec)
        args.append(s0)
    n_tok = batch * seq
    return pl.pallas_call(
        functools.partial(_hg_kernel, heads=heads, dk=dk, dv=dv, tt=tt, n_blocks=n, has_init=s0 is not None),
        grid=(batch, n),
        in_specs=in_specs,
        out_specs=[
            pl.BlockSpec((tt, vw), lambda b, t: (b * n + t, 0)),
            pl.BlockSpec((tt, vw), lambda b, t: (b * n + (n - 1 - t), 0)),
            state_spec,
        ],
        out_shape=[
            jax.ShapeDtypeStruct((n_tok, vw), F32),
            jax.ShapeDtypeStruct((n_tok, vw), F32),
            jax.ShapeDtypeStruct((batch, 2, heads, dk, dv), F32),
        ],
        scratch_shapes=[pltpu.VMEM((2, heads, dv, dk), F32),
                        pltpu.VMEM((2, (HG_LEVELS + 1) * HG_CHUNK, 2 * HG_CHUNK), BF16)]
        + 2 * ([pltpu.VMEM((2, tt, dk), F32)] * 4 + [pltpu.VMEM((2, tt, dk), BF16)] * 2),
        compiler_params=_cparams("arbitrary", "arbitrary"),
        name="hgrn2_scan",
    )(*args)


def _mix_out_kernel(*refs, d, ret_heads, hg_heads, ctx_blocks):
    scans = [refs[4 * p:4 * p + 4] for p in range(2)]
    (rg_ref, go_ref, h_ref, x_ref, mod_ref, g2_ref, wr_ref, wh_ref, wm_ref, wo_ref, wrt_ref,
     x1_ref, h2_ref, lgt_ref, o_ref, og_ref) = refs[8:]

    for p, in_set in enumerate((pl.program_id(0) < ctx_blocks, pl.program_id(0) >= ctx_blocks)):
        @pl.when(in_set)
        def _(p=p):
            o_ref[...] = scans[p][0][...] + scans[p][1][...]
            og_ref[...] = scans[p][2][...] + scans[p][3][...]

    dvr = o_ref.shape[1] // ret_heads
    dvh = og_ref.shape[1] // hg_heads
    tm = o_ref.shape[0]
    halves = [slice(0, tm // 2), slice(tm // 2, tm)]
    ret_in, hg_in = [], []
    for r in halves:
        parts = []
        for hh in range(ret_heads):
            oh = o_ref[r, hh * dvr:(hh + 1) * dvr]
            ctr = oh - jnp.mean(oh, axis=-1, keepdims=True)
            parts.append(ctr * lax.rsqrt(jnp.mean(ctr * ctr, axis=-1, keepdims=True) + NORM_EPS))
        ret_in.append((jnp.concatenate(parts, axis=1) * _silu(rg_ref[r, :])).astype(BF16))
        parts = [_rms(og_ref[r, hh * dvh:(hh + 1) * dvh]) for hh in range(hg_heads)]
        hg_in.append((jnp.concatenate(parts, axis=1) * _silu(go_ref[r, :])).astype(BF16))
    ret_out = [_dot(a, wr_ref[...]) for a in ret_in]
    hg_out = [_dot(a, wh_ref[...]) for a in hg_in]
    gates = [jax.nn.sigmoid(_dot(h_ref[r, :], wm_ref[...])) for r in halves]
    merged = [(g[:, :d] * a + g[:, d:] * b).astype(BF16) for g, a, b in zip(gates, ret_out, hg_out)]
    mix = [_dot(m, wo_ref[...]) for m in merged]
    w_hi, w_lo = wrt_ref[0], wrt_ref[1]
    for r, m in zip(halves, mix):
        x1 = x_ref[r, :] + mod_ref[:, 2 * d:3 * d] * m
        x1_ref[r, :] = x1
        h2 = _rms(x1) * g2_ref[...]
        h2 = h2 * (1.0 + mod_ref[:, 4 * d:5 * d]) + mod_ref[:, 3 * d:4 * d]
        h2_ref[r, :] = h2.astype(BF16)
        h_hi = h2.astype(BF16)
        h_lo = (h2 - h_hi.astype(F32)).astype(BF16)
        lgt_ref[:, r] = _dot_nt(w_hi, h_hi) + (_dot_nt(w_hi, h_lo) + _dot_nt(w_lo, h_hi))


def _mix_out_call(scans_ctx, scans_lat, proj, h, x, mod_l, mod_row, g2, w_ret_o, w_hg_o, w_merge, w_out,
                  w_router_t, tm, *, ret_heads, hg_heads, col_rg, col_go):
    n, d = x.shape
    vr, vh = scans_ctx[0].shape[1], scans_ctx[2].shape[1]
    ne = w_router_t.shape[1]
    ctx_blocks = scans_ctx[0].shape[0] // tm
    assert col_rg % vr == 0 and col_go % vh == 0

    def row(w):
        return pl.BlockSpec((tm, w), lambda i: (i, 0))

    def full(a):
        return pl.BlockSpec(a.shape, lambda i: (0,) * a.ndim)

    ctx_row = lambda a: pl.BlockSpec((tm, a.shape[1]), lambda i: (jnp.minimum(i, ctx_blocks - 1), 0))
    lat_row = lambda a: pl.BlockSpec((tm, a.shape[1]), lambda i: (jnp.maximum(i - ctx_blocks, 0), 0))
    return pl.pallas_call(
        functools.partial(_mix_out_kernel, d=d, ret_heads=ret_heads, hg_heads=hg_heads, ctx_blocks=ctx_blocks),
        grid=(n // tm,),
        in_specs=[ctx_row(a) for a in scans_ctx] + [lat_row(a) for a in scans_lat] + [
            pl.BlockSpec((tm, vr), lambda i: (i, col_rg // vr)),
            pl.BlockSpec((tm, vh), lambda i: (i, col_go // vh)),
            row(d), row(d),
            pl.BlockSpec((None, 1, 6 * d), lambda i: (mod_row(i), 0, 0)),
            pl.BlockSpec((1, d), lambda i: (0, 0)),
            full(w_ret_o), full(w_hg_o), full(w_merge), full(w_out), full(w_router_t),
        ],
        out_specs=[row(d), row(d), pl.BlockSpec((ne, tm), lambda i: (0, i))],
        out_shape=[
            jax.ShapeDtypeStruct((n, d), F32),
            jax.ShapeDtypeStruct((n, d), BF16),
            jax.ShapeDtypeStruct((ne, n), F32),
        ],
        scratch_shapes=[pltpu.VMEM((tm, vr), F32), pltpu.VMEM((tm, vh), F32)],
        compiler_params=_cparams("arbitrary"),
        name="mixer_out",
    )(*scans_ctx, *scans_lat, proj, proj, h, x, mod_l, g2.reshape(1, d), w_ret_o, w_hg_o, w_merge, w_out,
      w_router_t)


def _route_kernel(lg_ref, aff_ref, sel_ref, posx_ref, flag_ref, *, sets, caps):
    for (lo, n), cap in zip(sets, caps):
        _route_set(lg_ref, aff_ref, sel_ref, posx_ref, flag_ref, lo=lo, n=n, cap=cap)


def _route_set(lg_ref, aff_ref, sel_ref, posx_ref, flag_ref, *, lo, n, cap):
    cols = slice(lo, lo + n)
    lg = lg_ref[:, cols]
    ne = lg.shape[0]
    ex = jnp.exp(lg - jnp.max(lg, axis=0, keepdims=True))
    aff = ex / jnp.sum(ex, axis=0, keepdims=True)
    aff_ref[:, cols] = aff
    bits = pltpu.bitcast(aff, I32)

    def search(i, found):
        cand = found | lax.shift_left(jnp.int32(1), 30 - i)
        cnt = jnp.sum(jnp.where(bits >= cand, 1.0, 0.0), axis=1, keepdims=True)
        return jnp.where(cnt >= cap, cand, found)

    thr = lax.fori_loop(0, 31, search, jnp.zeros((ne, 1), I32))
    gt = bits > thr
    eq = bits == thr
    need = cap - jnp.sum(jnp.where(gt, 1.0, 0.0), axis=1, keepdims=True)
    upper = jnp.where(lax.broadcasted_iota(I32, (LANE, LANE), 0) < lax.broadcasted_iota(I32, (LANE, LANE), 1),
                      1.0, 0.0).astype(BF16)

    def exclusive_count(store):
        def body(ci, carry):
            sl = pl.ds(pl.multiple_of(lo + ci * LANE, LANE), LANE)
            x = flag_ref[:, sl]
            store(sl, carry + _dot(x.astype(BF16), upper))
            return carry + jnp.sum(x, axis=1, keepdims=True)

        lax.fori_loop(0, n // LANE, body, jnp.zeros((ne, 1), F32))

    flag_ref[:, cols] = jnp.where(eq, 1.0, 0.0)

    def store_tie_rank(sl, rank):
        posx_ref[:, sl] = rank.astype(I32)

    exclusive_count(store_tie_rank)
    sel = gt | (eq & (posx_ref[:, cols].astype(F32) < need))
    sel_ref[:, cols] = jnp.where(sel, 1, 0).astype(I32)
    flag_ref[:, cols] = jnp.where(sel, 1.0, 0.0)

    def store_pos(sl, cnt):
        posx_ref[:, sl] = cnt.astype(I32)

    exclusive_count(store_pos)


def _route_call(logits_t, sets, caps):
    ne, n = logits_t.shape
    spec = pl.BlockSpec((ne, n), lambda: (0, 0))
    return pl.pallas_call(
        functools.partial(_route_kernel, sets=sets, caps=caps),
        in_specs=[spec],
        out_specs=[spec, spec, spec],
        out_shape=[
            jax.ShapeDtypeStruct((ne, n), F32),
            jax.ShapeDtypeStruct((ne, n), I32),
            jax.ShapeDtypeStruct((ne, n), I32),
        ],
        scratch_shapes=[pltpu.VMEM((ne, n), F32)],
        compiler_params=pltpu.CompilerParams(vmem_limit_bytes=VMEM_LIMIT),
        name="expert_choice_route",
    )(logits_t)


def _moe_round_masks(b, r, base_ref, sel_ref, posx_ref, ne, value_rows=None):
    n_tok = sel_ref.shape[1]
    row_iota = lax.broadcasted_iota(I32, (MOE_ROUND, n_tok), 0)
    pieces = []
    for e in range(ne):
        local = posx_ref[e:e + 1, :] - (base_ref[b * ne + e] + r * MOE_ROUND)
        hit = (sel_ref[e:e + 1, :] == 1) & (local == row_iota)
        value = 1.0 if value_rows is None else value_rows[e:e + 1, :]
        pieces.append(jnp.where(hit, value, 0.0).astype(BF16))
    return jnp.concatenate(pieces, axis=0)


def _dispatch_kernel(base_ref, start_ref, pc_ref, nr_ref, h_ref, sel_ref, posx_ref, xe_zero_ref, xe_ref,
                     stage_ref, sem, *, ne):
    del xe_zero_ref
    b = pl.program_id(0)
    n_rounds = nr_ref[b]

    def pieces(r, slot, fn):
        for e in range(ne):
            rem = pc_ref[b * ne + e] - r * MOE_ROUND
            dst = start_ref[b * ne + e] + r * MOE_ROUND
            for size, cond, off in (
                    (MOE_ROUND, rem >= MOE_ROUND, 0),
                    (32, (rem > 0) & (rem < MOE_ROUND) & ((rem & 32) != 0), 0),
                    (16, (rem > 0) & (rem < MOE_ROUND) & ((rem & 16) != 0), rem & 32)):
                @pl.when(cond)
                def _(size=size, off=off, e=e, dst=dst):
                    src_rows = pl.ds(pl.multiple_of(e * MOE_ROUND + off, BF16_SUBLANE), size)
                    dst_rows = pl.ds(pl.multiple_of(dst + off, BF16_SUBLANE), size)
                    fn(pltpu.make_async_copy(stage_ref.at[slot, src_rows], xe_ref.at[e, dst_rows], sem.at[slot]))

    def round_body(r, carry):
        slot = r % 2
        onehot = _moe_round_masks(b, r, base_ref, sel_ref, posx_ref, ne)
        stage_ref[slot] = _dot(onehot, h_ref[...]).astype(BF16)

        @pl.when(r >= 1)
        def _():
            pieces(r - 1, 1 - slot, lambda cp: cp.wait())

        pieces(r, slot, lambda cp: cp.start())
        return carry

    lax.fori_loop(0, n_rounds, round_body, 0)

    @pl.when(n_rounds >= 1)
    def _():
        pieces(n_rounds - 1, (n_rounds - 1) % 2, lambda cp: cp.wait())


def _dispatch_call(h2, sel, posx, tables, list_rows, *, sb):
    n, d = h2.shape
    ne = sel.shape[0]
    xe_zero = jnp.zeros((ne, list_rows, d), BF16)
    grid_spec = pltpu.PrefetchScalarGridSpec(
        num_scalar_prefetch=4,
        grid=(n // sb,),
        in_specs=[
            pl.BlockSpec((sb, d), lambda b, *_: (b, 0)),
            pl.BlockSpec((ne, sb), lambda b, *_: (0, b)),
            pl.BlockSpec((ne, sb), lambda b, *_: (0, b)),
            pl.BlockSpec(memory_space=pl.ANY),
        ],
        out_specs=pl.BlockSpec(memory_space=pl.ANY),
        scratch_shapes=[pltpu.VMEM((2, ne * MOE_ROUND, d), BF16), pltpu.SemaphoreType.DMA((2,))],
    )
    return pl.pallas_call(
        functools.partial(_dispatch_kernel, ne=ne),
        grid_spec=grid_spec,
        out_shape=jax.ShapeDtypeStruct((ne, list_rows, d), BF16),
        input_output_aliases={7: 0},
        compiler_params=_cparams("arbitrary"),
        name="expert_dispatch",
    )(*tables, h2, sel, posx, xe_zero)


def _expert_ffn_kernel(tot_ref, x_ref, wg_ref, wu_ref, wd_ref, y_ref, wgb_ref, wub_ref, wdb_ref, acc_ref,
                       *, rt, n_ff):
    e, f, t = pl.program_id(0), pl.program_id(1), pl.program_id(2)

    @pl.when(t == 0)
    def _():
        wgb_ref[...] = wg_ref[...].astype(BF16)
        wub_ref[...] = wu_ref[...].astype(BF16)
        wdb_ref[...] = wd_ref[...].astype(BF16)

    rows = pl.ds(pl.multiple_of(t * rt, rt), rt)
    live = t * rt < tot_ref[e]

    @pl.when(live)
    def _():
        x = x_ref[...]
        hid = (_silu(_dot(x, wgb_ref[...])) * _dot(x, wub_ref[...])).astype(BF16)
        y = _dot(hid, wdb_ref[...])

        @pl.when(f == 0)
        def _():
            acc_ref[rows, :] = y

        @pl.when(f != 0)
        def _():
            acc_ref[rows, :] += y

    @pl.when((f == n_ff - 1) & live)
    def _():
        y_ref[...] = acc_ref[rows, :].astype(BF16)

    @pl.when((f == n_ff - 1) & jnp.logical_not(live))
    def _():
        y_ref[...] = jnp.zeros_like(y_ref)


def _expert_ffn_call(xe, totals, wg, wu, wd, layer, *, rt, fc):
    ne, list_rows, d = xe.shape
    ff = wg.shape[-1]
    n_ff = ff // fc
    grid_spec = pltpu.PrefetchScalarGridSpec(
        num_scalar_prefetch=1,
        grid=(ne, n_ff, list_rows // rt),
        in_specs=[
            pl.BlockSpec((None, rt, d), lambda e, f, t, *_: (e, t, 0)),
            pl.BlockSpec((None, None, d, fc), lambda e, f, t, *_: (layer, e, 0, f)),
            pl.BlockSpec((None, None, d, fc), lambda e, f, t, *_: (layer, e, 0, f)),
            pl.BlockSpec((None, None, fc, d), lambda e, f, t, *_: (layer, e, f, 0)),
        ],
        out_specs=pl.BlockSpec((None, rt, d), lambda e, f, t, *_: (e, jnp.where(f == n_ff - 1, t, 0), 0)),
        scratch_shapes=[pltpu.VMEM((d, fc), BF16), pltpu.VMEM((d, fc), BF16), pltpu.VMEM((fc, d), BF16),
                        pltpu.VMEM((list_rows, d), F32)],
    )
    return pl.pallas_call(
        functools.partial(_expert_ffn_kernel, rt=rt, n_ff=n_ff),
        grid_spec=grid_spec,
        out_shape=jax.ShapeDtypeStruct((ne, list_rows, d), BF16),
        compiler_params=_cparams("arbitrary", "arbitrary", "arbitrary"),
        name="expert_ffn",
    )(totals, xe, wg, wu, wd)


def _combine_kernel(base_ref, start_ref, pc_ref, nr_ref, sel_ref, posx_ref, aff_ref, y_ref, out_ref, ybuf_ref, sem,
                    *, ne):
    b = pl.program_id(0)
    n_rounds = nr_ref[b]

    @pl.when(b == 0)
    def _():
        ybuf_ref[...] = jnp.zeros_like(ybuf_ref)

    def windows(r, slot, fn):
        for e in range(ne):
            @pl.when(pc_ref[b * ne + e] - r * MOE_ROUND > 0)
            def _(e=e):
                src = pl.ds(pl.multiple_of(start_ref[b * ne + e] + r * MOE_ROUND, BF16_SUBLANE), MOE_ROUND)
                fn(pltpu.make_async_copy(y_ref.at[e, src], ybuf_ref.at[slot, pl.ds(e * MOE_ROUND, MOE_ROUND)],
                                         sem.at[slot]))

    out_ref[...] = jnp.zeros_like(out_ref)

    @pl.when(n_rounds >= 1)
    def _():
        windows(0, 0, lambda cp: cp.start())

    def round_body(r, carry):
        slot = r % 2

        @pl.when(r + 1 < n_rounds)
        def _():
            windows(r + 1, 1 - slot, lambda cp: cp.start())

        windows(r, slot, lambda cp: cp.wait())
        weights = _moe_round_masks(b, r, base_ref, sel_ref, posx_ref, ne, value_rows=aff_ref)
        out_ref[...] += _dot_tn(weights, ybuf_ref[slot])
        return carry

    lax.fori_loop(0, n_rounds, round_body, 0)


def _combine_call(ye, sel, posx, aff, tables, *, sb):
    ne, _, d = ye.shape
    n = sel.shape[1]
    grid_spec = pltpu.PrefetchScalarGridSpec(
        num_scalar_prefetch=4,
        grid=(n // sb,),
        in_specs=[
            pl.BlockSpec((ne, sb), lambda b, *_: (0, b)),
            pl.BlockSpec((ne, sb), lambda b, *_: (0, b)),
            pl.BlockSpec((ne, sb), lambda b, *_: (0, b)),
            pl.BlockSpec(memory_space=pl.ANY),
        ],
        out_specs=pl.BlockSpec((sb, d), lambda b, *_: (b, 0)),
        scratch_shapes=[pltpu.VMEM((2, ne * MOE_ROUND, d), BF16), pltpu.SemaphoreType.DMA((2,))],
    )
    return pl.pallas_call(
        functools.partial(_combine_kernel, ne=ne),
        grid_spec=grid_spec,
        out_shape=jax.ShapeDtypeStruct((n, d), F32),
        compiler_params=_cparams("arbitrary"),
        name="expert_combine",
    )(*tables, sel, posx, aff, ye)


def _moe_tables(posx, sets, caps, sb, sbc):
    ne = posx.shape[0]
    fine = [jnp.concatenate([posx[:, lo:lo + n:sbc], jnp.full((ne, 1), cap, I32)], axis=1)
            for (lo, n), cap in zip(sets, caps)]
    step = sb // sbc
    base = jnp.concatenate([e[:, :-1:step].T for e in fine], axis=0)
    cnt = jnp.concatenate([(e[:, step::step] - e[:, :-1:step]).T for e in fine], axis=0)
    pc = (cnt + (BF16_SUBLANE - 1)) // BF16_SUBLANE * BF16_SUBLANE
    ends = jnp.cumsum(pc, axis=0)
    start = ends - pc
    rounds = lambda rows: jnp.max((rows + (MOE_ROUND - 1)) // MOE_ROUND, axis=1)
    flat = lambda *arrays: tuple(a.reshape(-1).astype(I32) for a in arrays)
    dispatch = flat(base, start, pc, rounds(pc))

    base_f = jnp.concatenate([e[:, :-1].T for e in fine], axis=0)
    cnt_f = jnp.concatenate([(e[:, 1:] - e[:, :-1]).T for e in fine], axis=0)
    first = jnp.repeat(start - base, step, axis=0) + base_f
    off = first % BF16_SUBLANE
    rows_f = jnp.where(cnt_f > 0, off + cnt_f, 0)
    combine = flat(base_f - off, first - off, rows_f, rounds(rows_f))
    return dispatch, combine, ends[-1].astype(I32)


def _rope_tables(n_tokens, dk):
    rows = n_tokens // GRID_W
    r, c = jnp.meshgrid(jnp.arange(rows), jnp.arange(GRID_W), indexing="ij")
    pos = jnp.stack([r.reshape(-1), c.reshape(-1)], axis=-1).astype(F32)
    nf = dk // 4
    inv_freq = ROPE_BASE ** (-jnp.arange(nf, dtype=F32) / nf)
    ang = pos[:, :, None] * inv_freq
    cos, sin = jnp.cos(ang), jnp.sin(ang)
    cos_t = jnp.concatenate([cos[:, 0], cos[:, 0], cos[:, 1], cos[:, 1]], axis=-1)
    sin_t = jnp.concatenate([-sin[:, 0], sin[:, 0], -sin[:, 1], sin[:, 1]], axis=-1)
    return cos_t, sin_t


def kernel(x_prompt, x_sample, state_ret, state_hgrn, c, c_ctx, ada_w, ada_b, norm_mix_g, norm_ffn_g, w_in,
           ret_gamma_logit, hg_lb_logit, w_ret_o, w_hg_o, w_merge, w_out, w_router, w_exp_gate, w_exp_up,
           w_exp_down, final_g):
    b_ctx, t_ctx, d = x_prompt.shape
    b_lat, t_lat, _ = x_sample.shape
    depth = w_in.shape[0]
    ret_heads, ret_dk, ret_dv = state_ret.shape[3:]
    hg_heads, hg_dk, hg_dv = state_hgrn.shape[3:]
    ne = w_router.shape[-1]
    n_ctx, n_lat = b_ctx * t_ctx, b_lat * t_lat
    n_tok = n_ctx + n_lat
    qw, vw = ret_heads * ret_dk, ret_heads * ret_dv
    kw, hw = hg_heads * hg_dk, hg_heads * hg_dv
    col_rg = 2 * qw + vw
    col_gq = col_rg + vw
    col_gf = col_gq + kw
    col_gi = col_gf + 2 * kw
    col_go = col_gi + hw
    assert col_go + hw == w_in.shape[-1] and b_lat < MOD_ROWS

    tm = 256
    assert t_ctx % tm == 0 and t_lat % tm == 0
    tm_proj = 1024 if n_tok % 1024 == 0 else tm
    sb = 1024 if (n_ctx % 4096 == 0 and n_lat % 4096 == 0) else 256
    assert n_ctx % sb == 0 and n_lat % sb == 0
    sets = ((0, n_ctx), (n_ctx, n_lat))
    caps = [CAPACITY_FACTOR * n_set // ne for _, n_set in sets]
    sbc = 256
    list_rows = sum(caps) + BF16_SUBLANE * (n_tok // sb) + MOE_ROUND
    n_tiles = max(1, list_rows // 512)
    rt = -(-list_rows // (n_tiles * BF16_SUBLANE)) * BF16_SUBLANE
    list_rows = n_tiles * rt

    def mod_rows(tmx):
        return lambda i: jnp.where(i < n_ctx // tmx, 0, 1 + (i - n_ctx // tmx) // (t_lat // tmx))

    mod_row = mod_rows(tm)
    tm_norm = 512 if (n_ctx % 512 == 0 and t_lat % 512 == 0) else tm
    norm_row = mod_rows(tm_norm)

    x = jnp.concatenate([x_prompt.reshape(n_ctx, d), x_sample.reshape(n_lat, d)], axis=0)
    cvec = jnp.zeros((MOD_ROWS, d), F32).at[0].set(c_ctx).at[1:1 + b_lat].set(c)
    mod = _mod_call(cvec, ada_w, ada_b).reshape(depth, MOD_ROWS, 1, 6 * d)

    log_gamma = jax.nn.log_sigmoid(ret_gamma_logit.astype(F32))
    p_lb = jax.nn.softmax(hg_lb_logit.astype(F32), axis=0)
    hg_lb = jnp.clip(jnp.cumsum(p_lb, axis=0) - p_lb[0:1], 0.0, 1.0 - 1e-6)
    rope = _rope_tables(t_lat, ret_dk)

    w_ret_o_b, w_hg_o_b, w_out_b = w_ret_o.astype(BF16), w_hg_o.astype(BF16), w_out.astype(BF16)
    w_merge_b = w_merge.astype(BF16)
    w_router_t = jnp.swapaxes(w_router, 1, 2)
    w_router_hi = w_router_t.astype(BF16)
    w_router_t = jnp.stack([w_router_hi, (w_router_t - w_router_hi.astype(F32)).astype(BF16)], axis=1)
    fc = min(1024, w_exp_gate.shape[-1])
    tn = min(2048, w_in.shape[-1])

    _, h = _norm_call(x, None, None, mod[0], norm_mix_g[0], norm_row, tm_norm)
    new_ret, new_hg = [], []
    for l in range(depth):
        proj = _proj_call(h, w_in, l, tm_proj, tn)

        ret_kw = dict(heads=ret_heads, dk=ret_dk, dv=ret_dv)
        orf_c, orb_c, s_ret = _ret_call(proj, log_gamma[l], None, None, tok0=0, batch=b_ctx, seq=t_ctx, **ret_kw)
        orf_l, orb_l, _ = _ret_call(proj, log_gamma[l], state_ret[:, l], rope, tok0=n_ctx, batch=b_lat, seq=t_lat,
                                    **ret_kw)
        hg_kw = dict(heads=hg_heads, dk=hg_dk, dv=hg_dv, col_q=col_gq, col_f=col_gf, col_i=col_gi)
        ogf_c, ogb_c, s_hg = _hg_call(proj, hg_lb[l], None, tok0=0, batch=b_ctx, seq=t_ctx, **hg_kw)
        ogf_l, ogb_l, _ = _hg_call(proj, hg_lb[l], state_hgrn[:, l], tok0=n_ctx, batch=b_lat, seq=t_lat, **hg_kw)
        new_ret.append(s_ret)
        new_hg.append(s_hg)

        x, h2, logits_t = _mix_out_call(
            (orf_c, orb_c, ogf_c, ogb_c), (orf_l, orb_l, ogf_l, ogb_l), proj, h, x, mod[l],
            mod_row, norm_ffn_g[l], w_ret_o_b[l], w_hg_o_b[l], w_merge_b[l], w_out_b[l], w_router_t[l], tm,
            ret_heads=ret_heads, hg_heads=hg_heads, col_rg=col_rg, col_go=col_go)

        aff, sel, posx = _route_call(logits_t, sets, caps)
        dispatch_tables, combine_tables, totals = _moe_tables(posx, sets, caps, sb, sbc)
        xe = _dispatch_call(h2, sel, posx, dispatch_tables, list_rows, sb=sb)
        ye = _expert_ffn_call(xe, totals, w_exp_gate, w_exp_up, w_exp_down, l, rt=rt, fc=fc)
        ffn = _combine_call(ye, sel, posx, aff, combine_tables, sb=sbc)

        if l + 1 < depth:
            x, h = _norm_call(x, ffn, mod[l], mod[l + 1], norm_mix_g[l + 1], norm_row, tm_norm)
        else:
            _, y_prompt = _norm_call(x, ffn, mod[l], None, final_g, norm_row, tm_norm, rows=(0, n_ctx))
            _, y_sample = _norm_call(x, ffn, mod[l], None, final_g, norm_row, tm_norm, rows=(n_ctx, n_lat))

    y_prompt = y_prompt.reshape(b_ctx, t_ctx, d)
    y_sample = y_sample.reshape(b_lat, t_lat, d)
    return y_prompt, y_sample, jnp.stack(new_ret, axis=1), jnp.stack(new_hg, axis=1)
```

```python
import functools

import jax
import jax.numpy as jnp
from jax import lax
from jax.experimental import pallas as pl
from jax.experimental.pallas import tpu as pltpu

F32 = jnp.float32
BF16 = jnp.bfloat16
I32 = jnp.int32
HIGHEST = lax.Precision.HIGHEST

NORM_EPS = 1e-6
LOG2_E = 1.4426950408889634
ROPE_BASE = 10000.0
GRID_W = 64
CAPACITY_FACTOR = 2
RET_CHUNK = 128
RET_CHUNKS_PER_STEP = 2
HG_LEVELS = 6
HG_CHUNK = 1 << HG_LEVELS
HG_LOG_SIGMOID_LINEAR = -60.0
MOD_ROWS = 8
LANE = 128
F32_SUBLANE = 8
BF16_SUBLANE = 16
MOE_ROUND = 64
VMEM_LIMIT = 62 * 1024 * 1024


def _cparams(*sem):
    return pltpu.CompilerParams(dimension_semantics=sem, vmem_limit_bytes=VMEM_LIMIT)


def _dot(a, b, **kw):
    return jnp.dot(a, b, preferred_element_type=F32, **kw)


def _dot_nt(a, b, **kw):
    return lax.dot_general(a, b, (((1,), (1,)), ((), ())), preferred_element_type=F32, **kw)


def _dot_tn(a, b, **kw):
    return lax.dot_general(a, b, (((0,), (0,)), ((), ())), preferred_element_type=F32, **kw)


def _silu(x):
    return x * jax.nn.sigmoid(x)


def _mod_kernel(c_ref, w_ref, b_ref, o_ref):
    s = _silu(c_ref[...])
    o_ref[...] = _dot(s, w_ref[...], precision=HIGHEST) + b_ref[...]


def _mod_call(cvec, ada_w, ada_b):
    depth, d, six_d = ada_w.shape
    tn = 6 * LANE * 2
    assert six_d % tn == 0
    return pl.pallas_call(
        _mod_kernel,
        grid=(depth, six_d // tn),
        in_specs=[
            pl.BlockSpec((MOD_ROWS, d), lambda l, j: (0, 0)),
            pl.BlockSpec((None, d, tn), lambda l, j: (l, 0, j)),
            pl.BlockSpec((None, 1, tn), lambda l, j: (l, 0, j)),
        ],
        out_specs=pl.BlockSpec((None, MOD_ROWS, tn), lambda l, j: (l, 0, j)),
        out_shape=jax.ShapeDtypeStruct((depth, MOD_ROWS, six_d), F32),
        compiler_params=_cparams("arbitrary", "arbitrary"),
        name="adaln_mod",
    )(cvec, ada_w, ada_b.reshape(depth, 1, six_d))


def _rms(x):
    return x * lax.rsqrt(jnp.mean(x * x, axis=-1, keepdims=True) + NORM_EPS)


def _norm_kernel(*refs, d, residual, final):
    it = iter(refs)
    x_ref = next(it)
    f_ref, gate_ref = (next(it), next(it)) if residual else (None, None)
    mod_ref = None if final else next(it)
    g_ref = next(it)
    xo_ref = next(it) if (residual and not final) else None
    h_ref = next(it)
    x = x_ref[...]
    if residual:
        x = x + gate_ref[...] * f_ref[...]
        if xo_ref is not None:
            xo_ref[...] = x
    y = _rms(x) * g_ref[...]
    if not final:
        y = y * (1.0 + mod_ref[:, d:2 * d]) + mod_ref[:, 0:d]
    h_ref[...] = y.astype(h_ref.dtype)


def _norm_call(x, ffn, mod_gate, mod_next, g, mod_row, tm, rows=None):
    n, d = x.shape
    residual = ffn is not None
    final = mod_next is None
    r0, n_out = (0, n) if rows is None else rows
    blk0 = r0 // tm
    row = pl.BlockSpec((tm, d), lambda i: (blk0 + i, 0))
    in_specs, args = [row], [x]
    if residual:
        in_specs += [row, pl.BlockSpec((None, 1, d), lambda i: (mod_row(blk0 + i), 0, 5))]
        args += [ffn, mod_gate]
    if not final:
        in_specs.append(pl.BlockSpec((None, 1, 2 * d), lambda i: (mod_row(blk0 + i), 0, 0)))
        args.append(mod_next)
    in_specs.append(pl.BlockSpec((1, d), lambda i: (0, 0)))
    args.append(g.reshape(1, d))
    out_row = pl.BlockSpec((tm, d), lambda i: (i, 0))
    out_specs, out_shape = [out_row], [jax.ShapeDtypeStruct((n_out, d), F32 if final else BF16)]
    if residual and not final:
        out_specs = [out_row, out_row]
        out_shape = [jax.ShapeDtypeStruct((n_out, d), F32)] + out_shape
    outs = pl.pallas_call(
        functools.partial(_norm_kernel, d=d, residual=residual, final=final),
        grid=(n_out // tm,),
        in_specs=in_specs,
        out_specs=out_specs,
        out_shape=out_shape,
        compiler_params=_cparams("arbitrary"),
        name="residual_norm",
    )(*args)
    return outs if len(outs) == 2 else (x, outs[0])


def _proj_kernel(h_ref, w_ref, o_ref, wb_ref):
    @pl.when(pl.program_id(1) == 0)
    def _():
        wb_ref[...] = w_ref[...].astype(BF16)

    o_ref[...] = _dot(h_ref[...], wb_ref[...])


def _proj_call(h, w, layer, tm, tn):
    n, d = h.shape
    cols = w.shape[-1]
    return pl.pallas_call(
        _proj_kernel,
        grid=(cols // tn, n // tm),
        in_specs=[
            pl.BlockSpec((tm, d), lambda j, i: (i, 0)),
            pl.BlockSpec((None, d, tn), lambda j, i: (layer, 0, j)),
        ],
        out_specs=pl.BlockSpec((tm, tn), lambda j, i: (i, j)),
        out_shape=jax.ShapeDtypeStruct((n, cols), F32),
        scratch_shapes=[pltpu.VMEM((d, tn), BF16)],
        compiler_params=_cparams("arbitrary", "arbitrary"),
        name="in_proj",
    )(h, w)


def _swap_half_pairs(x):
    lane = lax.broadcasted_iota(I32, x.shape, 1)
    return jnp.where((lane & 32) == 0, pltpu.roll(x, LANE - 32, 1), pltpu.roll(x, 32, 1))


def _ret_kernel(*refs, heads, dk, dv, chunk, n_chunks, has_init, has_rope):
    it = iter(refs)
    lg_ref = next(it)
    qkv = [[next(it) for _ in range(3)] for _ in range(2)]
    s0_ref = next(it) if has_init else None
    rope = [[next(it) for _ in range(2)] for _ in range(2)] if has_rope else None
    o_refs = [next(it), next(it)]
    sfin_ref = next(it)
    s_ref = next(it)

    c = pl.program_id(1)

    @pl.when(c == 0)
    def _():
        if has_init:
            s_ref[...] = s0_ref[...]
        else:
            s_ref[...] = jnp.zeros_like(s_ref)

    L = chunk
    ii = lax.broadcasted_iota(I32, (L, L), 0)
    jj = lax.broadcasted_iota(I32, (L, L), 1)
    pcol = lax.broadcasted_iota(I32, (L, 1), 0).astype(F32)
    n_sub = qkv[0][0].shape[0] // L
    jobs = [(d, hh, r) for d in range(2) for hh in range(heads) for r in range(n_sub)]
    qs, ks, vs, scores, q_in, k_out = [], [], [], [], [], []
    for d, hh, r in jobs:
        q_ref, k_ref, v_ref = qkv[d]
        rows = slice(r * L, (r + 1) * L)
        q = q_ref[rows, hh * dk:(hh + 1) * dk]
        k = k_ref[rows, hh * dk:(hh + 1) * dk] * (dk ** -0.5)
        if has_rope:
            cos, sin = rope[d][0][rows, :], rope[d][1][rows, :]
            q = q * cos + _swap_half_pairs(q) * sin
            k = k * cos + _swap_half_pairs(k) * sin
        qs.append(q)
        ks.append(k)
        vs.append(v_ref[rows, hh * dv:(hh + 1) * dv].astype(BF16))
    for (d, hh, r), q, k in zip(jobs, qs, ks):
        dif = ((ii - jj) if d == 0 else (jj - ii)).astype(F32)
        lg = lg_ref[d, hh]
        decay = jnp.where(dif >= 0.0, jnp.exp(lg * jnp.maximum(dif, 0.0)), 0.0)
        scores.append((_dot_nt(q.astype(BF16), k.astype(BF16)) * decay).astype(BF16))
    for (d, hh, r), q, k in zip(jobs, qs, ks):
        lg = lg_ref[d, hh]
        q_pow, k_pow = (pcol + 1.0, (L - 1.0) - pcol) if d == 0 else (L - pcol, pcol)
        q_in.append((q * jnp.exp(lg * q_pow)).astype(BF16))
        k_out.append((k * jnp.exp(lg * k_pow)).T.astype(BF16))
    intra = [_dot(sc, v) for sc, v in zip(scores, vs)]
    kvs = [_dot(ko, v) for ko, v in zip(k_out, vs)]
    for d in range(2):
        for hh in range(heads):
            s = s_ref[d, hh]
            chunk_decay = jnp.exp(jnp.full((1, 1), lg_ref[d, hh] * L, F32))
            for r in (range(n_sub) if d == 0 else range(n_sub - 1, -1, -1)):
                j = (d * heads + hh) * n_sub + r
                o_refs[d][r * L:(r + 1) * L, hh * dv:(hh + 1) * dv] = intra[j] + _dot(q_in[j], s.astype(BF16))
                s = s * chunk_decay + kvs[j]
            s_ref[d, hh] = s

    @pl.when(c == n_chunks - 1)
    def _():
        sfin_ref[...] = s_ref[...]


def _ret_call(proj, log_gamma, s0, rope, *, tok0, batch, seq, heads, dk, dv):
    rb = RET_CHUNK * RET_CHUNKS_PER_STEP
    n = seq // rb
    qw, vw = heads * dk, heads * dv
    assert seq % rb == 0 and tok0 % rb == 0 and vw % qw == 0
    r0 = tok0 // rb

    def fwd(b, c):
        return r0 + b * n + c

    def bwd(b, c):
        return r0 + b * n + (n - 1 - c)

    in_specs = [pl.BlockSpec(memory_space=pltpu.SMEM)]
    args = [log_gamma]
    for rmap in (fwd, bwd):
        in_specs += [
            pl.BlockSpec((rb, qw), lambda b, c, rmap=rmap: (rmap(b, c), 0)),
            pl.BlockSpec((rb, qw), lambda b, c, rmap=rmap: (rmap(b, c), 1)),
            pl.BlockSpec((rb, vw), lambda b, c, rmap=rmap: (rmap(b, c), (2 * qw) // vw)),
        ]
        args += [proj, proj, proj]
    state_spec = pl.BlockSpec((None, 2, heads, dk, dv), lambda b, c: (b, 0, 0, 0, 0))
    if s0 is not None:
        in_specs.append(state_spec)
        args.append(s0)
    if rope is not None:
        for cmap in (lambda b, c: (c, 0), lambda b, c: (n - 1 - c, 0)):
            in_specs += [pl.BlockSpec((rb, dk), cmap), pl.BlockSpec((rb, dk), cmap)]
            args += [rope[0], rope[1]]
    n_tok = batch * seq
    return pl.pallas_call(
        functools.partial(_ret_kernel, heads=heads, dk=dk, dv=dv, chunk=RET_CHUNK, n_chunks=n,
                          has_init=s0 is not None, has_rope=rope is not None),
        grid=(batch, n),
        in_specs=in_specs,
        out_specs=[
            pl.BlockSpec((rb, vw), lambda b, c: (b * n + c, 0)),
            pl.BlockSpec((rb, vw), lambda b, c: (b * n + (n - 1 - c), 0)),
            state_spec,
        ],
        out_shape=[
            jax.ShapeDtypeStruct((n_tok, vw), F32),
            jax.ShapeDtypeStruct((n_tok, vw), F32),
            jax.ShapeDtypeStruct((batch, 2, heads, dk, dv), F32),
        ],
        scratch_shapes=[pltpu.VMEM((2, heads, dk, dv), F32)],
        compiler_params=_cparams("arbitrary", "arbitrary"),
        name="retention_scan",
    )(*args)


def _hg_scan_index(shape, axis, rev):
    i = lax.broadcasted_iota(I32, shape, axis) & (HG_CHUNK - 1)
    return (HG_CHUNK - 1 - i) if rev else i


def _hg_window_matrix(rev):
    C = HG_CHUNK
    ti = _hg_scan_index((C, 2 * C), 0, rev)
    tj = _hg_scan_index((C, 2 * C), 1, rev)
    blocks = [tj <= ti]
    for l in range(2, HG_LEVELS + 1):
        anchor = ((ti >> l) << l) + (1 << (l - 1)) - 1
        upper = ((ti >> (l - 1)) & 1) == 1
        blocks.append((upper & (tj > anchor) & (tj <= ti)) | (~upper & (tj > ti) & (tj <= anchor)))
    blocks.append(tj > ti)
    return jnp.concatenate([jnp.where(m, 1.0, 0.0) for m in blocks], axis=0).astype(BF16)


def _hg_kernel(*refs, heads, dk, dv, tt, n_blocks, has_init):
    it = iter(refs)
    qzv = [[next(it) for _ in range(3)] for _ in range(2)]
    lb_ref = next(it)
    s0_ref = next(it) if has_init else None
    o_refs = [next(it), next(it)]
    sfin_ref = next(it)
    st_ref = next(it)
    win_ref = next(it)
    gate_a = [next(it) for _ in range(6)]
    gate_b = [next(it) for _ in range(6)]

    t = pl.program_id(1)

    @pl.when((pl.program_id(0) == 0) & (t == 0))
    def _():
        for d in range(2):
            win_ref[d] = _hg_window_matrix(d == 1)

    @pl.when(t == 0)
    def _():
        for d in range(2):
            for hh in range(heads):
                st_ref[d, hh] = s0_ref[d, hh].T if has_init else jnp.zeros((dv, dk), F32)

    C = HG_CHUNK
    n_ch = tt // C
    chunks = [slice(c * C, (c + 1) * C) for c in range(n_ch)]
    levels, uppers = [], []
    for d in range(2):
        ti = _hg_scan_index((C, C), 0, d == 1)
        tj = _hg_scan_index((C, C), 1, d == 1)
        level = jnp.where(tj < ti, 1, 0)
        for l in range(1, HG_LEVELS):
            level = level + jnp.where((tj < ti) & ((ti >> l) != (tj >> l)), 1, 0)
        t_col = _hg_scan_index((C, 1), 0, d == 1)
        levels.append(level)
        uppers.append([None] + [((t_col >> (l - 1)) & 1) == 1 for l in range(1, HG_LEVELS + 1)])

    def gates(hh, g_refs):
        qg_ref, k_ref, f_ref, diag_ref, hi_ref, lo_ref = g_refs
        ck = pl.ds(pl.multiple_of(hh * dk, LANE), dk)
        for d in range(2):
            q_ref, z_ref, _ = qzv[d]
            lb = lb_ref[d, :, ck]
            z = z_ref[:, ck]
            e = jnp.exp(-jnp.abs(z))
            inv = 1.0 / (1.0 + e)
            pos = z > 0.0
            f = lb + (1.0 - lb) * (jnp.where(pos, 1.0, e) * inv)
            k = (1.0 - lb) * (jnp.where(pos, e, 1.0) * inv)
            tiny = (lb <= 0.0) & (z < HG_LOG_SIGMOID_LINEAR)
            log2_f = jnp.where(tiny, z * LOG2_E, jnp.log2(f))
            hi = log2_f.astype(BF16)
            qg = _silu(q_ref[:, ck]) * (dk ** -0.5)
            qg_ref[d], k_ref[d], f_ref[d] = qg, k, f
            hi_ref[d] = hi
            lo_ref[d] = (log2_f - hi.astype(F32)).astype(BF16)
            diag_ref[d] = jnp.broadcast_to(jnp.sum(qg * k, axis=-1, keepdims=True), qg.shape)

    def dots(hh, g_refs):
        qg_ref, k_ref, f_ref, diag_ref, hi_ref, lo_ref = g_refs
        cv = pl.ds(pl.multiple_of(hh * dv, LANE), dv)
        jobs = [(d, r) for d in range(2) for r in chunks]
        qg, k, f, diag = ([ref[d] for d in range(2)] for ref in (qg_ref, k_ref, f_ref, diag_ref))
        v = [qzv[d][2][:, cv] for d in range(2)]
        vb = [a.astype(BF16) for a in v]
        wins = [win_ref[0], win_ref[1]]
        sums = [_dot(wins[d], jnp.concatenate([hi_ref[d, r, :], lo_ref[d, r, :]], axis=0)) for d, r in jobs]

        def query_or_key(d, r, l):
            half = 1 << (l - 1)
            if half % F32_SUBLANE:
                return jnp.where(uppers[d][l], qg[d][r], k[d][r])
            segs = []
            for s in range(C // half):
                src = qg[d] if (s % 2 == 1) != (d == 1) else k[d]
                segs.append(src[r.start + s * half:r.start + (s + 1) * half])
            return jnp.concatenate(segs, axis=0)

        pairs = []
        for (d, r), s in zip(jobs, sums):
            ps = [jnp.where(uppers[d][1], qg[d][r] * f[d][r], k[d][r]).astype(BF16)]
            for l in range(2, HG_LEVELS + 1):
                scale = jnp.exp2(s[(l - 1) * C:l * C])
                ps.append((query_or_key(d, r, l) * scale).astype(BF16))
            pairs.append(ps)
        grams = [[_dot_nt(p, p) for p in ps] for ps in pairs]
        atts = []
        for (d, r), gs in zip(jobs, grams):
            att = jnp.where(levels[d] == 1, gs[0], 0.0)
            for l in range(2, HG_LEVELS + 1):
                att = jnp.where(levels[d] == l, gs[l - 1], att)
            atts.append(att.astype(BF16))
        k_out = [(k[d][r] * jnp.exp2(s[HG_LEVELS * C:])).astype(BF16) for (d, r), s in zip(jobs, sums)]
        q_in = [(qg[d][r] * jnp.exp2(s[:C])).astype(BF16) for (d, r), s in zip(jobs, sums)]
        decays = [jnp.exp2(s[0:1] if d == 1 else s[C - 1:C]) for (d, r), s in zip(jobs, sums)]
        outs = [_dot(att, vb[d][r]) + diag[d][r] * v[d][r] for att, (d, r) in zip(atts, jobs)]
        kvs = [_dot_tn(vb[d][r], ko) for (d, r), ko in zip(jobs, k_out)]
        for d in range(2):
            st = st_ref[d, hh]
            for c in (range(n_ch - 1, -1, -1) if d == 1 else range(n_ch)):
                j = d * n_ch + c
                o_refs[d][chunks[c], cv] = outs[j] + _dot_nt(q_in[j], st.astype(BF16))
                st = st * decays[j] + kvs[j]
            st_ref[d, hh] = st

    gates(0, gate_a)

    def head_pair(i, carry):
        gates(2 * i + 1, gate_b)
        dots(2 * i, gate_a)
        dots(2 * i + 1, gate_b)
        gates(jnp.minimum(2 * i + 2, heads - 1), gate_a)
        return carry

    lax.fori_loop(0, heads // 2, head_pair, 0)

    @pl.when(t == n_blocks - 1)
    def _():
        for d in range(2):
            for hh in range(heads):
                sfin_ref[d, hh] = st_ref[d, hh].T


def _hg_call(proj, lb, s0, *, tok0, batch, seq, heads, dk, dv, col_q, col_f, col_i):
    tt = min(seq, 256)
    n = seq // tt
    kw, vw = heads * dk, heads * dv
    assert seq % tt == 0 and tok0 % tt == 0 and tt % HG_CHUNK == 0
    assert col_q % kw == 0 and col_f % kw == 0 and col_i % vw == 0
    r0 = tok0 // tt

    def fwd(b, t):
        return r0 + b * n + t

    def bwd(b, t):
        return r0 + b * n + (n - 1 - t)

    in_specs, args = [], []
    for d, rmap in enumerate((fwd, bwd)):
        in_specs += [
            pl.BlockSpec((tt, kw), lambda b, t, rmap=rmap: (rmap(b, t), col_q // kw)),
            pl.BlockSpec((tt, kw), lambda b, t, rmap=rmap, d=d: (rmap(b, t), col_f // kw + d)),
            pl.BlockSpec((tt, vw), lambda b, t, rmap=rmap: (rmap(b, t), col_i // vw)),
        ]
        args += [proj, proj, proj]
    in_specs.append(pl.BlockSpec((2, 1, kw), lambda b, t: (0, 0, 0)))
    args.append(lb.reshape(2, 1, kw))
    state_spec = pl.BlockSpec((None, 2, heads, dk, dv), lambda b, t: (b, 0, 0, 0, 0))
    if s0 is not None:
        in_specs.append(state_spec)
        args.append(s0)
    n_tok = batch * seq
    return pl.pallas_call(
        functools.partial(_hg_kernel, heads=heads, dk=dk, dv=dv, tt=tt, n_blocks=n, has_init=s0 is not None),
        grid=(batch, n),
        in_specs=in_specs,
        out_specs=[
            pl.BlockSpec((tt, vw), lambda b, t: (b * n + t, 0)),
            pl.BlockSpec((tt, vw), lambda b, t: (b * n + (n - 1 - t), 0)),
            state_spec,
        ],
        out_shape=[
            jax.ShapeDtypeStruct((n_tok, vw), F32),
            jax.ShapeDtypeStruct((n_tok, vw), F32),
            jax.ShapeDtypeStruct((batch, 2, heads, dk, dv), F32),
        ],
        scratch_shapes=[pltpu.VMEM((2, heads, dv, dk), F32),
                        pltpu.VMEM((2, (HG_LEVELS + 1) * HG_CHUNK, 2 * HG_CHUNK), BF16)]
        + 2 * ([pltpu.VMEM((2, tt, dk), F32)] * 4 + [pltpu.VMEM((2, tt, dk), BF16)] * 2),
        compiler_params=_cparams("arbitrary", "arbitrary"),
        name="hgrn2_scan",
    )(*args)


def _mix_out_kernel(*refs, d, ret_heads, hg_heads, ctx_blocks):
    scans = [refs[4 * p:4 * p + 4] for p in range(2)]
    (rg_ref, go_ref, h_ref, x_ref, mod_ref, g2_ref, wr_ref, wh_ref, wm_ref, wo_ref, wrt_ref,
     x1_ref, h2_ref, lgt_ref, o_ref, og_ref) = refs[8:]

    for p, in_set in enumerate((pl.program_id(0) < ctx_blocks, pl.program_id(0) >= ctx_blocks)):
        @pl.when(in_set)
        def _(p=p):
            o_ref[...] = scans[p][0][...] + scans[p][1][...]
            og_ref[...] = scans[p][2][...] + scans[p][3][...]

    dvr = o_ref.shape[1] // ret_heads
    dvh = og_ref.shape[1] // hg_heads
    tm = o_ref.shape[0]
    halves = [slice(0, tm // 2), slice(tm // 2, tm)]
    ret_in, hg_in = [], []
    for r in halves:
        parts = []
        for hh in range(ret_heads):
            oh = o_ref[r, hh * dvr:(hh + 1) * dvr]
            ctr = oh - jnp.mean(oh, axis=-1, keepdims=True)
            parts.append(ctr * lax.rsqrt(jnp.mean(ctr * ctr, axis=-1, keepdims=True) + NORM_EPS))
        ret_in.append((jnp.concatenate(parts, axis=1) * _silu(rg_ref[r, :])).astype(BF16))
        parts = [_rms(og_ref[r, hh * dvh:(hh + 1) * dvh]) for hh in range(hg_heads)]
        hg_in.append((jnp.concatenate(parts, axis=1) * _silu(go_ref[r, :])).astype(BF16))
    ret_out = [_dot(a, wr_ref[...]) for a in ret_in]
    hg_out = [_dot(a, wh_ref[...]) for a in hg_in]
    gates = [jax.nn.sigmoid(_dot(h_ref[r, :], wm_ref[...])) for r in halves]
    merged = [(g[:, :d] * a + g[:, d:] * b).astype(BF16) for g, a, b in zip(gates, ret_out, hg_out)]
    mix = [_dot(m, wo_ref[...]) for m in merged]
    w_hi, w_lo = wrt_ref[0], wrt_ref[1]
    for r, m in zip(halves, mix):
        x1 = x_ref[r, :] + mod_ref[:, 2 * d:3 * d] * m
        x1_ref[r, :] = x1
        h2 = _rms(x1) * g2_ref[...]
        h2 = h2 * (1.0 + mod_ref[:, 4 * d:5 * d]) + mod_ref[:, 3 * d:4 * d]
        h2_ref[r, :] = h2.astype(BF16)
        h_hi = h2.astype(BF16)
        h_lo = (h2 - h_hi.astype(F32)).astype(BF16)
        lgt_ref[:, r] = _dot_nt(w_hi, h_hi) + (_dot_nt(w_hi, h_lo) + _dot_nt(w_lo, h_hi))


def _mix_out_call(scans_ctx, scans_lat, proj, h, x, mod_l, mod_row, g2, w_ret_o, w_hg_o, w_merge, w_out,
                  w_router_t, tm, *, ret_heads, hg_heads, col_rg, col_go):
    n, d = x.shape
    vr, vh = scans_ctx[0].shape[1], scans_ctx[2].shape[1]
    ne = w_router_t.shape[1]
    ctx_blocks = scans_ctx[0].shape[0] // tm
    assert col_rg % vr == 0 and col_go % vh == 0

    def row(w):
        return pl.BlockSpec((tm, w), lambda i: (i, 0))

    def full(a):
        return pl.BlockSpec(a.shape, lambda i: (0,) * a.ndim)

    ctx_row = lambda a: pl.BlockSpec((tm, a.shape[1]), lambda i: (jnp.minimum(i, ctx_blocks - 1), 0))
    lat_row = lambda a: pl.BlockSpec((tm, a.shape[1]), lambda i: (jnp.maximum(i - ctx_blocks, 0), 0))
    return pl.pallas_call(
        functools.partial(_mix_out_kernel, d=d, ret_heads=ret_heads, hg_heads=hg_heads, ctx_blocks=ctx_blocks),
        grid=(n // tm,),
        in_specs=[ctx_row(a) for a in scans_ctx] + [lat_row(a) for a in scans_lat] + [
            pl.BlockSpec((tm, vr), lambda i: (i, col_rg // vr)),
            pl.BlockSpec((tm, vh), lambda i: (i, col_go // vh)),
            row(d), row(d),
            pl.BlockSpec((None, 1, 6 * d), lambda i: (mod_row(i), 0, 0)),
            pl.BlockSpec((1, d), lambda i: (0, 0)),
            full(w_ret_o), full(w_hg_o), full(w_merge), full(w_out), full(w_router_t),
        ],
        out_specs=[row(d), row(d), pl.BlockSpec((ne, tm), lambda i: (0, i))],
        out_shape=[
            jax.ShapeDtypeStruct((n, d), F32),
            jax.ShapeDtypeStruct((n, d), BF16),
            jax.ShapeDtypeStruct((ne, n), F32),
        ],
        scratch_shapes=[pltpu.VMEM((tm, vr), F32), pltpu.VMEM((tm, vh), F32)],
        compiler_params=_cparams("arbitrary"),
        name="mixer_out",
    )(*scans_ctx, *scans_lat, proj, proj, h, x, mod_l, g2.reshape(1, d), w_ret_o, w_hg_o, w_merge, w_out,
      w_router_t)


def _route_kernel(lg_ref, aff_ref, sel_ref, posx_ref, flag_ref, *, sets, caps):
    for (lo, n), cap in zip(sets, caps):
        _route_set(lg_ref, aff_ref, sel_ref, posx_ref, flag_ref, lo=lo, n=n, cap=cap)


def _route_set(lg_ref, aff_ref, sel_ref, posx_ref, flag_ref, *, lo, n, cap):
    cols = slice(lo, lo + n)
    lg = lg_ref[:, cols]
    ne = lg.shape[0]
    ex = jnp.exp(lg - jnp.max(lg, axis=0, keepdims=True))
    aff = ex / jnp.sum(ex, axis=0, keepdims=True)
    aff_ref[:, cols] = aff
    bits = pltpu.bitcast(aff, I32)

    def search(i, found):
        cand = found | lax.shift_left(jnp.int32(1), 30 - i)
        cnt = jnp.sum(jnp.where(bits >= cand, 1.0, 0.0), axis=1, keepdims=True)
        return jnp.where(cnt >= cap, cand, found)

    thr = lax.fori_loop(0, 31, search, jnp.zeros((ne, 1), I32))
    gt = bits > thr
    eq = bits == thr
    need = cap - jnp.sum(jnp.where(gt, 1.0, 0.0), axis=1, keepdims=True)
    upper = jnp.where(lax.broadcasted_iota(I32, (LANE, LANE), 0) < lax.broadcasted_iota(I32, (LANE, LANE), 1),
                      1.0, 0.0).astype(BF16)

    def exclusive_count(store):
        def body(ci, carry):
            sl = pl.ds(pl.multiple_of(lo + ci * LANE, LANE), LANE)
            x = flag_ref[:, sl]
            store(sl, carry + _dot(x.astype(BF16), upper))
            return carry + jnp.sum(x, axis=1, keepdims=True)

        lax.fori_loop(0, n // LANE, body, jnp.zeros((ne, 1), F32))

    flag_ref[:, cols] = jnp.where(eq, 1.0, 0.0)

    def store_tie_rank(sl, rank):
        posx_ref[:, sl] = rank.astype(I32)

    exclusive_count(store_tie_rank)
    sel = gt | (eq & (posx_ref[:, cols].astype(F32) < need))
    sel_ref[:, cols] = jnp.where(sel, 1, 0).astype(I32)
    flag_ref[:, cols] = jnp.where(sel, 1.0, 0.0)

    def store_pos(sl, cnt):
        posx_ref[:, sl] = cnt.astype(I32)

    exclusive_count(store_pos)


def _route_call(logits_t, sets, caps):
    ne, n = logits_t.shape
    spec = pl.BlockSpec((ne, n), lambda: (0, 0))
    return pl.pallas_call(
        functools.partial(_route_kernel, sets=sets, caps=caps),
        in_specs=[spec],
        out_specs=[spec, spec, spec],
        out_shape=[
            jax.ShapeDtypeStruct((ne, n), F32),
            jax.ShapeDtypeStruct((ne, n), I32),
            jax.ShapeDtypeStruct((ne, n), I32),
        ],
        scratch_shapes=[pltpu.VMEM((ne, n), F32)],
        compiler_params=pltpu.CompilerParams(vmem_limit_bytes=VMEM_LIMIT),
        name="expert_choice_route",
    )(logits_t)


def _moe_round_masks(b, r, base_ref, sel_ref, posx_ref, ne, value_rows=None):
    n_tok = sel_ref.shape[1]
    row_iota = lax.broadcasted_iota(I32, (MOE_ROUND, n_tok), 0)
    pieces = []
    for e in range(ne):
        local = posx_ref[e:e + 1, :] - (base_ref[b * ne + e] + r * MOE_ROUND)
        hit = (sel_ref[e:e + 1, :] == 1) & (local == row_iota)
        value = 1.0 if value_rows is None else value_rows[e:e + 1, :]
        pieces.append(jnp.where(hit, value, 0.0).astype(BF16))
    return jnp.concatenate(pieces, axis=0)


def _dispatch_kernel(base_ref, start_ref, pc_ref, nr_ref, first_ref, h_ref, sel_ref, posx_ref, xe_zero_ref, xe_ref,
                     stage_ref, sem, *, ne):
    del xe_zero_ref
    b = pl.program_id(0)
    n_rounds = nr_ref[b]
    prev = jnp.maximum(b - 1, 0)
    prev_pending = (b > 0) & (nr_ref[prev] >= 1)

    def pieces(blk, r, fn):
        slot = (first_ref[blk] + r) % 2
        for e in range(ne):
            rem = pc_ref[blk * ne + e] - r * MOE_ROUND
            dst = start_ref[blk * ne + e] + r * MOE_ROUND
            for size, cond, off in (
                    (MOE_ROUND, rem >= MOE_ROUND, 0),
                    (32, (rem > 0) & (rem < MOE_ROUND) & ((rem & 32) != 0), 0),
                    (16, (rem > 0) & (rem < MOE_ROUND) & ((rem & 16) != 0), rem & 32)):
                @pl.when(cond)
                def _(size=size, off=off, e=e, dst=dst):
                    src_rows = pl.ds(pl.multiple_of(e * MOE_ROUND + off, BF16_SUBLANE), size)
                    dst_rows = pl.ds(pl.multiple_of(dst + off, BF16_SUBLANE), size)
                    fn(pltpu.make_async_copy(stage_ref.at[slot, src_rows], xe_ref.at[e, dst_rows], sem.at[slot]))

    def wait_previous_block():
        pieces(prev, nr_ref[prev] - 1, lambda cp: cp.wait())

    def round_body(r, carry):
        onehot = _moe_round_masks(b, r, base_ref, sel_ref, posx_ref, ne)
        stage_ref[(first_ref[b] + r) % 2] = _dot(onehot, h_ref[...]).astype(BF16)

        @pl.when(r >= 1)
        def _():
            pieces(b, r - 1, lambda cp: cp.wait())

        @pl.when((r == 0) & prev_pending)
        def _():
            wait_previous_block()

        pieces(b, r, lambda cp: cp.start())
        return carry

    lax.fori_loop(0, n_rounds, round_body, 0)

    @pl.when((n_rounds == 0) & prev_pending)
    def _():
        wait_previous_block()

    @pl.when((n_rounds >= 1) & (b == pl.num_programs(0) - 1))
    def _():
        pieces(b, n_rounds - 1, lambda cp: cp.wait())


def _dispatch_call(h2, sel, posx, tables, list_rows, *, sb):
    n, d = h2.shape
    ne = sel.shape[0]
    xe_zero = jnp.zeros((ne, list_rows, d), BF16)
    grid_spec = pltpu.PrefetchScalarGridSpec(
        num_scalar_prefetch=len(tables),
        grid=(n // sb,),
        in_specs=[
            pl.BlockSpec((sb, d), lambda b, *_: (b, 0)),
            pl.BlockSpec((ne, sb), lambda b, *_: (0, b)),
            pl.BlockSpec((ne, sb), lambda b, *_: (0, b)),
            pl.BlockSpec(memory_space=pl.ANY),
        ],
        out_specs=pl.BlockSpec(memory_space=pl.ANY),
        scratch_shapes=[pltpu.VMEM((2, ne * MOE_ROUND, d), BF16), pltpu.SemaphoreType.DMA((2,))],
    )
    return pl.pallas_call(
        functools.partial(_dispatch_kernel, ne=ne),
        grid_spec=grid_spec,
        out_shape=jax.ShapeDtypeStruct((ne, list_rows, d), BF16),
        input_output_aliases={len(tables) + 3: 0},
        compiler_params=_cparams("arbitrary"),
        name="expert_dispatch",
    )(*tables, h2, sel, posx, xe_zero)


def _expert_ffn_kernel(tot_ref, x_ref, wg_ref, wu_ref, wd_ref, y_ref, wgb_ref, wub_ref, wdb_ref, acc_ref,
                       *, rt, n_ff):
    e, f, t = pl.program_id(0), pl.program_id(1), pl.program_id(2)

    @pl.when(t == 0)
    def _():
        wgb_ref[...] = wg_ref[...].astype(BF16)
        wub_ref[...] = wu_ref[...].astype(BF16)
        wdb_ref[...] = wd_ref[...].astype(BF16)

    rows = pl.ds(pl.multiple_of(t * rt, rt), rt)
    live = t * rt < tot_ref[e]

    @pl.when(live)
    def _():
        x = x_ref[...]
        hid = (_silu(_dot(x, wgb_ref[...])) * _dot(x, wub_ref[...])).astype(BF16)
        y = _dot(hid, wdb_ref[...])

        @pl.when(f == 0)
        def _():
            acc_ref[rows, :] = y

        @pl.when(f != 0)
        def _():
            acc_ref[rows, :] += y

    @pl.when((f == n_ff - 1) & live)
    def _():
        y_ref[...] = acc_ref[rows, :].astype(BF16)

    @pl.when((f == n_ff - 1) & jnp.logical_not(live))
    def _():
        y_ref[...] = jnp.zeros_like(y_ref)


def _expert_ffn_call(xe, totals, wg, wu, wd, layer, *, rt, fc):
    ne, list_rows, d = xe.shape
    ff = wg.shape[-1]
    n_ff = ff // fc
    grid_spec = pltpu.PrefetchScalarGridSpec(
        num_scalar_prefetch=1,
        grid=(ne, n_ff, list_rows // rt),
        in_specs=[
            pl.BlockSpec((None, rt, d), lambda e, f, t, *_: (e, t, 0)),
            pl.BlockSpec((None, None, d, fc), lambda e, f, t, *_: (layer, e, 0, f)),
            pl.BlockSpec((None, None, d, fc), lambda e, f, t, *_: (layer, e, 0, f)),
            pl.BlockSpec((None, None, fc, d), lambda e, f, t, *_: (layer, e, f, 0)),
        ],
        out_specs=pl.BlockSpec((None, rt, d), lambda e, f, t, *_: (e, jnp.where(f == n_ff - 1, t, 0), 0)),
        scratch_shapes=[pltpu.VMEM((d, fc), BF16), pltpu.VMEM((d, fc), BF16), pltpu.VMEM((fc, d), BF16),
                        pltpu.VMEM((list_rows, d), F32)],
    )
    return pl.pallas_call(
        functools.partial(_expert_ffn_kernel, rt=rt, n_ff=n_ff),
        grid_spec=grid_spec,
        out_shape=jax.ShapeDtypeStruct((ne, list_rows, d), BF16),
        compiler_params=_cparams("arbitrary", "arbitrary", "arbitrary"),
        name="expert_ffn",
    )(totals, xe, wg, wu, wd)


def _combine_kernel(base_ref, start_ref, pc_ref, nr_ref, first_ref, sel_ref, posx_ref, aff_ref, y_ref, out_ref,
                    ybuf_ref, sem, *, ne):
    b = pl.program_id(0)
    last = pl.num_programs(0) - 1
    n_rounds = nr_ref[b]
    first = first_ref[b]

    def windows(blk, r, fn):
        slot = (first_ref[blk] + r) % 2
        for e in range(ne):
            @pl.when(pc_ref[blk * ne + e] - r * MOE_ROUND > 0)
            def _(e=e):
                src = pl.ds(pl.multiple_of(start_ref[blk * ne + e] + r * MOE_ROUND, BF16_SUBLANE), MOE_ROUND)
                fn(pltpu.make_async_copy(y_ref.at[e, src], ybuf_ref.at[slot, pl.ds(e * MOE_ROUND, MOE_ROUND)],
                                         sem.at[slot]))

    def start_next_block():
        @pl.when(b < last)
        def _():
            windows(jnp.minimum(b + 1, last), 0, lambda cp: cp.start())

    @pl.when(b == 0)
    def _():
        ybuf_ref[...] = jnp.zeros_like(ybuf_ref)
        windows(0, 0, lambda cp: cp.start())

    out_ref[...] = jnp.zeros_like(out_ref)

    @pl.when(n_rounds == 0)
    def _():
        start_next_block()

    def round_body(r, carry):
        @pl.when(r + 1 < n_rounds)
        def _():
            windows(b, r + 1, lambda cp: cp.start())

        @pl.when(r + 1 == n_rounds)
        def _():
            start_next_block()

        windows(b, r, lambda cp: cp.wait())
        weights = _moe_round_masks(b, r, base_ref, sel_ref, posx_ref, ne, value_rows=aff_ref)
        out_ref[...] += _dot_tn(weights, ybuf_ref[(first + r) % 2])
        return carry

    lax.fori_loop(0, n_rounds, round_body, 0)


def _combine_call(ye, sel, posx, aff, tables, *, sb):
    ne, _, d = ye.shape
    n = sel.shape[1]
    grid_spec = pltpu.PrefetchScalarGridSpec(
        num_scalar_prefetch=len(tables),
        grid=(n // sb,),
        in_specs=[
            pl.BlockSpec((ne, sb), lambda b, *_: (0, b)),
            pl.BlockSpec((ne, sb), lambda b, *_: (0, b)),
            pl.BlockSpec((ne, sb), lambda b, *_: (0, b)),
            pl.BlockSpec(memory_space=pl.ANY),
        ],
        out_specs=pl.BlockSpec((sb, d), lambda b, *_: (b, 0)),
        scratch_shapes=[pltpu.VMEM((2, ne * MOE_ROUND, d), BF16), pltpu.SemaphoreType.DMA((2,))],
    )
    return pl.pallas_call(
        functools.partial(_combine_kernel, ne=ne),
        grid_spec=grid_spec,
        out_shape=jax.ShapeDtypeStruct((n, d), F32),
        compiler_params=_cparams("arbitrary"),
        name="expert_combine",
    )(*tables, sel, posx, aff, ye)


def _moe_tables(posx, sets, caps, sb, sbc):
    ne = posx.shape[0]
    fine = [jnp.concatenate([posx[:, lo:lo + n:sbc], jnp.full((ne, 1), cap, I32)], axis=1)
            for (lo, n), cap in zip(sets, caps)]
    step = sb // sbc
    base = jnp.concatenate([e[:, :-1:step].T for e in fine], axis=0)
    cnt = jnp.concatenate([(e[:, step::step] - e[:, :-1:step]).T for e in fine], axis=0)
    pc = (cnt + (BF16_SUBLANE - 1)) // BF16_SUBLANE * BF16_SUBLANE
    ends = jnp.cumsum(pc, axis=0)
    start = ends - pc
    rounds = lambda rows: jnp.max((rows + (MOE_ROUND - 1)) // MOE_ROUND, axis=1)
    flat = lambda *arrays: tuple(a.reshape(-1).astype(I32) for a in arrays)
    rounds_c = rounds(pc)
    dispatch = flat(base, start, pc, rounds_c, jnp.cumsum(rounds_c) - rounds_c)

    base_f = jnp.concatenate([e[:, :-1].T for e in fine], axis=0)
    cnt_f = jnp.concatenate([(e[:, 1:] - e[:, :-1]).T for e in fine], axis=0)
    first = jnp.repeat(start - base, step, axis=0) + base_f
    off = first % BF16_SUBLANE
    rows_f = jnp.where(cnt_f > 0, off + cnt_f, 0)
    rounds_f = rounds(rows_f)
    combine = flat(base_f - off, first - off, rows_f, rounds_f, jnp.cumsum(rounds_f) - rounds_f)
    return dispatch, combine, ends[-1].astype(I32)


def _rope_tables(n_tokens, dk):
    rows = n_tokens // GRID_W
    r, c = jnp.meshgrid(jnp.arange(rows), jnp.arange(GRID_W), indexing="ij")
    pos = jnp.stack([r.reshape(-1), c.reshape(-1)], axis=-1).astype(F32)
    nf = dk // 4
    inv_freq = ROPE_BASE ** (-jnp.arange(nf, dtype=F32) / nf)
    ang = pos[:, :, None] * inv_freq
    cos, sin = jnp.cos(ang), jnp.sin(ang)
    cos_t = jnp.concatenate([cos[:, 0], cos[:, 0], cos[:, 1], cos[:, 1]], axis=-1)
    sin_t = jnp.concatenate([-sin[:, 0], sin[:, 0], -sin[:, 1], sin[:, 1]], axis=-1)
    return cos_t, sin_t


def kernel(x_prompt, x_sample, state_ret, state_hgrn, c, c_ctx, ada_w, ada_b, norm_mix_g, norm_ffn_g, w_in,
           ret_gamma_logit, hg_lb_logit, w_ret_o, w_hg_o, w_merge, w_out, w_router, w_exp_gate, w_exp_up,
           w_exp_down, final_g):
    b_ctx, t_ctx, d = x_prompt.shape
    b_lat, t_lat, _ = x_sample.shape
    depth = w_in.shape[0]
    ret_heads, ret_dk, ret_dv = state_ret.shape[3:]
    hg_heads, hg_dk, hg_dv = state_hgrn.shape[3:]
    ne = w_router.shape[-1]
    n_ctx, n_lat = b_ctx * t_ctx, b_lat * t_lat
    n_tok = n_ctx + n_lat
    qw, vw = ret_heads * ret_dk, ret_heads * ret_dv
    kw, hw = hg_heads * hg_dk, hg_heads * hg_dv
    col_rg = 2 * qw + vw
    col_gq = col_rg + vw
    col_gf = col_gq + kw
    col_gi = col_gf + 2 * kw
    col_go = col_gi + hw
    assert col_go + hw == w_in.shape[-1] and b_lat < MOD_ROWS

    tm = 256
    assert t_ctx % tm == 0 and t_lat % tm == 0
    tm_proj = 1024 if n_tok % 1024 == 0 else tm
    sb = 1024 if (n_ctx % 4096 == 0 and n_lat % 4096 == 0) else 256
    assert n_ctx % sb == 0 and n_lat % sb == 0
    sets = ((0, n_ctx), (n_ctx, n_lat))
    caps = [CAPACITY_FACTOR * n_set // ne for _, n_set in sets]
    sbc = 256
    list_rows = sum(caps) + BF16_SUBLANE * (n_tok // sb) + MOE_ROUND
    n_tiles = max(1, list_rows // 512)
    rt = -(-list_rows // (n_tiles * BF16_SUBLANE)) * BF16_SUBLANE
    list_rows = n_tiles * rt

    def mod_rows(tmx):
        return lambda i: jnp.where(i < n_ctx // tmx, 0, 1 + (i - n_ctx // tmx) // (t_lat // tmx))

    mod_row = mod_rows(tm)
    tm_norm = 512 if (n_ctx % 512 == 0 and t_lat % 512 == 0) else tm
    norm_row = mod_rows(tm_norm)

    x = jnp.concatenate([x_prompt.reshape(n_ctx, d), x_sample.reshape(n_lat, d)], axis=0)
    cvec = jnp.zeros((MOD_ROWS, d), F32).at[0].set(c_ctx).at[1:1 + b_lat].set(c)
    mod = _mod_call(cvec, ada_w, ada_b).reshape(depth, MOD_ROWS, 1, 6 * d)

    log_gamma = jax.nn.log_sigmoid(ret_gamma_logit.astype(F32))
    p_lb = jax.nn.softmax(hg_lb_logit.astype(F32), axis=0)
    hg_lb = jnp.clip(jnp.cumsum(p_lb, axis=0) - p_lb[0:1], 0.0, 1.0 - 1e-6)
    rope = _rope_tables(t_lat, ret_dk)

    w_ret_o_b, w_hg_o_b, w_out_b = w_ret_o.astype(BF16), w_hg_o.astype(BF16), w_out.astype(BF16)
    w_merge_b = w_merge.astype(BF16)
    w_router_t = jnp.swapaxes(w_router, 1, 2)
    w_router_hi = w_router_t.astype(BF16)
    w_router_t = jnp.stack([w_router_hi, (w_router_t - w_router_hi.astype(F32)).astype(BF16)], axis=1)
    fc = min(1024, w_exp_gate.shape[-1])
    tn = min(2048, w_in.shape[-1])

    _, h = _norm_call(x, None, None, mod[0], norm_mix_g[0], norm_row, tm_norm)
    new_ret, new_hg = [], []
    for l in range(depth):
        proj = _proj_call(h, w_in, l, tm_proj, tn)

        ret_kw = dict(heads=ret_heads, dk=ret_dk, dv=ret_dv)
        orf_c, orb_c, s_ret = _ret_call(proj, log_gamma[l], None, None, tok0=0, batch=b_ctx, seq=t_ctx, **ret_kw)
        orf_l, orb_l, _ = _ret_call(proj, log_gamma[l], state_ret[:, l], rope, tok0=n_ctx, batch=b_lat, seq=t_lat,
                                    **ret_kw)
        hg_kw = dict(heads=hg_heads, dk=hg_dk, dv=hg_dv, col_q=col_gq, col_f=col_gf, col_i=col_gi)
        ogf_c, ogb_c, s_hg = _hg_call(proj, hg_lb[l], None, tok0=0, batch=b_ctx, seq=t_ctx, **hg_kw)
        ogf_l, ogb_l, _ = _hg_call(proj, hg_lb[l], state_hgrn[:, l], tok0=n_ctx, batch=b_lat, seq=t_lat, **hg_kw)
        new_ret.append(s_ret)
        new_hg.append(s_hg)

        x, h2, logits_t = _mix_out_call(
            (orf_c, orb_c, ogf_c, ogb_c), (orf_l, orb_l, ogf_l, ogb_l), proj, h, x, mod[l],
            mod_row, norm_ffn_g[l], w_ret_o_b[l], w_hg_o_b[l], w_merge_b[l], w_out_b[l], w_router_t[l], tm,
            ret_heads=ret_heads, hg_heads=hg_heads, col_rg=col_rg, col_go=col_go)

        aff, sel, posx = _route_call(logits_t, sets, caps)
        dispatch_tables, combine_tables, totals = _moe_tables(posx, sets, caps, sb, sbc)
        xe = _dispatch_call(h2, sel, posx, dispatch_tables, list_rows, sb=sb)
        ye = _expert_ffn_call(xe, totals, w_exp_gate, w_exp_up, w_exp_down, l, rt=rt, fc=fc)
        ffn = _combine_call(ye, sel, posx, aff, combine_tables, sb=sbc)

        if l + 1 < depth:
            x, h = _norm_call(x, ffn, mod[l], mod[l + 1], norm_mix_g[l + 1], norm_row, tm_norm)
        else:
            _, y_prompt = _norm_call(x, ffn, mod[l], None, final_g, norm_row, tm_norm, rows=(0, n_ctx))
            _, y_sample = _norm_call(x, ffn, mod[l], None, final_g, norm_row, tm_norm, rows=(n_ctx, n_lat))

    y_prompt = y_prompt.reshape(b_ctx, t_ctx, d)
    y_sample = y_sample.reshape(b_lat, t_lat, d)
    return y_prompt, y_sample, jnp.stack(new_ret, axis=1), jnp.stack(new_hg, axis=1)
```

```python
import functools

import jax
import jax.numpy as jnp
from jax import lax
from jax.experimental import pallas as pl
from jax.experimental.pallas import tpu as pltpu

F32 = jnp.float32
BF16 = jnp.bfloat16
I32 = jnp.int32
HIGHEST = lax.Precision.HIGHEST

NORM_EPS = 1e-6
LOG2_E = 1.4426950408889634
ROPE_BASE = 10000.0
GRID_W = 64
CAPACITY_FACTOR = 2
RET_CHUNK = 128
RET_CHUNKS_PER_STEP = 2
HG_LEVELS = 6
HG_CHUNK = 1 << HG_LEVELS
HG_LOG_SIGMOID_LINEAR = -60.0
MOD_ROWS = 8
LANE = 128
F32_SUBLANE = 8
BF16_SUBLANE = 16
MOE_ROUND = 64
VMEM_LIMIT = 62 * 1024 * 1024


def _cparams(*sem):
    return pltpu.CompilerParams(dimension_semantics=sem, vmem_limit_bytes=VMEM_LIMIT)


def _dot(a, b, **kw):
    return jnp.dot(a, b, preferred_element_type=F32, **kw)


def _dot_nt(a, b, **kw):
    return lax.dot_general(a, b, (((1,), (1,)), ((), ())), preferred_element_type=F32, **kw)


def _dot_tn(a, b, **kw):
    return lax.dot_general(a, b, (((0,), (0,)), ((), ())), preferred_element_type=F32, **kw)


def _silu(x):
    return x * jax.nn.sigmoid(x)


def _mod_kernel(c_ref, w_ref, b_ref, o_ref):
    s = _silu(c_ref[...])
    o_ref[...] = _dot(s, w_ref[...], precision=HIGHEST) + b_ref[...]


def _mod_call(cvec, ada_w, ada_b):
    depth, d, six_d = ada_w.shape
    tn = 6 * LANE * 2
    assert six_d % tn == 0
    return pl.pallas_call(
        _mod_kernel,
        grid=(depth, six_d // tn),
        in_specs=[
            pl.BlockSpec((MOD_ROWS, d), lambda l, j: (0, 0)),
            pl.BlockSpec((None, d, tn), lambda l, j: (l, 0, j)),
            pl.BlockSpec((None, 1, tn), lambda l, j: (l, 0, j)),
        ],
        out_specs=pl.BlockSpec((None, MOD_ROWS, tn), lambda l, j: (l, 0, j)),
        out_shape=jax.ShapeDtypeStruct((depth, MOD_ROWS, six_d), F32),
        compiler_params=_cparams("arbitrary", "arbitrary"),
        name="adaln_mod",
    )(cvec, ada_w, ada_b.reshape(depth, 1, six_d))


def _rms(x):
    return x * lax.rsqrt(jnp.mean(x * x, axis=-1, keepdims=True) + NORM_EPS)


def _norm_kernel(*refs, d, residual, final):
    it = iter(refs)
    x_ref = next(it)
    f_ref, gate_ref = (next(it), next(it)) if residual else (None, None)
    mod_ref = None if final else next(it)
    g_ref = next(it)
    xo_ref = next(it) if (residual and not final) else None
    h_ref = next(it)
    x = x_ref[...]
    if residual:
        x = x + gate_ref[...] * f_ref[...]
        if xo_ref is not None:
            xo_ref[...] = x
    y = _rms(x) * g_ref[...]
    if not final:
        y = y * (1.0 + mod_ref[:, d:2 * d]) + mod_ref[:, 0:d]
    h_ref[...] = y.astype(h_ref.dtype)


def _norm_call(x, ffn, mod_gate, mod_next, g, mod_row, tm, rows=None):
    n, d = x.shape
    residual = ffn is not None
    final = mod_next is None
    r0, n_out = (0, n) if rows is None else rows
    blk0 = r0 // tm
    row = pl.BlockSpec((tm, d), lambda i: (blk0 + i, 0))
    in_specs, args = [row], [x]
    if residual:
        in_specs += [row, pl.BlockSpec((None, 1, d), lambda i: (mod_row(blk0 + i), 0, 5))]
        args += [ffn, mod_gate]
    if not final:
        in_specs.append(pl.BlockSpec((None, 1, 2 * d), lambda i: (mod_row(blk0 + i), 0, 0)))
        args.append(mod_next)
    in_specs.append(pl.BlockSpec((1, d), lambda i: (0, 0)))
    args.append(g.reshape(1, d))
    out_row = pl.BlockSpec((tm, d), lambda i: (i, 0))
    out_specs, out_shape = [out_row], [jax.ShapeDtypeStruct((n_out, d), F32 if final else BF16)]
    if residual and not final:
        out_specs = [out_row, out_row]
        out_shape = [jax.ShapeDtypeStruct((n_out, d), F32)] + out_shape
    outs = pl.pallas_call(
        functools.partial(_norm_kernel, d=d, residual=residual, final=final),
        grid=(n_out // tm,),
        in_specs=in_specs,
        out_specs=out_specs,
        out_shape=out_shape,
        compiler_params=_cparams("arbitrary"),
        name="residual_norm",
    )(*args)
    return outs if len(outs) == 2 else (x, outs[0])


def _proj_kernel(h_ref, w_ref, o_ref, wb_ref):
    @pl.when(pl.program_id(1) == 0)
    def _():
        wb_ref[...] = w_ref[...].astype(BF16)

    o_ref[...] = _dot(h_ref[...], wb_ref[...])


def _proj_call(h, w, layer, tm, tn):
    n, d = h.shape
    cols = w.shape[-1]
    return pl.pallas_call(
        _proj_kernel,
        grid=(cols // tn, n // tm),
        in_specs=[
            pl.BlockSpec((tm, d), lambda j, i: (i, 0)),
            pl.BlockSpec((None, d, tn), lambda j, i: (layer, 0, j)),
        ],
        out_specs=pl.BlockSpec((tm, tn), lambda j, i: (i, j)),
        out_shape=jax.ShapeDtypeStruct((n, cols), F32),
        scratch_shapes=[pltpu.VMEM((d, tn), BF16)],
        compiler_params=_cparams("arbitrary", "arbitrary"),
        name="in_proj",
    )(h, w)


def _swap_half_pairs(x):
    lane = lax.broadcasted_iota(I32, x.shape, 1)
    return jnp.where((lane & 32) == 0, pltpu.roll(x, LANE - 32, 1), pltpu.roll(x, 32, 1))


def _ret_kernel(*refs, heads, dk, dv, chunk, n_chunks, has_init, has_rope):
    it = iter(refs)
    lg_ref = next(it)
    qkv = [[next(it) for _ in range(3)] for _ in range(2)]
    s0_ref = next(it) if has_init else None
    rope = [[next(it) for _ in range(2)] for _ in range(2)] if has_rope else None
    o_refs = [next(it), next(it)]
    sfin_ref = next(it)
    s_ref = next(it)

    c = pl.program_id(1)

    @pl.when(c == 0)
    def _():
        if has_init:
            s_ref[...] = s0_ref[...]
        else:
            s_ref[...] = jnp.zeros_like(s_ref)

    L = chunk
    ii = lax.broadcasted_iota(I32, (L, L), 0)
    jj = lax.broadcasted_iota(I32, (L, L), 1)
    pcol = lax.broadcasted_iota(I32, (L, 1), 0).astype(F32)
    n_sub = qkv[0][0].shape[0] // L
    jobs = [(d, hh, r) for d in range(2) for hh in range(heads) for r in range(n_sub)]
    qs, ks, vs, scores, q_in, k_out = [], [], [], [], [], []
    for d, hh, r in jobs:
        q_ref, k_ref, v_ref = qkv[d]
        rows = slice(r * L, (r + 1) * L)
        q = q_ref[rows, hh * dk:(hh + 1) * dk]
        k = k_ref[rows, hh * dk:(hh + 1) * dk] * (dk ** -0.5)
        if has_rope:
            cos, sin = rope[d][0][rows, :], rope[d][1][rows, :]
            q = q * cos + _swap_half_pairs(q) * sin
            k = k * cos + _swap_half_pairs(k) * sin
        qs.append(q)
        ks.append(k)
        vs.append(v_ref[rows, hh * dv:(hh + 1) * dv].astype(BF16))
    for (d, hh, r), q, k in zip(jobs, qs, ks):
        dif = ((ii - jj) if d == 0 else (jj - ii)).astype(F32)
        lg = lg_ref[d, hh]
        decay = jnp.where(dif >= 0.0, jnp.exp(lg * jnp.maximum(dif, 0.0)), 0.0)
        scores.append((_dot_nt(q.astype(BF16), k.astype(BF16)) * decay).astype(BF16))
    for (d, hh, r), q, k in zip(jobs, qs, ks):
        lg = lg_ref[d, hh]
        q_pow, k_pow = (pcol + 1.0, (L - 1.0) - pcol) if d == 0 else (L - pcol, pcol)
        q_in.append((q * jnp.exp(lg * q_pow)).astype(BF16))
        k_out.append((k * jnp.exp(lg * k_pow)).T.astype(BF16))
    intra = [_dot(sc, v) for sc, v in zip(scores, vs)]
    kvs = [_dot(ko, v) for ko, v in zip(k_out, vs)]
    for d in range(2):
        for hh in range(heads):
            s = s_ref[d, hh]
            chunk_decay = jnp.exp(jnp.full((1, 1), lg_ref[d, hh] * L, F32))
            for r in (range(n_sub) if d == 0 else range(n_sub - 1, -1, -1)):
                j = (d * heads + hh) * n_sub + r
                o_refs[d][r * L:(r + 1) * L, hh * dv:(hh + 1) * dv] = intra[j] + _dot(q_in[j], s.astype(BF16))
                s = s * chunk_decay + kvs[j]
            s_ref[d, hh] = s

    @pl.when(c == n_chunks - 1)
    def _():
        sfin_ref[...] = s_ref[...]


def _ret_call(proj, log_gamma, s0, rope, *, tok0, batch, seq, heads, dk, dv):
    rb = RET_CHUNK * RET_CHUNKS_PER_STEP
    n = seq // rb
    qw, vw = heads * dk, heads * dv
    assert seq % rb == 0 and tok0 % rb == 0 and vw % qw == 0
    r0 = tok0 // rb

    def fwd(b, c):
        return r0 + b * n + c

    def bwd(b, c):
        return r0 + b * n + (n - 1 - c)

    in_specs = [pl.BlockSpec(memory_space=pltpu.SMEM)]
    args = [log_gamma]
    for rmap in (fwd, bwd):
        in_specs += [
            pl.BlockSpec((rb, qw), lambda b, c, rmap=rmap: (rmap(b, c), 0)),
            pl.BlockSpec((rb, qw), lambda b, c, rmap=rmap: (rmap(b, c), 1)),
            pl.BlockSpec((rb, vw), lambda b, c, rmap=rmap: (rmap(b, c), (2 * qw) // vw)),
        ]
        args += [proj, proj, proj]
    state_spec = pl.BlockSpec((None, 2, heads, dk, dv), lambda b, c: (b, 0, 0, 0, 0))
    if s0 is not None:
        in_specs.append(state_spec)
        args.append(s0)
    if rope is not None:
        for cmap in (lambda b, c: (c, 0), lambda b, c: (n - 1 - c, 0)):
            in_specs += [pl.BlockSpec((rb, dk), cmap), pl.BlockSpec((rb, dk), cmap)]
            args += [rope[0], rope[1]]
    n_tok = batch * seq
    return pl.pallas_call(
        functools.partial(_ret_kernel, heads=heads, dk=dk, dv=dv, chunk=RET_CHUNK, n_chunks=n,
                          has_init=s0 is not None, has_rope=rope is not None),
        grid=(batch, n),
        in_specs=in_specs,
        out_specs=[
            pl.BlockSpec((rb, vw), lambda b, c: (b * n + c, 0)),
            pl.BlockSpec((rb, vw), lambda b, c: (b * n + (n - 1 - c), 0)),
            state_spec,
        ],
        out_shape=[
            jax.ShapeDtypeStruct((n_tok, vw), F32),
            jax.ShapeDtypeStruct((n_tok, vw), F32),
            jax.ShapeDtypeStruct((batch, 2, heads, dk, dv), F32),
        ],
        scratch_shapes=[pltpu.VMEM((2, heads, dk, dv), F32)],
        compiler_params=_cparams("arbitrary", "arbitrary"),
        name="retention_scan",
    )(*args)


def _hg_scan_index(shape, axis, rev):
    i = lax.broadcasted_iota(I32, shape, axis) & (HG_CHUNK - 1)
    return (HG_CHUNK - 1 - i) if rev else i


def _hg_window_matrix(rev):
    C = HG_CHUNK
    ti = _hg_scan_index((C, 2 * C), 0, rev)
    tj = _hg_scan_index((C, 2 * C), 1, rev)
    blocks = [tj <= ti]
    for l in range(2, HG_LEVELS + 1):
        anchor = ((ti >> l) << l) + (1 << (l - 1)) - 1
        upper = ((ti >> (l - 1)) & 1) == 1
        blocks.append((upper & (tj > anchor) & (tj <= ti)) | (~upper & (tj > ti) & (tj <= anchor)))
    blocks.append(tj > ti)
    return jnp.concatenate([jnp.where(m, 1.0, 0.0) for m in blocks], axis=0).astype(BF16)


def _hg_kernel(*refs, heads, dk, dv, tt, n_blocks, has_init):
    it = iter(refs)
    qzv = [[next(it) for _ in range(3)] for _ in range(2)]
    lb_ref = next(it)
    s0_ref = next(it) if has_init else None
    o_refs = [next(it), next(it)]
    sfin_ref = next(it)
    st_ref = next(it)
    win_ref = next(it)
    gate_a = [next(it) for _ in range(6)]
    gate_b = [next(it) for _ in range(6)]

    t = pl.program_id(1)

    @pl.when((pl.program_id(0) == 0) & (t == 0))
    def _():
        for d in range(2):
            win_ref[d] = _hg_window_matrix(d == 1)

    @pl.when(t == 0)
    def _():
        for d in range(2):
            for hh in range(heads):
                st_ref[d, hh] = s0_ref[d, hh].T if has_init else jnp.zeros((dv, dk), F32)

    C = HG_CHUNK
    n_ch = tt // C
    chunks = [slice(c * C, (c + 1) * C) for c in range(n_ch)]
    levels, uppers = [], []
    for d in range(2):
        ti = _hg_scan_index((C, C), 0, d == 1)
        tj = _hg_scan_index((C, C), 1, d == 1)
        level = jnp.where(tj < ti, 1, 0)
        for l in range(1, HG_LEVELS):
            level = level + jnp.where((tj < ti) & ((ti >> l) != (tj >> l)), 1, 0)
        t_col = _hg_scan_index((C, 1), 0, d == 1)
        levels.append(level)
        uppers.append([None] + [((t_col >> (l - 1)) & 1) == 1 for l in range(1, HG_LEVELS + 1)])

    def gates(hh, g_refs):
        qg_ref, k_ref, f_ref, diag_ref, hi_ref, lo_ref = g_refs
        ck = pl.ds(pl.multiple_of(hh * dk, LANE), dk)
        for d in range(2):
            q_ref, z_ref, _ = qzv[d]
            lb = lb_ref[d, :, ck]
            z = z_ref[:, ck]
            e = jnp.exp(-jnp.abs(z))
            inv = 1.0 / (1.0 + e)
            pos = z > 0.0
            f = lb + (1.0 - lb) * (jnp.where(pos, 1.0, e) * inv)
            k = (1.0 - lb) * (jnp.where(pos, e, 1.0) * inv)
            tiny = (lb <= 0.0) & (z < HG_LOG_SIGMOID_LINEAR)
            log2_f = jnp.where(tiny, z * LOG2_E, jnp.log2(f))
            hi = log2_f.astype(BF16)
            qg = _silu(q_ref[:, ck]) * (dk ** -0.5)
            qg_ref[d], k_ref[d], f_ref[d] = qg, k, f
            hi_ref[d] = hi
            lo_ref[d] = (log2_f - hi.astype(F32)).astype(BF16)
            diag_ref[d] = jnp.broadcast_to(jnp.sum(qg * k, axis=-1, keepdims=True), qg.shape)

    def dots(hh, g_refs):
        qg_ref, k_ref, f_ref, diag_ref, hi_ref, lo_ref = g_refs
        cv = pl.ds(pl.multiple_of(hh * dv, LANE), dv)
        jobs = [(d, r) for d in range(2) for r in chunks]
        qg, k, f, diag = ([ref[d] for d in range(2)] for ref in (qg_ref, k_ref, f_ref, diag_ref))
        v = [qzv[d][2][:, cv] for d in range(2)]
        vb = [a.astype(BF16) for a in v]
        wins = [win_ref[0], win_ref[1]]
        sums = [_dot(wins[d], jnp.concatenate([hi_ref[d, r, :], lo_ref[d, r, :]], axis=0)) for d, r in jobs]

        def query_or_key(d, r, l):
            half = 1 << (l - 1)
            if half % F32_SUBLANE:
                return jnp.where(uppers[d][l], qg[d][r], k[d][r])
            segs = []
            for s in range(C // half):
                src = qg[d] if (s % 2 == 1) != (d == 1) else k[d]
                segs.append(src[r.start + s * half:r.start + (s + 1) * half])
            return jnp.concatenate(segs, axis=0)

        pairs = []
        for (d, r), s in zip(jobs, sums):
            ps = [jnp.where(uppers[d][1], qg[d][r] * f[d][r], k[d][r]).astype(BF16)]
            for l in range(2, HG_LEVELS + 1):
                scale = jnp.exp2(s[(l - 1) * C:l * C])
                ps.append((query_or_key(d, r, l) * scale).astype(BF16))
            pairs.append(ps)
        grams = [[_dot_nt(p, p) for p in ps] for ps in pairs]
        atts = []
        for (d, r), gs in zip(jobs, grams):
            att = jnp.where(levels[d] == 1, gs[0], 0.0)
            for l in range(2, HG_LEVELS + 1):
                att = jnp.where(levels[d] == l, gs[l - 1], att)
            atts.append(att.astype(BF16))
        k_out = [(k[d][r] * jnp.exp2(s[HG_LEVELS * C:])).astype(BF16) for (d, r), s in zip(jobs, sums)]
        q_in = [(qg[d][r] * jnp.exp2(s[:C])).astype(BF16) for (d, r), s in zip(jobs, sums)]
        decays = [jnp.exp2(s[0:1] if d == 1 else s[C - 1:C]) for (d, r), s in zip(jobs, sums)]
        outs = [_dot(att, vb[d][r]) + diag[d][r] * v[d][r] for att, (d, r) in zip(atts, jobs)]
        kvs = [_dot_tn(vb[d][r], ko) for (d, r), ko in zip(jobs, k_out)]
        for d in range(2):
            st = st_ref[d, hh]
            for c in (range(n_ch - 1, -1, -1) if d == 1 else range(n_ch)):
                j = d * n_ch + c
                o_refs[d][chunks[c], cv] = outs[j] + _dot_nt(q_in[j], st.astype(BF16))
                st = st * decays[j] + kvs[j]
            st_ref[d, hh] = st

    gates(0, gate_a)

    def head_pair(i, carry):
        gates(2 * i + 1, gate_b)
        dots(2 * i, gate_a)
        dots(2 * i + 1, gate_b)
        gates(jnp.minimum(2 * i + 2, heads - 1), gate_a)
        return carry

    lax.fori_loop(0, heads // 2, head_pair, 0)

    @pl.when(t == n_blocks - 1)
    def _():
        for d in range(2):
            for hh in range(heads):
                sfin_ref[d, hh] = st_ref[d, hh].T


def _hg_call(proj, lb, s0, *, tok0, batch, seq, heads, dk, dv, col_q, col_f, col_i):
    tt = min(seq, 256)
    n = seq // tt
    kw, vw = heads * dk, heads * dv
    assert seq % tt == 0 and tok0 % tt == 0 and tt % HG_CHUNK == 0
    assert col_q % kw == 0 and col_f % kw == 0 and col_i % vw == 0
    r0 = tok0 // tt

    def fwd(b, t):
        return r0 + b * n + t

    def bwd(b, t):
        return r0 + b * n + (n - 1 - t)

    in_specs, args = [], []
    for d, rmap in enumerate((fwd, bwd)):
        in_specs += [
            pl.BlockSpec((tt, kw), lambda b, t, rmap=rmap: (rmap(b, t), col_q // kw)),
            pl.BlockSpec((tt, kw), lambda b, t, rmap=rmap, d=d: (rmap(b, t), col_f // kw + d)),
            pl.BlockSpec((tt, vw), lambda b, t, rmap=rmap: (rmap(b, t), col_i // vw)),
        ]
        args += [proj, proj, proj]
    in_specs.append(pl.BlockSpec((2, 1, kw), lambda b, t: (0, 0, 0)))
    args.append(lb.reshape(2, 1, kw))
    state_spec = pl.BlockSpec((None, 2, heads, dk, dv), lambda b, t: (b, 0, 0, 0, 0))
    if s0 is not None:
        in_specs.append(state_spec)
        args.append(s0)
    n_tok = batch * seq
    return pl.pallas_call(
        functools.partial(_hg_kernel, heads=heads, dk=dk, dv=dv, tt=tt, n_blocks=n, has_init=s0 is not None),
        grid=(batch, n),
        in_specs=in_specs,
        out_specs=[
            pl.BlockSpec((tt, vw), lambda b, t: (b * n + t, 0)),
            pl.BlockSpec((tt, vw), lambda b, t: (b * n + (n - 1 - t), 0)),
            state_spec,
        ],
        out_shape=[
            jax.ShapeDtypeStruct((n_tok, vw), F32),
            jax.ShapeDtypeStruct((n_tok, vw), F32),
            jax.ShapeDtypeStruct((batch, 2, heads, dk, dv), F32),
        ],
        scratch_shapes=[pltpu.VMEM((2, heads, dv, dk), F32),
                        pltpu.VMEM((2, (HG_LEVELS + 1) * HG_CHUNK, 2 * HG_CHUNK), BF16)]
        + 2 * ([pltpu.VMEM((2, tt, dk), F32)] * 4 + [pltpu.VMEM((2, tt, dk), BF16)] * 2),
        compiler_params=_cparams("arbitrary", "arbitrary"),
        name="hgrn2_scan",
    )(*args)


def _mix_out_kernel(*refs, d, ret_heads, hg_heads, ctx_blocks):
    scans = [refs[4 * p:4 * p + 4] for p in range(2)]
    (rg_ref, go_ref, h_ref, x_ref, mod_ref, g2_ref, wr_ref, wh_ref, wm_ref, wo_ref, wrt_ref,
     x1_ref, h2_ref, lgt_ref, o_ref, og_ref) = refs[8:]

    for p, in_set in enumerate((pl.program_id(0) < ctx_blocks, pl.program_id(0) >= ctx_blocks)):
        @pl.when(in_set)
        def _(p=p):
            o_ref[...] = scans[p][0][...] + scans[p][1][...]
            og_ref[...] = scans[p][2][...] + scans[p][3][...]

    dvr = o_ref.shape[1] // ret_heads
    dvh = og_ref.shape[1] // hg_heads
    tm = o_ref.shape[0]
    halves = [slice(0, tm // 2), slice(tm // 2, tm)]
    ret_in, hg_in = [], []
    for r in halves:
        parts = []
        for hh in range(ret_heads):
            oh = o_ref[r, hh * dvr:(hh + 1) * dvr]
            ctr = oh - jnp.mean(oh, axis=-1, keepdims=True)
            parts.append(ctr * lax.rsqrt(jnp.mean(ctr * ctr, axis=-1, keepdims=True) + NORM_EPS))
        ret_in.append((jnp.concatenate(parts, axis=1) * _silu(rg_ref[r, :])).astype(BF16))
        parts = [_rms(og_ref[r, hh * dvh:(hh + 1) * dvh]) for hh in range(hg_heads)]
        hg_in.append((jnp.concatenate(parts, axis=1) * _silu(go_ref[r, :])).astype(BF16))
    ret_out = [_dot(a, wr_ref[...]) for a in ret_in]
    hg_out = [_dot(a, wh_ref[...]) for a in hg_in]
    gates = [jax.nn.sigmoid(_dot(h_ref[r, :], wm_ref[...])) for r in halves]
    merged = [(g[:, :d] * a + g[:, d:] * b).astype(BF16) for g, a, b in zip(gates, ret_out, hg_out)]
    mix = [_dot(m, wo_ref[...]) for m in merged]
    w_hi, w_lo = wrt_ref[0], wrt_ref[1]
    for r, m in zip(halves, mix):
        x1 = x_ref[r, :] + mod_ref[:, 2 * d:3 * d] * m
        x1_ref[r, :] = x1
        h2 = _rms(x1) * g2_ref[...]
        h2 = h2 * (1.0 + mod_ref[:, 4 * d:5 * d]) + mod_ref[:, 3 * d:4 * d]
        h2_ref[r, :] = h2.astype(BF16)
        h_hi = h2.astype(BF16)
        h_lo = (h2 - h_hi.astype(F32)).astype(BF16)
        lgt_ref[:, r] = _dot_nt(w_hi, h_hi) + (_dot_nt(w_hi, h_lo) + _dot_nt(w_lo, h_hi))


def _mix_out_call(scans_ctx, scans_lat, proj, h, x, mod_l, mod_row, g2, w_ret_o, w_hg_o, w_merge, w_out,
                  w_router_t, tm, *, ret_heads, hg_heads, col_rg, col_go):
    n, d = x.shape
    vr, vh = scans_ctx[0].shape[1], scans_ctx[2].shape[1]
    ne = w_router_t.shape[1]
    ctx_blocks = scans_ctx[0].shape[0] // tm
    assert col_rg % vr == 0 and col_go % vh == 0

    def row(w):
        return pl.BlockSpec((tm, w), lambda i: (i, 0))

    def full(a):
        return pl.BlockSpec(a.shape, lambda i: (0,) * a.ndim)

    ctx_row = lambda a: pl.BlockSpec((tm, a.shape[1]), lambda i: (jnp.minimum(i, ctx_blocks - 1), 0))
    lat_row = lambda a: pl.BlockSpec((tm, a.shape[1]), lambda i: (jnp.maximum(i - ctx_blocks, 0), 0))
    return pl.pallas_call(
        functools.partial(_mix_out_kernel, d=d, ret_heads=ret_heads, hg_heads=hg_heads, ctx_blocks=ctx_blocks),
        grid=(n // tm,),
        in_specs=[ctx_row(a) for a in scans_ctx] + [lat_row(a) for a in scans_lat] + [
            pl.BlockSpec((tm, vr), lambda i: (i, col_rg // vr)),
            pl.BlockSpec((tm, vh), lambda i: (i, col_go // vh)),
            row(d), row(d),
            pl.BlockSpec((None, 1, 6 * d), lambda i: (mod_row(i), 0, 0)),
            pl.BlockSpec((1, d), lambda i: (0, 0)),
            full(w_ret_o), full(w_hg_o), full(w_merge), full(w_out), full(w_router_t),
        ],
        out_specs=[row(d), row(d), pl.BlockSpec((ne, tm), lambda i: (0, i))],
        out_shape=[
            jax.ShapeDtypeStruct((n, d), F32),
            jax.ShapeDtypeStruct((n, d), BF16),
            jax.ShapeDtypeStruct((ne, n), F32),
        ],
        scratch_shapes=[pltpu.VMEM((tm, vr), F32), pltpu.VMEM((tm, vh), F32)],
        compiler_params=_cparams("arbitrary"),
        name="mixer_out",
    )(*scans_ctx, *scans_lat, proj, proj, h, x, mod_l, g2.reshape(1, d), w_ret_o, w_hg_o, w_merge, w_out,
      w_router_t)


def _route_kernel(lg_ref, aff_ref, sel_ref, posx_ref, flag_ref, *, sets, caps):
    for (lo, n), cap in zip(sets, caps):
        _route_set(lg_ref, aff_ref, sel_ref, posx_ref, flag_ref, lo=lo, n=n, cap=cap)


def _route_set(lg_ref, aff_ref, sel_ref, posx_ref, flag_ref, *, lo, n, cap):
    cols = slice(lo, lo + n)
    lg = lg_ref[:, cols]
    ne = lg.shape[0]
    ex = jnp.exp(lg - jnp.max(lg, axis=0, keepdims=True))
    aff = ex / jnp.sum(ex, axis=0, keepdims=True)
    aff_ref[:, cols] = aff
    bits = pltpu.bitcast(aff, I32)

    def search(i, found):
        cand = found | lax.shift_left(jnp.int32(1), 30 - i)
        cnt = jnp.sum(jnp.where(bits >= cand, 1.0, 0.0), axis=1, keepdims=True)
        return jnp.where(cnt >= cap, cand, found)

    thr = lax.fori_loop(0, 31, search, jnp.zeros((ne, 1), I32))
    gt = bits > thr
    eq = bits == thr
    need = cap - jnp.sum(jnp.where(gt, 1.0, 0.0), axis=1, keepdims=True)
    upper = jnp.where(lax.broadcasted_iota(I32, (LANE, LANE), 0) < lax.broadcasted_iota(I32, (LANE, LANE), 1),
                      1.0, 0.0).astype(BF16)

    def exclusive_count(store):
        def body(ci, carry):
            sl = pl.ds(pl.multiple_of(lo + ci * LANE, LANE), LANE)
            x = flag_ref[:, sl]
            store(sl, carry + _dot(x.astype(BF16), upper))
            return carry + jnp.sum(x, axis=1, keepdims=True)

        lax.fori_loop(0, n // LANE, body, jnp.zeros((ne, 1), F32))

    flag_ref[:, cols] = jnp.where(eq, 1.0, 0.0)

    def store_tie_rank(sl, rank):
        posx_ref[:, sl] = rank.astype(I32)

    exclusive_count(store_tie_rank)
    sel = gt | (eq & (posx_ref[:, cols].astype(F32) < need))
    sel_ref[:, cols] = jnp.where(sel, 1, 0).astype(I32)
    flag_ref[:, cols] = jnp.where(sel, 1.0, 0.0)

    def store_pos(sl, cnt):
        posx_ref[:, sl] = cnt.astype(I32)

    exclusive_count(store_pos)


def _route_call(logits_t, sets, caps):
    ne, n = logits_t.shape
    spec = pl.BlockSpec((ne, n), lambda: (0, 0))
    return pl.pallas_call(
        functools.partial(_route_kernel, sets=sets, caps=caps),
        in_specs=[spec],
        out_specs=[spec, spec, spec],
        out_shape=[
            jax.ShapeDtypeStruct((ne, n), F32),
            jax.ShapeDtypeStruct((ne, n), I32),
            jax.ShapeDtypeStruct((ne, n), I32),
        ],
        scratch_shapes=[pltpu.VMEM((ne, n), F32)],
        compiler_params=pltpu.CompilerParams(vmem_limit_bytes=VMEM_LIMIT),
        name="expert_choice_route",
    )(logits_t)


def _moe_round_masks(b, r, base_ref, sel_ref, posx_ref, ne, value_rows=None):
    n_tok = sel_ref.shape[1]
    row_iota = lax.broadcasted_iota(I32, (MOE_ROUND, n_tok), 0)
    pieces = []
    for e in range(ne):
        local = posx_ref[e:e + 1, :] - (base_ref[b * ne + e] + r * MOE_ROUND)
        hit = (sel_ref[e:e + 1, :] == 1) & (local == row_iota)
        value = 1.0 if value_rows is None else value_rows[e:e + 1, :]
        pieces.append(jnp.where(hit, value, 0.0).astype(BF16))
    return jnp.concatenate(pieces, axis=0)


def _dispatch_kernel(base_ref, start_ref, pc_ref, nr_ref, first_ref, h_ref, sel_ref, posx_ref, xe_zero_ref, xe_ref,
                     stage_ref, sem, *, ne):
    del xe_zero_ref
    b = pl.program_id(0)
    n_rounds = nr_ref[b]
    prev = jnp.maximum(b - 1, 0)
    prev_pending = (b > 0) & (nr_ref[prev] >= 1)

    def pieces(blk, r, fn):
        slot = (first_ref[blk] + r) % 2
        for e in range(ne):
            rem = pc_ref[blk * ne + e] - r * MOE_ROUND
            dst = start_ref[blk * ne + e] + r * MOE_ROUND
            for size, cond, off in (
                    (MOE_ROUND, rem >= MOE_ROUND, 0),
                    (32, (rem > 0) & (rem < MOE_ROUND) & ((rem & 32) != 0), 0),
                    (16, (rem > 0) & (rem < MOE_ROUND) & ((rem & 16) != 0), rem & 32)):
                @pl.when(cond)
                def _(size=size, off=off, e=e, dst=dst):
                    src_rows = pl.ds(pl.multiple_of(e * MOE_ROUND + off, BF16_SUBLANE), size)
                    dst_rows = pl.ds(pl.multiple_of(dst + off, BF16_SUBLANE), size)
                    fn(pltpu.make_async_copy(stage_ref.at[slot, src_rows], xe_ref.at[e, dst_rows], sem.at[slot]))

    def wait_previous_block():
        pieces(prev, nr_ref[prev] - 1, lambda cp: cp.wait())

    def round_body(r, carry):
        onehot = _moe_round_masks(b, r, base_ref, sel_ref, posx_ref, ne)
        stage_ref[(first_ref[b] + r) % 2] = _dot(onehot, h_ref[...]).astype(BF16)

        @pl.when(r >= 1)
        def _():
            pieces(b, r - 1, lambda cp: cp.wait())

        @pl.when((r == 0) & prev_pending)
        def _():
            wait_previous_block()

        pieces(b, r, lambda cp: cp.start())
        return carry

    lax.fori_loop(0, n_rounds, round_body, 0)

    @pl.when((n_rounds == 0) & prev_pending)
    def _():
        wait_previous_block()

    @pl.when((n_rounds >= 1) & (b == pl.num_programs(0) - 1))
    def _():
        pieces(b, n_rounds - 1, lambda cp: cp.wait())


def _dispatch_call(h2, sel, posx, tables, list_rows, *, sb):
    n, d = h2.shape
    ne = sel.shape[0]
    xe_zero = jnp.zeros((ne, list_rows, d), BF16)
    grid_spec = pltpu.PrefetchScalarGridSpec(
        num_scalar_prefetch=len(tables),
        grid=(n // sb,),
        in_specs=[
            pl.BlockSpec((sb, d), lambda b, *_: (b, 0)),
            pl.BlockSpec((ne, sb), lambda b, *_: (0, b)),
            pl.BlockSpec((ne, sb), lambda b, *_: (0, b)),
            pl.BlockSpec(memory_space=pl.ANY),
        ],
        out_specs=pl.BlockSpec(memory_space=pl.ANY),
        scratch_shapes=[pltpu.VMEM((2, ne * MOE_ROUND, d), BF16), pltpu.SemaphoreType.DMA((2,))],
    )
    return pl.pallas_call(
        functools.partial(_dispatch_kernel, ne=ne),
        grid_spec=grid_spec,
        out_shape=jax.ShapeDtypeStruct((ne, list_rows, d), BF16),
        input_output_aliases={len(tables) + 3: 0},
        compiler_params=_cparams("arbitrary"),
        name="expert_dispatch",
    )(*tables, h2, sel, posx, xe_zero)


def _expert_ffn_kernel(tot_ref, x_ref, wg_ref, wu_ref, wd_ref, y_ref, wgf_ref, wuf_ref, wdf_ref, wgb_ref, wub_ref,
                       wdb_ref, acc_ref, sem, *, layer, rt, n_ff):
    e, f, t = pl.program_id(0), pl.program_id(1), pl.program_id(2)
    fc = wgb_ref.shape[1]
    group = e * n_ff + f
    n_groups = pl.num_programs(0) * n_ff

    def weight_copies(g, fn):
        ge, gf, slot = g // n_ff, g % n_ff, g % 2
        cols = pl.ds(pl.multiple_of(gf * fc, LANE), fc)
        fn(pltpu.make_async_copy(wg_ref.at[layer, ge, :, cols], wgf_ref.at[slot], sem.at[slot]))
        fn(pltpu.make_async_copy(wu_ref.at[layer, ge, :, cols], wuf_ref.at[slot], sem.at[slot]))
        fn(pltpu.make_async_copy(wd_ref.at[layer, ge, cols, :], wdf_ref.at[slot], sem.at[slot]))

    @pl.when(t == 0)
    def _():
        @pl.when(group == 0)
        def _():
            weight_copies(group, lambda cp: cp.start())

        weight_copies(group, lambda cp: cp.wait())

        @pl.when(group + 1 < n_groups)
        def _():
            weight_copies(jnp.minimum(group + 1, n_groups - 1), lambda cp: cp.start())

        slot = group % 2
        wgb_ref[...] = wgf_ref[slot].astype(BF16)
        wub_ref[...] = wuf_ref[slot].astype(BF16)
        wdb_ref[...] = wdf_ref[slot].astype(BF16)

    rows = pl.ds(pl.multiple_of(t * rt, rt), rt)
    live = t * rt < tot_ref[e]

    @pl.when(live)
    def _():
        x = x_ref[...]
        hid = (_silu(_dot(x, wgb_ref[...])) * _dot(x, wub_ref[...])).astype(BF16)
        y = _dot(hid, wdb_ref[...])

        @pl.when(f == 0)
        def _():
            acc_ref[rows, :] = y

        @pl.when(f != 0)
        def _():
            acc_ref[rows, :] += y

    @pl.when((f == n_ff - 1) & live)
    def _():
        y_ref[...] = acc_ref[rows, :].astype(BF16)

    @pl.when((f == n_ff - 1) & jnp.logical_not(live))
    def _():
        y_ref[...] = jnp.zeros_like(y_ref)


def _expert_ffn_call(xe, totals, wg, wu, wd, layer, *, rt, fc):
    ne, list_rows, d = xe.shape
    ff = wg.shape[-1]
    n_ff = ff // fc
    grid_spec = pltpu.PrefetchScalarGridSpec(
        num_scalar_prefetch=1,
        grid=(ne, n_ff, list_rows // rt),
        in_specs=[
            pl.BlockSpec((None, rt, d), lambda e, f, t, *_: (e, t, 0)),
            pl.BlockSpec(memory_space=pl.ANY),
            pl.BlockSpec(memory_space=pl.ANY),
            pl.BlockSpec(memory_space=pl.ANY),
        ],
        out_specs=pl.BlockSpec((None, rt, d), lambda e, f, t, *_: (e, jnp.where(f == n_ff - 1, t, 0), 0)),
        scratch_shapes=[pltpu.VMEM((2, d, fc), F32), pltpu.VMEM((2, d, fc), F32), pltpu.VMEM((2, fc, d), F32),
                        pltpu.VMEM((d, fc), BF16), pltpu.VMEM((d, fc), BF16), pltpu.VMEM((fc, d), BF16),
                        pltpu.VMEM((list_rows, d), F32), pltpu.SemaphoreType.DMA((2,))],
    )
    return pl.pallas_call(
        functools.partial(_expert_ffn_kernel, layer=layer, rt=rt, n_ff=n_ff),
        grid_spec=grid_spec,
        out_shape=jax.ShapeDtypeStruct((ne, list_rows, d), BF16),
        compiler_params=_cparams("arbitrary", "arbitrary", "arbitrary"),
        name="expert_ffn",
    )(totals, xe, wg, wu, wd)


def _combine_kernel(base_ref, start_ref, pc_ref, nr_ref, first_ref, sel_ref, posx_ref, aff_ref, y_ref, out_ref,
                    ybuf_ref, sem, *, ne):
    b = pl.program_id(0)
    last = pl.num_programs(0) - 1
    n_rounds = nr_ref[b]
    first = first_ref[b]

    def windows(blk, r, fn):
        slot = (first_ref[blk] + r) % 2
        for e in range(ne):
            @pl.when(pc_ref[blk * ne + e] - r * MOE_ROUND > 0)
            def _(e=e):
                src = pl.ds(pl.multiple_of(start_ref[blk * ne + e] + r * MOE_ROUND, BF16_SUBLANE), MOE_ROUND)
                fn(pltpu.make_async_copy(y_ref.at[e, src], ybuf_ref.at[slot, pl.ds(e * MOE_ROUND, MOE_ROUND)],
                                         sem.at[slot]))

    def start_next_block():
        @pl.when(b < last)
        def _():
            windows(jnp.minimum(b + 1, last), 0, lambda cp: cp.start())

    @pl.when(b == 0)
    def _():
        ybuf_ref[...] = jnp.zeros_like(ybuf_ref)
        windows(0, 0, lambda cp: cp.start())

    out_ref[...] = jnp.zeros_like(out_ref)

    @pl.when(n_rounds == 0)
    def _():
        start_next_block()

    def round_body(r, carry):
        @pl.when(r + 1 < n_rounds)
        def _():
            windows(b, r + 1, lambda cp: cp.start())

        @pl.when(r + 1 == n_rounds)
        def _():
            start_next_block()

        windows(b, r, lambda cp: cp.wait())
        weights = _moe_round_masks(b, r, base_ref, sel_ref, posx_ref, ne, value_rows=aff_ref)
        out_ref[...] += _dot_tn(weights, ybuf_ref[(first + r) % 2])
        return carry

    lax.fori_loop(0, n_rounds, round_body, 0)


def _combine_call(ye, sel, posx, aff, tables, *, sb):
    ne, _, d = ye.shape
    n = sel.shape[1]
    grid_spec = pltpu.PrefetchScalarGridSpec(
        num_scalar_prefetch=len(tables),
        grid=(n // sb,),
        in_specs=[
            pl.BlockSpec((ne, sb), lambda b, *_: (0, b)),
            pl.BlockSpec((ne, sb), lambda b, *_: (0, b)),
            pl.BlockSpec((ne, sb), lambda b, *_: (0, b)),
            pl.BlockSpec(memory_space=pl.ANY),
        ],
        out_specs=pl.BlockSpec((sb, d), lambda b, *_: (b, 0)),
        scratch_shapes=[pltpu.VMEM((2, ne * MOE_ROUND, d), BF16), pltpu.SemaphoreType.DMA((2,))],
    )
    return pl.pallas_call(
        functools.partial(_combine_kernel, ne=ne),
        grid_spec=grid_spec,
        out_shape=jax.ShapeDtypeStruct((n, d), F32),
        compiler_params=_cparams("arbitrary"),
        name="expert_combine",
    )(*tables, sel, posx, aff, ye)


def _moe_tables(posx, sets, caps, sb, sbc):
    ne = posx.shape[0]
    fine = [jnp.concatenate([posx[:, lo:lo + n:sbc], jnp.full((ne, 1), cap, I32)], axis=1)
            for (lo, n), cap in zip(sets, caps)]
    step = sb // sbc
    base = jnp.concatenate([e[:, :-1:step].T for e in fine], axis=0)
    cnt = jnp.concatenate([(e[:, step::step] - e[:, :-1:step]).T for e in fine], axis=0)
    pc = (cnt + (BF16_SUBLANE - 1)) // BF16_SUBLANE * BF16_SUBLANE
    ends = jnp.cumsum(pc, axis=0)
    start = ends - pc
    rounds = lambda rows: jnp.max((rows + (MOE_ROUND - 1)) // MOE_ROUND, axis=1)
    flat = lambda *arrays: tuple(a.reshape(-1).astype(I32) for a in arrays)
    rounds_c = rounds(pc)
    dispatch = flat(base, start, pc, rounds_c, jnp.cumsum(rounds_c) - rounds_c)

    base_f = jnp.concatenate([e[:, :-1].T for e in fine], axis=0)
    cnt_f = jnp.concatenate([(e[:, 1:] - e[:, :-1]).T for e in fine], axis=0)
    first = jnp.repeat(start - base, step, axis=0) + base_f
    off = first % BF16_SUBLANE
    rows_f = jnp.where(cnt_f > 0, off + cnt_f, 0)
    rounds_f = rounds(rows_f)
    combine = flat(base_f - off, first - off, rows_f, rounds_f, jnp.cumsum(rounds_f) - rounds_f)
    return dispatch, combine, ends[-1].astype(I32)


def _rope_tables(n_tokens, dk):
    rows = n_tokens // GRID_W
    r, c = jnp.meshgrid(jnp.arange(rows), jnp.arange(GRID_W), indexing="ij")
    pos = jnp.stack([r.reshape(-1), c.reshape(-1)], axis=-1).astype(F32)
    nf = dk // 4
    inv_freq = ROPE_BASE ** (-jnp.arange(nf, dtype=F32) / nf)
    ang = pos[:, :, None] * inv_freq
    cos, sin = jnp.cos(ang), jnp.sin(ang)
    cos_t = jnp.concatenate([cos[:, 0], cos[:, 0], cos[:, 1], cos[:, 1]], axis=-1)
    sin_t = jnp.concatenate([-sin[:, 0], sin[:, 0], -sin[:, 1], sin[:, 1]], axis=-1)
    return cos_t, sin_t


def kernel(x_prompt, x_sample, state_ret, state_hgrn, c, c_ctx, ada_w, ada_b, norm_mix_g, norm_ffn_g, w_in,
           ret_gamma_logit, hg_lb_logit, w_ret_o, w_hg_o, w_merge, w_out, w_router, w_exp_gate, w_exp_up,
           w_exp_down, final_g):
    b_ctx, t_ctx, d = x_prompt.shape
    b_lat, t_lat, _ = x_sample.shape
    depth = w_in.shape[0]
    ret_heads, ret_dk, ret_dv = state_ret.shape[3:]
    hg_heads, hg_dk, hg_dv = state_hgrn.shape[3:]
    ne = w_router.shape[-1]
    n_ctx, n_lat = b_ctx * t_ctx, b_lat * t_lat
    n_tok = n_ctx + n_lat
    qw, vw = ret_heads * ret_dk, ret_heads * ret_dv
    kw, hw = hg_heads * hg_dk, hg_heads * hg_dv
    col_rg = 2 * qw + vw
    col_gq = col_rg + vw
    col_gf = col_gq + kw
    col_gi = col_gf + 2 * kw
    col_go = col_gi + hw
    assert col_go + hw == w_in.shape[-1] and b_lat < MOD_ROWS

    tm = 256
    assert t_ctx % tm == 0 and t_lat % tm == 0
    tm_proj = 1024 if n_tok % 1024 == 0 else tm
    sb = 1024 if (n_ctx % 4096 == 0 and n_lat % 4096 == 0) else 256
    assert n_ctx % sb == 0 and n_lat % sb == 0
    sets = ((0, n_ctx), (n_ctx, n_lat))
    caps = [CAPACITY_FACTOR * n_set // ne for _, n_set in sets]
    sbc = 256
    list_rows = sum(caps) + BF16_SUBLANE * (n_tok // sb) + MOE_ROUND
    n_tiles = max(1, list_rows // 512)
    rt = -(-list_rows // (n_tiles * BF16_SUBLANE)) * BF16_SUBLANE
    list_rows = n_tiles * rt

    def mod_rows(tmx):
        return lambda i: jnp.where(i < n_ctx // tmx, 0, 1 + (i - n_ctx // tmx) // (t_lat // tmx))

    mod_row = mod_rows(tm)
    tm_norm = 512 if (n_ctx % 512 == 0 and t_lat % 512 == 0) else tm
    norm_row = mod_rows(tm_norm)

    x = jnp.concatenate([x_prompt.reshape(n_ctx, d), x_sample.reshape(n_lat, d)], axis=0)
    cvec = jnp.zeros((MOD_ROWS, d), F32).at[0].set(c_ctx).at[1:1 + b_lat].set(c)
    mod = _mod_call(cvec, ada_w, ada_b).reshape(depth, MOD_ROWS, 1, 6 * d)

    log_gamma = jax.nn.log_sigmoid(ret_gamma_logit.astype(F32))
    p_lb = jax.nn.softmax(hg_lb_logit.astype(F32), axis=0)
    hg_lb = jnp.clip(jnp.cumsum(p_lb, axis=0) - p_lb[0:1], 0.0, 1.0 - 1e-6)
    rope = _rope_tables(t_lat, ret_dk)

    w_ret_o_b, w_hg_o_b, w_out_b = w_ret_o.astype(BF16), w_hg_o.astype(BF16), w_out.astype(BF16)
    w_merge_b = w_merge.astype(BF16)
    w_router_t = jnp.swapaxes(w_router, 1, 2)
    w_router_hi = w_router_t.astype(BF16)
    w_router_t = jnp.stack([w_router_hi, (w_router_t - w_router_hi.astype(F32)).astype(BF16)], axis=1)
    fc = min(1024, w_exp_gate.shape[-1])
    tn = min(2048, w_in.shape[-1])

    _, h = _norm_call(x, None, None, mod[0], norm_mix_g[0], norm_row, tm_norm)
    new_ret, new_hg = [], []
    for l in range(depth):
        proj = _proj_call(h, w_in, l, tm_proj, tn)

        ret_kw = dict(heads=ret_heads, dk=ret_dk, dv=ret_dv)
        orf_c, orb_c, s_ret = _ret_call(proj, log_gamma[l], None, None, tok0=0, batch=b_ctx, seq=t_ctx, **ret_kw)
        orf_l, orb_l, _ = _ret_call(proj, log_gamma[l], state_ret[:, l], rope, tok0=n_ctx, batch=b_lat, seq=t_lat,
                                    **ret_kw)
        hg_kw = dict(heads=hg_heads, dk=hg_dk, dv=hg_dv, col_q=col_gq, col_f=col_gf, col_i=col_gi)
        ogf_c, ogb_c, s_hg = _hg_call(proj, hg_lb[l], None, tok0=0, batch=b_ctx, seq=t_ctx, **hg_kw)
        ogf_l, ogb_l, _ = _hg_call(proj, hg_lb[l], state_hgrn[:, l], tok0=n_ctx, batch=b_lat, seq=t_lat, **hg_kw)
        new_ret.append(s_ret)
        new_hg.append(s_hg)

        x, h2, logits_t = _mix_out_call(
            (orf_c, orb_c, ogf_c, ogb_c), (orf_l, orb_l, ogf_l, ogb_l), proj, h, x, mod[l],
            mod_row, norm_ffn_g[l], w_ret_o_b[l], w_hg_o_b[l], w_merge_b[l], w_out_b[l], w_router_t[l], tm,
            ret_heads=ret_heads, hg_heads=hg_heads, col_rg=col_rg, col_go=col_go)

        aff, sel, posx = _route_call(logits_t, sets, caps)
        dispatch_tables, combine_tables, totals = _moe_tables(posx, sets, caps, sb, sbc)
        xe = _dispatch_call(h2, sel, posx, dispatch_tables, list_rows, sb=sb)
        ye = _expert_ffn_call(xe, totals, w_exp_gate, w_exp_up, w_exp_down, l, rt=rt, fc=fc)
        ffn = _combine_call(ye, sel, posx, aff, combine_tables, sb=sbc)

        if l + 1 < depth:
            x, h = _norm_call(x, ffn, mod[l], mod[l + 1], norm_mix_g[l + 1], norm_row, tm_norm)
        else:
            _, y_prompt = _norm_call(x, ffn, mod[l], None, final_g, norm_row, tm_norm, rows=(0, n_ctx))
            _, y_sample = _norm_call(x, ffn, mod[l], None, final_g, norm_row, tm_norm, rows=(n_ctx, n_lat))

    y_prompt = y_prompt.reshape(b_ctx, t_ctx, d)
    y_sample = y_sample.reshape(b_lat, t_lat, d)
    return y_prompt, y_sample, jnp.stack(new_ret, axis=1), jnp.stack(new_hg, axis=1)
```

```python
import functools

import jax
import jax.numpy as jnp
from jax import lax
from jax.experimental import pallas as pl
from jax.experimental.pallas import tpu as pltpu

F32 = jnp.float32
BF16 = jnp.bfloat16
I32 = jnp.int32
HIGHEST = lax.Precision.HIGHEST

NORM_EPS = 1e-6
LOG2_E = 1.4426950408889634
ROPE_BASE = 10000.0
GRID_W = 64
CAPACITY_FACTOR = 2
RET_CHUNK = 128
RET_CHUNKS_PER_STEP = 4
HG_LEVELS = 6
HG_CHUNK = 1 << HG_LEVELS
HG_ROWS_PER_STEP = 512
HG_LOG_SIGMOID_LINEAR = -60.0
MOD_ROWS = 8
LANE = 128
F32_SUBLANE = 8
BF16_SUBLANE = 16
MOE_ROUND = 64
VMEM_LIMIT = 62 * 1024 * 1024


def _cparams(*sem):
    return pltpu.CompilerParams(dimension_semantics=sem, vmem_limit_bytes=VMEM_LIMIT)


def _dot(a, b, **kw):
    return jnp.dot(a, b, preferred_element_type=F32, **kw)


def _dot_nt(a, b, **kw):
    return lax.dot_general(a, b, (((1,), (1,)), ((), ())), preferred_element_type=F32, **kw)


def _dot_tn(a, b, **kw):
    return lax.dot_general(a, b, (((0,), (0,)), ((), ())), preferred_element_type=F32, **kw)


def _silu(x):
    return x * jax.nn.sigmoid(x)


def _mod_kernel(c_ref, w_ref, b_ref, o_ref):
    s = _silu(c_ref[...])
    o_ref[...] = _dot(s, w_ref[...], precision=HIGHEST) + b_ref[...]


def _mod_call(cvec, ada_w, ada_b):
    depth, d, six_d = ada_w.shape
    tn = 6 * LANE * 2
    assert six_d % tn == 0
    return pl.pallas_call(
        _mod_kernel,
        grid=(depth, six_d // tn),
        in_specs=[
            pl.BlockSpec((MOD_ROWS, d), lambda l, j: (0, 0)),
            pl.BlockSpec((None, d, tn), lambda l, j: (l, 0, j)),
            pl.BlockSpec((None, 1, tn), lambda l, j: (l, 0, j)),
        ],
        out_specs=pl.BlockSpec((None, MOD_ROWS, tn), lambda l, j: (l, 0, j)),
        out_shape=jax.ShapeDtypeStruct((depth, MOD_ROWS, six_d), F32),
        compiler_params=_cparams("arbitrary", "arbitrary"),
        name="adaln_mod",
    )(cvec, ada_w, ada_b.reshape(depth, 1, six_d))


def _rms(x):
    return x * lax.rsqrt(jnp.mean(x * x, axis=-1, keepdims=True) + NORM_EPS)


def _norm_kernel(*refs, d, residual, final):
    it = iter(refs)
    x_ref = next(it)
    f_ref, gate_ref = (next(it), next(it)) if residual else (None, None)
    mod_ref = None if final else next(it)
    g_ref = next(it)
    xo_ref = next(it) if (residual and not final) else None
    h_ref = next(it)
    x = x_ref[...]
    if residual:
        x = x + gate_ref[...] * f_ref[...]
        if xo_ref is not None:
            xo_ref[...] = x
    y = _rms(x) * g_ref[...]
    if not final:
        y = y * (1.0 + mod_ref[:, d:2 * d]) + mod_ref[:, 0:d]
    h_ref[...] = y.astype(h_ref.dtype)


def _norm_call(x, ffn, mod_gate, mod_next, g, mod_row, tm, rows=None):
    n, d = x.shape
    residual = ffn is not None
    final = mod_next is None
    r0, n_out = (0, n) if rows is None else rows
    blk0 = r0 // tm
    row = pl.BlockSpec((tm, d), lambda i: (blk0 + i, 0))
    in_specs, args = [row], [x]
    if residual:
        in_specs += [row, pl.BlockSpec((None, 1, d), lambda i: (mod_row(blk0 + i), 0, 5))]
        args += [ffn, mod_gate]
    if not final:
        in_specs.append(pl.BlockSpec((None, 1, 2 * d), lambda i: (mod_row(blk0 + i), 0, 0)))
        args.append(mod_next)
    in_specs.append(pl.BlockSpec((1, d), lambda i: (0, 0)))
    args.append(g.reshape(1, d))
    out_row = pl.BlockSpec((tm, d), lambda i: (i, 0))
    out_specs, out_shape = [out_row], [jax.ShapeDtypeStruct((n_out, d), F32 if final else BF16)]
    if residual and not final:
        out_specs = [out_row, out_row]
        out_shape = [jax.ShapeDtypeStruct((n_out, d), F32)] + out_shape
    outs = pl.pallas_call(
        functools.partial(_norm_kernel, d=d, residual=residual, final=final),
        grid=(n_out // tm,),
        in_specs=in_specs,
        out_specs=out_specs,
        out_shape=out_shape,
        compiler_params=_cparams("arbitrary"),
        name="residual_norm",
    )(*args)
    return outs if len(outs) == 2 else (x, outs[0])


def _proj_kernel(h_ref, w_ref, o_ref, wb_ref):
    @pl.when(pl.program_id(1) == 0)
    def _():
        wb_ref[...] = w_ref[...].astype(BF16)

    o_ref[...] = _dot(h_ref[...], wb_ref[...])


def _proj_call(h, w, layer, tm, tn):
    n, d = h.shape
    cols = w.shape[-1]
    return pl.pallas_call(
        _proj_kernel,
        grid=(cols // tn, n // tm),
        in_specs=[
            pl.BlockSpec((tm, d), lambda j, i: (i, 0)),
            pl.BlockSpec((None, d, tn), lambda j, i: (layer, 0, j)),
        ],
        out_specs=pl.BlockSpec((tm, tn), lambda j, i: (i, j)),
        out_shape=jax.ShapeDtypeStruct((n, cols), F32),
        scratch_shapes=[pltpu.VMEM((d, tn), BF16)],
        compiler_params=_cparams("arbitrary", "arbitrary"),
        name="in_proj",
    )(h, w)


def _swap_half_pairs(x):
    lane = lax.broadcasted_iota(I32, x.shape, 1)
    return jnp.where((lane & 32) == 0, pltpu.roll(x, LANE - 32, 1), pltpu.roll(x, 32, 1))


def _ret_kernel(*refs, heads, dk, dv, chunk, n_chunks, has_init, has_rope):
    it = iter(refs)
    lg_ref = next(it)
    qkv = [[next(it) for _ in range(3)] for _ in range(2)]
    s0_ref = next(it) if has_init else None
    rope = [[next(it) for _ in range(2)] for _ in range(2)] if has_rope else None
    o_refs = [next(it), next(it)]
    sfin_ref = next(it)
    s_ref = next(it)

    c = pl.program_id(1)

    @pl.when(c == 0)
    def _():
        if has_init:
            s_ref[...] = s0_ref[...]
        else:
            s_ref[...] = jnp.zeros_like(s_ref)

    L = chunk
    ii = lax.broadcasted_iota(I32, (L, L), 0)
    jj = lax.broadcasted_iota(I32, (L, L), 1)
    pcol = lax.broadcasted_iota(I32, (L, 1), 0).astype(F32)
    n_sub = qkv[0][0].shape[0] // L
    jobs = [(d, hh, r) for d in range(2) for hh in range(heads) for r in range(n_sub)]
    qs, ks, vs, scores, q_in, k_out = [], [], [], [], [], []
    for d, hh, r in jobs:
        q_ref, k_ref, v_ref = qkv[d]
        rows = slice(r * L, (r + 1) * L)
        q = q_ref[rows, hh * dk:(hh + 1) * dk]
        k = k_ref[rows, hh * dk:(hh + 1) * dk] * (dk ** -0.5)
        if has_rope:
            cos, sin = rope[d][0][rows, :], rope[d][1][rows, :]
            q = q * cos + _swap_half_pairs(q) * sin
            k = k * cos + _swap_half_pairs(k) * sin
        qs.append(q)
        ks.append(k)
        vs.append(v_ref[rows, hh * dv:(hh + 1) * dv].astype(BF16))
    for (d, hh, r), q, k in zip(jobs, qs, ks):
        dif = ((ii - jj) if d == 0 else (jj - ii)).astype(F32)
        lg = lg_ref[d, hh]
        decay = jnp.where(dif >= 0.0, jnp.exp(lg * jnp.maximum(dif, 0.0)), 0.0)
        scores.append((_dot_nt(q.astype(BF16), k.astype(BF16)) * decay).astype(BF16))
    for (d, hh, r), q, k in zip(jobs, qs, ks):
        lg = lg_ref[d, hh]
        q_pow, k_pow = (pcol + 1.0, (L - 1.0) - pcol) if d == 0 else (L - pcol, pcol)
        q_in.append((q * jnp.exp(lg * q_pow)).astype(BF16))
        k_out.append((k * jnp.exp(lg * k_pow)).T.astype(BF16))
    intra = [_dot(sc, v) for sc, v in zip(scores, vs)]
    kvs = [_dot(ko, v) for ko, v in zip(k_out, vs)]
    for d in range(2):
        for hh in range(heads):
            s = s_ref[d, hh]
            chunk_decay = jnp.exp(jnp.full((1, 1), lg_ref[d, hh] * L, F32))
            for r in (range(n_sub) if d == 0 else range(n_sub - 1, -1, -1)):
                j = (d * heads + hh) * n_sub + r
                o_refs[d][r * L:(r + 1) * L, hh * dv:(hh + 1) * dv] = intra[j] + _dot(q_in[j], s.astype(BF16))
                s = s * chunk_decay + kvs[j]
            s_ref[d, hh] = s

    @pl.when(c == n_chunks - 1)
    def _():
        sfin_ref[...] = s_ref[...]


def _ret_call(proj, log_gamma, s0, rope, *, tok0, batch, seq, heads, dk, dv):
    rb = min(seq, RET_CHUNK * RET_CHUNKS_PER_STEP)
    n = seq // rb
    qw, vw = heads * dk, heads * dv
    assert seq % rb == 0 and tok0 % rb == 0 and vw % qw == 0
    r0 = tok0 // rb

    def fwd(b, c):
        return r0 + b * n + c

    def bwd(b, c):
        return r0 + b * n + (n - 1 - c)

    in_specs = [pl.BlockSpec(memory_space=pltpu.SMEM)]
    args = [log_gamma]
    for rmap in (fwd, bwd):
        in_specs += [
            pl.BlockSpec((rb, qw), lambda b, c, rmap=rmap: (rmap(b, c), 0)),
            pl.BlockSpec((rb, qw), lambda b, c, rmap=rmap: (rmap(b, c), 1)),
            pl.BlockSpec((rb, vw), lambda b, c, rmap=rmap: (rmap(b, c), (2 * qw) // vw)),
        ]
        args += [proj, proj, proj]
    state_spec = pl.BlockSpec((None, 2, heads, dk, dv), lambda b, c: (b, 0, 0, 0, 0))
    if s0 is not None:
        in_specs.append(state_spec)
        args.append(s0)
    if rope is not None:
        for cmap in (lambda b, c: (c, 0), lambda b, c: (n - 1 - c, 0)):
            in_specs += [pl.BlockSpec((rb, dk), cmap), pl.BlockSpec((rb, dk), cmap)]
            args += [rope[0], rope[1]]
    n_tok = batch * seq
    return pl.pallas_call(
        functools.partial(_ret_kernel, heads=heads, dk=dk, dv=dv, chunk=RET_CHUNK, n_chunks=n,
                          has_init=s0 is not None, has_rope=rope is not None),
        grid=(batch, n),
        in_specs=in_specs,
        out_specs=[
            pl.BlockSpec((rb, vw), lambda b, c: (b * n + c, 0)),
            pl.BlockSpec((rb, vw), lambda b, c: (b * n + (n - 1 - c), 0)),
            state_spec,
        ],
        out_shape=[
            jax.ShapeDtypeStruct((n_tok, vw), F32),
            jax.ShapeDtypeStruct((n_tok, vw), F32),
            jax.ShapeDtypeStruct((batch, 2, heads, dk, dv), F32),
        ],
        scratch_shapes=[pltpu.VMEM((2, heads, dk, dv), F32)],
        compiler_params=_cparams("arbitrary", "arbitrary"),
        name="retention_scan",
    )(*args)


def _hg_scan_index(shape, axis, rev):
    i = lax.broadcasted_iota(I32, shape, axis) & (HG_CHUNK - 1)
    return (HG_CHUNK - 1 - i) if rev else i


def _hg_window_matrix(rev):
    C = HG_CHUNK
    ti = _hg_scan_index((C, 2 * C), 0, rev)
    tj = _hg_scan_index((C, 2 * C), 1, rev)
    blocks = [tj <= ti]
    for l in range(2, HG_LEVELS + 1):
        anchor = ((ti >> l) << l) + (1 << (l - 1)) - 1
        upper = ((ti >> (l - 1)) & 1) == 1
        blocks.append((upper & (tj > anchor) & (tj <= ti)) | (~upper & (tj > ti) & (tj <= anchor)))
    blocks.append(tj > ti)
    return jnp.concatenate([jnp.where(m, 1.0, 0.0) for m in blocks], axis=0).astype(BF16)


def _hg_kernel(*refs, heads, dk, dv, tt, n_blocks, has_init):
    it = iter(refs)
    qzv = [[next(it) for _ in range(3)] for _ in range(2)]
    lb_ref = next(it)
    s0_ref = next(it) if has_init else None
    o_refs = [next(it), next(it)]
    sfin_ref = next(it)
    st_ref = next(it)
    win_ref = next(it)
    gate_a = [next(it) for _ in range(6)]
    gate_b = [next(it) for _ in range(6)]

    t = pl.program_id(1)

    @pl.when((pl.program_id(0) == 0) & (t == 0))
    def _():
        for d in range(2):
            win_ref[d] = _hg_window_matrix(d == 1)

    @pl.when(t == 0)
    def _():
        for d in range(2):
            for hh in range(heads):
                st_ref[d, hh] = s0_ref[d, hh].T if has_init else jnp.zeros((dv, dk), F32)

    C = HG_CHUNK
    n_ch = tt // C
    chunks = [slice(c * C, (c + 1) * C) for c in range(n_ch)]
    levels, uppers = [], []
    for d in range(2):
        ti = _hg_scan_index((C, C), 0, d == 1)
        tj = _hg_scan_index((C, C), 1, d == 1)
        level = jnp.where(tj < ti, 1, 0)
        for l in range(1, HG_LEVELS):
            level = level + jnp.where((tj < ti) & ((ti >> l) != (tj >> l)), 1, 0)
        t_col = _hg_scan_index((C, 1), 0, d == 1)
        levels.append(level)
        uppers.append([None] + [((t_col >> (l - 1)) & 1) == 1 for l in range(1, HG_LEVELS + 1)])

    def gates(hh, g_refs):
        qg_ref, k_ref, f_ref, diag_ref, hi_ref, lo_ref = g_refs
        ck = pl.ds(pl.multiple_of(hh * dk, LANE), dk)
        for d in range(2):
            q_ref, z_ref, _ = qzv[d]
            lb = lb_ref[d, :, ck]
            z = z_ref[:, ck]
            e = jnp.exp(-jnp.abs(z))
            inv = 1.0 / (1.0 + e)
            pos = z > 0.0
            f = lb + (1.0 - lb) * (jnp.where(pos, 1.0, e) * inv)
            k = (1.0 - lb) * (jnp.where(pos, e, 1.0) * inv)
            tiny = (lb <= 0.0) & (z < HG_LOG_SIGMOID_LINEAR)
            log2_f = jnp.where(tiny, z * LOG2_E, jnp.log2(f))
            hi = log2_f.astype(BF16)
            qg = _silu(q_ref[:, ck]) * (dk ** -0.5)
            qg_ref[d], k_ref[d], f_ref[d] = qg, k, f
            hi_ref[d] = hi
            lo_ref[d] = (log2_f - hi.astype(F32)).astype(BF16)
            diag_ref[d] = jnp.broadcast_to(jnp.sum(qg * k, axis=-1, keepdims=True), qg.shape)

    def dots(hh, g_refs):
        qg_ref, k_ref, f_ref, diag_ref, hi_ref, lo_ref = g_refs
        cv = pl.ds(pl.multiple_of(hh * dv, LANE), dv)
        jobs = [(d, r) for d in range(2) for r in chunks]
        qg, k, f, diag = ([ref[d] for d in range(2)] for ref in (qg_ref, k_ref, f_ref, diag_ref))
        v = [qzv[d][2][:, cv] for d in range(2)]
        vb = [a.astype(BF16) for a in v]
        wins = [win_ref[0], win_ref[1]]
        sums = [_dot(wins[d], jnp.concatenate([hi_ref[d, r, :], lo_ref[d, r, :]], axis=0)) for d, r in jobs]

        def query_or_key(d, r, l):
            half = 1 << (l - 1)
            if half % F32_SUBLANE:
                return jnp.where(uppers[d][l], qg[d][r], k[d][r])
            segs = []
            for s in range(C // half):
                src = qg[d] if (s % 2 == 1) != (d == 1) else k[d]
                segs.append(src[r.start + s * half:r.start + (s + 1) * half])
            return jnp.concatenate(segs, axis=0)

        pairs = []
        for (d, r), s in zip(jobs, sums):
            ps = [jnp.where(uppers[d][1], qg[d][r] * f[d][r], k[d][r]).astype(BF16)]
            for l in range(2, HG_LEVELS + 1):
                scale = jnp.exp2(s[(l - 1) * C:l * C])
                ps.append((query_or_key(d, r, l) * scale).astype(BF16))
            pairs.append(ps)
        grams = [[_dot_nt(p, p) for p in ps] for ps in pairs]
        atts = []
        for (d, r), gs in zip(jobs, grams):
            att = jnp.where(levels[d] == 1, gs[0], 0.0)
            for l in range(2, HG_LEVELS + 1):
                att = jnp.where(levels[d] == l, gs[l - 1], att)
            atts.append(att.astype(BF16))
        k_out = [(k[d][r] * jnp.exp2(s[HG_LEVELS * C:])).astype(BF16) for (d, r), s in zip(jobs, sums)]
        q_in = [(qg[d][r] * jnp.exp2(s[:C])).astype(BF16) for (d, r), s in zip(jobs, sums)]
        decays = [jnp.exp2(s[0:1] if d == 1 else s[C - 1:C]) for (d, r), s in zip(jobs, sums)]
        outs = [_dot(att, vb[d][r]) + diag[d][r] * v[d][r] for att, (d, r) in zip(atts, jobs)]
        kvs = [_dot_tn(vb[d][r], ko) for (d, r), ko in zip(jobs, k_out)]
        for d in range(2):
            st = st_ref[d, hh]
            for c in (range(n_ch - 1, -1, -1) if d == 1 else range(n_ch)):
                j = d * n_ch + c
                o_refs[d][chunks[c], cv] = outs[j] + _dot_nt(q_in[j], st.astype(BF16))
                st = st * decays[j] + kvs[j]
            st_ref[d, hh] = st

    gates(0, gate_a)

    def head_pair(i, carry):
        gates(2 * i + 1, gate_b)
        dots(2 * i, gate_a)
        dots(2 * i + 1, gate_b)
        gates(jnp.minimum(2 * i + 2, heads - 1), gate_a)
        return carry

    lax.fori_loop(0, heads // 2, head_pair, 0)

    @pl.when(t == n_blocks - 1)
    def _():
        for d in range(2):
            for hh in range(heads):
                sfin_ref[d, hh] = st_ref[d, hh].T


def _hg_call(proj, lb, s0, *, tok0, batch, seq, heads, dk, dv, col_q, col_f, col_i):
    tt = min(seq, HG_ROWS_PER_STEP)
    n = seq // tt
    kw, vw = heads * dk, heads * dv
    assert seq % tt == 0 and tok0 % tt == 0 and tt % HG_CHUNK == 0
    assert col_q % kw == 0 and col_f % kw == 0 and col_i % vw == 0
    r0 = tok0 // tt

    def fwd(b, t):
        return r0 + b * n + t

    def bwd(b, t):
        return r0 + b * n + (n - 1 - t)

    in_specs, args = [], []
    for d, rmap in enumerate((fwd, bwd)):
        in_specs += [
            pl.BlockSpec((tt, kw), lambda b, t, rmap=rmap: (rmap(b, t), col_q // kw)),
            pl.BlockSpec((tt, kw), lambda b, t, rmap=rmap, d=d: (rmap(b, t), col_f // kw + d)),
            pl.BlockSpec((tt, vw), lambda b, t, rmap=rmap: (rmap(b, t), col_i // vw)),
        ]
        args += [proj, proj, proj]
    in_specs.append(pl.BlockSpec((2, 1, kw), lambda b, t: (0, 0, 0)))
    args.append(lb.reshape(2, 1, kw))
    state_spec = pl.BlockSpec((None, 2, heads, dk, dv), lambda b, t: (b, 0, 0, 0, 0))
    if s0 is not None:
        in_specs.append(state_spec)
        args.append(s0)
    n_tok = batch * seq
    return pl.pallas_call(
        functools.partial(_hg_kernel, heads=heads, dk=dk, dv=dv, tt=tt, n_blocks=n, has_init=s0 is not None),
        grid=(batch, n),
        in_specs=in_specs,
        out_specs=[
            pl.BlockSpec((tt, vw), lambda b, t: (b * n + t, 0)),
            pl.BlockSpec((tt, vw), lambda b, t: (b * n + (n - 1 - t), 0)),
            state_spec,
        ],
        out_shape=[
            jax.ShapeDtypeStruct((n_tok, vw), F32),
            jax.ShapeDtypeStruct((n_tok, vw), F32),
            jax.ShapeDtypeStruct((batch, 2, heads, dk, dv), F32),
        ],
        scratch_shapes=[pltpu.VMEM((2, heads, dv, dk), F32),
                        pltpu.VMEM((2, (HG_LEVELS + 1) * HG_CHUNK, 2 * HG_CHUNK), BF16)]
        + 2 * ([pltpu.VMEM((2, tt, dk), F32)] * 4 + [pltpu.VMEM((2, tt, dk), BF16)] * 2),
        compiler_params=_cparams("arbitrary", "arbitrary"),
        name="hgrn2_scan",
    )(*args)


def _mix_out_kernel(*refs, d, ret_heads, hg_heads, ctx_blocks):
    scans = [refs[4 * p:4 * p + 4] for p in range(2)]
    (rg_ref, go_ref, h_ref, x_ref, mod_ref, g2_ref, wr_ref, wh_ref, wm_ref, wo_ref, wrt_ref,
     x1_ref, h2_ref, lgt_ref, o_ref, og_ref) = refs[8:]

    for p, in_set in enumerate((pl.program_id(0) < ctx_blocks, pl.program_id(0) >= ctx_blocks)):
        @pl.when(in_set)
        def _(p=p):
            o_ref[...] = scans[p][0][...] + scans[p][1][...]
            og_ref[...] = scans[p][2][...] + scans[p][3][...]

    dvr = o_ref.shape[1] // ret_heads
    dvh = og_ref.shape[1] // hg_heads
    tm = o_ref.shape[0]
    halves = [slice(0, tm // 2), slice(tm // 2, tm)]
    ret_in, hg_in = [], []
    for r in halves:
        parts = []
        for hh in range(ret_heads):
            oh = o_ref[r, hh * dvr:(hh + 1) * dvr]
            ctr = oh - jnp.mean(oh, axis=-1, keepdims=True)
            parts.append(ctr * lax.rsqrt(jnp.mean(ctr * ctr, axis=-1, keepdims=True) + NORM_EPS))
        ret_in.append((jnp.concatenate(parts, axis=1) * _silu(rg_ref[r, :])).astype(BF16))
        parts = [_rms(og_ref[r, hh * dvh:(hh + 1) * dvh]) for hh in range(hg_heads)]
        hg_in.append((jnp.concatenate(parts, axis=1) * _silu(go_ref[r, :])).astype(BF16))
    ret_out = [_dot(a, wr_ref[...]) for a in ret_in]
    hg_out = [_dot(a, wh_ref[...]) for a in hg_in]
    gates = [jax.nn.sigmoid(_dot(h_ref[r, :], wm_ref[...])) for r in halves]
    merged = [(g[:, :d] * a + g[:, d:] * b).astype(BF16) for g, a, b in zip(gates, ret_out, hg_out)]
    mix = [_dot(m, wo_ref[...]) for m in merged]
    w_hi, w_lo = wrt_ref[0], wrt_ref[1]
    for r, m in zip(halves, mix):
        x1 = x_ref[r, :] + mod_ref[:, 2 * d:3 * d] * m
        x1_ref[r, :] = x1
        h2 = _rms(x1) * g2_ref[...]
        h2 = h2 * (1.0 + mod_ref[:, 4 * d:5 * d]) + mod_ref[:, 3 * d:4 * d]
        h2_ref[r, :] = h2.astype(BF16)
        h_hi = h2.astype(BF16)
        h_lo = (h2 - h_hi.astype(F32)).astype(BF16)
        lgt_ref[:, r] = _dot_nt(w_hi, h_hi) + (_dot_nt(w_hi, h_lo) + _dot_nt(w_lo, h_hi))


def _mix_out_call(scans_ctx, scans_lat, proj, h, x, mod_l, mod_row, g2, w_ret_o, w_hg_o, w_merge, w_out,
                  w_router_t, tm, *, ret_heads, hg_heads, col_rg, col_go):
    n, d = x.shape
    vr, vh = scans_ctx[0].shape[1], scans_ctx[2].shape[1]
    ne = w_router_t.shape[1]
    ctx_blocks = scans_ctx[0].shape[0] // tm
    assert col_rg % vr == 0 and col_go % vh == 0

    def row(w):
        return pl.BlockSpec((tm, w), lambda i: (i, 0))

    def full(a):
        return pl.BlockSpec(a.shape, lambda i: (0,) * a.ndim)

    ctx_row = lambda a: pl.BlockSpec((tm, a.shape[1]), lambda i: (jnp.minimum(i, ctx_blocks - 1), 0))
    lat_row = lambda a: pl.BlockSpec((tm, a.shape[1]), lambda i: (jnp.maximum(i - ctx_blocks, 0), 0))
    return pl.pallas_call(
        functools.partial(_mix_out_kernel, d=d, ret_heads=ret_heads, hg_heads=hg_heads, ctx_blocks=ctx_blocks),
        grid=(n // tm,),
        in_specs=[ctx_row(a) for a in scans_ctx] + [lat_row(a) for a in scans_lat] + [
            pl.BlockSpec((tm, vr), lambda i: (i, col_rg // vr)),
            pl.BlockSpec((tm, vh), lambda i: (i, col_go // vh)),
            row(d), row(d),
            pl.BlockSpec((None, 1, 6 * d), lambda i: (mod_row(i), 0, 0)),
            pl.BlockSpec((1, d), lambda i: (0, 0)),
            full(w_ret_o), full(w_hg_o), full(w_merge), full(w_out), full(w_router_t),
        ],
        out_specs=[row(d), row(d), pl.BlockSpec((ne, tm), lambda i: (0, i))],
        out_shape=[
            jax.ShapeDtypeStruct((n, d), F32),
            jax.ShapeDtypeStruct((n, d), BF16),
            jax.ShapeDtypeStruct((ne, n), F32),
        ],
        scratch_shapes=[pltpu.VMEM((tm, vr), F32), pltpu.VMEM((tm, vh), F32)],
        compiler_params=_cparams("arbitrary"),
        name="mixer_out",
    )(*scans_ctx, *scans_lat, proj, proj, h, x, mod_l, g2.reshape(1, d), w_ret_o, w_hg_o, w_merge, w_out,
      w_router_t)


def _route_kernel(lg_ref, aff_ref, sel_ref, posx_ref, flag_ref, *, sets, caps):
    for (lo, n), cap in zip(sets, caps):
        _route_set(lg_ref, aff_ref, sel_ref, posx_ref, flag_ref, lo=lo, n=n, cap=cap)


def _route_set(lg_ref, aff_ref, sel_ref, posx_ref, flag_ref, *, lo, n, cap):
    cols = slice(lo, lo + n)
    lg = lg_ref[:, cols]
    ne = lg.shape[0]
    ex = jnp.exp(lg - jnp.max(lg, axis=0, keepdims=True))
    aff = ex / jnp.sum(ex, axis=0, keepdims=True)
    aff_ref[:, cols] = aff
    bits = pltpu.bitcast(aff, I32)

    def search(i, found):
        cand = found | lax.shift_left(jnp.int32(1), 30 - i)
        cnt = jnp.sum(jnp.where(bits >= cand, 1.0, 0.0), axis=1, keepdims=True)
        return jnp.where(cnt >= cap, cand, found)

    thr = lax.fori_loop(0, 31, search, jnp.zeros((ne, 1), I32))
    gt = bits > thr
    eq = bits == thr
    need = cap - jnp.sum(jnp.where(gt, 1.0, 0.0), axis=1, keepdims=True)
    upper = jnp.where(lax.broadcasted_iota(I32, (LANE, LANE), 0) < lax.broadcasted_iota(I32, (LANE, LANE), 1),
                      1.0, 0.0).astype(BF16)

    def exclusive_count(store):
        def body(ci, carry):
            sl = pl.ds(pl.multiple_of(lo + ci * LANE, LANE), LANE)
            x = flag_ref[:, sl]
            store(sl, carry + _dot(x.astype(BF16), upper))
            return carry + jnp.sum(x, axis=1, keepdims=True)

        lax.fori_loop(0, n // LANE, body, jnp.zeros((ne, 1), F32))

    flag_ref[:, cols] = jnp.where(eq, 1.0, 0.0)

    def store_tie_rank(sl, rank):
        posx_ref[:, sl] = rank.astype(I32)

    exclusive_count(store_tie_rank)
    sel = gt | (eq & (posx_ref[:, cols].astype(F32) < need))
    sel_ref[:, cols] = jnp.where(sel, 1, 0).astype(I32)
    flag_ref[:, cols] = jnp.where(sel, 1.0, 0.0)

    def store_pos(sl, cnt):
        posx_ref[:, sl] = cnt.astype(I32)

    exclusive_count(store_pos)


def _route_call(logits_t, sets, caps):
    ne, n = logits_t.shape
    spec = pl.BlockSpec((ne, n), lambda: (0, 0))
    return pl.pallas_call(
        functools.partial(_route_kernel, sets=sets, caps=caps),
        in_specs=[spec],
        out_specs=[spec, spec, spec],
        out_shape=[
            jax.ShapeDtypeStruct((ne, n), F32),
            jax.ShapeDtypeStruct((ne, n), I32),
            jax.ShapeDtypeStruct((ne, n), I32),
        ],
        scratch_shapes=[pltpu.VMEM((ne, n), F32)],
        compiler_params=pltpu.CompilerParams(vmem_limit_bytes=VMEM_LIMIT),
        name="expert_choice_route",
    )(logits_t)


def _moe_round_masks(b, r, base_ref, sel_ref, posx_ref, ne, value_rows=None):
    n_tok = sel_ref.shape[1]
    row_iota = lax.broadcasted_iota(I32, (MOE_ROUND, n_tok), 0)
    pieces = []
    for e in range(ne):
        local = posx_ref[e:e + 1, :] - (base_ref[b * ne + e] + r * MOE_ROUND)
        hit = (sel_ref[e:e + 1, :] == 1) & (local == row_iota)
        value = 1.0 if value_rows is None else value_rows[e:e + 1, :]
        pieces.append(jnp.where(hit, value, 0.0).astype(BF16))
    return jnp.concatenate(pieces, axis=0)


def _dispatch_kernel(base_ref, start_ref, pc_ref, nr_ref, first_ref, h_ref, sel_ref, posx_ref, xe_zero_ref, xe_ref,
                     stage_ref, sem, *, ne):
    del xe_zero_ref
    b = pl.program_id(0)
    n_rounds = nr_ref[b]
    prev = jnp.maximum(b - 1, 0)
    prev_pending = (b > 0) & (nr_ref[prev] >= 1)

    def pieces(blk, r, fn):
        slot = (first_ref[blk] + r) % 2
        for e in range(ne):
            rem = pc_ref[blk * ne + e] - r * MOE_ROUND
            dst = start_ref[blk * ne + e] + r * MOE_ROUND
            for size, cond, off in (
                    (MOE_ROUND, rem >= MOE_ROUND, 0),
                    (32, (rem > 0) & (rem < MOE_ROUND) & ((rem & 32) != 0), 0),
                    (16, (rem > 0) & (rem < MOE_ROUND) & ((rem & 16) != 0), rem & 32)):
                @pl.when(cond)
                def _(size=size, off=off, e=e, dst=dst):
                    src_rows = pl.ds(pl.multiple_of(e * MOE_ROUND + off, BF16_SUBLANE), size)
                    dst_rows = pl.ds(pl.multiple_of(dst + off, BF16_SUBLANE), size)
                    fn(pltpu.make_async_copy(stage_ref.at[slot, src_rows], xe_ref.at[e, dst_rows], sem.at[slot]))

    def wait_previous_block():
        pieces(prev, nr_ref[prev] - 1, lambda cp: cp.wait())

    def round_body(r, carry):
        onehot = _moe_round_masks(b, r, base_ref, sel_ref, posx_ref, ne)
        stage_ref[(first_ref[b] + r) % 2] = _dot(onehot, h_ref[...]).astype(BF16)

        @pl.when(r >= 1)
        def _():
            pieces(b, r - 1, lambda cp: cp.wait())

        @pl.when((r == 0) & prev_pending)
        def _():
            wait_previous_block()

        pieces(b, r, lambda cp: cp.start())
        return carry

    lax.fori_loop(0, n_rounds, round_body, 0)

    @pl.when((n_rounds == 0) & prev_pending)
    def _():
        wait_previous_block()

    @pl.when((n_rounds >= 1) & (b == pl.num_programs(0) - 1))
    def _():
        pieces(b, n_rounds - 1, lambda cp: cp.wait())


def _dispatch_call(h2, sel, posx, tables, list_rows, *, sb):
    n, d = h2.shape
    ne = sel.shape[0]
    xe_zero = jnp.zeros((ne, list_rows, d), BF16)
    grid_spec = pltpu.PrefetchScalarGridSpec(
        num_scalar_prefetch=len(tables),
        grid=(n // sb,),
        in_specs=[
            pl.BlockSpec((sb, d), lambda b, *_: (b, 0)),
            pl.BlockSpec((ne, sb), lambda b, *_: (0, b)),
            pl.BlockSpec((ne, sb), lambda b, *_: (0, b)),
            pl.BlockSpec(memory_space=pl.ANY),
        ],
        out_specs=pl.BlockSpec(memory_space=pl.ANY),
        scratch_shapes=[pltpu.VMEM((2, ne * MOE_ROUND, d), BF16), pltpu.SemaphoreType.DMA((2,))],
    )
    return pl.pallas_call(
        functools.partial(_dispatch_kernel, ne=ne),
        grid_spec=grid_spec,
        out_shape=jax.ShapeDtypeStruct((ne, list_rows, d), BF16),
        input_output_aliases={len(tables) + 3: 0},
        compiler_params=_cparams("arbitrary"),
        name="expert_dispatch",
    )(*tables, h2, sel, posx, xe_zero)


def _expert_ffn_kernel(tot_ref, x_ref, wg_ref, wu_ref, wd_ref, y_ref, wgf_ref, wuf_ref, wdf_ref, wgb_ref, wub_ref,
                       wdb_ref, acc_ref, sem, *, layer, rt, n_ff):
    e, f, t = pl.program_id(0), pl.program_id(1), pl.program_id(2)
    fc = wgb_ref.shape[1]
    group = e * n_ff + f
    n_groups = pl.num_programs(0) * n_ff

    def weight_copies(g, fn):
        ge, gf, slot = g // n_ff, g % n_ff, g % 2
        cols = pl.ds(pl.multiple_of(gf * fc, LANE), fc)
        fn(pltpu.make_async_copy(wg_ref.at[layer, ge, :, cols], wgf_ref.at[slot], sem.at[slot]))
        fn(pltpu.make_async_copy(wu_ref.at[layer, ge, :, cols], wuf_ref.at[slot], sem.at[slot]))
        fn(pltpu.make_async_copy(wd_ref.at[layer, ge, cols, :], wdf_ref.at[slot], sem.at[slot]))

    @pl.when(t == 0)
    def _():
        @pl.when(group == 0)
        def _():
            weight_copies(group, lambda cp: cp.start())

        weight_copies(group, lambda cp: cp.wait())

        @pl.when(group + 1 < n_groups)
        def _():
            weight_copies(jnp.minimum(group + 1, n_groups - 1), lambda cp: cp.start())

        slot = group % 2
        wgb_ref[...] = wgf_ref[slot].astype(BF16)
        wub_ref[...] = wuf_ref[slot].astype(BF16)
        wdb_ref[...] = wdf_ref[slot].astype(BF16)

    rows = pl.ds(pl.multiple_of(t * rt, rt), rt)
    live = t * rt < tot_ref[e]

    @pl.when(live)
    def _():
        x = x_ref[...]
        hid = (_silu(_dot(x, wgb_ref[...])) * _dot(x, wub_ref[...])).astype(BF16)
        y = _dot(hid, wdb_ref[...])

        @pl.when(f == 0)
        def _():
            acc_ref[rows, :] = y

        @pl.when(f != 0)
        def _():
            acc_ref[rows, :] += y

    @pl.when((f == n_ff - 1) & live)
    def _():
        y_ref[...] = acc_ref[rows, :].astype(BF16)

    @pl.when((f == n_ff - 1) & jnp.logical_not(live))
    def _():
        y_ref[...] = jnp.zeros_like(y_ref)


def _expert_ffn_call(xe, totals, wg, wu, wd, layer, *, rt, fc):
    ne, list_rows, d = xe.shape
    ff = wg.shape[-1]
    n_ff = ff // fc
    grid_spec = pltpu.PrefetchScalarGridSpec(
        num_scalar_prefetch=1,
        grid=(ne, n_ff, list_rows // rt),
        in_specs=[
            pl.BlockSpec((None, rt, d), lambda e, f, t, *_: (e, t, 0)),
            pl.BlockSpec(memory_space=pl.ANY),
            pl.BlockSpec(memory_space=pl.ANY),
            pl.BlockSpec(memory_space=pl.ANY),
        ],
        out_specs=pl.BlockSpec((None, rt, d), lambda e, f, t, *_: (e, jnp.where(f == n_ff - 1, t, 0), 0)),
        scratch_shapes=[pltpu.VMEM((2, d, fc), F32), pltpu.VMEM((2, d, fc), F32), pltpu.VMEM((2, fc, d), F32),
                        pltpu.VMEM((d, fc), BF16), pltpu.VMEM((d, fc), BF16), pltpu.VMEM((fc, d), BF16),
                        pltpu.VMEM((list_rows, d), F32), pltpu.SemaphoreType.DMA((2,))],
    )
    return pl.pallas_call(
        functools.partial(_expert_ffn_kernel, layer=layer, rt=rt, n_ff=n_ff),
        grid_spec=grid_spec,
        out_shape=jax.ShapeDtypeStruct((ne, list_rows, d), BF16),
        compiler_params=_cparams("arbitrary", "arbitrary", "arbitrary"),
        name="expert_ffn",
    )(totals, xe, wg, wu, wd)


def _combine_kernel(base_ref, start_ref, pc_ref, nr_ref, first_ref, sel_ref, posx_ref, aff_ref, y_ref, out_ref,
                    ybuf_ref, sem, *, ne):
    b = pl.program_id(0)
    last = pl.num_programs(0) - 1
    n_rounds = nr_ref[b]
    first = first_ref[b]

    def windows(blk, r, fn):
        slot = (first_ref[blk] + r) % 2
        for e in range(ne):
            @pl.when(pc_ref[blk * ne + e] - r * MOE_ROUND > 0)
            def _(e=e):
                src = pl.ds(pl.multiple_of(start_ref[blk * ne + e] + r * MOE_ROUND, BF16_SUBLANE), MOE_ROUND)
                fn(pltpu.make_async_copy(y_ref.at[e, src], ybuf_ref.at[slot, pl.ds(e * MOE_ROUND, MOE_ROUND)],
                                         sem.at[slot]))

    def start_next_block():
        @pl.when(b < last)
        def _():
            windows(jnp.minimum(b + 1, last), 0, lambda cp: cp.start())

    @pl.when(b == 0)
    def _():
        ybuf_ref[...] = jnp.zeros_like(ybuf_ref)
        windows(0, 0, lambda cp: cp.start())

    out_ref[...] = jnp.zeros_like(out_ref)

    @pl.when(n_rounds == 0)
    def _():
        start_next_block()

    def round_body(r, carry):
        @pl.when(r + 1 < n_rounds)
        def _():
            windows(b, r + 1, lambda cp: cp.start())

        @pl.when(r + 1 == n_rounds)
        def _():
            start_next_block()

        windows(b, r, lambda cp: cp.wait())
        weights = _moe_round_masks(b, r, base_ref, sel_ref, posx_ref, ne, value_rows=aff_ref)
        out_ref[...] += _dot_tn(weights, ybuf_ref[(first + r) % 2])
        return carry

    lax.fori_loop(0, n_rounds, round_body, 0)


def _combine_call(ye, sel, posx, aff, tables, *, sb):
    ne, _, d = ye.shape
    n = sel.shape[1]
    grid_spec = pltpu.PrefetchScalarGridSpec(
        num_scalar_prefetch=len(tables),
        grid=(n // sb,),
        in_specs=[
            pl.BlockSpec((ne, sb), lambda b, *_: (0, b)),
            pl.BlockSpec((ne, sb), lambda b, *_: (0, b)),
            pl.BlockSpec((ne, sb), lambda b, *_: (0, b)),
            pl.BlockSpec(memory_space=pl.ANY),
        ],
        out_specs=pl.BlockSpec((sb, d), lambda b, *_: (b, 0)),
        scratch_shapes=[pltpu.VMEM((2, ne * MOE_ROUND, d), BF16), pltpu.SemaphoreType.DMA((2,))],
    )
    return pl.pallas_call(
        functools.partial(_combine_kernel, ne=ne),
        grid_spec=grid_spec,
        out_shape=jax.ShapeDtypeStruct((n, d), F32),
        compiler_params=_cparams("arbitrary"),
        name="expert_combine",
    )(*tables, sel, posx, aff, ye)


def _moe_tables(posx, sets, caps, sb, sbc):
    ne = posx.shape[0]
    fine = [jnp.concatenate([posx[:, lo:lo + n:sbc], jnp.full((ne, 1), cap, I32)], axis=1)
            for (lo, n), cap in zip(sets, caps)]
    step = sb // sbc
    base = jnp.concatenate([e[:, :-1:step].T for e in fine], axis=0)
    cnt = jnp.concatenate([(e[:, step::step] - e[:, :-1:step]).T for e in fine], axis=0)
    pc = (cnt + (BF16_SUBLANE - 1)) // BF16_SUBLANE * BF16_SUBLANE
    ends = jnp.cumsum(pc, axis=0)
    start = ends - pc
    rounds = lambda rows: jnp.max((rows + (MOE_ROUND - 1)) // MOE_ROUND, axis=1)
    flat = lambda *arrays: tuple(a.reshape(-1).astype(I32) for a in arrays)
    rounds_c = rounds(pc)
    dispatch = flat(base, start, pc, rounds_c, jnp.cumsum(rounds_c) - rounds_c)

    base_f = jnp.concatenate([e[:, :-1].T for e in fine], axis=0)
    cnt_f = jnp.concatenate([(e[:, 1:] - e[:, :-1]).T for e in fine], axis=0)
    first = jnp.repeat(start - base, step, axis=0) + base_f
    off = first % BF16_SUBLANE
    rows_f = jnp.where(cnt_f > 0, off + cnt_f, 0)
    rounds_f = rounds(rows_f)
    combine = flat(base_f - off, first - off, rows_f, rounds_f, jnp.cumsum(rounds_f) - rounds_f)
    return dispatch, combine, ends[-1].astype(I32)


def _rope_tables(n_tokens, dk):
    rows = n_tokens // GRID_W
    r, c = jnp.meshgrid(jnp.arange(rows), jnp.arange(GRID_W), indexing="ij")
    pos = jnp.stack([r.reshape(-1), c.reshape(-1)], axis=-1).astype(F32)
    nf = dk // 4
    inv_freq = ROPE_BASE ** (-jnp.arange(nf, dtype=F32) / nf)
    ang = pos[:, :, None] * inv_freq
    cos, sin = jnp.cos(ang), jnp.sin(ang)
    cos_t = jnp.concatenate([cos[:, 0], cos[:, 0], cos[:, 1], cos[:, 1]], axis=-1)
    sin_t = jnp.concatenate([-sin[:, 0], sin[:, 0], -sin[:, 1], sin[:, 1]], axis=-1)
    return cos_t, sin_t


def kernel(x_prompt, x_sample, state_ret, state_hgrn, c, c_ctx, ada_w, ada_b, norm_mix_g, norm_ffn_g, w_in,
           ret_gamma_logit, hg_lb_logit, w_ret_o, w_hg_o, w_merge, w_out, w_router, w_exp_gate, w_exp_up,
           w_exp_down, final_g):
    b_ctx, t_ctx, d = x_prompt.shape
    b_lat, t_lat, _ = x_sample.shape
    depth = w_in.shape[0]
    ret_heads, ret_dk, ret_dv = state_ret.shape[3:]
    hg_heads, hg_dk, hg_dv = state_hgrn.shape[3:]
    ne = w_router.shape[-1]
    n_ctx, n_lat = b_ctx * t_ctx, b_lat * t_lat
    n_tok = n_ctx + n_lat
    qw, vw = ret_heads * ret_dk, ret_heads * ret_dv
    kw, hw = hg_heads * hg_dk, hg_heads * hg_dv
    col_rg = 2 * qw + vw
    col_gq = col_rg + vw
    col_gf = col_gq + kw
    col_gi = col_gf + 2 * kw
    col_go = col_gi + hw
    assert col_go + hw == w_in.shape[-1] and b_lat < MOD_ROWS

    tm = 256
    assert t_ctx % tm == 0 and t_lat % tm == 0
    tm_proj = 1024 if n_tok % 1024 == 0 else tm
    sb = 1024 if (n_ctx % 4096 == 0 and n_lat % 4096 == 0) else 256
    assert n_ctx % sb == 0 and n_lat % sb == 0
    sets = ((0, n_ctx), (n_ctx, n_lat))
    caps = [CAPACITY_FACTOR * n_set // ne for _, n_set in sets]
    sbc = 256
    list_rows = sum(caps) + BF16_SUBLANE * (n_tok // sb) + MOE_ROUND
    n_tiles = max(1, list_rows // 512)
    rt = -(-list_rows // (n_tiles * BF16_SUBLANE)) * BF16_SUBLANE
    list_rows = n_tiles * rt

    def mod_rows(tmx):
        return lambda i: jnp.where(i < n_ctx // tmx, 0, 1 + (i - n_ctx // tmx) // (t_lat // tmx))

    mod_row = mod_rows(tm)
    tm_norm = 512 if (n_ctx % 512 == 0 and t_lat % 512 == 0) else tm
    norm_row = mod_rows(tm_norm)

    x = jnp.concatenate([x_prompt.reshape(n_ctx, d), x_sample.reshape(n_lat, d)], axis=0)
    cvec = jnp.zeros((MOD_ROWS, d), F32).at[0].set(c_ctx).at[1:1 + b_lat].set(c)
    mod = _mod_call(cvec, ada_w, ada_b).reshape(depth, MOD_ROWS, 1, 6 * d)

    log_gamma = jax.nn.log_sigmoid(ret_gamma_logit.astype(F32))
    p_lb = jax.nn.softmax(hg_lb_logit.astype(F32), axis=0)
    hg_lb = jnp.clip(jnp.cumsum(p_lb, axis=0) - p_lb[0:1], 0.0, 1.0 - 1e-6)
    rope = _rope_tables(t_lat, ret_dk)

    w_ret_o_b, w_hg_o_b, w_out_b = w_ret_o.astype(BF16), w_hg_o.astype(BF16), w_out.astype(BF16)
    w_merge_b = w_merge.astype(BF16)
    w_router_t = jnp.swapaxes(w_router, 1, 2)
    w_router_hi = w_router_t.astype(BF16)
    w_router_t = jnp.stack([w_router_hi, (w_router_t - w_router_hi.astype(F32)).astype(BF16)], axis=1)
    fc = min(1024, w_exp_gate.shape[-1])
    tn = min(2048, w_in.shape[-1])

    _, h = _norm_call(x, None, None, mod[0], norm_mix_g[0], norm_row, tm_norm)
    new_ret, new_hg = [], []
    for l in range(depth):
        proj = _proj_call(h, w_in, l, tm_proj, tn)

        ret_kw = dict(heads=ret_heads, dk=ret_dk, dv=ret_dv)
        orf_c, orb_c, s_ret = _ret_call(proj, log_gamma[l], None, None, tok0=0, batch=b_ctx, seq=t_ctx, **ret_kw)
        orf_l, orb_l, _ = _ret_call(proj, log_gamma[l], state_ret[:, l], rope, tok0=n_ctx, batch=b_lat, seq=t_lat,
                                    **ret_kw)
        hg_kw = dict(heads=hg_heads, dk=hg_dk, dv=hg_dv, col_q=col_gq, col_f=col_gf, col_i=col_gi)
        ogf_c, ogb_c, s_hg = _hg_call(proj, hg_lb[l], None, tok0=0, batch=b_ctx, seq=t_ctx, **hg_kw)
        ogf_l, ogb_l, _ = _hg_call(proj, hg_lb[l], state_hgrn[:, l], tok0=n_ctx, batch=b_lat, seq=t_lat, **hg_kw)
        new_ret.append(s_ret)
        new_hg.append(s_hg)

        x, h2, logits_t = _mix_out_call(
            (orf_c, orb_c, ogf_c, ogb_c), (orf_l, orb_l, ogf_l, ogb_l), proj, h, x, mod[l],
            mod_row, norm_ffn_g[l], w_ret_o_b[l], w_hg_o_b[l], w_merge_b[l], w_out_b[l], w_router_t[l], tm,
            ret_heads=ret_heads, hg_heads=hg_heads, col_rg=col_rg, col_go=col_go)

        aff, sel, posx = _route_call(logits_t, sets, caps)
        dispatch_tables, combine_tables, totals = _moe_tables(posx, sets, caps, sb, sbc)
        xe = _dispatch_call(h2, sel, posx, dispatch_tables, list_rows, sb=sb)
        ye = _expert_ffn_call(xe, totals, w_exp_gate, w_exp_up, w_exp_down, l, rt=rt, fc=fc)
        ffn = _combine_call(ye, sel, posx, aff, combine_tables, sb=sbc)

        if l + 1 < depth:
            x, h = _norm_call(x, ffn, mod[l], mod[l + 1], norm_mix_g[l + 1], norm_row, tm_norm)
        else:
            _, y_prompt = _norm_call(x, ffn, mod[l], None, final_g, norm_row, tm_norm, rows=(0, n_ctx))
            _, y_sample = _norm_call(x, ffn, mod[l], None, final_g, norm_row, tm_norm, rows=(n_ctx, n_lat))

    y_prompt = y_prompt.reshape(b_ctx, t_ctx, d)
    y_sample = y_sample.reshape(b_lat, t_lat, d)
    return y_prompt, y_sample, jnp.stack(new_ret, axis=1), jnp.stack(new_hg, axis=1)
```

```python
import functools

import jax
import jax.numpy as jnp
from jax import lax
from jax.experimental import pallas as pl
from jax.experimental.pallas import tpu as pltpu

F32 = jnp.float32
BF16 = jnp.bfloat16
I32 = jnp.int32
HIGHEST = lax.Precision.HIGHEST

NORM_EPS = 1e-6
LOG2_E = 1.4426950408889634
ROPE_BASE = 10000.0
GRID_W = 64
CAPACITY_FACTOR = 2
RET_CHUNK = 128
RET_CHUNKS_PER_STEP = 4
HG_LEVELS = 6
HG_CHUNK = 1 << HG_LEVELS
HG_ROWS_PER_STEP = 512
HG_LOG_SIGMOID_LINEAR = -60.0
MOD_ROWS = 8
LANE = 128
F32_SUBLANE = 8
BF16_SUBLANE = 16
MOE_ROUND = 64
VMEM_LIMIT = 62 * 1024 * 1024


def _cparams(*sem):
    return pltpu.CompilerParams(dimension_semantics=sem, vmem_limit_bytes=VMEM_LIMIT)


def _dot(a, b, **kw):
    return jnp.dot(a, b, preferred_element_type=F32, **kw)


def _dot_nt(a, b, **kw):
    return lax.dot_general(a, b, (((1,), (1,)), ((), ())), preferred_element_type=F32, **kw)


def _dot_tn(a, b, **kw):
    return lax.dot_general(a, b, (((0,), (0,)), ((), ())), preferred_element_type=F32, **kw)


def _silu(x):
    return x * jax.nn.sigmoid(x)


def _mod_kernel(c_ref, w_ref, b_ref, o_ref):
    s = _silu(c_ref[...])
    o_ref[...] = _dot(s, w_ref[...], precision=HIGHEST) + b_ref[...]


def _mod_call(cvec, ada_w, ada_b):
    depth, d, six_d = ada_w.shape
    tn = 6 * LANE * 2
    assert six_d % tn == 0
    return pl.pallas_call(
        _mod_kernel,
        grid=(depth, six_d // tn),
        in_specs=[
            pl.BlockSpec((MOD_ROWS, d), lambda l, j: (0, 0)),
            pl.BlockSpec((None, d, tn), lambda l, j: (l, 0, j)),
            pl.BlockSpec((None, 1, tn), lambda l, j: (l, 0, j)),
        ],
        out_specs=pl.BlockSpec((None, MOD_ROWS, tn), lambda l, j: (l, 0, j)),
        out_shape=jax.ShapeDtypeStruct((depth, MOD_ROWS, six_d), F32),
        compiler_params=_cparams("arbitrary", "arbitrary"),
        name="adaln_mod",
    )(cvec, ada_w, ada_b.reshape(depth, 1, six_d))


def _rms(x):
    return x * lax.rsqrt(jnp.mean(x * x, axis=-1, keepdims=True) + NORM_EPS)


def _norm_kernel(*refs, d, residual, final):
    it = iter(refs)
    x_ref = next(it)
    f_ref, gate_ref = (next(it), next(it)) if residual else (None, None)
    mod_ref = None if final else next(it)
    g_ref = next(it)
    xo_ref = next(it) if (residual and not final) else None
    h_ref = next(it)
    x = x_ref[...]
    if residual:
        x = x + gate_ref[...] * f_ref[...]
        if xo_ref is not None:
            xo_ref[...] = x
    y = _rms(x) * g_ref[...]
    if not final:
        y = y * (1.0 + mod_ref[:, d:2 * d]) + mod_ref[:, 0:d]
    h_ref[...] = y.astype(h_ref.dtype)


def _norm_call(x, ffn, mod_gate, mod_next, g, mod_row, tm, rows=None):
    n, d = x.shape
    residual = ffn is not None
    final = mod_next is None
    r0, n_out = (0, n) if rows is None else rows
    blk0 = r0 // tm
    row = pl.BlockSpec((tm, d), lambda i: (blk0 + i, 0))
    in_specs, args = [row], [x]
    if residual:
        in_specs += [row, pl.BlockSpec((None, 1, d), lambda i: (mod_row(blk0 + i), 0, 5))]
        args += [ffn, mod_gate]
    if not final:
        in_specs.append(pl.BlockSpec((None, 1, 2 * d), lambda i: (mod_row(blk0 + i), 0, 0)))
        args.append(mod_next)
    in_specs.append(pl.BlockSpec((1, d), lambda i: (0, 0)))
    args.append(g.reshape(1, d))
    out_row = pl.BlockSpec((tm, d), lambda i: (i, 0))
    out_specs, out_shape = [out_row], [jax.ShapeDtypeStruct((n_out, d), F32 if final else BF16)]
    if residual and not final:
        out_specs = [out_row, out_row]
        out_shape = [jax.ShapeDtypeStruct((n_out, d), F32)] + out_shape
    outs = pl.pallas_call(
        functools.partial(_norm_kernel, d=d, residual=residual, final=final),
        grid=(n_out // tm,),
        in_specs=in_specs,
        out_specs=out_specs,
        out_shape=out_shape,
        compiler_params=_cparams("arbitrary"),
        name="residual_norm",
    )(*args)
    return outs if len(outs) == 2 else (x, outs[0])


def _proj_kernel(h_ref, w_ref, o_ref, wb_ref):
    @pl.when(pl.program_id(1) == 0)
    def _():
        wb_ref[...] = w_ref[...].astype(BF16)

    o_ref[...] = _dot(h_ref[...], wb_ref[...])


def _proj_call(h, w, layer, tm, tn):
    n, d = h.shape
    cols = w.shape[-1]
    return pl.pallas_call(
        _proj_kernel,
        grid=(cols // tn, n // tm),
        in_specs=[
            pl.BlockSpec((tm, d), lambda j, i: (i, 0)),
            pl.BlockSpec((None, d, tn), lambda j, i: (layer, 0, j)),
        ],
        out_specs=pl.BlockSpec((tm, tn), lambda j, i: (i, j)),
        out_shape=jax.ShapeDtypeStruct((n, cols), F32),
        scratch_shapes=[pltpu.VMEM((d, tn), BF16)],
        compiler_params=_cparams("arbitrary", "arbitrary"),
        name="in_proj",
    )(h, w)


def _swap_half_pairs(x):
    lane = lax.broadcasted_iota(I32, x.shape, 1)
    return jnp.where((lane & 32) == 0, pltpu.roll(x, LANE - 32, 1), pltpu.roll(x, 32, 1))


def _ret_kernel(*refs, heads, dk, dv, chunk, n_chunks, has_init, has_rope):
    it = iter(refs)
    lg_ref = next(it)
    qkv = [[next(it) for _ in range(3)] for _ in range(2)]
    s0_ref = next(it) if has_init else None
    rope = [[next(it) for _ in range(2)] for _ in range(2)] if has_rope else None
    o_refs = [next(it), next(it)]
    sfin_ref = next(it)
    s_ref = next(it)

    c = pl.program_id(1)

    @pl.when(c == 0)
    def _():
        if has_init:
            s_ref[...] = s0_ref[...]
        else:
            s_ref[...] = jnp.zeros_like(s_ref)

    L = chunk
    ii = lax.broadcasted_iota(I32, (L, L), 0)
    jj = lax.broadcasted_iota(I32, (L, L), 1)
    pcol = lax.broadcasted_iota(I32, (L, 1), 0).astype(F32)
    n_sub = qkv[0][0].shape[0] // L
    jobs = [(d, hh, r) for d in range(2) for hh in range(heads) for r in range(n_sub)]
    qs, ks, vs, scores, q_in, k_out = [], [], [], [], [], []
    for d, hh, r in jobs:
        q_ref, k_ref, v_ref = qkv[d]
        rows = slice(r * L, (r + 1) * L)
        q = q_ref[rows, hh * dk:(hh + 1) * dk]
        k = k_ref[rows, hh * dk:(hh + 1) * dk] * (dk ** -0.5)
        if has_rope:
            cos, sin = rope[d][0][rows, :], rope[d][1][rows, :]
            q = q * cos + _swap_half_pairs(q) * sin
            k = k * cos + _swap_half_pairs(k) * sin
        qs.append(q)
        ks.append(k)
        vs.append(v_ref[rows, hh * dv:(hh + 1) * dv].astype(BF16))
    for (d, hh, r), q, k in zip(jobs, qs, ks):
        dif = ((ii - jj) if d == 0 else (jj - ii)).astype(F32)
        lg = lg_ref[d, hh]
        decay = jnp.where(dif >= 0.0, jnp.exp(lg * jnp.maximum(dif, 0.0)), 0.0)
        scores.append((_dot_nt(q.astype(BF16), k.astype(BF16)) * decay).astype(BF16))
    for (d, hh, r), q, k in zip(jobs, qs, ks):
        lg = lg_ref[d, hh]
        q_pow, k_pow = (pcol + 1.0, (L - 1.0) - pcol) if d == 0 else (L - pcol, pcol)
        q_in.append((q * jnp.exp(lg * q_pow)).astype(BF16))
        k_out.append((k * jnp.exp(lg * k_pow)).T.astype(BF16))
    intra = [_dot(sc, v) for sc, v in zip(scores, vs)]
    kvs = [_dot(ko, v) for ko, v in zip(k_out, vs)]
    for d in range(2):
        for hh in range(heads):
            s = s_ref[d, hh]
            chunk_decay = jnp.exp(jnp.full((1, 1), lg_ref[d, hh] * L, F32))
            for r in (range(n_sub) if d == 0 else range(n_sub - 1, -1, -1)):
                j = (d * heads + hh) * n_sub + r
                o_refs[d][r * L:(r + 1) * L, hh * dv:(hh + 1) * dv] = intra[j] + _dot(q_in[j], s.astype(BF16))
                s = s * chunk_decay + kvs[j]
            s_ref[d, hh] = s

    @pl.when(c == n_chunks - 1)
    def _():
        sfin_ref[...] = s_ref[...]


def _ret_call(proj, log_gamma, s0, rope, *, tok0, batch, seq, heads, dk, dv):
    rb = min(seq, RET_CHUNK * RET_CHUNKS_PER_STEP)
    n = seq // rb
    qw, vw = heads * dk, heads * dv
    assert seq % rb == 0 and tok0 % rb == 0 and vw % qw == 0
    r0 = tok0 // rb

    def fwd(b, c):
        return r0 + b * n + c

    def bwd(b, c):
        return r0 + b * n + (n - 1 - c)

    in_specs = [pl.BlockSpec(memory_space=pltpu.SMEM)]
    args = [log_gamma]
    for rmap in (fwd, bwd):
        in_specs += [
            pl.BlockSpec((rb, qw), lambda b, c, rmap=rmap: (rmap(b, c), 0)),
            pl.BlockSpec((rb, qw), lambda b, c, rmap=rmap: (rmap(b, c), 1)),
            pl.BlockSpec((rb, vw), lambda b, c, rmap=rmap: (rmap(b, c), (2 * qw) // vw)),
        ]
        args += [proj, proj, proj]
    state_spec = pl.BlockSpec((None, 2, heads, dk, dv), lambda b, c: (b, 0, 0, 0, 0))
    if s0 is not None:
        in_specs.append(state_spec)
        args.append(s0)
    if rope is not None:
        for cmap in (lambda b, c: (c, 0), lambda b, c: (n - 1 - c, 0)):
            in_specs += [pl.BlockSpec((rb, dk), cmap), pl.BlockSpec((rb, dk), cmap)]
            args += [rope[0], rope[1]]
    n_tok = batch * seq
    return pl.pallas_call(
        functools.partial(_ret_kernel, heads=heads, dk=dk, dv=dv, chunk=RET_CHUNK, n_chunks=n,
                          has_init=s0 is not None, has_rope=rope is not None),
        grid=(batch, n),
        in_specs=in_specs,
        out_specs=[
            pl.BlockSpec((rb, vw), lambda b, c: (b * n + c, 0)),
            pl.BlockSpec((rb, vw), lambda b, c: (b * n + (n - 1 - c), 0)),
            state_spec,
        ],
        out_shape=[
            jax.ShapeDtypeStruct((n_tok, vw), F32),
            jax.ShapeDtypeStruct((n_tok, vw), F32),
            jax.ShapeDtypeStruct((batch, 2, heads, dk, dv), F32),
        ],
        scratch_shapes=[pltpu.VMEM((2, heads, dk, dv), F32)],
        compiler_params=_cparams("arbitrary", "arbitrary"),
        name="retention_scan",
    )(*args)


def _hg_scan_index(shape, axis, rev):
    i = lax.broadcasted_iota(I32, shape, axis) & (HG_CHUNK - 1)
    return (HG_CHUNK - 1 - i) if rev else i


def _hg_window_matrix(rev):
    C = HG_CHUNK
    ti = _hg_scan_index((C, 2 * C), 0, rev)
    tj = _hg_scan_index((C, 2 * C), 1, rev)
    blocks = [tj <= ti]
    for l in range(2, HG_LEVELS + 1):
        anchor = ((ti >> l) << l) + (1 << (l - 1)) - 1
        upper = ((ti >> (l - 1)) & 1) == 1
        blocks.append((upper & (tj > anchor) & (tj <= ti)) | (~upper & (tj > ti) & (tj <= anchor)))
    blocks.append(tj > ti)
    return jnp.concatenate([jnp.where(m, 1.0, 0.0) for m in blocks], axis=0).astype(BF16)


def _hg_kernel(*refs, heads, dk, dv, tt, n_blocks, has_init):
    it = iter(refs)
    qzv = [[next(it) for _ in range(3)] for _ in range(2)]
    lb_ref = next(it)
    s0_ref = next(it) if has_init else None
    o_refs = [next(it), next(it)]
    sfin_ref = next(it)
    st_ref = next(it)
    win_ref = next(it)
    gate_a = [next(it) for _ in range(6)]
    gate_b = [next(it) for _ in range(6)]

    t = pl.program_id(1)

    @pl.when((pl.program_id(0) == 0) & (t == 0))
    def _():
        for d in range(2):
            win_ref[d] = _hg_window_matrix(d == 1)

    @pl.when(t == 0)
    def _():
        for d in range(2):
            for hh in range(heads):
                st_ref[d, hh] = s0_ref[d, hh].T if has_init else jnp.zeros((dv, dk), F32)

    C = HG_CHUNK
    n_ch = tt // C
    chunks = [slice(c * C, (c + 1) * C) for c in range(n_ch)]
    levels, uppers = [], []
    for d in range(2):
        ti = _hg_scan_index((C, C), 0, d == 1)
        tj = _hg_scan_index((C, C), 1, d == 1)
        level = jnp.where(tj < ti, 1, 0)
        for l in range(1, HG_LEVELS):
            level = level + jnp.where((tj < ti) & ((ti >> l) != (tj >> l)), 1, 0)
        t_col = _hg_scan_index((C, 1), 0, d == 1)
        levels.append(level)
        uppers.append([None] + [((t_col >> (l - 1)) & 1) == 1 for l in range(1, HG_LEVELS + 1)])

    def gates(hh, g_refs):
        qg_ref, k_ref, f_ref, diag_ref, hi_ref, lo_ref = g_refs
        ck = pl.ds(pl.multiple_of(hh * dk, LANE), dk)
        for d in range(2):
            q_ref, z_ref, _ = qzv[d]
            lb = lb_ref[d, :, ck]
            z = z_ref[:, ck]
            e = jnp.exp(-jnp.abs(z))
            inv = 1.0 / (1.0 + e)
            pos = z > 0.0
            f = lb + (1.0 - lb) * (jnp.where(pos, 1.0, e) * inv)
            k = (1.0 - lb) * (jnp.where(pos, e, 1.0) * inv)
            tiny = (lb <= 0.0) & (z < HG_LOG_SIGMOID_LINEAR)
            log2_f = jnp.where(tiny, z * LOG2_E, jnp.log2(f))
            hi = log2_f.astype(BF16)
            qg = _silu(q_ref[:, ck]) * (dk ** -0.5)
            qg_ref[d], k_ref[d], f_ref[d] = qg, k, f
            hi_ref[d] = hi
            lo_ref[d] = (log2_f - hi.astype(F32)).astype(BF16)
            diag_ref[d] = jnp.broadcast_to(jnp.sum(qg * k, axis=-1, keepdims=True), qg.shape)

    def dots(hh, g_refs):
        qg_ref, k_ref, f_ref, diag_ref, hi_ref, lo_ref = g_refs
        cv = pl.ds(pl.multiple_of(hh * dv, LANE), dv)
        jobs = [(d, r) for d in range(2) for r in chunks]
        qg, k, f, diag = ([ref[d] for d in range(2)] for ref in (qg_ref, k_ref, f_ref, diag_ref))
        v = [qzv[d][2][:, cv] for d in range(2)]
        vb = [a.astype(BF16) for a in v]
        wins = [win_ref[0], win_ref[1]]
        sums = [_dot(wins[d], jnp.concatenate([hi_ref[d, r, :], lo_ref[d, r, :]], axis=0)) for d, r in jobs]

        def query_or_key(d, r, l):
            half = 1 << (l - 1)
            if half % F32_SUBLANE:
                return jnp.where(uppers[d][l], qg[d][r], k[d][r])
            segs = []
            for s in range(C // half):
                src = qg[d] if (s % 2 == 1) != (d == 1) else k[d]
                segs.append(src[r.start + s * half:r.start + (s + 1) * half])
            return jnp.concatenate(segs, axis=0)

        pairs = []
        for (d, r), s in zip(jobs, sums):
            ps = [jnp.where(uppers[d][1], qg[d][r] * f[d][r], k[d][r]).astype(BF16)]
            for l in range(2, HG_LEVELS + 1):
                scale = jnp.exp2(s[(l - 1) * C:l * C])
                ps.append((query_or_key(d, r, l) * scale).astype(BF16))
            pairs.append(ps)
        grams = [[_dot_nt(p, p) for p in ps] for ps in pairs]
        atts = []
        for (d, r), gs in zip(jobs, grams):
            att = jnp.where(levels[d] == 1, gs[0], 0.0)
            for l in range(2, HG_LEVELS + 1):
                att = jnp.where(levels[d] == l, gs[l - 1], att)
            atts.append(att.astype(BF16))
        k_out = [(k[d][r] * jnp.exp2(s[HG_LEVELS * C:])).astype(BF16) for (d, r), s in zip(jobs, sums)]
        q_in = [(qg[d][r] * jnp.exp2(s[:C])).astype(BF16) for (d, r), s in zip(jobs, sums)]
        decays = [jnp.exp2(s[0:1] if d == 1 else s[C - 1:C]) for (d, r), s in zip(jobs, sums)]
        outs = [_dot(att, vb[d][r]) + diag[d][r] * v[d][r] for att, (d, r) in zip(atts, jobs)]
        kvs = [_dot_tn(vb[d][r], ko) for (d, r), ko in zip(jobs, k_out)]
        for d in range(2):
            st = st_ref[d, hh]
            for c in (range(n_ch - 1, -1, -1) if d == 1 else range(n_ch)):
                j = d * n_ch + c
                o_refs[d][chunks[c], cv] = outs[j] + _dot_nt(q_in[j], st.astype(BF16))
                st = st * decays[j] + kvs[j]
            st_ref[d, hh] = st

    gates(0, gate_a)

    def head_pair(i, carry):
        gates(2 * i + 1, gate_b)
        dots(2 * i, gate_a)
        dots(2 * i + 1, gate_b)
        gates(jnp.minimum(2 * i + 2, heads - 1), gate_a)
        return carry

    lax.fori_loop(0, heads // 2, head_pair, 0)

    @pl.when(t == n_blocks - 1)
    def _():
        for d in range(2):
            for hh in range(heads):
                sfin_ref[d, hh] = st_ref[d, hh].T


def _hg_call(proj, lb, s0, *, tok0, batch, seq, heads, dk, dv, col_q, col_f, col_i):
    tt = min(seq, HG_ROWS_PER_STEP)
    n = seq // tt
    kw, vw = heads * dk, heads * dv
    assert seq % tt == 0 and tok0 % tt == 0 and tt % HG_CHUNK == 0
    assert col_q % kw == 0 and col_f % kw == 0 and col_i % vw == 0
    r0 = tok0 // tt

    def fwd(b, t):
        return r0 + b * n + t

    def bwd(b, t):
        return r0 + b * n + (n - 1 - t)

    in_specs, args = [], []
    for d, rmap in enumerate((fwd, bwd)):
        in_specs += [
            pl.BlockSpec((tt, kw), lambda b, t, rmap=rmap: (rmap(b, t), col_q // kw)),
            pl.BlockSpec((tt, kw), lambda b, t, rmap=rmap, d=d: (rmap(b, t), col_f // kw + d)),
            pl.BlockSpec((tt, vw), lambda b, t, rmap=rmap: (rmap(b, t), col_i // vw)),
        ]
        args += [proj, proj, proj]
    in_specs.append(pl.BlockSpec((2, 1, kw), lambda b, t: (0, 0, 0)))
    args.append(lb.reshape(2, 1, kw))
    state_spec = pl.BlockSpec((None, 2, heads, dk, dv), lambda b, t: (b, 0, 0, 0, 0))
    if s0 is not None:
        in_specs.append(state_spec)
        args.append(s0)
    n_tok = batch * seq
    return pl.pallas_call(
        functools.partial(_hg_kernel, heads=heads, dk=dk, dv=dv, tt=tt, n_blocks=n, has_init=s0 is not None),
        grid=(batch, n),
        in_specs=in_specs,
        out_specs=[
            pl.BlockSpec((tt, vw), lambda b, t: (b * n + t, 0)),
            pl.BlockSpec((tt, vw), lambda b, t: (b * n + (n - 1 - t), 0)),
            state_spec,
        ],
        out_shape=[
            jax.ShapeDtypeStruct((n_tok, vw), F32),
            jax.ShapeDtypeStruct((n_tok, vw), F32),
            jax.ShapeDtypeStruct((batch, 2, heads, dk, dv), F32),
        ],
        scratch_shapes=[pltpu.VMEM((2, heads, dv, dk), F32),
                        pltpu.VMEM((2, (HG_LEVELS + 1) * HG_CHUNK, 2 * HG_CHUNK), BF16)]
        + 2 * ([pltpu.VMEM((2, tt, dk), F32)] * 4 + [pltpu.VMEM((2, tt, dk), BF16)] * 2),
        compiler_params=_cparams("arbitrary", "arbitrary"),
        name="hgrn2_scan",
    )(*args)


def _mix_out_kernel(*refs, d, ret_heads, hg_heads, ctx_blocks):
    scans = [refs[4 * p:4 * p + 4] for p in range(2)]
    (rg_ref, go_ref, h_ref, x_ref, mod_ref, g2_ref, wr_ref, wh_ref, wm_ref, wo_ref, wrt_ref,
     x1_ref, h2_ref, lgt_ref) = refs[8:]

    is_ctx = pl.program_id(0) < ctx_blocks

    def both_directions(i, r, cols):
        return jnp.where(is_ctx, scans[0][i][r, cols] + scans[0][i + 1][r, cols],
                         scans[1][i][r, cols] + scans[1][i + 1][r, cols])

    dvr = scans[0][0].shape[1] // ret_heads
    dvh = scans[0][2].shape[1] // hg_heads
    tm = x_ref.shape[0]
    halves = [slice(0, tm // 2), slice(tm // 2, tm)]
    ret_in, hg_in = [], []
    for r in halves:
        parts = []
        for hh in range(ret_heads):
            oh = both_directions(0, r, slice(hh * dvr, (hh + 1) * dvr))
            ctr = oh - jnp.mean(oh, axis=-1, keepdims=True)
            parts.append(ctr * lax.rsqrt(jnp.mean(ctr * ctr, axis=-1, keepdims=True) + NORM_EPS))
        ret_in.append((jnp.concatenate(parts, axis=1) * _silu(rg_ref[r, :])).astype(BF16))
        parts = [_rms(both_directions(2, r, slice(hh * dvh, (hh + 1) * dvh))) for hh in range(hg_heads)]
        hg_in.append((jnp.concatenate(parts, axis=1) * _silu(go_ref[r, :])).astype(BF16))
    ret_out = [_dot(a, wr_ref[...]) for a in ret_in]
    hg_out = [_dot(a, wh_ref[...]) for a in hg_in]
    gates = [jax.nn.sigmoid(_dot(h_ref[r, :], wm_ref[...])) for r in halves]
    merged = [(g[:, :d] * a + g[:, d:] * b).astype(BF16) for g, a, b in zip(gates, ret_out, hg_out)]
    mix = [_dot(m, wo_ref[...]) for m in merged]
    w_hi, w_lo = wrt_ref[0], wrt_ref[1]
    for r, m in zip(halves, mix):
        x1 = x_ref[r, :] + mod_ref[:, 2 * d:3 * d] * m
        x1_ref[r, :] = x1
        h2 = _rms(x1) * g2_ref[...]
        h2 = h2 * (1.0 + mod_ref[:, 4 * d:5 * d]) + mod_ref[:, 3 * d:4 * d]
        h2_ref[r, :] = h2.astype(BF16)
        h_hi = h2.astype(BF16)
        h_lo = (h2 - h_hi.astype(F32)).astype(BF16)
        lgt_ref[:, r] = _dot_nt(w_hi, h_hi) + (_dot_nt(w_hi, h_lo) + _dot_nt(w_lo, h_hi))


def _mix_out_call(scans_ctx, scans_lat, proj, h, x, mod_l, mod_row, g2, w_ret_o, w_hg_o, w_merge, w_out,
                  w_router_t, tm, *, ret_heads, hg_heads, col_rg, col_go):
    n, d = x.shape
    vr, vh = scans_ctx[0].shape[1], scans_ctx[2].shape[1]
    ne = w_router_t.shape[1]
    ctx_blocks = scans_ctx[0].shape[0] // tm
    assert col_rg % vr == 0 and col_go % vh == 0

    def row(w):
        return pl.BlockSpec((tm, w), lambda i: (i, 0))

    def full(a):
        return pl.BlockSpec(a.shape, lambda i: (0,) * a.ndim)

    ctx_row = lambda a: pl.BlockSpec((tm, a.shape[1]), lambda i: (jnp.minimum(i, ctx_blocks - 1), 0))
    lat_row = lambda a: pl.BlockSpec((tm, a.shape[1]), lambda i: (jnp.maximum(i - ctx_blocks, 0), 0))
    return pl.pallas_call(
        functools.partial(_mix_out_kernel, d=d, ret_heads=ret_heads, hg_heads=hg_heads, ctx_blocks=ctx_blocks),
        grid=(n // tm,),
        in_specs=[ctx_row(a) for a in scans_ctx] + [lat_row(a) for a in scans_lat] + [
            pl.BlockSpec((tm, vr), lambda i: (i, col_rg // vr)),
            pl.BlockSpec((tm, vh), lambda i: (i, col_go // vh)),
            row(d), row(d),
            pl.BlockSpec((None, 1, 6 * d), lambda i: (mod_row(i), 0, 0)),
            pl.BlockSpec((1, d), lambda i: (0, 0)),
            full(w_ret_o), full(w_hg_o), full(w_merge), full(w_out), full(w_router_t),
        ],
        out_specs=[row(d), row(d), pl.BlockSpec((ne, tm), lambda i: (0, i))],
        out_shape=[
            jax.ShapeDtypeStruct((n, d), F32),
            jax.ShapeDtypeStruct((n, d), BF16),
            jax.ShapeDtypeStruct((ne, n), F32),
        ],
        compiler_params=_cparams("arbitrary"),
        name="mixer_out",
    )(*scans_ctx, *scans_lat, proj, proj, h, x, mod_l, g2.reshape(1, d), w_ret_o, w_hg_o, w_merge, w_out,
      w_router_t)


def _route_kernel(lg_ref, aff_ref, sel_ref, posx_ref, flag_ref, *, sets, caps):
    for (lo, n), cap in zip(sets, caps):
        _route_set(lg_ref, aff_ref, sel_ref, posx_ref, flag_ref, lo=lo, n=n, cap=cap)


def _route_set(lg_ref, aff_ref, sel_ref, posx_ref, flag_ref, *, lo, n, cap):
    cols = slice(lo, lo + n)
    lg = lg_ref[:, cols]
    ne = lg.shape[0]
    ex = jnp.exp(lg - jnp.max(lg, axis=0, keepdims=True))
    aff = ex / jnp.sum(ex, axis=0, keepdims=True)
    aff_ref[:, cols] = aff
    bits = pltpu.bitcast(aff, I32)

    def search(i, found):
        cand = found | lax.shift_left(jnp.int32(1), 30 - i)
        cnt = jnp.sum(jnp.where(bits >= cand, 1.0, 0.0), axis=1, keepdims=True)
        return jnp.where(cnt >= cap, cand, found)

    thr = lax.fori_loop(0, 31, search, jnp.zeros((ne, 1), I32))
    gt = bits > thr
    eq = bits == thr
    need = cap - jnp.sum(jnp.where(gt, 1.0, 0.0), axis=1, keepdims=True)
    upper = jnp.where(lax.broadcasted_iota(I32, (LANE, LANE), 0) < lax.broadcasted_iota(I32, (LANE, LANE), 1),
                      1.0, 0.0).astype(BF16)

    def exclusive_count(store):
        def body(ci, carry):
            sl = pl.ds(pl.multiple_of(lo + ci * LANE, LANE), LANE)
            x = flag_ref[:, sl]
            store(sl, carry + _dot(x.astype(BF16), upper))
            return carry + jnp.sum(x, axis=1, keepdims=True)

        lax.fori_loop(0, n // LANE, body, jnp.zeros((ne, 1), F32))

    flag_ref[:, cols] = jnp.where(eq, 1.0, 0.0)

    def store_tie_rank(sl, rank):
        posx_ref[:, sl] = rank.astype(I32)

    exclusive_count(store_tie_rank)
    sel = gt | (eq & (posx_ref[:, cols].astype(F32) < need))
    sel_ref[:, cols] = jnp.where(sel, 1, 0).astype(I32)
    flag_ref[:, cols] = jnp.where(sel, 1.0, 0.0)

    def store_pos(sl, cnt):
        posx_ref[:, sl] = cnt.astype(I32)

    exclusive_count(store_pos)


def _route_call(logits_t, sets, caps):
    ne, n = logits_t.shape
    spec = pl.BlockSpec((ne, n), lambda: (0, 0))
    return pl.pallas_call(
        functools.partial(_route_kernel, sets=sets, caps=caps),
        in_specs=[spec],
        out_specs=[spec, spec, spec],
        out_shape=[
            jax.ShapeDtypeStruct((ne, n), F32),
            jax.ShapeDtypeStruct((ne, n), I32),
            jax.ShapeDtypeStruct((ne, n), I32),
        ],
        scratch_shapes=[pltpu.VMEM((ne, n), F32)],
        compiler_params=pltpu.CompilerParams(vmem_limit_bytes=VMEM_LIMIT),
        name="expert_choice_route",
    )(logits_t)


def _moe_round_masks(b, r, base_ref, sel_ref, posx_ref, ne, value_rows=None):
    n_tok = sel_ref.shape[1]
    row_iota = lax.broadcasted_iota(I32, (MOE_ROUND, n_tok), 0)
    pieces = []
    for e in range(ne):
        local = posx_ref[e:e + 1, :] - (base_ref[b * ne + e] + r * MOE_ROUND)
        hit = (sel_ref[e:e + 1, :] == 1) & (local == row_iota)
        value = 1.0 if value_rows is None else value_rows[e:e + 1, :]
        pieces.append(jnp.where(hit, value, 0.0).astype(BF16))
    return jnp.concatenate(pieces, axis=0)


def _dispatch_kernel(base_ref, start_ref, pc_ref, nr_ref, first_ref, h_ref, sel_ref, posx_ref, xe_zero_ref, xe_ref,
                     stage_ref, sem, *, ne):
    del xe_zero_ref
    b = pl.program_id(0)
    n_rounds = nr_ref[b]
    prev = jnp.maximum(b - 1, 0)
    prev_pending = (b > 0) & (nr_ref[prev] >= 1)

    def pieces(blk, r, fn):
        slot = (first_ref[blk] + r) % 2
        for e in range(ne):
            rem = pc_ref[blk * ne + e] - r * MOE_ROUND
            dst = start_ref[blk * ne + e] + r * MOE_ROUND
            for size, cond, off in (
                    (MOE_ROUND, rem >= MOE_ROUND, 0),
                    (32, (rem > 0) & (rem < MOE_ROUND) & ((rem & 32) != 0), 0),
                    (16, (rem > 0) & (rem < MOE_ROUND) & ((rem & 16) != 0), rem & 32)):
                @pl.when(cond)
                def _(size=size, off=off, e=e, dst=dst):
                    src_rows = pl.ds(pl.multiple_of(e * MOE_ROUND + off, BF16_SUBLANE), size)
                    dst_rows = pl.ds(pl.multiple_of(dst + off, BF16_SUBLANE), size)
                    fn(pltpu.make_async_copy(stage_ref.at[slot, src_rows], xe_ref.at[e, dst_rows], sem.at[slot]))

    def wait_previous_block():
        pieces(prev, nr_ref[prev] - 1, lambda cp: cp.wait())

    def round_body(r, carry):
        onehot = _moe_round_masks(b, r, base_ref, sel_ref, posx_ref, ne)
        stage_ref[(first_ref[b] + r) % 2] = _dot(onehot, h_ref[...]).astype(BF16)

        @pl.when(r >= 1)
        def _():
            pieces(b, r - 1, lambda cp: cp.wait())

        @pl.when((r == 0) & prev_pending)
        def _():
            wait_previous_block()

        pieces(b, r, lambda cp: cp.start())
        return carry

    lax.fori_loop(0, n_rounds, round_body, 0)

    @pl.when((n_rounds == 0) & prev_pending)
    def _():
        wait_previous_block()

    @pl.when((n_rounds >= 1) & (b == pl.num_programs(0) - 1))
    def _():
        pieces(b, n_rounds - 1, lambda cp: cp.wait())


def _dispatch_call(h2, sel, posx, tables, list_rows, *, sb):
    n, d = h2.shape
    ne = sel.shape[0]
    xe_zero = jnp.zeros((ne, list_rows, d), BF16)
    grid_spec = pltpu.PrefetchScalarGridSpec(
        num_scalar_prefetch=len(tables),
        grid=(n // sb,),
        in_specs=[
            pl.BlockSpec((sb, d), lambda b, *_: (b, 0)),
            pl.BlockSpec((ne, sb), lambda b, *_: (0, b)),
            pl.BlockSpec((ne, sb), lambda b, *_: (0, b)),
            pl.BlockSpec(memory_space=pl.ANY),
        ],
        out_specs=pl.BlockSpec(memory_space=pl.ANY),
        scratch_shapes=[pltpu.VMEM((2, ne * MOE_ROUND, d), BF16), pltpu.SemaphoreType.DMA((2,))],
    )
    return pl.pallas_call(
        functools.partial(_dispatch_kernel, ne=ne),
        grid_spec=grid_spec,
        out_shape=jax.ShapeDtypeStruct((ne, list_rows, d), BF16),
        input_output_aliases={len(tables) + 3: 0},
        compiler_params=_cparams("arbitrary"),
        name="expert_dispatch",
    )(*tables, h2, sel, posx, xe_zero)


def _expert_ffn_kernel(tot_ref, x_ref, wg_ref, wu_ref, wd_ref, y_ref, wgf_ref, wuf_ref, wdf_ref, wgb_ref, wub_ref,
                       wdb_ref, acc_ref, sem, *, layer, rt, n_ff):
    e, f, t = pl.program_id(0), pl.program_id(1), pl.program_id(2)
    fc = wgb_ref.shape[1]
    group = e * n_ff + f
    n_groups = pl.num_programs(0) * n_ff

    def weight_copies(g, fn):
        ge, gf, slot = g // n_ff, g % n_ff, g % 2
        cols = pl.ds(pl.multiple_of(gf * fc, LANE), fc)
        fn(pltpu.make_async_copy(wg_ref.at[layer, ge, :, cols], wgf_ref.at[slot], sem.at[slot]))
        fn(pltpu.make_async_copy(wu_ref.at[layer, ge, :, cols], wuf_ref.at[slot], sem.at[slot]))
        fn(pltpu.make_async_copy(wd_ref.at[layer, ge, cols, :], wdf_ref.at[slot], sem.at[slot]))

    @pl.when(t == 0)
    def _():
        @pl.when(group == 0)
        def _():
            weight_copies(group, lambda cp: cp.start())

        weight_copies(group, lambda cp: cp.wait())

        @pl.when(group + 1 < n_groups)
        def _():
            weight_copies(jnp.minimum(group + 1, n_groups - 1), lambda cp: cp.start())

        slot = group % 2
        wgb_ref[...] = wgf_ref[slot].astype(BF16)
        wub_ref[...] = wuf_ref[slot].astype(BF16)
        wdb_ref[...] = wdf_ref[slot].astype(BF16)

    rows = pl.ds(pl.multiple_of(t * rt, rt), rt)
    live = t * rt < tot_ref[e]

    @pl.when(live)
    def _():
        x = x_ref[...]
        hid = (_silu(_dot(x, wgb_ref[...])) * _dot(x, wub_ref[...])).astype(BF16)
        y = _dot(hid, wdb_ref[...])

        @pl.when(f == 0)
        def _():
            acc_ref[rows, :] = y

        @pl.when(f != 0)
        def _():
            acc_ref[rows, :] += y

    @pl.when((f == n_ff - 1) & live)
    def _():
        y_ref[...] = acc_ref[rows, :].astype(BF16)

    @pl.when((f == n_ff - 1) & jnp.logical_not(live))
    def _():
        y_ref[...] = jnp.zeros_like(y_ref)


def _expert_ffn_call(xe, totals, wg, wu, wd, layer, *, rt, fc):
    ne, list_rows, d = xe.shape
    ff = wg.shape[-1]
    n_ff = ff // fc
    grid_spec = pltpu.PrefetchScalarGridSpec(
        num_scalar_prefetch=1,
        grid=(ne, n_ff, list_rows // rt),
        in_specs=[
            pl.BlockSpec((None, rt, d), lambda e, f, t, *_: (e, t, 0)),
            pl.BlockSpec(memory_space=pl.ANY),
            pl.BlockSpec(memory_space=pl.ANY),
            pl.BlockSpec(memory_space=pl.ANY),
        ],
        out_specs=pl.BlockSpec((None, rt, d), lambda e, f, t, *_: (e, jnp.where(f == n_ff - 1, t, 0), 0)),
        scratch_shapes=[pltpu.VMEM((2, d, fc), F32), pltpu.VMEM((2, d, fc), F32), pltpu.VMEM((2, fc, d), F32),
                        pltpu.VMEM((d, fc), BF16), pltpu.VMEM((d, fc), BF16), pltpu.VMEM((fc, d), BF16),
                        pltpu.VMEM((list_rows, d), F32), pltpu.SemaphoreType.DMA((2,))],
    )
    return pl.pallas_call(
        functools.partial(_expert_ffn_kernel, layer=layer, rt=rt, n_ff=n_ff),
        grid_spec=grid_spec,
        out_shape=jax.ShapeDtypeStruct((ne, list_rows, d), BF16),
        compiler_params=_cparams("arbitrary", "arbitrary", "arbitrary"),
        name="expert_ffn",
    )(totals, xe, wg, wu, wd)


def _combine_kernel(base_ref, start_ref, pc_ref, nr_ref, first_ref, sel_ref, posx_ref, aff_ref, y_ref, out_ref,
                    ybuf_ref, sem, *, ne):
    b = pl.program_id(0)
    last = pl.num_programs(0) - 1
    n_rounds = nr_ref[b]
    first = first_ref[b]

    def windows(blk, r, fn):
        slot = (first_ref[blk] + r) % 2
        for e in range(ne):
            @pl.when(pc_ref[blk * ne + e] - r * MOE_ROUND > 0)
            def _(e=e):
                src = pl.ds(pl.multiple_of(start_ref[blk * ne + e] + r * MOE_ROUND, BF16_SUBLANE), MOE_ROUND)
                fn(pltpu.make_async_copy(y_ref.at[e, src], ybuf_ref.at[slot, pl.ds(e * MOE_ROUND, MOE_ROUND)],
                                         sem.at[slot]))

    def start_next_block():
        @pl.when(b < last)
        def _():
            windows(jnp.minimum(b + 1, last), 0, lambda cp: cp.start())

    @pl.when(b == 0)
    def _():
        ybuf_ref[...] = jnp.zeros_like(ybuf_ref)
        windows(0, 0, lambda cp: cp.start())

    out_ref[...] = jnp.zeros_like(out_ref)

    @pl.when(n_rounds == 0)
    def _():
        start_next_block()

    def round_body(r, carry):
        @pl.when(r + 1 < n_rounds)
        def _():
            windows(b, r + 1, lambda cp: cp.start())

        @pl.when(r + 1 == n_rounds)
        def _():
            start_next_block()

        windows(b, r, lambda cp: cp.wait())
        weights = _moe_round_masks(b, r, base_ref, sel_ref, posx_ref, ne, value_rows=aff_ref)
        out_ref[...] += _dot_tn(weights, ybuf_ref[(first + r) % 2])
        return carry

    lax.fori_loop(0, n_rounds, round_body, 0)


def _combine_call(ye, sel, posx, aff, tables, *, sb):
    ne, _, d = ye.shape
    n = sel.shape[1]
    grid_spec = pltpu.PrefetchScalarGridSpec(
        num_scalar_prefetch=len(tables),
        grid=(n // sb,),
        in_specs=[
            pl.BlockSpec((ne, sb), lambda b, *_: (0, b)),
            pl.BlockSpec((ne, sb), lambda b, *_: (0, b)),
            pl.BlockSpec((ne, sb), lambda b, *_: (0, b)),
            pl.BlockSpec(memory_space=pl.ANY),
        ],
        out_specs=pl.BlockSpec((sb, d), lambda b, *_: (b, 0)),
        scratch_shapes=[pltpu.VMEM((2, ne * MOE_ROUND, d), BF16), pltpu.SemaphoreType.DMA((2,))],
    )
    return pl.pallas_call(
        functools.partial(_combine_kernel, ne=ne),
        grid_spec=grid_spec,
        out_shape=jax.ShapeDtypeStruct((n, d), F32),
        compiler_params=_cparams("arbitrary"),
        name="expert_combine",
    )(*tables, sel, posx, aff, ye)


def _moe_tables(posx, sets, caps, sb, sbc):
    ne = posx.shape[0]
    fine = [jnp.concatenate([posx[:, lo:lo + n:sbc], jnp.full((ne, 1), cap, I32)], axis=1)
            for (lo, n), cap in zip(sets, caps)]
    step = sb // sbc
    base = jnp.concatenate([e[:, :-1:step].T for e in fine], axis=0)
    cnt = jnp.concatenate([(e[:, step::step] - e[:, :-1:step]).T for e in fine], axis=0)
    pc = (cnt + (BF16_SUBLANE - 1)) // BF16_SUBLANE * BF16_SUBLANE
    ends = jnp.cumsum(pc, axis=0)
    start = ends - pc
    rounds = lambda rows: jnp.max((rows + (MOE_ROUND - 1)) // MOE_ROUND, axis=1)
    flat = lambda *arrays: tuple(a.reshape(-1).astype(I32) for a in arrays)
    rounds_c = rounds(pc)
    dispatch = flat(base, start, pc, rounds_c, jnp.cumsum(rounds_c) - rounds_c)

    base_f = jnp.concatenate([e[:, :-1].T for e in fine], axis=0)
    cnt_f = jnp.concatenate([(e[:, 1:] - e[:, :-1]).T for e in fine], axis=0)
    first = jnp.repeat(start - base, step, axis=0) + base_f
    off = first % BF16_SUBLANE
    rows_f = jnp.where(cnt_f > 0, off + cnt_f, 0)
    rounds_f = rounds(rows_f)
    combine = flat(base_f - off, first - off, rows_f, rounds_f, jnp.cumsum(rounds_f) - rounds_f)
    return dispatch, combine, ends[-1].astype(I32)


def _rope_tables(n_tokens, dk):
    rows = n_tokens // GRID_W
    r, c = jnp.meshgrid(jnp.arange(rows), jnp.arange(GRID_W), indexing="ij")
    pos = jnp.stack([r.reshape(-1), c.reshape(-1)], axis=-1).astype(F32)
    nf = dk // 4
    inv_freq = ROPE_BASE ** (-jnp.arange(nf, dtype=F32) / nf)
    ang = pos[:, :, None] * inv_freq
    cos, sin = jnp.cos(ang), jnp.sin(ang)
    cos_t = jnp.concatenate([cos[:, 0], cos[:, 0], cos[:, 1], cos[:, 1]], axis=-1)
    sin_t = jnp.concatenate([-sin[:, 0], sin[:, 0], -sin[:, 1], sin[:, 1]], axis=-1)
    return cos_t, sin_t


def kernel(x_prompt, x_sample, state_ret, state_hgrn, c, c_ctx, ada_w, ada_b, norm_mix_g, norm_ffn_g, w_in,
           ret_gamma_logit, hg_lb_logit, w_ret_o, w_hg_o, w_merge, w_out, w_router, w_exp_gate, w_exp_up,
           w_exp_down, final_g):
    b_ctx, t_ctx, d = x_prompt.shape
    b_lat, t_lat, _ = x_sample.shape
    depth = w_in.shape[0]
    ret_heads, ret_dk, ret_dv = state_ret.shape[3:]
    hg_heads, hg_dk, hg_dv = state_hgrn.shape[3:]
    ne = w_router.shape[-1]
    n_ctx, n_lat = b_ctx * t_ctx, b_lat * t_lat
    n_tok = n_ctx + n_lat
    qw, vw = ret_heads * ret_dk, ret_heads * ret_dv
    kw, hw = hg_heads * hg_dk, hg_heads * hg_dv
    col_rg = 2 * qw + vw
    col_gq = col_rg + vw
    col_gf = col_gq + kw
    col_gi = col_gf + 2 * kw
    col_go = col_gi + hw
    assert col_go + hw == w_in.shape[-1] and b_lat < MOD_ROWS

    tm = 256
    assert t_ctx % tm == 0 and t_lat % tm == 0
    tm_proj = 1024 if n_tok % 1024 == 0 else tm
    sb = 1024 if (n_ctx % 4096 == 0 and n_lat % 4096 == 0) else 256
    assert n_ctx % sb == 0 and n_lat % sb == 0
    sets = ((0, n_ctx), (n_ctx, n_lat))
    caps = [CAPACITY_FACTOR * n_set // ne for _, n_set in sets]
    sbc = 256
    list_rows = sum(caps) + BF16_SUBLANE * (n_tok // sb) + MOE_ROUND
    n_tiles = max(1, list_rows // 512)
    rt = -(-list_rows // (n_tiles * BF16_SUBLANE)) * BF16_SUBLANE
    list_rows = n_tiles * rt

    def mod_rows(tmx):
        return lambda i: jnp.where(i < n_ctx // tmx, 0, 1 + (i - n_ctx // tmx) // (t_lat // tmx))

    mod_row = mod_rows(tm)
    tm_norm = 512 if (n_ctx % 512 == 0 and t_lat % 512 == 0) else tm
    norm_row = mod_rows(tm_norm)

    x = jnp.concatenate([x_prompt.reshape(n_ctx, d), x_sample.reshape(n_lat, d)], axis=0)
    cvec = jnp.zeros((MOD_ROWS, d), F32).at[0].set(c_ctx).at[1:1 + b_lat].set(c)
    mod = _mod_call(cvec, ada_w, ada_b).reshape(depth, MOD_ROWS, 1, 6 * d)

    log_gamma = jax.nn.log_sigmoid(ret_gamma_logit.astype(F32))
    p_lb = jax.nn.softmax(hg_lb_logit.astype(F32), axis=0)
    hg_lb = jnp.clip(jnp.cumsum(p_lb, axis=0) - p_lb[0:1], 0.0, 1.0 - 1e-6)
    rope = _rope_tables(t_lat, ret_dk)

    w_ret_o_b, w_hg_o_b, w_out_b = w_ret_o.astype(BF16), w_hg_o.astype(BF16), w_out.astype(BF16)
    w_merge_b = w_merge.astype(BF16)
    w_router_t = jnp.swapaxes(w_router, 1, 2)
    w_router_hi = w_router_t.astype(BF16)
    w_router_t = jnp.stack([w_router_hi, (w_router_t - w_router_hi.astype(F32)).astype(BF16)], axis=1)
    fc = min(1024, w_exp_gate.shape[-1])
    tn = min(2048, w_in.shape[-1])

    _, h = _norm_call(x, None, None, mod[0], norm_mix_g[0], norm_row, tm_norm)
    new_ret, new_hg = [], []
    for l in range(depth):
        proj = _proj_call(h, w_in, l, tm_proj, tn)

        ret_kw = dict(heads=ret_heads, dk=ret_dk, dv=ret_dv)
        orf_c, orb_c, s_ret = _ret_call(proj, log_gamma[l], None, None, tok0=0, batch=b_ctx, seq=t_ctx, **ret_kw)
        orf_l, orb_l, _ = _ret_call(proj, log_gamma[l], state_ret[:, l], rope, tok0=n_ctx, batch=b_lat, seq=t_lat,
                                    **ret_kw)
        hg_kw = dict(heads=hg_heads, dk=hg_dk, dv=hg_dv, col_q=col_gq, col_f=col_gf, col_i=col_gi)
        ogf_c, ogb_c, s_hg = _hg_call(proj, hg_lb[l], None, tok0=0, batch=b_ctx, seq=t_ctx, **hg_kw)
        ogf_l, ogb_l, _ = _hg_call(proj, hg_lb[l], state_hgrn[:, l], tok0=n_ctx, batch=b_lat, seq=t_lat, **hg_kw)
        new_ret.append(s_ret)
        new_hg.append(s_hg)

        x, h2, logits_t = _mix_out_call(
            (orf_c, orb_c, ogf_c, ogb_c), (orf_l, orb_l, ogf_l, ogb_l), proj, h, x, mod[l],
            mod_row, norm_ffn_g[l], w_ret_o_b[l], w_hg_o_b[l], w_merge_b[l], w_out_b[l], w_router_t[l], tm,
            ret_heads=ret_heads, hg_heads=hg_heads, col_rg=col_rg, col_go=col_go)

        aff, sel, posx = _route_call(logits_t, sets, caps)
        dispatch_tables, combine_tables, totals = _moe_tables(posx, sets, caps, sb, sbc)
        xe = _dispatch_call(h2, sel, posx, dispatch_tables, list_rows, sb=sb)
        ye = _expert_ffn_call(xe, totals, w_exp_gate, w_exp_up, w_exp_down, l, rt=rt, fc=fc)
        ffn = _combine_call(ye, sel, posx, aff, combine_tables, sb=sbc)

        if l + 1 < depth:
            x, h = _norm_call(x, ffn, mod[l], mod[l + 1], norm_mix_g[l + 1], norm_row, tm_norm)
        else:
            _, y_prompt = _norm_call(x, ffn, mod[l], None, final_g, norm_row, tm_norm, rows=(0, n_ctx))
            _, y_sample = _norm_call(x, ffn, mod[l], None, final_g, norm_row, tm_norm, rows=(n_ctx, n_lat))

    y_prompt = y_prompt.reshape(b_ctx, t_ctx, d)
    y_sample = y_sample.reshape(b_lat, t_lat, d)
    return y_prompt, y_sample, jnp.stack(new_ret, axis=1), jnp.stack(new_hg, axis=1)
```

```python
import functools

import jax
import jax.numpy as jnp
from jax import lax
from jax.experimental import pallas as pl
from jax.experimental.pallas import tpu as pltpu

F32 = jnp.float32
BF16 = jnp.bfloat16
I32 = jnp.int32
HIGHEST = lax.Precision.HIGHEST

NORM_EPS = 1e-6
LOG2_E = 1.4426950408889634
ROPE_BASE = 10000.0
GRID_W = 64
CAPACITY_FACTOR = 2
RET_CHUNK = 128
RET_CHUNKS_PER_STEP = 4
HG_LEVELS = 6
HG_CHUNK = 1 << HG_LEVELS
HG_ROWS_PER_STEP = 512
HG_LOG_SIGMOID_LINEAR = -60.0
MOD_ROWS = 8
LANE = 128
F32_SUBLANE = 8
BF16_SUBLANE = 16
MOE_ROUND = 64
VMEM_LIMIT = 62 * 1024 * 1024


def _cparams(*sem):
    return pltpu.CompilerParams(dimension_semantics=sem, vmem_limit_bytes=VMEM_LIMIT)


def _dot(a, b, **kw):
    return jnp.dot(a, b, preferred_element_type=F32, **kw)


def _dot_nt(a, b, **kw):
    return lax.dot_general(a, b, (((1,), (1,)), ((), ())), preferred_element_type=F32, **kw)


def _dot_tn(a, b, **kw):
    return lax.dot_general(a, b, (((0,), (0,)), ((), ())), preferred_element_type=F32, **kw)


def _silu(x):
    return x * jax.nn.sigmoid(x)


def _mod_kernel(c_ref, w_ref, b_ref, o_ref):
    s = _silu(c_ref[...])
    o_ref[...] = _dot(s, w_ref[...], precision=HIGHEST) + b_ref[...]


def _mod_call(cvec, ada_w, ada_b):
    depth, d, six_d = ada_w.shape
    tn = 6 * LANE * 2
    assert six_d % tn == 0
    return pl.pallas_call(
        _mod_kernel,
        grid=(depth, six_d // tn),
        in_specs=[
            pl.BlockSpec((MOD_ROWS, d), lambda l, j: (0, 0)),
            pl.BlockSpec((None, d, tn), lambda l, j: (l, 0, j)),
            pl.BlockSpec((None, 1, tn), lambda l, j: (l, 0, j)),
        ],
        out_specs=pl.BlockSpec((None, MOD_ROWS, tn), lambda l, j: (l, 0, j)),
        out_shape=jax.ShapeDtypeStruct((depth, MOD_ROWS, six_d), F32),
        compiler_params=_cparams("arbitrary", "arbitrary"),
        name="adaln_mod",
    )(cvec, ada_w, ada_b.reshape(depth, 1, six_d))


def _rms(x):
    return x * lax.rsqrt(jnp.mean(x * x, axis=-1, keepdims=True) + NORM_EPS)


def _norm_kernel(*refs, d, residual, final):
    it = iter(refs)
    x_ref = next(it)
    f_ref, gate_ref = (next(it), next(it)) if residual else (None, None)
    mod_ref = None if final else next(it)
    g_ref = next(it)
    xo_ref = next(it) if (residual and not final) else None
    h_ref = next(it)
    x = x_ref[...]
    if residual:
        x = x + gate_ref[...] * f_ref[...]
        if xo_ref is not None:
            xo_ref[...] = x
    y = _rms(x) * g_ref[...]
    if not final:
        y = y * (1.0 + mod_ref[:, d:2 * d]) + mod_ref[:, 0:d]
    h_ref[...] = y.astype(h_ref.dtype)


def _norm_call(x, ffn, mod_gate, mod_next, g, mod_row, tm, rows=None):
    n, d = x.shape
    residual = ffn is not None
    final = mod_next is None
    r0, n_out = (0, n) if rows is None else rows
    blk0 = r0 // tm
    row = pl.BlockSpec((tm, d), lambda i: (blk0 + i, 0))
    in_specs, args = [row], [x]
    if residual:
        in_specs += [row, pl.BlockSpec((None, 1, d), lambda i: (mod_row(blk0 + i), 0, 5))]
        args += [ffn, mod_gate]
    if not final:
        in_specs.append(pl.BlockSpec((None, 1, 2 * d), lambda i: (mod_row(blk0 + i), 0, 0)))
        args.append(mod_next)
    in_specs.append(pl.BlockSpec((1, d), lambda i: (0, 0)))
    args.append(g.reshape(1, d))
    out_row = pl.BlockSpec((tm, d), lambda i: (i, 0))
    out_specs, out_shape = [out_row], [jax.ShapeDtypeStruct((n_out, d), F32 if final else BF16)]
    if residual and not final:
        out_specs = [out_row, out_row]
        out_shape = [jax.ShapeDtypeStruct((n_out, d), F32)] + out_shape
    outs = pl.pallas_call(
        functools.partial(_norm_kernel, d=d, residual=residual, final=final),
        grid=(n_out // tm,),
        in_specs=in_specs,
        out_specs=out_specs,
        out_shape=out_shape,
        compiler_params=_cparams("arbitrary"),
        name="residual_norm",
    )(*args)
    return outs if len(outs) == 2 else (x, outs[0])


def _proj_kernel(h_ref, w_ref, o_ref, wb_ref):
    @pl.when(pl.program_id(1) == 0)
    def _():
        wb_ref[...] = w_ref[...].astype(BF16)

    o_ref[...] = _dot(h_ref[...], wb_ref[...])


def _proj_call(h, w, layer, tm, tn):
    n, d = h.shape
    cols = w.shape[-1]
    return pl.pallas_call(
        _proj_kernel,
        grid=(cols // tn, n // tm),
        in_specs=[
            pl.BlockSpec((tm, d), lambda j, i: (i, 0)),
            pl.BlockSpec((None, d, tn), lambda j, i: (layer, 0, j)),
        ],
        out_specs=pl.BlockSpec((tm, tn), lambda j, i: (i, j)),
        out_shape=jax.ShapeDtypeStruct((n, cols), F32),
        scratch_shapes=[pltpu.VMEM((d, tn), BF16)],
        compiler_params=_cparams("arbitrary", "arbitrary"),
        name="in_proj",
    )(h, w)


def _swap_half_pairs(x):
    lane = lax.broadcasted_iota(I32, x.shape, 1)
    return jnp.where((lane & 32) == 0, pltpu.roll(x, LANE - 32, 1), pltpu.roll(x, 32, 1))


def _ret_kernel(*refs, heads, dk, dv, chunk, n_chunks, has_init, has_rope):
    it = iter(refs)
    lg_ref = next(it)
    qkv = [[next(it) for _ in range(3)] for _ in range(2)]
    s0_ref = next(it) if has_init else None
    rope = [[next(it) for _ in range(2)] for _ in range(2)] if has_rope else None
    o_refs = [next(it), next(it)]
    sfin_ref = next(it)
    s_ref = next(it)

    c = pl.program_id(1)

    @pl.when(c == 0)
    def _():
        if has_init:
            s_ref[...] = s0_ref[...]
        else:
            s_ref[...] = jnp.zeros_like(s_ref)

    L = chunk
    ii = lax.broadcasted_iota(I32, (L, L), 0)
    jj = lax.broadcasted_iota(I32, (L, L), 1)
    pcol = lax.broadcasted_iota(I32, (L, 1), 0).astype(F32)
    n_sub = qkv[0][0].shape[0] // L
    jobs = [(d, hh, r) for d in range(2) for hh in range(heads) for r in range(n_sub)]
    qs, ks, vs, scores, q_in, k_out = [], [], [], [], [], []
    for d, hh, r in jobs:
        q_ref, k_ref, v_ref = qkv[d]
        rows = slice(r * L, (r + 1) * L)
        q = q_ref[rows, hh * dk:(hh + 1) * dk]
        k = k_ref[rows, hh * dk:(hh + 1) * dk] * (dk ** -0.5)
        if has_rope:
            cos, sin = rope[d][0][rows, :], rope[d][1][rows, :]
            q = q * cos + _swap_half_pairs(q) * sin
            k = k * cos + _swap_half_pairs(k) * sin
        qs.append(q)
        ks.append(k)
        vs.append(v_ref[rows, hh * dv:(hh + 1) * dv].astype(BF16))
    for (d, hh, r), q, k in zip(jobs, qs, ks):
        dif = ((ii - jj) if d == 0 else (jj - ii)).astype(F32)
        lg = lg_ref[d, hh]
        decay = jnp.where(dif >= 0.0, jnp.exp(lg * jnp.maximum(dif, 0.0)), 0.0)
        scores.append((_dot_nt(q.astype(BF16), k.astype(BF16)) * decay).astype(BF16))
    for (d, hh, r), q, k in zip(jobs, qs, ks):
        lg = lg_ref[d, hh]
        q_pow, k_pow = (pcol + 1.0, (L - 1.0) - pcol) if d == 0 else (L - pcol, pcol)
        q_in.append((q * jnp.exp(lg * q_pow)).astype(BF16))
        k_out.append((k * jnp.exp(lg * k_pow)).T.astype(BF16))
    intra = [_dot(sc, v) for sc, v in zip(scores, vs)]
    kvs = [_dot(ko, v) for ko, v in zip(k_out, vs)]
    for d in range(2):
        for hh in range(heads):
            s = s_ref[d, hh]
            chunk_decay = jnp.exp(jnp.full((1, 1), lg_ref[d, hh] * L, F32))
            for r in (range(n_sub) if d == 0 else range(n_sub - 1, -1, -1)):
                j = (d * heads + hh) * n_sub + r
                o_refs[d][r * L:(r + 1) * L, hh * dv:(hh + 1) * dv] = intra[j] + _dot(q_in[j], s.astype(BF16))
                s = s * chunk_decay + kvs[j]
            s_ref[d, hh] = s

    @pl.when(c == n_chunks - 1)
    def _():
        sfin_ref[...] = s_ref[...]


def _ret_call(proj, log_gamma, s0, rope, *, tok0, batch, seq, heads, dk, dv):
    rb = min(seq, RET_CHUNK * RET_CHUNKS_PER_STEP)
    n = seq // rb
    qw, vw = heads * dk, heads * dv
    assert seq % rb == 0 and tok0 % rb == 0 and vw % qw == 0
    r0 = tok0 // rb

    def fwd(b, c):
        return r0 + b * n + c

    def bwd(b, c):
        return r0 + b * n + (n - 1 - c)

    in_specs = [pl.BlockSpec(memory_space=pltpu.SMEM)]
    args = [log_gamma]
    for rmap in (fwd, bwd):
        in_specs += [
            pl.BlockSpec((rb, qw), lambda b, c, rmap=rmap: (rmap(b, c), 0)),
            pl.BlockSpec((rb, qw), lambda b, c, rmap=rmap: (rmap(b, c), 1)),
            pl.BlockSpec((rb, vw), lambda b, c, rmap=rmap: (rmap(b, c), (2 * qw) // vw)),
        ]
        args += [proj, proj, proj]
    state_spec = pl.BlockSpec((None, 2, heads, dk, dv), lambda b, c: (b, 0, 0, 0, 0))
    if s0 is not None:
        in_specs.append(state_spec)
        args.append(s0)
    if rope is not None:
        for cmap in (lambda b, c: (c, 0), lambda b, c: (n - 1 - c, 0)):
            in_specs += [pl.BlockSpec((rb, dk), cmap), pl.BlockSpec((rb, dk), cmap)]
            args += [rope[0], rope[1]]
    n_tok = batch * seq
    return pl.pallas_call(
        functools.partial(_ret_kernel, heads=heads, dk=dk, dv=dv, chunk=RET_CHUNK, n_chunks=n,
                          has_init=s0 is not None, has_rope=rope is not None),
        grid=(batch, n),
        in_specs=in_specs,
        out_specs=[
            pl.BlockSpec((rb, vw), lambda b, c: (b * n + c, 0)),
            pl.BlockSpec((rb, vw), lambda b, c: (b * n + (n - 1 - c), 0)),
            state_spec,
        ],
        out_shape=[
            jax.ShapeDtypeStruct((n_tok, vw), F32),
            jax.ShapeDtypeStruct((n_tok, vw), F32),
            jax.ShapeDtypeStruct((batch, 2, heads, dk, dv), F32),
        ],
        scratch_shapes=[pltpu.VMEM((2, heads, dk, dv), F32)],
        compiler_params=_cparams("arbitrary", "arbitrary"),
        name="retention_scan",
    )(*args)


def _hg_scan_index(shape, axis, rev):
    i = lax.broadcasted_iota(I32, shape, axis) & (HG_CHUNK - 1)
    return (HG_CHUNK - 1 - i) if rev else i


def _hg_window_matrix(rev):
    C = HG_CHUNK
    ti = _hg_scan_index((C, 2 * C), 0, rev)
    tj = _hg_scan_index((C, 2 * C), 1, rev)
    blocks = [tj <= ti]
    for l in range(2, HG_LEVELS + 1):
        anchor = ((ti >> l) << l) + (1 << (l - 1)) - 1
        upper = ((ti >> (l - 1)) & 1) == 1
        blocks.append((upper & (tj > anchor) & (tj <= ti)) | (~upper & (tj > ti) & (tj <= anchor)))
    blocks.append(tj > ti)
    return jnp.concatenate([jnp.where(m, 1.0, 0.0) for m in blocks], axis=0).astype(BF16)


def _hg_kernel(*refs, heads, dk, dv, tt, n_blocks, has_init):
    it = iter(refs)
    qzv = [[next(it) for _ in range(3)] for _ in range(2)]
    lb_ref = next(it)
    s0_ref = next(it) if has_init else None
    o_refs = [next(it), next(it)]
    sfin_ref = next(it)
    st_ref = next(it)
    win_ref = next(it)
    gate_a = [next(it) for _ in range(6)]
    gate_b = [next(it) for _ in range(6)]

    t = pl.program_id(1)

    @pl.when((pl.program_id(0) == 0) & (t == 0))
    def _():
        for d in range(2):
            win_ref[d] = _hg_window_matrix(d == 1)

    @pl.when(t == 0)
    def _():
        for d in range(2):
            for hh in range(heads):
                st_ref[d, hh] = s0_ref[d, hh].T if has_init else jnp.zeros((dv, dk), F32)

    C = HG_CHUNK
    n_ch = tt // C
    chunks = [slice(c * C, (c + 1) * C) for c in range(n_ch)]
    levels, uppers = [], []
    for d in range(2):
        ti = _hg_scan_index((C, C), 0, d == 1)
        tj = _hg_scan_index((C, C), 1, d == 1)
        level = jnp.where(tj < ti, 1, 0)
        for l in range(1, HG_LEVELS):
            level = level + jnp.where((tj < ti) & ((ti >> l) != (tj >> l)), 1, 0)
        t_col = _hg_scan_index((C, 1), 0, d == 1)
        levels.append(level)
        uppers.append([None] + [((t_col >> (l - 1)) & 1) == 1 for l in range(1, HG_LEVELS + 1)])

    def gates(hh, g_refs):
        qg_ref, k_ref, f_ref, diag_ref, hi_ref, lo_ref = g_refs
        ck = pl.ds(pl.multiple_of(hh * dk, LANE), dk)
        for d in range(2):
            q_ref, z_ref, _ = qzv[d]
            lb = lb_ref[d, :, ck]
            z = z_ref[:, ck]
            e = jnp.exp(-jnp.abs(z))
            inv = 1.0 / (1.0 + e)
            pos = z > 0.0
            f = lb + (1.0 - lb) * (jnp.where(pos, 1.0, e) * inv)
            k = (1.0 - lb) * (jnp.where(pos, e, 1.0) * inv)
            tiny = (lb <= 0.0) & (z < HG_LOG_SIGMOID_LINEAR)
            log2_f = jnp.where(tiny, z * LOG2_E, jnp.log2(f))
            hi = log2_f.astype(BF16)
            qg = _silu(q_ref[:, ck]) * (dk ** -0.5)
            qg_ref[d], k_ref[d], f_ref[d] = qg, k, f
            hi_ref[d] = hi
            lo_ref[d] = (log2_f - hi.astype(F32)).astype(BF16)
            diag_ref[d] = jnp.broadcast_to(jnp.sum(qg * k, axis=-1, keepdims=True), qg.shape)

    def dots(hh, g_refs):
        qg_ref, k_ref, f_ref, diag_ref, hi_ref, lo_ref = g_refs
        cv = pl.ds(pl.multiple_of(hh * dv, LANE), dv)
        jobs = [(d, r) for d in range(2) for r in chunks]
        qg, k, f, diag = ([ref[d] for d in range(2)] for ref in (qg_ref, k_ref, f_ref, diag_ref))
        v = [qzv[d][2][:, cv] for d in range(2)]
        vb = [a.astype(BF16) for a in v]
        wins = [win_ref[0], win_ref[1]]
        sums = [_dot(wins[d], jnp.concatenate([hi_ref[d, r, :], lo_ref[d, r, :]], axis=0)) for d, r in jobs]

        def query_or_key(d, r, l):
            half = 1 << (l - 1)
            if half % F32_SUBLANE:
                return jnp.where(uppers[d][l], qg[d][r], k[d][r])
            segs = []
            for s in range(C // half):
                src = qg[d] if (s % 2 == 1) != (d == 1) else k[d]
                segs.append(src[r.start + s * half:r.start + (s + 1) * half])
            return jnp.concatenate(segs, axis=0)

        pairs = []
        for (d, r), s in zip(jobs, sums):
            ps = [jnp.where(uppers[d][1], qg[d][r] * f[d][r], k[d][r]).astype(BF16)]
            for l in range(2, HG_LEVELS + 1):
                scale = jnp.exp2(s[(l - 1) * C:l * C])
                ps.append((query_or_key(d, r, l) * scale).astype(BF16))
            pairs.append(ps)
        grams = [[_dot_nt(p, p) for p in ps] for ps in pairs]
        atts = []
        for (d, r), gs in zip(jobs, grams):
            att = jnp.where(levels[d] == 1, gs[0], 0.0)
            for l in range(2, HG_LEVELS + 1):
                att = jnp.where(levels[d] == l, gs[l - 1], att)
            atts.append(att.astype(BF16))
        k_out = [(k[d][r] * jnp.exp2(s[HG_LEVELS * C:])).astype(BF16) for (d, r), s in zip(jobs, sums)]
        q_in = [(qg[d][r] * jnp.exp2(s[:C])).astype(BF16) for (d, r), s in zip(jobs, sums)]
        decays = [jnp.exp2(s[0:1] if d == 1 else s[C - 1:C]) for (d, r), s in zip(jobs, sums)]
        outs = [_dot(att, vb[d][r]) + diag[d][r] * v[d][r] for att, (d, r) in zip(atts, jobs)]
        kvs = [_dot_tn(vb[d][r], ko) for (d, r), ko in zip(jobs, k_out)]
        for d in range(2):
            st = st_ref[d, hh]
            for c in (range(n_ch - 1, -1, -1) if d == 1 else range(n_ch)):
                j = d * n_ch + c
                o_refs[d][chunks[c], cv] = outs[j] + _dot_nt(q_in[j], st.astype(BF16))
                st = st * decays[j] + kvs[j]
            st_ref[d, hh] = st

    gates(0, gate_a)

    def head_pair(i, carry):
        gates(2 * i + 1, gate_b)
        dots(2 * i, gate_a)
        dots(2 * i + 1, gate_b)
        gates(jnp.minimum(2 * i + 2, heads - 1), gate_a)
        return carry

    lax.fori_loop(0, heads // 2, head_pair, 0)

    @pl.when(t == n_blocks - 1)
    def _():
        for d in range(2):
            for hh in range(heads):
                sfin_ref[d, hh] = st_ref[d, hh].T


def _hg_call(proj, lb, s0, *, tok0, batch, seq, heads, dk, dv, col_q, col_f, col_i):
    tt = min(seq, HG_ROWS_PER_STEP)
    n = seq // tt
    kw, vw = heads * dk, heads * dv
    assert seq % tt == 0 and tok0 % tt == 0 and tt % HG_CHUNK == 0
    assert col_q % kw == 0 and col_f % kw == 0 and col_i % vw == 0
    r0 = tok0 // tt

    def fwd(b, t):
        return r0 + b * n + t

    def bwd(b, t):
        return r0 + b * n + (n - 1 - t)

    in_specs, args = [], []
    for d, rmap in enumerate((fwd, bwd)):
        in_specs += [
            pl.BlockSpec((tt, kw), lambda b, t, rmap=rmap: (rmap(b, t), col_q // kw)),
            pl.BlockSpec((tt, kw), lambda b, t, rmap=rmap, d=d: (rmap(b, t), col_f // kw + d)),
            pl.BlockSpec((tt, vw), lambda b, t, rmap=rmap: (rmap(b, t), col_i // vw)),
        ]
        args += [proj, proj, proj]
    in_specs.append(pl.BlockSpec((2, 1, kw), lambda b, t: (0, 0, 0)))
    args.append(lb.reshape(2, 1, kw))
    state_spec = pl.BlockSpec((None, 2, heads, dk, dv), lambda b, t: (b, 0, 0, 0, 0))
    if s0 is not None:
        in_specs.append(state_spec)
        args.append(s0)
    n_tok = batch * seq
    return pl.pallas_call(
        functools.partial(_hg_kernel, heads=heads, dk=dk, dv=dv, tt=tt, n_blocks=n, has_init=s0 is not None),
        grid=(batch, n),
        in_specs=in_specs,
        out_specs=[
            pl.BlockSpec((tt, vw), lambda b, t: (b * n + t, 0)),
            pl.BlockSpec((tt, vw), lambda b, t: (b * n + (n - 1 - t), 0)),
            state_spec,
        ],
        out_shape=[
            jax.ShapeDtypeStruct((n_tok, vw), F32),
            jax.ShapeDtypeStruct((n_tok, vw), F32),
            jax.ShapeDtypeStruct((batch, 2, heads, dk, dv), F32),
        ],
        scratch_shapes=[pltpu.VMEM((2, heads, dv, dk), F32),
                        pltpu.VMEM((2, (HG_LEVELS + 1) * HG_CHUNK, 2 * HG_CHUNK), BF16)]
        + 2 * ([pltpu.VMEM((2, tt, dk), F32)] * 4 + [pltpu.VMEM((2, tt, dk), BF16)] * 2),
        compiler_params=_cparams("arbitrary", "arbitrary"),
        name="hgrn2_scan",
    )(*args)


def _mix_out_kernel(*refs, d, ret_heads, hg_heads, ctx_blocks):
    scans = [refs[4 * p:4 * p + 4] for p in range(2)]
    (rg_ref, go_ref, h_ref, x_ref, mod_ref, g2_ref, wr_ref, wh_ref, wm_ref, wo_ref, wrt_ref,
     x1_ref, h2_ref, lgt_ref) = refs[8:]

    is_ctx = pl.program_id(0) < ctx_blocks

    def both_directions(i, r, cols):
        return jnp.where(is_ctx, scans[0][i][r, cols] + scans[0][i + 1][r, cols],
                         scans[1][i][r, cols] + scans[1][i + 1][r, cols])

    dvr = scans[0][0].shape[1] // ret_heads
    dvh = scans[0][2].shape[1] // hg_heads
    tm = x_ref.shape[0]
    halves = [slice(0, tm // 2), slice(tm // 2, tm)]
    ret_in, hg_in = [], []
    for r in halves:
        parts = []
        for hh in range(ret_heads):
            oh = both_directions(0, r, slice(hh * dvr, (hh + 1) * dvr))
            ctr = oh - jnp.mean(oh, axis=-1, keepdims=True)
            parts.append(ctr * lax.rsqrt(jnp.mean(ctr * ctr, axis=-1, keepdims=True) + NORM_EPS))
        ret_in.append((jnp.concatenate(parts, axis=1) * _silu(rg_ref[r, :])).astype(BF16))
        parts = [_rms(both_directions(2, r, slice(hh * dvh, (hh + 1) * dvh))) for hh in range(hg_heads)]
        hg_in.append((jnp.concatenate(parts, axis=1) * _silu(go_ref[r, :])).astype(BF16))
    ret_out = [_dot(a, wr_ref[...]) for a in ret_in]
    hg_out = [_dot(a, wh_ref[...]) for a in hg_in]
    gates = [jax.nn.sigmoid(_dot(h_ref[r, :], wm_ref[...])) for r in halves]
    merged = [(g[:, :d] * a + g[:, d:] * b).astype(BF16) for g, a, b in zip(gates, ret_out, hg_out)]
    mix = [_dot(m, wo_ref[...]) for m in merged]
    w_hi, w_lo = wrt_ref[0], wrt_ref[1]
    for r, m in zip(halves, mix):
        x1 = x_ref[r, :] + mod_ref[:, 2 * d:3 * d] * m
        x1_ref[r, :] = x1
        h2 = _rms(x1) * g2_ref[...]
        h2 = h2 * (1.0 + mod_ref[:, 4 * d:5 * d]) + mod_ref[:, 3 * d:4 * d]
        h2_ref[r, :] = h2.astype(BF16)
        h_hi = h2.astype(BF16)
        h_lo = (h2 - h_hi.astype(F32)).astype(BF16)
        lgt_ref[:, r] = _dot_nt(w_hi, h_hi) + (_dot_nt(w_hi, h_lo) + _dot_nt(w_lo, h_hi))


def _mix_out_call(scans_ctx, scans_lat, proj, h, x, mod_l, mod_row, g2, w_ret_o, w_hg_o, w_merge, w_out,
                  w_router_t, tm, *, ret_heads, hg_heads, col_rg, col_go):
    n, d = x.shape
    vr, vh = scans_ctx[0].shape[1], scans_ctx[2].shape[1]
    ne = w_router_t.shape[1]
    ctx_blocks = scans_ctx[0].shape[0] // tm
    assert col_rg % vr == 0 and col_go % vh == 0

    def row(w):
        return pl.BlockSpec((tm, w), lambda i: (i, 0))

    def full(a):
        return pl.BlockSpec(a.shape, lambda i: (0,) * a.ndim)

    ctx_row = lambda a: pl.BlockSpec((tm, a.shape[1]), lambda i: (jnp.minimum(i, ctx_blocks - 1), 0))
    lat_row = lambda a: pl.BlockSpec((tm, a.shape[1]), lambda i: (jnp.maximum(i - ctx_blocks, 0), 0))
    return pl.pallas_call(
        functools.partial(_mix_out_kernel, d=d, ret_heads=ret_heads, hg_heads=hg_heads, ctx_blocks=ctx_blocks),
        grid=(n // tm,),
        in_specs=[ctx_row(a) for a in scans_ctx] + [lat_row(a) for a in scans_lat] + [
            pl.BlockSpec((tm, vr), lambda i: (i, col_rg // vr)),
            pl.BlockSpec((tm, vh), lambda i: (i, col_go // vh)),
            row(d), row(d),
            pl.BlockSpec((None, 1, 6 * d), lambda i: (mod_row(i), 0, 0)),
            pl.BlockSpec((1, d), lambda i: (0, 0)),
            full(w_ret_o), full(w_hg_o), full(w_merge), full(w_out), full(w_router_t),
        ],
        out_specs=[row(d), row(d), pl.BlockSpec((ne, tm), lambda i: (0, i))],
        out_shape=[
            jax.ShapeDtypeStruct((n, d), F32),
            jax.ShapeDtypeStruct((n, d), BF16),
            jax.ShapeDtypeStruct((ne, n), F32),
        ],
        compiler_params=_cparams("arbitrary"),
        name="mixer_out",
    )(*scans_ctx, *scans_lat, proj, proj, h, x, mod_l, g2.reshape(1, d), w_ret_o, w_hg_o, w_merge, w_out,
      w_router_t)


def _route_kernel(lg_ref, aff_ref, sel_ref, posx_ref, flag_ref, *, sets, caps):
    for (lo, n), cap in zip(sets, caps):
        _route_set(lg_ref, aff_ref, sel_ref, posx_ref, flag_ref, lo=lo, n=n, cap=cap)


def _route_set(lg_ref, aff_ref, sel_ref, posx_ref, flag_ref, *, lo, n, cap):
    cols = slice(lo, lo + n)
    lg = lg_ref[:, cols]
    ne = lg.shape[0]
    ex = jnp.exp(lg - jnp.max(lg, axis=0, keepdims=True))
    aff = ex / jnp.sum(ex, axis=0, keepdims=True)
    aff_ref[:, cols] = aff
    bits = pltpu.bitcast(aff, I32)

    def search(i, found):
        cand = found | lax.shift_left(jnp.int32(1), 30 - i)
        cnt = jnp.sum(jnp.where(bits >= cand, 1.0, 0.0), axis=1, keepdims=True)
        return jnp.where(cnt >= cap, cand, found)

    thr = lax.fori_loop(0, 31, search, jnp.zeros((ne, 1), I32))
    gt = bits > thr
    eq = bits == thr
    need = cap - jnp.sum(jnp.where(gt, 1.0, 0.0), axis=1, keepdims=True)
    upper = jnp.where(lax.broadcasted_iota(I32, (LANE, LANE), 0) < lax.broadcasted_iota(I32, (LANE, LANE), 1),
                      1.0, 0.0).astype(BF16)

    def exclusive_count(store):
        def body(ci, carry):
            sl = pl.ds(pl.multiple_of(lo + ci * LANE, LANE), LANE)
            x = flag_ref[:, sl]
            store(sl, carry + _dot(x.astype(BF16), upper))
            return carry + jnp.sum(x, axis=1, keepdims=True)

        lax.fori_loop(0, n // LANE, body, jnp.zeros((ne, 1), F32))

    flag_ref[:, cols] = jnp.where(eq, 1.0, 0.0)

    def store_tie_rank(sl, rank):
        posx_ref[:, sl] = rank.astype(I32)

    exclusive_count(store_tie_rank)
    sel = gt | (eq & (posx_ref[:, cols].astype(F32) < need))
    sel_ref[:, cols] = jnp.where(sel, 1, 0).astype(I32)
    flag_ref[:, cols] = jnp.where(sel, 1.0, 0.0)

    def store_pos(sl, cnt):
        posx_ref[:, sl] = cnt.astype(I32)

    exclusive_count(store_pos)


def _route_call(logits_t, sets, caps):
    ne, n = logits_t.shape
    spec = pl.BlockSpec((ne, n), lambda: (0, 0))
    return pl.pallas_call(
        functools.partial(_route_kernel, sets=sets, caps=caps),
        in_specs=[spec],
        out_specs=[spec, spec, spec],
        out_shape=[
            jax.ShapeDtypeStruct((ne, n), F32),
            jax.ShapeDtypeStruct((ne, n), I32),
            jax.ShapeDtypeStruct((ne, n), I32),
        ],
        scratch_shapes=[pltpu.VMEM((ne, n), F32)],
        compiler_params=pltpu.CompilerParams(vmem_limit_bytes=VMEM_LIMIT),
        name="expert_choice_route",
    )(logits_t)


def _moe_round_masks(b, r, base_ref, sel_ref, posx_ref, ne, value_rows=None):
    n_tok = sel_ref.shape[1]
    row_iota = lax.broadcasted_iota(I32, (MOE_ROUND, n_tok), 0)
    pieces = []
    for e in range(ne):
        local = posx_ref[e:e + 1, :] - (base_ref[b * ne + e] + r * MOE_ROUND)
        hit = (sel_ref[e:e + 1, :] == 1) & (local == row_iota)
        value = 1.0 if value_rows is None else value_rows[e:e + 1, :]
        pieces.append(jnp.where(hit, value, 0.0).astype(BF16))
    return jnp.concatenate(pieces, axis=0)


def _dispatch_kernel(base_ref, start_ref, pc_ref, nr_ref, first_ref, h_ref, sel_ref, posx_ref, xe_zero_ref, xe_ref,
                     stage_ref, sem, *, ne):
    del xe_zero_ref
    b = pl.program_id(0)
    n_rounds = nr_ref[b]
    prev = jnp.maximum(b - 1, 0)
    prev_pending = (b > 0) & (nr_ref[prev] >= 1)

    def pieces(blk, r, fn):
        slot = (first_ref[blk] + r) % 2
        for e in range(ne):
            rem = pc_ref[blk * ne + e] - r * MOE_ROUND
            dst = start_ref[blk * ne + e] + r * MOE_ROUND
            for size, cond, off in (
                    (MOE_ROUND, rem >= MOE_ROUND, 0),
                    (32, (rem > 0) & (rem < MOE_ROUND) & ((rem & 32) != 0), 0),
                    (16, (rem > 0) & (rem < MOE_ROUND) & ((rem & 16) != 0), rem & 32)):
                @pl.when(cond)
                def _(size=size, off=off, e=e, dst=dst):
                    src_rows = pl.ds(pl.multiple_of(e * MOE_ROUND + off, BF16_SUBLANE), size)
                    dst_rows = pl.ds(pl.multiple_of(dst + off, BF16_SUBLANE), size)
                    fn(pltpu.make_async_copy(stage_ref.at[slot, src_rows], xe_ref.at[e, dst_rows], sem.at[slot]))

    def wait_previous_block():
        pieces(prev, nr_ref[prev] - 1, lambda cp: cp.wait())

    def round_body(r, carry):
        onehot = _moe_round_masks(b, r, base_ref, sel_ref, posx_ref, ne)
        stage_ref[(first_ref[b] + r) % 2] = _dot(onehot, h_ref[...]).astype(BF16)

        @pl.when(r >= 1)
        def _():
            pieces(b, r - 1, lambda cp: cp.wait())

        @pl.when((r == 0) & prev_pending)
        def _():
            wait_previous_block()

        pieces(b, r, lambda cp: cp.start())
        return carry

    lax.fori_loop(0, n_rounds, round_body, 0)

    @pl.when((n_rounds == 0) & prev_pending)
    def _():
        wait_previous_block()

    @pl.when((n_rounds >= 1) & (b == pl.num_programs(0) - 1))
    def _():
        pieces(b, n_rounds - 1, lambda cp: cp.wait())


def _dispatch_call(h2, sel, posx, tables, list_rows, old_lists, *, sb):
    n, d = h2.shape
    ne = sel.shape[0]
    xe_zero = jnp.zeros((ne, list_rows, d), BF16) if old_lists is None else old_lists
    grid_spec = pltpu.PrefetchScalarGridSpec(
        num_scalar_prefetch=len(tables),
        grid=(n // sb,),
        in_specs=[
            pl.BlockSpec((sb, d), lambda b, *_: (b, 0)),
            pl.BlockSpec((ne, sb), lambda b, *_: (0, b)),
            pl.BlockSpec((ne, sb), lambda b, *_: (0, b)),
            pl.BlockSpec(memory_space=pl.ANY),
        ],
        out_specs=pl.BlockSpec(memory_space=pl.ANY),
        scratch_shapes=[pltpu.VMEM((2, ne * MOE_ROUND, d), BF16), pltpu.SemaphoreType.DMA((2,))],
    )
    return pl.pallas_call(
        functools.partial(_dispatch_kernel, ne=ne),
        grid_spec=grid_spec,
        out_shape=jax.ShapeDtypeStruct((ne, list_rows, d), BF16),
        input_output_aliases={len(tables) + 3: 0},
        compiler_params=_cparams("arbitrary"),
        name="expert_dispatch",
    )(*tables, h2, sel, posx, xe_zero)


def _expert_ffn_kernel(tot_ref, x_ref, wg_ref, wu_ref, wd_ref, y_ref, wgf_ref, wuf_ref, wdf_ref, wgb_ref, wub_ref,
                       wdb_ref, acc_ref, sem, *, layer, rt, n_ff):
    e, f, t = pl.program_id(0), pl.program_id(1), pl.program_id(2)
    fc = wgb_ref.shape[1]
    group = e * n_ff + f
    n_groups = pl.num_programs(0) * n_ff

    def weight_copies(g, fn):
        ge, gf, slot = g // n_ff, g % n_ff, g % 2
        cols = pl.ds(pl.multiple_of(gf * fc, LANE), fc)
        fn(pltpu.make_async_copy(wg_ref.at[layer, ge, :, cols], wgf_ref.at[slot], sem.at[slot]))
        fn(pltpu.make_async_copy(wu_ref.at[layer, ge, :, cols], wuf_ref.at[slot], sem.at[slot]))
        fn(pltpu.make_async_copy(wd_ref.at[layer, ge, cols, :], wdf_ref.at[slot], sem.at[slot]))

    @pl.when(t == 0)
    def _():
        @pl.when(group == 0)
        def _():
            weight_copies(group, lambda cp: cp.start())

        weight_copies(group, lambda cp: cp.wait())

        @pl.when(group + 1 < n_groups)
        def _():
            weight_copies(jnp.minimum(group + 1, n_groups - 1), lambda cp: cp.start())

        slot = group % 2
        wgb_ref[...] = wgf_ref[slot].astype(BF16)
        wub_ref[...] = wuf_ref[slot].astype(BF16)
        wdb_ref[...] = wdf_ref[slot].astype(BF16)

    rows = pl.ds(pl.multiple_of(t * rt, rt), rt)
    live = t * rt < tot_ref[e]

    @pl.when(live)
    def _():
        x = x_ref[...]
        hid = (_silu(_dot(x, wgb_ref[...])) * _dot(x, wub_ref[...])).astype(BF16)
        y = _dot(hid, wdb_ref[...])

        @pl.when(f == 0)
        def _():
            acc_ref[rows, :] = y

        @pl.when(f != 0)
        def _():
            acc_ref[rows, :] += y

    @pl.when((f == n_ff - 1) & live)
    def _():
        y_ref[...] = acc_ref[rows, :].astype(BF16)

    @pl.when((f == n_ff - 1) & jnp.logical_not(live))
    def _():
        y_ref[...] = jnp.zeros_like(y_ref)


def _expert_ffn_call(xe, totals, wg, wu, wd, layer, *, rt, fc):
    ne, list_rows, d = xe.shape
    ff = wg.shape[-1]
    n_ff = ff // fc
    grid_spec = pltpu.PrefetchScalarGridSpec(
        num_scalar_prefetch=1,
        grid=(ne, n_ff, list_rows // rt),
        in_specs=[
            pl.BlockSpec((None, rt, d), lambda e, f, t, *_: (e, t, 0)),
            pl.BlockSpec(memory_space=pl.ANY),
            pl.BlockSpec(memory_space=pl.ANY),
            pl.BlockSpec(memory_space=pl.ANY),
        ],
        out_specs=pl.BlockSpec((None, rt, d), lambda e, f, t, *_: (e, jnp.where(f == n_ff - 1, t, 0), 0)),
        scratch_shapes=[pltpu.VMEM((2, d, fc), F32), pltpu.VMEM((2, d, fc), F32), pltpu.VMEM((2, fc, d), F32),
                        pltpu.VMEM((d, fc), BF16), pltpu.VMEM((d, fc), BF16), pltpu.VMEM((fc, d), BF16),
                        pltpu.VMEM((list_rows, d), F32), pltpu.SemaphoreType.DMA((2,))],
    )
    return pl.pallas_call(
        functools.partial(_expert_ffn_kernel, layer=layer, rt=rt, n_ff=n_ff),
        grid_spec=grid_spec,
        out_shape=jax.ShapeDtypeStruct((ne, list_rows, d), BF16),
        compiler_params=_cparams("arbitrary", "arbitrary", "arbitrary"),
        name="expert_ffn",
    )(totals, xe, wg, wu, wd)


def _combine_kernel(base_ref, start_ref, pc_ref, nr_ref, first_ref, sel_ref, posx_ref, aff_ref, y_ref, out_ref,
                    ybuf_ref, sem, *, ne):
    b = pl.program_id(0)
    last = pl.num_programs(0) - 1
    n_rounds = nr_ref[b]
    first = first_ref[b]

    def windows(blk, r, fn):
        slot = (first_ref[blk] + r) % 2
        for e in range(ne):
            @pl.when(pc_ref[blk * ne + e] - r * MOE_ROUND > 0)
            def _(e=e):
                src = pl.ds(pl.multiple_of(start_ref[blk * ne + e] + r * MOE_ROUND, BF16_SUBLANE), MOE_ROUND)
                fn(pltpu.make_async_copy(y_ref.at[e, src], ybuf_ref.at[slot, pl.ds(e * MOE_ROUND, MOE_ROUND)],
                                         sem.at[slot]))

    def start_next_block():
        @pl.when(b < last)
        def _():
            windows(jnp.minimum(b + 1, last), 0, lambda cp: cp.start())

    @pl.when(b == 0)
    def _():
        ybuf_ref[...] = jnp.zeros_like(ybuf_ref)
        windows(0, 0, lambda cp: cp.start())

    out_ref[...] = jnp.zeros_like(out_ref)

    @pl.when(n_rounds == 0)
    def _():
        start_next_block()

    def round_body(r, carry):
        @pl.when(r + 1 < n_rounds)
        def _():
            windows(b, r + 1, lambda cp: cp.start())

        @pl.when(r + 1 == n_rounds)
        def _():
            start_next_block()

        windows(b, r, lambda cp: cp.wait())
        weights = _moe_round_masks(b, r, base_ref, sel_ref, posx_ref, ne, value_rows=aff_ref)
        out_ref[...] += _dot_tn(weights, ybuf_ref[(first + r) % 2])
        return carry

    lax.fori_loop(0, n_rounds, round_body, 0)


def _combine_call(ye, sel, posx, aff, tables, *, sb):
    ne, _, d = ye.shape
    n = sel.shape[1]
    grid_spec = pltpu.PrefetchScalarGridSpec(
        num_scalar_prefetch=len(tables),
        grid=(n // sb,),
        in_specs=[
            pl.BlockSpec((ne, sb), lambda b, *_: (0, b)),
            pl.BlockSpec((ne, sb), lambda b, *_: (0, b)),
            pl.BlockSpec((ne, sb), lambda b, *_: (0, b)),
            pl.BlockSpec(memory_space=pl.ANY),
        ],
        out_specs=pl.BlockSpec((sb, d), lambda b, *_: (b, 0)),
        scratch_shapes=[pltpu.VMEM((2, ne * MOE_ROUND, d), BF16), pltpu.SemaphoreType.DMA((2,))],
    )
    return pl.pallas_call(
        functools.partial(_combine_kernel, ne=ne),
        grid_spec=grid_spec,
        out_shape=jax.ShapeDtypeStruct((n, d), F32),
        compiler_params=_cparams("arbitrary"),
        name="expert_combine",
    )(*tables, sel, posx, aff, ye)


def _moe_tables(posx, sets, caps, sb, sbc):
    ne = posx.shape[0]
    fine = [jnp.concatenate([posx[:, lo:lo + n:sbc], jnp.full((ne, 1), cap, I32)], axis=1)
            for (lo, n), cap in zip(sets, caps)]
    step = sb // sbc
    base = jnp.concatenate([e[:, :-1:step].T for e in fine], axis=0)
    cnt = jnp.concatenate([(e[:, step::step] - e[:, :-1:step]).T for e in fine], axis=0)
    pc = (cnt + (BF16_SUBLANE - 1)) // BF16_SUBLANE * BF16_SUBLANE
    ends = jnp.cumsum(pc, axis=0)
    start = ends - pc
    rounds = lambda rows: jnp.max((rows + (MOE_ROUND - 1)) // MOE_ROUND, axis=1)
    flat = lambda *arrays: tuple(a.reshape(-1).astype(I32) for a in arrays)
    rounds_c = rounds(pc)
    dispatch = flat(base, start, pc, rounds_c, jnp.cumsum(rounds_c) - rounds_c)

    base_f = jnp.concatenate([e[:, :-1].T for e in fine], axis=0)
    cnt_f = jnp.concatenate([(e[:, 1:] - e[:, :-1]).T for e in fine], axis=0)
    first = jnp.repeat(start - base, step, axis=0) + base_f
    off = first % BF16_SUBLANE
    rows_f = jnp.where(cnt_f > 0, off + cnt_f, 0)
    rounds_f = rounds(rows_f)
    combine = flat(base_f - off, first - off, rows_f, rounds_f, jnp.cumsum(rounds_f) - rounds_f)
    return dispatch, combine, ends[-1].astype(I32)


def _rope_tables(n_tokens, dk):
    rows = n_tokens // GRID_W
    r, c = jnp.meshgrid(jnp.arange(rows), jnp.arange(GRID_W), indexing="ij")
    pos = jnp.stack([r.reshape(-1), c.reshape(-1)], axis=-1).astype(F32)
    nf = dk // 4
    inv_freq = ROPE_BASE ** (-jnp.arange(nf, dtype=F32) / nf)
    ang = pos[:, :, None] * inv_freq
    cos, sin = jnp.cos(ang), jnp.sin(ang)
    cos_t = jnp.concatenate([cos[:, 0], cos[:, 0], cos[:, 1], cos[:, 1]], axis=-1)
    sin_t = jnp.concatenate([-sin[:, 0], sin[:, 0], -sin[:, 1], sin[:, 1]], axis=-1)
    return cos_t, sin_t


def kernel(x_prompt, x_sample, state_ret, state_hgrn, c, c_ctx, ada_w, ada_b, norm_mix_g, norm_ffn_g, w_in,
           ret_gamma_logit, hg_lb_logit, w_ret_o, w_hg_o, w_merge, w_out, w_router, w_exp_gate, w_exp_up,
           w_exp_down, final_g):
    b_ctx, t_ctx, d = x_prompt.shape
    b_lat, t_lat, _ = x_sample.shape
    depth = w_in.shape[0]
    ret_heads, ret_dk, ret_dv = state_ret.shape[3:]
    hg_heads, hg_dk, hg_dv = state_hgrn.shape[3:]
    ne = w_router.shape[-1]
    n_ctx, n_lat = b_ctx * t_ctx, b_lat * t_lat
    n_tok = n_ctx + n_lat
    qw, vw = ret_heads * ret_dk, ret_heads * ret_dv
    kw, hw = hg_heads * hg_dk, hg_heads * hg_dv
    col_rg = 2 * qw + vw
    col_gq = col_rg + vw
    col_gf = col_gq + kw
    col_gi = col_gf + 2 * kw
    col_go = col_gi + hw
    assert col_go + hw == w_in.shape[-1] and b_lat < MOD_ROWS

    tm = 256
    assert t_ctx % tm == 0 and t_lat % tm == 0
    tm_proj = 1024 if n_tok % 1024 == 0 else tm
    sb = 1024 if (n_ctx % 4096 == 0 and n_lat % 4096 == 0) else 256
    assert n_ctx % sb == 0 and n_lat % sb == 0
    sets = ((0, n_ctx), (n_ctx, n_lat))
    caps = [CAPACITY_FACTOR * n_set // ne for _, n_set in sets]
    sbc = 256
    list_rows = sum(caps) + BF16_SUBLANE * (n_tok // sb) + MOE_ROUND
    n_tiles = max(1, list_rows // 512)
    rt = -(-list_rows // (n_tiles * BF16_SUBLANE)) * BF16_SUBLANE
    list_rows = n_tiles * rt

    def mod_rows(tmx):
        return lambda i: jnp.where(i < n_ctx // tmx, 0, 1 + (i - n_ctx // tmx) // (t_lat // tmx))

    mod_row = mod_rows(tm)
    tm_norm = max(t for t in (1024, 512, tm) if n_ctx % t == 0 and t_lat % t == 0)
    norm_row = mod_rows(tm_norm)

    x = jnp.concatenate([x_prompt.reshape(n_ctx, d), x_sample.reshape(n_lat, d)], axis=0)
    cvec = jnp.zeros((MOD_ROWS, d), F32).at[0].set(c_ctx).at[1:1 + b_lat].set(c)
    mod = _mod_call(cvec, ada_w, ada_b).reshape(depth, MOD_ROWS, 1, 6 * d)

    log_gamma = jax.nn.log_sigmoid(ret_gamma_logit.astype(F32))
    p_lb = jax.nn.softmax(hg_lb_logit.astype(F32), axis=0)
    hg_lb = jnp.clip(jnp.cumsum(p_lb, axis=0) - p_lb[0:1], 0.0, 1.0 - 1e-6)
    rope = _rope_tables(t_lat, ret_dk)

    w_ret_o_b, w_hg_o_b, w_out_b = w_ret_o.astype(BF16), w_hg_o.astype(BF16), w_out.astype(BF16)
    w_merge_b = w_merge.astype(BF16)
    w_router_t = jnp.swapaxes(w_router, 1, 2)
    w_router_hi = w_router_t.astype(BF16)
    w_router_t = jnp.stack([w_router_hi, (w_router_t - w_router_hi.astype(F32)).astype(BF16)], axis=1)
    fc = min(1024, w_exp_gate.shape[-1])
    tn = min(2048, w_in.shape[-1])

    _, h = _norm_call(x, None, None, mod[0], norm_mix_g[0], norm_row, tm_norm)
    new_ret, new_hg = [], []
    xe = None
    for l in range(depth):
        proj = _proj_call(h, w_in, l, tm_proj, tn)

        ret_kw = dict(heads=ret_heads, dk=ret_dk, dv=ret_dv)
        orf_c, orb_c, s_ret = _ret_call(proj, log_gamma[l], None, None, tok0=0, batch=b_ctx, seq=t_ctx, **ret_kw)
        orf_l, orb_l, _ = _ret_call(proj, log_gamma[l], state_ret[:, l], rope, tok0=n_ctx, batch=b_lat, seq=t_lat,
                                    **ret_kw)
        hg_kw = dict(heads=hg_heads, dk=hg_dk, dv=hg_dv, col_q=col_gq, col_f=col_gf, col_i=col_gi)
        ogf_c, ogb_c, s_hg = _hg_call(proj, hg_lb[l], None, tok0=0, batch=b_ctx, seq=t_ctx, **hg_kw)
        ogf_l, ogb_l, _ = _hg_call(proj, hg_lb[l], state_hgrn[:, l], tok0=n_ctx, batch=b_lat, seq=t_lat, **hg_kw)
        new_ret.append(s_ret)
        new_hg.append(s_hg)

        x, h2, logits_t = _mix_out_call(
            (orf_c, orb_c, ogf_c, ogb_c), (orf_l, orb_l, ogf_l, ogb_l), proj, h, x, mod[l],
            mod_row, norm_ffn_g[l], w_ret_o_b[l], w_hg_o_b[l], w_merge_b[l], w_out_b[l], w_router_t[l], tm,
            ret_heads=ret_heads, hg_heads=hg_heads, col_rg=col_rg, col_go=col_go)

        aff, sel, posx = _route_call(logits_t, sets, caps)
        dispatch_tables, combine_tables, totals = _moe_tables(posx, sets, caps, sb, sbc)
        xe = _dispatch_call(h2, sel, posx, dispatch_tables, list_rows, xe, sb=sb)
        ye = _expert_ffn_call(xe, totals, w_exp_gate, w_exp_up, w_exp_down, l, rt=rt, fc=fc)
        ffn = _combine_call(ye, sel, posx, aff, combine_tables, sb=sbc)

        if l + 1 < depth:
            x, h = _norm_call(x, ffn, mod[l], mod[l + 1], norm_mix_g[l + 1], norm_row, tm_norm)
        else:
            _, y_prompt = _norm_call(x, ffn, mod[l], None, final_g, norm_row, tm_norm, rows=(0, n_ctx))
            _, y_sample = _norm_call(x, ffn, mod[l], None, final_g, norm_row, tm_norm, rows=(n_ctx, n_lat))

    y_prompt = y_prompt.reshape(b_ctx, t_ctx, d)
    y_sample = y_sample.reshape(b_lat, t_lat, d)
    return y_prompt, y_sample, jnp.stack(new_ret, axis=1), jnp.stack(new_hg, axis=1)
```

```python
import functools

import jax
import jax.numpy as jnp
from jax import lax
from jax.experimental import pallas as pl
from jax.experimental.pallas import tpu as pltpu

F32 = jnp.float32
BF16 = jnp.bfloat16
I32 = jnp.int32
HIGHEST = lax.Precision.HIGHEST

NORM_EPS = 1e-6
LOG2_E = 1.4426950408889634
ROPE_BASE = 10000.0
GRID_W = 64
CAPACITY_FACTOR = 2
RET_CHUNK = 128
RET_CHUNKS_PER_STEP = 4
HG_LEVELS = 6
HG_CHUNK = 1 << HG_LEVELS
HG_ROWS_PER_STEP = 512
HG_LOG_SIGMOID_LINEAR = -60.0
MOD_ROWS = 8
LANE = 128
F32_SUBLANE = 8
BF16_SUBLANE = 16
MOE_ROUND = 64
VMEM_LIMIT = 62 * 1024 * 1024


def _cparams(*sem):
    return pltpu.CompilerParams(dimension_semantics=sem, vmem_limit_bytes=VMEM_LIMIT)


def _dot(a, b, **kw):
    return jnp.dot(a, b, preferred_element_type=F32, **kw)


def _dot_nt(a, b, **kw):
    return lax.dot_general(a, b, (((1,), (1,)), ((), ())), preferred_element_type=F32, **kw)


def _dot_tn(a, b, **kw):
    return lax.dot_general(a, b, (((0,), (0,)), ((), ())), preferred_element_type=F32, **kw)


def _silu(x):
    return x * jax.nn.sigmoid(x)


def _mod_kernel(c_ref, w_ref, b_ref, o_ref):
    s = _silu(c_ref[...])
    o_ref[...] = _dot(s, w_ref[...], precision=HIGHEST) + b_ref[...]


def _mod_call(cvec, ada_w, ada_b):
    depth, d, six_d = ada_w.shape
    tn = 6 * LANE * 2
    assert six_d % tn == 0
    return pl.pallas_call(
        _mod_kernel,
        grid=(depth, six_d // tn),
        in_specs=[
            pl.BlockSpec((MOD_ROWS, d), lambda l, j: (0, 0)),
            pl.BlockSpec((None, d, tn), lambda l, j: (l, 0, j)),
            pl.BlockSpec((None, 1, tn), lambda l, j: (l, 0, j)),
        ],
        out_specs=pl.BlockSpec((None, MOD_ROWS, tn), lambda l, j: (l, 0, j)),
        out_shape=jax.ShapeDtypeStruct((depth, MOD_ROWS, six_d), F32),
        compiler_params=_cparams("arbitrary", "arbitrary"),
        name="adaln_mod",
    )(cvec, ada_w, ada_b.reshape(depth, 1, six_d))


def _rms(x):
    return x * lax.rsqrt(jnp.mean(x * x, axis=-1, keepdims=True) + NORM_EPS)


def _norm_kernel(*refs, d, residual, final):
    it = iter(refs)
    x_ref = next(it)
    f_ref, gate_ref = (next(it), next(it)) if residual else (None, None)
    mod_ref = None if final else next(it)
    g_ref = next(it)
    xo_ref = next(it) if (residual and not final) else None
    h_ref = next(it)
    x = x_ref[...]
    if residual:
        x = x + gate_ref[...] * f_ref[...]
        if xo_ref is not None:
            xo_ref[...] = x
    y = _rms(x) * g_ref[...]
    if not final:
        y = y * (1.0 + mod_ref[:, d:2 * d]) + mod_ref[:, 0:d]
    h_ref[...] = y.astype(h_ref.dtype)


def _norm_call(x, ffn, mod_gate, mod_next, g, mod_row, tm, rows=None):
    n, d = x.shape
    residual = ffn is not None
    final = mod_next is None
    r0, n_out = (0, n) if rows is None else rows
    blk0 = r0 // tm
    row = pl.BlockSpec((tm, d), lambda i: (blk0 + i, 0))
    in_specs, args = [row], [x]
    if residual:
        in_specs += [row, pl.BlockSpec((None, 1, d), lambda i: (mod_row(blk0 + i), 0, 5))]
        args += [ffn, mod_gate]
    if not final:
        in_specs.append(pl.BlockSpec((None, 1, 2 * d), lambda i: (mod_row(blk0 + i), 0, 0)))
        args.append(mod_next)
    in_specs.append(pl.BlockSpec((1, d), lambda i: (0, 0)))
    args.append(g.reshape(1, d))
    out_row = pl.BlockSpec((tm, d), lambda i: (i, 0))
    out_specs, out_shape = [out_row], [jax.ShapeDtypeStruct((n_out, d), F32 if final else BF16)]
    if residual and not final:
        out_specs = [out_row, out_row]
        out_shape = [jax.ShapeDtypeStruct((n_out, d), F32)] + out_shape
    outs = pl.pallas_call(
        functools.partial(_norm_kernel, d=d, residual=residual, final=final),
        grid=(n_out // tm,),
        in_specs=in_specs,
        out_specs=out_specs,
        out_shape=out_shape,
        compiler_params=_cparams("arbitrary"),
        name="residual_norm",
    )(*args)
    return outs if len(outs) == 2 else (x, outs[0])


def _proj_kernel(h_ref, w_ref, o_ref, wb_ref):
    @pl.when(pl.program_id(1) == 0)
    def _():
        wb_ref[...] = w_ref[...].astype(BF16)

    o_ref[...] = _dot(h_ref[...], wb_ref[...])


def _proj_call(h, w, layer, tm, tn):
    n, d = h.shape
    cols = w.shape[-1]
    return pl.pallas_call(
        _proj_kernel,
        grid=(cols // tn, n // tm),
        in_specs=[
            pl.BlockSpec((tm, d), lambda j, i: (i, 0)),
            pl.BlockSpec((None, d, tn), lambda j, i: (layer, 0, j)),
        ],
        out_specs=pl.BlockSpec((tm, tn), lambda j, i: (i, j)),
        out_shape=jax.ShapeDtypeStruct((n, cols), F32),
        scratch_shapes=[pltpu.VMEM((d, tn), BF16)],
        compiler_params=_cparams("arbitrary", "arbitrary"),
        name="in_proj",
    )(h, w)


def _swap_half_pairs(x):
    lane = lax.broadcasted_iota(I32, x.shape, 1)
    return jnp.where((lane & 32) == 0, pltpu.roll(x, LANE - 32, 1), pltpu.roll(x, 32, 1))


def _ret_kernel(*refs, heads, dk, dv, chunk, n_chunks, has_init, has_rope):
    it = iter(refs)
    lg_ref = next(it)
    qkv = [[next(it) for _ in range(3)] for _ in range(2)]
    s0_ref = next(it) if has_init else None
    rope = [[next(it) for _ in range(2)] for _ in range(2)] if has_rope else None
    o_refs = [next(it), next(it)]
    sfin_ref = next(it)
    s_ref = next(it)

    c = pl.program_id(1)

    @pl.when(c == 0)
    def _():
        if has_init:
            s_ref[...] = s0_ref[...]
        else:
            s_ref[...] = jnp.zeros_like(s_ref)

    L = chunk
    ii = lax.broadcasted_iota(I32, (L, L), 0)
    jj = lax.broadcasted_iota(I32, (L, L), 1)
    pcol = lax.broadcasted_iota(I32, (L, 1), 0).astype(F32)
    n_sub = qkv[0][0].shape[0] // L
    jobs = [(d, hh, r) for d in range(2) for hh in range(heads) for r in range(n_sub)]
    qs, ks, vs, scores, q_in, k_out = [], [], [], [], [], []
    for d, hh, r in jobs:
        q_ref, k_ref, v_ref = qkv[d]
        rows = slice(r * L, (r + 1) * L)
        q = q_ref[rows, hh * dk:(hh + 1) * dk]
        k = k_ref[rows, hh * dk:(hh + 1) * dk] * (dk ** -0.5)
        if has_rope:
            cos, sin = rope[d][0][rows, :], rope[d][1][rows, :]
            q = q * cos + _swap_half_pairs(q) * sin
            k = k * cos + _swap_half_pairs(k) * sin
        qs.append(q)
        ks.append(k)
        vs.append(v_ref[rows, hh * dv:(hh + 1) * dv].astype(BF16))
    for (d, hh, r), q, k in zip(jobs, qs, ks):
        dif = ((ii - jj) if d == 0 else (jj - ii)).astype(F32)
        lg = lg_ref[d, hh]
        decay = jnp.where(dif >= 0.0, jnp.exp(lg * jnp.maximum(dif, 0.0)), 0.0)
        scores.append((_dot_nt(q.astype(BF16), k.astype(BF16)) * decay).astype(BF16))
    for (d, hh, r), q, k in zip(jobs, qs, ks):
        lg = lg_ref[d, hh]
        q_pow, k_pow = (pcol + 1.0, (L - 1.0) - pcol) if d == 0 else (L - pcol, pcol)
        q_in.append((q * jnp.exp(lg * q_pow)).astype(BF16))
        k_out.append((k * jnp.exp(lg * k_pow)).T.astype(BF16))
    intra = [_dot(sc, v) for sc, v in zip(scores, vs)]
    kvs = [_dot(ko, v) for ko, v in zip(k_out, vs)]
    for d in range(2):
        for hh in range(heads):
            s = s_ref[d, hh]
            chunk_decay = jnp.exp(jnp.full((1, 1), lg_ref[d, hh] * L, F32))
            for r in (range(n_sub) if d == 0 else range(n_sub - 1, -1, -1)):
                j = (d * heads + hh) * n_sub + r
                o_refs[d][r * L:(r + 1) * L, hh * dv:(hh + 1) * dv] = intra[j] + _dot(q_in[j], s.astype(BF16))
                s = s * chunk_decay + kvs[j]
            s_ref[d, hh] = s

    @pl.when(c == n_chunks - 1)
    def _():
        sfin_ref[...] = s_ref[...]


def _ret_call(proj, log_gamma, s0, rope, *, tok0, batch, seq, heads, dk, dv):
    rb = min(seq, RET_CHUNK * RET_CHUNKS_PER_STEP)
    n = seq // rb
    qw, vw = heads * dk, heads * dv
    assert seq % rb == 0 and tok0 % rb == 0 and vw % qw == 0
    r0 = tok0 // rb

    def fwd(b, c):
        return r0 + b * n + c

    def bwd(b, c):
        return r0 + b * n + (n - 1 - c)

    in_specs = [pl.BlockSpec(memory_space=pltpu.SMEM)]
    args = [log_gamma]
    for rmap in (fwd, bwd):
        in_specs += [
            pl.BlockSpec((rb, qw), lambda b, c, rmap=rmap: (rmap(b, c), 0)),
            pl.BlockSpec((rb, qw), lambda b, c, rmap=rmap: (rmap(b, c), 1)),
            pl.BlockSpec((rb, vw), lambda b, c, rmap=rmap: (rmap(b, c), (2 * qw) // vw)),
        ]
        args += [proj, proj, proj]
    state_spec = pl.BlockSpec((None, 2, heads, dk, dv), lambda b, c: (b, 0, 0, 0, 0))
    if s0 is not None:
        in_specs.append(state_spec)
        args.append(s0)
    if rope is not None:
        for cmap in (lambda b, c: (c, 0), lambda b, c: (n - 1 - c, 0)):
            in_specs += [pl.BlockSpec((rb, dk), cmap), pl.BlockSpec((rb, dk), cmap)]
            args += [rope[0], rope[1]]
    n_tok = batch * seq
    return pl.pallas_call(
        functools.partial(_ret_kernel, heads=heads, dk=dk, dv=dv, chunk=RET_CHUNK, n_chunks=n,
                          has_init=s0 is not None, has_rope=rope is not None),
        grid=(batch, n),
        in_specs=in_specs,
        out_specs=[
            pl.BlockSpec((rb, vw), lambda b, c: (b * n + c, 0)),
            pl.BlockSpec((rb, vw), lambda b, c: (b * n + (n - 1 - c), 0)),
            state_spec,
        ],
        out_shape=[
            jax.ShapeDtypeStruct((n_tok, vw), F32),
            jax.ShapeDtypeStruct((n_tok, vw), F32),
            jax.ShapeDtypeStruct((batch, 2, heads, dk, dv), F32),
        ],
        scratch_shapes=[pltpu.VMEM((2, heads, dk, dv), F32)],
        compiler_params=_cparams("arbitrary", "arbitrary"),
        name="retention_scan",
    )(*args)


def _hg_scan_index(shape, axis, rev):
    i = lax.broadcasted_iota(I32, shape, axis) & (HG_CHUNK - 1)
    return (HG_CHUNK - 1 - i) if rev else i


def _hg_window_matrix(rev):
    C = HG_CHUNK
    ti = _hg_scan_index((C, 2 * C), 0, rev)
    tj = _hg_scan_index((C, 2 * C), 1, rev)
    blocks = [tj <= ti]
    for l in range(2, HG_LEVELS + 1):
        anchor = ((ti >> l) << l) + (1 << (l - 1)) - 1
        upper = ((ti >> (l - 1)) & 1) == 1
        blocks.append((upper & (tj > anchor) & (tj <= ti)) | (~upper & (tj > ti) & (tj <= anchor)))
    blocks.append(tj > ti)
    return jnp.concatenate([jnp.where(m, 1.0, 0.0) for m in blocks], axis=0).astype(BF16)


def _hg_kernel(*refs, heads, dk, dv, tt, n_blocks, has_init):
    it = iter(refs)
    qzv = [[next(it) for _ in range(3)] for _ in range(2)]
    lb_ref = next(it)
    s0_ref = next(it) if has_init else None
    o_refs = [next(it), next(it)]
    sfin_ref = next(it)
    st_ref = next(it)
    win_ref = next(it)
    gate_a = [next(it) for _ in range(6)]
    gate_b = [next(it) for _ in range(6)]

    t = pl.program_id(1)

    @pl.when((pl.program_id(0) == 0) & (t == 0))
    def _():
        for d in range(2):
            win_ref[d] = _hg_window_matrix(d == 1)

    @pl.when(t == 0)
    def _():
        for d in range(2):
            for hh in range(heads):
                st_ref[d, hh] = s0_ref[d, hh].T if has_init else jnp.zeros((dv, dk), F32)

    C = HG_CHUNK
    n_ch = tt // C
    chunks = [slice(c * C, (c + 1) * C) for c in range(n_ch)]
    levels, uppers = [], []
    for d in range(2):
        ti = _hg_scan_index((C, C), 0, d == 1)
        tj = _hg_scan_index((C, C), 1, d == 1)
        level = jnp.where(tj < ti, 1, 0)
        for l in range(1, HG_LEVELS):
            level = level + jnp.where((tj < ti) & ((ti >> l) != (tj >> l)), 1, 0)
        t_col = _hg_scan_index((C, 1), 0, d == 1)
        levels.append(level)
        uppers.append([None] + [((t_col >> (l - 1)) & 1) == 1 for l in range(1, HG_LEVELS + 1)])

    def gates(hh, g_refs):
        qg_ref, k_ref, f_ref, diag_ref, hi_ref, lo_ref = g_refs
        ck = pl.ds(pl.multiple_of(hh * dk, LANE), dk)
        for d in range(2):
            q_ref, z_ref, _ = qzv[d]
            lb = lb_ref[d, :, ck]
            z = z_ref[:, ck]
            e = jnp.exp(-jnp.abs(z))
            inv = 1.0 / (1.0 + e)
            pos = z > 0.0
            f = lb + (1.0 - lb) * (jnp.where(pos, 1.0, e) * inv)
            k = (1.0 - lb) * (jnp.where(pos, e, 1.0) * inv)
            tiny = (lb <= 0.0) & (z < HG_LOG_SIGMOID_LINEAR)
            log2_f = jnp.where(tiny, z * LOG2_E, jnp.log2(f))
            hi = log2_f.astype(BF16)
            qg = _silu(q_ref[:, ck]) * (dk ** -0.5)
            qg_ref[d], k_ref[d], f_ref[d] = qg, k, f
            hi_ref[d] = hi
            lo_ref[d] = (log2_f - hi.astype(F32)).astype(BF16)
            diag_ref[d] = jnp.broadcast_to(jnp.sum(qg * k, axis=-1, keepdims=True), qg.shape)

    def dots(hh, g_refs):
        qg_ref, k_ref, f_ref, diag_ref, hi_ref, lo_ref = g_refs
        cv = pl.ds(pl.multiple_of(hh * dv, LANE), dv)
        jobs = [(d, r) for d in range(2) for r in chunks]
        qg, k, f, diag = ([ref[d] for d in range(2)] for ref in (qg_ref, k_ref, f_ref, diag_ref))
        v = [qzv[d][2][:, cv] for d in range(2)]
        vb = [a.astype(BF16) for a in v]
        wins = [win_ref[0], win_ref[1]]
        sums = [_dot(wins[d], jnp.concatenate([hi_ref[d, r, :], lo_ref[d, r, :]], axis=0)) for d, r in jobs]

        def query_or_key(d, r, l):
            half = 1 << (l - 1)
            if half % F32_SUBLANE:
                return jnp.where(uppers[d][l], qg[d][r], k[d][r])
            segs = []
            for s in range(C // half):
                src = qg[d] if (s % 2 == 1) != (d == 1) else k[d]
                segs.append(src[r.start + s * half:r.start + (s + 1) * half])
            return jnp.concatenate(segs, axis=0)

        pairs = []
        for (d, r), s in zip(jobs, sums):
            ps = [jnp.where(uppers[d][1], qg[d][r] * f[d][r], k[d][r]).astype(BF16)]
            for l in range(2, HG_LEVELS + 1):
                scale = jnp.exp2(s[(l - 1) * C:l * C])
                ps.append((query_or_key(d, r, l) * scale).astype(BF16))
            pairs.append(ps)
        grams = [[_dot_nt(p, p) for p in ps] for ps in pairs]
        atts = []
        for (d, r), gs in zip(jobs, grams):
            att = jnp.where(levels[d] == 1, gs[0], 0.0)
            for l in range(2, HG_LEVELS + 1):
                att = jnp.where(levels[d] == l, gs[l - 1], att)
            atts.append(att.astype(BF16))
        k_out = [(k[d][r] * jnp.exp2(s[HG_LEVELS * C:])).astype(BF16) for (d, r), s in zip(jobs, sums)]
        q_in = [(qg[d][r] * jnp.exp2(s[:C])).astype(BF16) for (d, r), s in zip(jobs, sums)]
        decays = [jnp.exp2(s[0:1] if d == 1 else s[C - 1:C]) for (d, r), s in zip(jobs, sums)]
        outs = [_dot(att, vb[d][r]) + diag[d][r] * v[d][r] for att, (d, r) in zip(atts, jobs)]
        kvs = [_dot_tn(vb[d][r], ko) for (d, r), ko in zip(jobs, k_out)]
        for d in range(2):
            st = st_ref[d, hh]
            for c in (range(n_ch - 1, -1, -1) if d == 1 else range(n_ch)):
                j = d * n_ch + c
                o_refs[d][chunks[c], cv] = outs[j] + _dot_nt(q_in[j], st.astype(BF16))
                st = st * decays[j] + kvs[j]
            st_ref[d, hh] = st

    gates(0, gate_a)

    def head_pair(i, carry):
        gates(2 * i + 1, gate_b)
        dots(2 * i, gate_a)
        dots(2 * i + 1, gate_b)
        gates(jnp.minimum(2 * i + 2, heads - 1), gate_a)
        return carry

    lax.fori_loop(0, heads // 2, head_pair, 0)

    @pl.when(t == n_blocks - 1)
    def _():
        for d in range(2):
            for hh in range(heads):
                sfin_ref[d, hh] = st_ref[d, hh].T


def _hg_call(proj, lb, s0, *, tok0, batch, seq, heads, dk, dv, col_q, col_f, col_i):
    tt = min(seq, HG_ROWS_PER_STEP)
    n = seq // tt
    kw, vw = heads * dk, heads * dv
    assert seq % tt == 0 and tok0 % tt == 0 and tt % HG_CHUNK == 0
    assert col_q % kw == 0 and col_f % kw == 0 and col_i % vw == 0
    r0 = tok0 // tt

    def fwd(b, t):
        return r0 + b * n + t

    def bwd(b, t):
        return r0 + b * n + (n - 1 - t)

    in_specs, args = [], []
    for d, rmap in enumerate((fwd, bwd)):
        in_specs += [
            pl.BlockSpec((tt, kw), lambda b, t, rmap=rmap: (rmap(b, t), col_q // kw)),
            pl.BlockSpec((tt, kw), lambda b, t, rmap=rmap, d=d: (rmap(b, t), col_f // kw + d)),
            pl.BlockSpec((tt, vw), lambda b, t, rmap=rmap: (rmap(b, t), col_i // vw)),
        ]
        args += [proj, proj, proj]
    in_specs.append(pl.BlockSpec((2, 1, kw), lambda b, t: (0, 0, 0)))
    args.append(lb.reshape(2, 1, kw))
    state_spec = pl.BlockSpec((None, 2, heads, dk, dv), lambda b, t: (b, 0, 0, 0, 0))
    if s0 is not None:
        in_specs.append(state_spec)
        args.append(s0)
    n_tok = batch * seq
    return pl.pallas_call(
        functools.partial(_hg_kernel, heads=heads, dk=dk, dv=dv, tt=tt, n_blocks=n, has_init=s0 is not None),
        grid=(batch, n),
        in_specs=in_specs,
        out_specs=[
            pl.BlockSpec((tt, vw), lambda b, t: (b * n + t, 0)),
            pl.BlockSpec((tt, vw), lambda b, t: (b * n + (n - 1 - t), 0)),
            state_spec,
        ],
        out_shape=[
            jax.ShapeDtypeStruct((n_tok, vw), F32),
            jax.ShapeDtypeStruct((n_tok, vw), F32),
            jax.ShapeDtypeStruct((batch, 2, heads, dk, dv), F32),
        ],
        scratch_shapes=[pltpu.VMEM((2, heads, dv, dk), F32),
                        pltpu.VMEM((2, (HG_LEVELS + 1) * HG_CHUNK, 2 * HG_CHUNK), BF16)]
        + 2 * ([pltpu.VMEM((2, tt, dk), F32)] * 4 + [pltpu.VMEM((2, tt, dk), BF16)] * 2),
        compiler_params=_cparams("arbitrary", "arbitrary"),
        name="hgrn2_scan",
    )(*args)


def _mix_out_kernel(*refs, d, ret_heads, hg_heads, ctx_blocks):
    scans = [refs[4 * p:4 * p + 4] for p in range(2)]
    (rg_ref, go_ref, h_ref, x_ref, mod_ref, g2_ref, wr_ref, wh_ref, wm_ref, wo_ref, wrt_ref,
     x1_ref, h2_ref, lgt_ref) = refs[8:]

    is_ctx = pl.program_id(0) < ctx_blocks

    def both_directions(i, r, cols):
        return jnp.where(is_ctx, scans[0][i][r, cols] + scans[0][i + 1][r, cols],
                         scans[1][i][r, cols] + scans[1][i + 1][r, cols])

    dvr = scans[0][0].shape[1] // ret_heads
    dvh = scans[0][2].shape[1] // hg_heads
    tm = x_ref.shape[0]
    halves = [slice(0, tm // 2), slice(tm // 2, tm)]
    ret_in, hg_in = [], []
    for r in halves:
        parts = []
        for hh in range(ret_heads):
            oh = both_directions(0, r, slice(hh * dvr, (hh + 1) * dvr))
            ctr = oh - jnp.mean(oh, axis=-1, keepdims=True)
            parts.append(ctr * lax.rsqrt(jnp.mean(ctr * ctr, axis=-1, keepdims=True) + NORM_EPS))
        ret_in.append((jnp.concatenate(parts, axis=1) * _silu(rg_ref[r, :])).astype(BF16))
        parts = [_rms(both_directions(2, r, slice(hh * dvh, (hh + 1) * dvh))) for hh in range(hg_heads)]
        hg_in.append((jnp.concatenate(parts, axis=1) * _silu(go_ref[r, :])).astype(BF16))
    ret_out = [_dot(a, wr_ref[...]) for a in ret_in]
    hg_out = [_dot(a, wh_ref[...]) for a in hg_in]
    gates = [jax.nn.sigmoid(_dot(h_ref[r, :], wm_ref[...])) for r in halves]
    merged = [(g[:, :d] * a + g[:, d:] * b).astype(BF16) for g, a, b in zip(gates, ret_out, hg_out)]
    mix = [_dot(m, wo_ref[...]) for m in merged]
    w_hi, w_lo = wrt_ref[0], wrt_ref[1]
    for r, m in zip(halves, mix):
        x1 = x_ref[r, :] + mod_ref[:, 2 * d:3 * d] * m
        x1_ref[r, :] = x1
        h2 = _rms(x1) * g2_ref[...]
        h2 = h2 * (1.0 + mod_ref[:, 4 * d:5 * d]) + mod_ref[:, 3 * d:4 * d]
        h2_ref[r, :] = h2.astype(BF16)
        h_hi = h2.astype(BF16)
        h_lo = (h2 - h_hi.astype(F32)).astype(BF16)
        lgt_ref[:, r] = _dot_nt(w_hi, h_hi) + (_dot_nt(w_hi, h_lo) + _dot_nt(w_lo, h_hi))


def _mix_out_call(scans_ctx, scans_lat, proj, h, x, mod_l, mod_row, g2, w_ret_o, w_hg_o, w_merge, w_out,
                  w_router_t, tm, *, ret_heads, hg_heads, col_rg, col_go):
    n, d = x.shape
    vr, vh = scans_ctx[0].shape[1], scans_ctx[2].shape[1]
    ne = w_router_t.shape[1]
    ctx_blocks = scans_ctx[0].shape[0] // tm
    assert col_rg % vr == 0 and col_go % vh == 0

    def row(w):
        return pl.BlockSpec((tm, w), lambda i: (i, 0))

    def full(a):
        return pl.BlockSpec(a.shape, lambda i: (0,) * a.ndim)

    ctx_row = lambda a: pl.BlockSpec((tm, a.shape[1]), lambda i: (jnp.minimum(i, ctx_blocks - 1), 0))
    lat_row = lambda a: pl.BlockSpec((tm, a.shape[1]), lambda i: (jnp.maximum(i - ctx_blocks, 0), 0))
    return pl.pallas_call(
        functools.partial(_mix_out_kernel, d=d, ret_heads=ret_heads, hg_heads=hg_heads, ctx_blocks=ctx_blocks),
        grid=(n // tm,),
        in_specs=[ctx_row(a) for a in scans_ctx] + [lat_row(a) for a in scans_lat] + [
            pl.BlockSpec((tm, vr), lambda i: (i, col_rg // vr)),
            pl.BlockSpec((tm, vh), lambda i: (i, col_go // vh)),
            row(d), row(d),
            pl.BlockSpec((None, 1, 6 * d), lambda i: (mod_row(i), 0, 0)),
            pl.BlockSpec((1, d), lambda i: (0, 0)),
            full(w_ret_o), full(w_hg_o), full(w_merge), full(w_out), full(w_router_t),
        ],
        out_specs=[row(d), row(d), pl.BlockSpec((ne, tm), lambda i: (0, i))],
        out_shape=[
            jax.ShapeDtypeStruct((n, d), F32),
            jax.ShapeDtypeStruct((n, d), BF16),
            jax.ShapeDtypeStruct((ne, n), F32),
        ],
        compiler_params=_cparams("arbitrary"),
        name="mixer_out",
    )(*scans_ctx, *scans_lat, proj, proj, h, x, mod_l, g2.reshape(1, d), w_ret_o, w_hg_o, w_merge, w_out,
      w_router_t)


def _route_kernel(lg_ref, aff_ref, sel_ref, posx_ref, flag_ref, *, sets, caps):
    for (lo, n), cap in zip(sets, caps):
        _route_set(lg_ref, aff_ref, sel_ref, posx_ref, flag_ref, lo=lo, n=n, cap=cap)


def _route_set(lg_ref, aff_ref, sel_ref, posx_ref, flag_ref, *, lo, n, cap):
    cols = slice(lo, lo + n)
    lg = lg_ref[:, cols]
    ne = lg.shape[0]
    ex = jnp.exp(lg - jnp.max(lg, axis=0, keepdims=True))
    aff = ex / jnp.sum(ex, axis=0, keepdims=True)
    aff_ref[:, cols] = aff
    bits = pltpu.bitcast(aff, I32)

    def search(i, found):
        cand = found | lax.shift_left(jnp.int32(1), 30 - i)
        cnt = jnp.sum(jnp.where(bits >= cand, 1.0, 0.0), axis=1, keepdims=True)
        return jnp.where(cnt >= cap, cand, found)

    thr = lax.fori_loop(0, 31, search, jnp.zeros((ne, 1), I32))
    gt = bits > thr
    eq = bits == thr
    need = cap - jnp.sum(jnp.where(gt, 1.0, 0.0), axis=1, keepdims=True)
    upper = jnp.where(lax.broadcasted_iota(I32, (LANE, LANE), 0) < lax.broadcasted_iota(I32, (LANE, LANE), 1),
                      1.0, 0.0).astype(BF16)

    def exclusive_count(store):
        def body(ci, carry):
            sl = pl.ds(pl.multiple_of(lo + ci * LANE, LANE), LANE)
            x = flag_ref[:, sl]
            store(sl, carry + _dot(x.astype(BF16), upper))
            return carry + jnp.sum(x, axis=1, keepdims=True)

        lax.fori_loop(0, n // LANE, body, jnp.zeros((ne, 1), F32))

    flag_ref[:, cols] = jnp.where(eq, 1.0, 0.0)

    def store_tie_rank(sl, rank):
        posx_ref[:, sl] = rank.astype(I32)

    exclusive_count(store_tie_rank)
    sel = gt | (eq & (posx_ref[:, cols].astype(F32) < need))
    sel_ref[:, cols] = jnp.where(sel, 1, 0).astype(I32)
    flag_ref[:, cols] = jnp.where(sel, 1.0, 0.0)

    def store_pos(sl, cnt):
        posx_ref[:, sl] = cnt.astype(I32)

    exclusive_count(store_pos)


def _route_call(logits_t, sets, caps):
    ne, n = logits_t.shape
    spec = pl.BlockSpec((ne, n), lambda: (0, 0))
    return pl.pallas_call(
        functools.partial(_route_kernel, sets=sets, caps=caps),
        in_specs=[spec],
        out_specs=[spec, spec, spec],
        out_shape=[
            jax.ShapeDtypeStruct((ne, n), F32),
            jax.ShapeDtypeStruct((ne, n), I32),
            jax.ShapeDtypeStruct((ne, n), I32),
        ],
        scratch_shapes=[pltpu.VMEM((ne, n), F32)],
        compiler_params=pltpu.CompilerParams(vmem_limit_bytes=VMEM_LIMIT),
        name="expert_choice_route",
    )(logits_t)


def _start_copy(copy, expert):
    copy.start(priority=expert % 2)


def _wait_copy(copy, expert):
    del expert
    copy.wait()


def _moe_round_masks(b, r, base_ref, sel_ref, posx_ref, ne, value_rows=None):
    n_tok = sel_ref.shape[1]
    row_iota = lax.broadcasted_iota(I32, (MOE_ROUND, n_tok), 0)
    pieces = []
    for e in range(ne):
        local = posx_ref[e:e + 1, :] - (base_ref[b * ne + e] + r * MOE_ROUND)
        hit = (sel_ref[e:e + 1, :] == 1) & (local == row_iota)
        value = 1.0 if value_rows is None else value_rows[e:e + 1, :]
        pieces.append(jnp.where(hit, value, 0.0).astype(BF16))
    return jnp.concatenate(pieces, axis=0)


def _dispatch_kernel(base_ref, start_ref, pc_ref, nr_ref, first_ref, h_ref, sel_ref, posx_ref, xe_zero_ref, xe_ref,
                     stage_ref, sem, *, ne):
    del xe_zero_ref
    b = pl.program_id(0)
    n_rounds = nr_ref[b]
    prev = jnp.maximum(b - 1, 0)
    prev_pending = (b > 0) & (nr_ref[prev] >= 1)

    def pieces(blk, r, fn):
        slot = (first_ref[blk] + r) % 2
        for e in range(ne):
            rem = pc_ref[blk * ne + e] - r * MOE_ROUND
            dst = start_ref[blk * ne + e] + r * MOE_ROUND
            for size, cond, off in (
                    (MOE_ROUND, rem >= MOE_ROUND, 0),
                    (32, (rem > 0) & (rem < MOE_ROUND) & ((rem & 32) != 0), 0),
                    (16, (rem > 0) & (rem < MOE_ROUND) & ((rem & 16) != 0), rem & 32)):
                @pl.when(cond)
                def _(size=size, off=off, e=e, dst=dst):
                    src_rows = pl.ds(pl.multiple_of(e * MOE_ROUND + off, BF16_SUBLANE), size)
                    dst_rows = pl.ds(pl.multiple_of(dst + off, BF16_SUBLANE), size)
                    fn(pltpu.make_async_copy(stage_ref.at[slot, src_rows], xe_ref.at[e, dst_rows], sem.at[slot]), e)

    def wait_previous_block():
        pieces(prev, nr_ref[prev] - 1, _wait_copy)

    def round_body(r, carry):
        onehot = _moe_round_masks(b, r, base_ref, sel_ref, posx_ref, ne)
        stage_ref[(first_ref[b] + r) % 2] = _dot(onehot, h_ref[...]).astype(BF16)

        @pl.when(r >= 1)
        def _():
            pieces(b, r - 1, _wait_copy)

        @pl.when((r == 0) & prev_pending)
        def _():
            wait_previous_block()

        pieces(b, r, _start_copy)
        return carry

    lax.fori_loop(0, n_rounds, round_body, 0)

    @pl.when((n_rounds == 0) & prev_pending)
    def _():
        wait_previous_block()

    @pl.when((n_rounds >= 1) & (b == pl.num_programs(0) - 1))
    def _():
        pieces(b, n_rounds - 1, _wait_copy)


def _dispatch_call(h2, sel, posx, tables, list_rows, old_lists, *, sb):
    n, d = h2.shape
    ne = sel.shape[0]
    xe_zero = jnp.zeros((ne, list_rows, d), BF16) if old_lists is None else old_lists
    grid_spec = pltpu.PrefetchScalarGridSpec(
        num_scalar_prefetch=len(tables),
        grid=(n // sb,),
        in_specs=[
            pl.BlockSpec((sb, d), lambda b, *_: (b, 0)),
            pl.BlockSpec((ne, sb), lambda b, *_: (0, b)),
            pl.BlockSpec((ne, sb), lambda b, *_: (0, b)),
            pl.BlockSpec(memory_space=pl.ANY),
        ],
        out_specs=pl.BlockSpec(memory_space=pl.ANY),
        scratch_shapes=[pltpu.VMEM((2, ne * MOE_ROUND, d), BF16), pltpu.SemaphoreType.DMA((2,))],
    )
    return pl.pallas_call(
        functools.partial(_dispatch_kernel, ne=ne),
        grid_spec=grid_spec,
        out_shape=jax.ShapeDtypeStruct((ne, list_rows, d), BF16),
        input_output_aliases={len(tables) + 3: 0},
        compiler_params=_cparams("arbitrary"),
        name="expert_dispatch",
    )(*tables, h2, sel, posx, xe_zero)


def _expert_ffn_kernel(tot_ref, x_ref, wg_ref, wu_ref, wd_ref, y_ref, wgf_ref, wuf_ref, wdf_ref, wgb_ref, wub_ref,
                       wdb_ref, acc_ref, sem, *, layer, rt, n_ff):
    e, f, t = pl.program_id(0), pl.program_id(1), pl.program_id(2)
    fc = wgb_ref.shape[1]
    group = e * n_ff + f
    n_groups = pl.num_programs(0) * n_ff

    def weight_copies(g, fn):
        ge, gf, slot = g // n_ff, g % n_ff, g % 2
        cols = pl.ds(pl.multiple_of(gf * fc, LANE), fc)
        fn(pltpu.make_async_copy(wg_ref.at[layer, ge, :, cols], wgf_ref.at[slot], sem.at[slot]))
        fn(pltpu.make_async_copy(wu_ref.at[layer, ge, :, cols], wuf_ref.at[slot], sem.at[slot]))
        fn(pltpu.make_async_copy(wd_ref.at[layer, ge, cols, :], wdf_ref.at[slot], sem.at[slot]))

    @pl.when(t == 0)
    def _():
        @pl.when(group == 0)
        def _():
            weight_copies(group, lambda cp: cp.start())

        weight_copies(group, lambda cp: cp.wait())

        @pl.when(group + 1 < n_groups)
        def _():
            weight_copies(jnp.minimum(group + 1, n_groups - 1), lambda cp: cp.start())

        slot = group % 2
        wgb_ref[...] = wgf_ref[slot].astype(BF16)
        wub_ref[...] = wuf_ref[slot].astype(BF16)
        wdb_ref[...] = wdf_ref[slot].astype(BF16)

    rows = pl.ds(pl.multiple_of(t * rt, rt), rt)
    live = t * rt < tot_ref[e]

    @pl.when(live)
    def _():
        x = x_ref[...]
        hid = (_silu(_dot(x, wgb_ref[...])) * _dot(x, wub_ref[...])).astype(BF16)
        y = _dot(hid, wdb_ref[...])

        @pl.when(f == 0)
        def _():
            acc_ref[rows, :] = y

        @pl.when(f != 0)
        def _():
            acc_ref[rows, :] += y

    @pl.when((f == n_ff - 1) & live)
    def _():
        y_ref[...] = acc_ref[rows, :].astype(BF16)

    @pl.when((f == n_ff - 1) & jnp.logical_not(live))
    def _():
        y_ref[...] = jnp.zeros_like(y_ref)


def _expert_ffn_call(xe, totals, wg, wu, wd, layer, *, rt, fc):
    ne, list_rows, d = xe.shape
    ff = wg.shape[-1]
    n_ff = ff // fc
    grid_spec = pltpu.PrefetchScalarGridSpec(
        num_scalar_prefetch=1,
        grid=(ne, n_ff, list_rows // rt),
        in_specs=[
            pl.BlockSpec((None, rt, d), lambda e, f, t, *_: (e, t, 0)),
            pl.BlockSpec(memory_space=pl.ANY),
            pl.BlockSpec(memory_space=pl.ANY),
            pl.BlockSpec(memory_space=pl.ANY),
        ],
        out_specs=pl.BlockSpec((None, rt, d), lambda e, f, t, *_: (e, jnp.where(f == n_ff - 1, t, 0), 0)),
        scratch_shapes=[pltpu.VMEM((2, d, fc), F32), pltpu.VMEM((2, d, fc), F32), pltpu.VMEM((2, fc, d), F32),
                        pltpu.VMEM((d, fc), BF16), pltpu.VMEM((d, fc), BF16), pltpu.VMEM((fc, d), BF16),
                        pltpu.VMEM((list_rows, d), F32), pltpu.SemaphoreType.DMA((2,))],
    )
    return pl.pallas_call(
        functools.partial(_expert_ffn_kernel, layer=layer, rt=rt, n_ff=n_ff),
        grid_spec=grid_spec,
        out_shape=jax.ShapeDtypeStruct((ne, list_rows, d), BF16),
        compiler_params=_cparams("arbitrary", "arbitrary", "arbitrary"),
        name="expert_ffn",
    )(totals, xe, wg, wu, wd)


def _combine_kernel(base_ref, start_ref, pc_ref, nr_ref, first_ref, sel_ref, posx_ref, aff_ref, y_ref, out_ref,
                    ybuf_ref, sem, *, ne):
    b = pl.program_id(0)
    last = pl.num_programs(0) - 1
    n_rounds = nr_ref[b]
    first = first_ref[b]

    def windows(blk, r, fn):
        slot = (first_ref[blk] + r) % 2
        for e in range(ne):
            @pl.when(pc_ref[blk * ne + e] - r * MOE_ROUND > 0)
            def _(e=e):
                src = pl.ds(pl.multiple_of(start_ref[blk * ne + e] + r * MOE_ROUND, BF16_SUBLANE), MOE_ROUND)
                fn(pltpu.make_async_copy(y_ref.at[e, src], ybuf_ref.at[slot, pl.ds(e * MOE_ROUND, MOE_ROUND)],
                                         sem.at[slot]), e)

    def start_next_block():
        @pl.when(b < last)
        def _():
            windows(jnp.minimum(b + 1, last), 0, _start_copy)

    @pl.when(b == 0)
    def _():
        ybuf_ref[...] = jnp.zeros_like(ybuf_ref)
        windows(0, 0, _start_copy)

    out_ref[...] = jnp.zeros_like(out_ref)

    @pl.when(n_rounds == 0)
    def _():
        start_next_block()

    def round_body(r, carry):
        @pl.when(r + 1 < n_rounds)
        def _():
            windows(b, r + 1, _start_copy)

        @pl.when(r + 1 == n_rounds)
        def _():
            start_next_block()

        windows(b, r, _wait_copy)
        weights = _moe_round_masks(b, r, base_ref, sel_ref, posx_ref, ne, value_rows=aff_ref)
        out_ref[...] += _dot_tn(weights, ybuf_ref[(first + r) % 2])
        return carry

    lax.fori_loop(0, n_rounds, round_body, 0)


def _combine_call(ye, sel, posx, aff, tables, *, sb):
    ne, _, d = ye.shape
    n = sel.shape[1]
    grid_spec = pltpu.PrefetchScalarGridSpec(
        num_scalar_prefetch=len(tables),
        grid=(n // sb,),
        in_specs=[
            pl.BlockSpec((ne, sb), lambda b, *_: (0, b)),
            pl.BlockSpec((ne, sb), lambda b, *_: (0, b)),
            pl.BlockSpec((ne, sb), lambda b, *_: (0, b)),
            pl.BlockSpec(memory_space=pl.ANY),
        ],
        out_specs=pl.BlockSpec((sb, d), lambda b, *_: (b, 0)),
        scratch_shapes=[pltpu.VMEM((2, ne * MOE_ROUND, d), BF16), pltpu.SemaphoreType.DMA((2,))],
    )
    return pl.pallas_call(
        functools.partial(_combine_kernel, ne=ne),
        grid_spec=grid_spec,
        out_shape=jax.ShapeDtypeStruct((n, d), F32),
        compiler_params=_cparams("arbitrary"),
        name="expert_combine",
    )(*tables, sel, posx, aff, ye)


def _moe_tables(posx, sets, caps, sb, sbc):
    ne = posx.shape[0]
    fine = [jnp.concatenate([posx[:, lo:lo + n:sbc], jnp.full((ne, 1), cap, I32)], axis=1)
            for (lo, n), cap in zip(sets, caps)]
    step = sb // sbc
    base = jnp.concatenate([e[:, :-1:step].T for e in fine], axis=0)
    cnt = jnp.concatenate([(e[:, step::step] - e[:, :-1:step]).T for e in fine], axis=0)
    pc = (cnt + (BF16_SUBLANE - 1)) // BF16_SUBLANE * BF16_SUBLANE
    ends = jnp.cumsum(pc, axis=0)
    start = ends - pc
    rounds = lambda rows: jnp.max((rows + (MOE_ROUND - 1)) // MOE_ROUND, axis=1)
    flat = lambda *arrays: tuple(a.reshape(-1).astype(I32) for a in arrays)
    rounds_c = rounds(pc)
    dispatch = flat(base, start, pc, rounds_c, jnp.cumsum(rounds_c) - rounds_c)

    base_f = jnp.concatenate([e[:, :-1].T for e in fine], axis=0)
    cnt_f = jnp.concatenate([(e[:, 1:] - e[:, :-1]).T for e in fine], axis=0)
    first = jnp.repeat(start - base, step, axis=0) + base_f
    off = first % BF16_SUBLANE
    rows_f = jnp.where(cnt_f > 0, off + cnt_f, 0)
    rounds_f = rounds(rows_f)
    combine = flat(base_f - off, first - off, rows_f, rounds_f, jnp.cumsum(rounds_f) - rounds_f)
    return dispatch, combine, ends[-1].astype(I32)


def _rope_tables(n_tokens, dk):
    rows = n_tokens // GRID_W
    r, c = jnp.meshgrid(jnp.arange(rows), jnp.arange(GRID_W), indexing="ij")
    pos = jnp.stack([r.reshape(-1), c.reshape(-1)], axis=-1).astype(F32)
    nf = dk // 4
    inv_freq = ROPE_BASE ** (-jnp.arange(nf, dtype=F32) / nf)
    ang = pos[:, :, None] * inv_freq
    cos, sin = jnp.cos(ang), jnp.sin(ang)
    cos_t = jnp.concatenate([cos[:, 0], cos[:, 0], cos[:, 1], cos[:, 1]], axis=-1)
    sin_t = jnp.concatenate([-sin[:, 0], sin[:, 0], -sin[:, 1], sin[:, 1]], axis=-1)
    return cos_t, sin_t


def kernel(x_prompt, x_sample, state_ret, state_hgrn, c, c_ctx, ada_w, ada_b, norm_mix_g, norm_ffn_g, w_in,
           ret_gamma_logit, hg_lb_logit, w_ret_o, w_hg_o, w_merge, w_out, w_router, w_exp_gate, w_exp_up,
           w_exp_down, final_g):
    b_ctx, t_ctx, d = x_prompt.shape
    b_lat, t_lat, _ = x_sample.shape
    depth = w_in.shape[0]
    ret_heads, ret_dk, ret_dv = state_ret.shape[3:]
    hg_heads, hg_dk, hg_dv = state_hgrn.shape[3:]
    ne = w_router.shape[-1]
    n_ctx, n_lat = b_ctx * t_ctx, b_lat * t_lat
    n_tok = n_ctx + n_lat
    qw, vw = ret_heads * ret_dk, ret_heads * ret_dv
    kw, hw = hg_heads * hg_dk, hg_heads * hg_dv
    col_rg = 2 * qw + vw
    col_gq = col_rg + vw
    col_gf = col_gq + kw
    col_gi = col_gf + 2 * kw
    col_go = col_gi + hw
    assert col_go + hw == w_in.shape[-1] and b_lat < MOD_ROWS

    tm = 256
    assert t_ctx % tm == 0 and t_lat % tm == 0
    tm_proj = 1024 if n_tok % 1024 == 0 else tm
    sb = 1024 if (n_ctx % 4096 == 0 and n_lat % 4096 == 0) else 256
    assert n_ctx % sb == 0 and n_lat % sb == 0
    sets = ((0, n_ctx), (n_ctx, n_lat))
    caps = [CAPACITY_FACTOR * n_set // ne for _, n_set in sets]
    sbc = 256
    list_rows = sum(caps) + BF16_SUBLANE * (n_tok // sb) + MOE_ROUND
    n_tiles = max(1, list_rows // 512)
    rt = -(-list_rows // (n_tiles * BF16_SUBLANE)) * BF16_SUBLANE
    list_rows = n_tiles * rt

    def mod_rows(tmx):
        return lambda i: jnp.where(i < n_ctx // tmx, 0, 1 + (i - n_ctx // tmx) // (t_lat // tmx))

    mod_row = mod_rows(tm)
    tm_norm = max(t for t in (1024, 512, tm) if n_ctx % t == 0 and t_lat % t == 0)
    norm_row = mod_rows(tm_norm)

    x = jnp.concatenate([x_prompt.reshape(n_ctx, d), x_sample.reshape(n_lat, d)], axis=0)
    cvec = jnp.zeros((MOD_ROWS, d), F32).at[0].set(c_ctx).at[1:1 + b_lat].set(c)
    mod = _mod_call(cvec, ada_w, ada_b).reshape(depth, MOD_ROWS, 1, 6 * d)

    log_gamma = jax.nn.log_sigmoid(ret_gamma_logit.astype(F32))
    p_lb = jax.nn.softmax(hg_lb_logit.astype(F32), axis=0)
    hg_lb = jnp.clip(jnp.cumsum(p_lb, axis=0) - p_lb[0:1], 0.0, 1.0 - 1e-6)
    rope = _rope_tables(t_lat, ret_dk)

    w_ret_o_b, w_hg_o_b, w_out_b = w_ret_o.astype(BF16), w_hg_o.astype(BF16), w_out.astype(BF16)
    w_merge_b = w_merge.astype(BF16)
    w_router_t = jnp.swapaxes(w_router, 1, 2)
    w_router_hi = w_router_t.astype(BF16)
    w_router_t = jnp.stack([w_router_hi, (w_router_t - w_router_hi.astype(F32)).astype(BF16)], axis=1)
    fc = min(1024, w_exp_gate.shape[-1])
    tn = min(2048, w_in.shape[-1])

    _, h = _norm_call(x, None, None, mod[0], norm_mix_g[0], norm_row, tm_norm)
    new_ret, new_hg = [], []
    xe = None
    for l in range(depth):
        proj = _proj_call(h, w_in, l, tm_proj, tn)

        ret_kw = dict(heads=ret_heads, dk=ret_dk, dv=ret_dv)
        orf_c, orb_c, s_ret = _ret_call(proj, log_gamma[l], None, None, tok0=0, batch=b_ctx, seq=t_ctx, **ret_kw)
        orf_l, orb_l, _ = _ret_call(proj, log_gamma[l], state_ret[:, l], rope, tok0=n_ctx, batch=b_lat, seq=t_lat,
                                    **ret_kw)
        hg_kw = dict(heads=hg_heads, dk=hg_dk, dv=hg_dv, col_q=col_gq, col_f=col_gf, col_i=col_gi)
        ogf_c, ogb_c, s_hg = _hg_call(proj, hg_lb[l], None, tok0=0, batch=b_ctx, seq=t_ctx, **hg_kw)
        ogf_l, ogb_l, _ = _hg_call(proj, hg_lb[l], state_hgrn[:, l], tok0=n_ctx, batch=b_lat, seq=t_lat, **hg_kw)
        new_ret.append(s_ret)
        new_hg.append(s_hg)

        x, h2, logits_t = _mix_out_call(
            (orf_c, orb_c, ogf_c, ogb_c), (orf_l, orb_l, ogf_l, ogb_l), proj, h, x, mod[l],
            mod_row, norm_ffn_g[l], w_ret_o_b[l], w_hg_o_b[l], w_merge_b[l], w_out_b[l], w_router_t[l], tm,
            ret_heads=ret_heads, hg_heads=hg_heads, col_rg=col_rg, col_go=col_go)

        aff, sel, posx = _route_call(logits_t, sets, caps)
        dispatch_tables, combine_tables, totals = _moe_tables(posx, sets, caps, sb, sbc)
        xe = _dispatch_call(h2, sel, posx, dispatch_tables, list_rows, xe, sb=sb)
        ye = _expert_ffn_call(xe, totals, w_exp_gate, w_exp_up, w_exp_down, l, rt=rt, fc=fc)
        ffn = _combine_call(ye, sel, posx, aff, combine_tables, sb=sbc)

        if l + 1 < depth:
            x, h = _norm_call(x, ffn, mod[l], mod[l + 1], norm_mix_g[l + 1], norm_row, tm_norm)
        else:
            _, y_prompt = _norm_call(x, ffn, mod[l], None, final_g, norm_row, tm_norm, rows=(0, n_ctx))
            _, y_sample = _norm_call(x, ffn, mod[l], None, final_g, norm_row, tm_norm, rows=(n_ctx, n_lat))

    y_prompt = y_prompt.reshape(b_ctx, t_ctx, d)
    y_sample = y_sample.reshape(b_lat, t_lat, d)
    return y_prompt, y_sample, jnp.stack(new_ret, axis=1), jnp.stack(new_hg, axis=1)
```
